```python
import math
import jax
import jax.numpy as jnp
from jax import lax
import numpy as np

D_MODEL = 1024
BATCH = 16
SEQ = 2048
DEPTH = 2
DEC_BATCH = 128
DEC_SEQ = 1
PAST_LEN = 8192
PAGE_SIZE = 128

N_EVEN = (DEPTH + 1) // 2
N_ODD = DEPTH // 2
EPS = 1e-6
NEG_INF = -1e30
D_FF = 2816
ML_H = 4
ML_DH = 128
ML_W = ML_H * ML_DH
CONV_K = 4
MLSTM_CHUNK = 64
NSA_H = 8
NSA_KV = 2
NSA_R = NSA_H // NSA_KV
NSA_DH = 64
NSA_W = NSA_H * NSA_DH
NSA_KVW = NSA_KV * NSA_DH
NSA_SCALE = NSA_DH ** -0.5
CMP_STRIDE = 16
CMP_LEN = 2 * CMP_STRIDE
CMP_HID = 2 * NSA_DH
SLC_BLOCK = 64
SLC_RATIO = SLC_BLOCK // CMP_STRIDE
SLC_OVERLAP_W = (1.0, 2.0, 2.0, 2.0, 1.0)
N_SELECT = 16
WINDOW = 512
SLC_QBLOCK = 32
WIN_QBLOCK = 128
FORCE_SCORE = 1e6
REL_BUCKETS = 32
REL_MAX_DIST = 128
MLA_H = 16
MLA_NOPE = 64
MLA_ROPE = 32
MLA_V = 64
Q_LORA = 384
KV_LORA = 256
MLA_SCALE = (MLA_NOPE + MLA_ROPE) ** -0.5
MLA_QBLOCK = 128
ROPE_THETA = 10000.0
MEM_LEN = 256
XM_H = 4
XM_DH = 128
XM_W = XM_H * XM_DH
EV_SPLITS = (ML_W, ML_W, ML_W, ML_H, ML_H, NSA_W) + (NSA_KVW,) * 6 + (NSA_H * 3,)
EV_IN = sum(EV_SPLITS)
OD_SPLITS = (Q_LORA, KV_LORA, MLA_ROPE)
OD_IN = sum(OD_SPLITS)

kernel_name = 'hybrid_mlstm_nsa_mla_macaron_step'


def split_cols(a, sizes):
    idx = [int(s) for s in np.cumsum(sizes)[:-1]]
    return jnp.split(a, idx, axis=-1)


def rms_norm(x, g):
    xf = x.astype(jnp.float32)
    y = xf * lax.rsqrt(jnp.mean(xf * xf, axis=-1, keepdims=True) + EPS)
    return (y * g.astype(jnp.float32)).astype(x.dtype)


def swiglu_half(x, g, w_gate, w_up, w_down):
    h = rms_norm(x, g)
    return x + 0.5 * ((jax.nn.silu(h @ w_gate) * (h @ w_up)) @ w_down)


def t5_bucket(dist):
    n = jnp.maximum(dist, 0)
    exact = REL_BUCKETS // 2
    nf = jnp.maximum(n, exact).astype(jnp.float32)
    large = exact + (jnp.log(nf / exact) / math.log(REL_MAX_DIST / exact) * (REL_BUCKETS - exact)).astype(jnp.int32)
    return jnp.where(n < exact, n, jnp.minimum(large, REL_BUCKETS - 1))


def apply_rope(x, pos):
    half = x.shape[-1] // 2
    inv = ROPE_THETA ** (-jnp.arange(half, dtype=jnp.float32) / half)
    ang = pos.astype(jnp.float32)[:, None] * inv[None, :]
    ang = ang.reshape(ang.shape[:1] + (1,) * (x.ndim - 3) + (half,))
    cos, sin = jnp.cos(ang).astype(x.dtype), jnp.sin(ang).astype(x.dtype)
    x1, x2 = x[..., :half], x[..., half:]
    return jnp.concatenate([x1 * cos - x2 * sin, x1 * sin + x2 * cos], axis=-1)


def causal_conv(u, buf, w, b):
    S = u.shape[1]
    full = jnp.concatenate([buf.astype(u.dtype), u], axis=1)
    out = b + sum(full[:, j:j + S] * w[j] for j in range(CONV_K))
    return out, full[:, S:]


def gather_pages(pool, e, page_table):
    g = pool[e, page_table]
    return g.reshape((g.shape[0], g.shape[1] * g.shape[2]) + g.shape[3:])


def mlstm_chunkwise(q, k, v, i_pre, logf, C0, n0, m0):
    f32 = jnp.float32
    q, k, v, i_pre, logf = (a.astype(f32) for a in (q, k, v, i_pre, logf))
    B, H, S, D = q.shape
    L = MLSTM_CHUNK if S % MLSTM_CHUNK == 0 else S
    NC = S // L

    def chunks(a):
        return jnp.moveaxis(a.reshape((B, H, NC, L) + a.shape[3:]), 2, 0)

    causal = jnp.tril(jnp.ones((L, L), dtype=bool))

    def step(carry, inp):
        C, n, m = carry
        qc, kc, vc, ic, fc = inp
        b = jnp.cumsum(fc, axis=-1)
        g = b + m[..., None]
        dmat = jnp.where(causal, b[..., :, None] - b[..., None, :] + ic[..., None, :], -jnp.inf)
        mt = jnp.maximum(g, jnp.max(dmat, axis=-1))
        inter = jnp.exp(g - mt)
        sqk = jnp.einsum('bhtd,bhsd->bhts', qc, kc) * jnp.exp(dmat - mt[..., None])
        num = inter[..., None] * jnp.einsum('bhvd,bhtd->bhtv', C, qc) + jnp.einsum('bhts,bhsv->bhtv', sqk, vc)
        den = inter * jnp.einsum('bhd,bhtd->bht', n, qc) + jnp.sum(sqk, axis=-1)
        h = num / jnp.maximum(jnp.abs(den), jnp.exp(-mt))[..., None]
        b_end = b[..., -1]
        w_log = b_end[..., None] - b + ic
        m_new = jnp.maximum(b_end + m, jnp.max(w_log, axis=-1))
        decay = jnp.exp(b_end + m - m_new)
        w_in = jnp.exp(w_log - m_new[..., None])
        C_new = decay[..., None, None] * C + jnp.einsum('bhs,bhsv,bhsd->bhvd', w_in, vc, kc)
        n_new = decay[..., None] * n + jnp.einsum('bhs,bhsd->bhd', w_in, kc)
        return (C_new, n_new, m_new), h

    (C1, n1, m1), hs = lax.scan(step, (C0.astype(f32), n0.astype(f32), m0.astype(f32)),
                                tuple(chunks(a) for a in (q, k, v, i_pre, logf)))
    return jnp.moveaxis(hs, 0, 2).reshape(B, H, S, D), C1, n1, m1


def gqa_attend(q, k, v, mask, bias):
    s = jnp.einsum('bqgrd,bkgd->bgrqk', q, k).astype(jnp.float32) * NSA_SCALE
    s = s + jnp.transpose(bias, (2, 3, 0, 1)).astype(jnp.float32)
    s = jnp.where(mask, s, NEG_INF)
    p = jax.nn.softmax(s, axis=-1) * mask.astype(jnp.float32)
    o = jnp.einsum('bgrqk,bkgd->bqgrd', p.astype(v.dtype), v)
    return o, p


def slc_core(q, tq, idx, kg, vg, rel_bias):
    s = jnp.einsum('bqgrd,bqgnld->bqgnlr', q, kg).astype(jnp.float32) * NSA_SCALE
    pos = idx[..., None] * SLC_BLOCK + jnp.arange(SLC_BLOCK)
    dist = tq[None, :, None, None, None] - pos
    tbl = jnp.transpose(rel_bias.reshape(REL_BUCKETS, NSA_KV, NSA_R), (1, 0, 2))
    g_i = jnp.arange(NSA_KV)[None, None, :, None, None]
    s = s + tbl[g_i, t5_bucket(dist)].astype(jnp.float32)
    s = jnp.where((dist >= 0)[..., None], s, NEG_INF)
    B, Q, G, N, L, R = s.shape
    p = jax.nn.softmax(s.reshape(B, Q, G, N * L, R), axis=-2).reshape(s.shape)
    return jnp.einsum('bqgnlr,bqgnld->bqgrd', p.astype(vg.dtype), vg)


def to_chunks(a):
    B, T = a.shape[:2]
    pad = (-T) % CMP_STRIDE
    a = jnp.pad(a, ((0, 0), (0, pad), (0, 0), (0, 0)))
    return a.reshape((B, (T + pad) // CMP_STRIDE, CMP_STRIDE) + a.shape[2:])


def cmp_summaries(chunk_list, T, pe, w1, w2):
    lo = jnp.concatenate([jnp.einsum('bcjgd,jdh->bcgh', r, w1[:CMP_STRIDE]) for r in chunk_list], axis=1)
    hi = jnp.concatenate([jnp.einsum('bcjgd,jdh->bcgh', r, w1[CMP_STRIDE:]) for r in chunk_list], axis=1)
    n_cmp = (T - CMP_LEN) // CMP_STRIDE + 1
    hid = jax.nn.silu(lo[:, :n_cmp] + hi[:, 1:n_cmp + 1] + jnp.einsum('jd,jdh->h', pe, w1))
    return hid @ w2


def even_mixer(xn, pos0, past, w_in, w_out, conv_w, conv_b, ml_wq, ml_wk, ml_b_i, ml_b_f, ml_out_g,
               nsa_gq, nsa_gk_cmp, nsa_gk_slc, nsa_gk_win, pe_k, w1_k, w2_k, pe_v, w1_v, w2_v, gate_b, rel_bias):
    B, S, _ = xn.shape
    dt = xn.dtype
    (u, v_m, o_pre, i_pre, f_pre, q, kc, vc, ks, vs, kw, vw, g_pre) = split_cols(xn @ w_in, EV_SPLITS)
    tq = pos0 + jnp.arange(S)

    if past is None:
        conv_buf = jnp.zeros((B, CONV_K - 1, ML_W), dt)
        C0 = jnp.zeros((B, ML_H, ML_DH, ML_DH), jnp.float32)
        n0 = jnp.zeros((B, ML_H, ML_DH), jnp.float32)
        m0 = jnp.zeros((B, ML_H), jnp.float32)
    else:
        conv_buf, C0, n0, m0 = past['conv'], past['C'], past['n'], past['m']
    c, conv_new = causal_conv(u, conv_buf, conv_w, conv_b)
    ch = jax.nn.silu(c).reshape(B, S, ML_H, ML_DH)
    qm = jnp.einsum('bshd,hde->bhse', ch, ml_wq)
    km = jnp.einsum('bshd,hde->bhse', ch, ml_wk) * (ML_DH ** -0.5)
    vm = jnp.transpose(v_m.reshape(B, S, ML_H, ML_DH), (0, 2, 1, 3))
    ig = jnp.transpose(i_pre + ml_b_i, (0, 2, 1))
    lf = jax.nn.log_sigmoid(jnp.transpose(f_pre + ml_b_f, (0, 2, 1)).astype(jnp.float32))
    hm, C1, n1, m1 = mlstm_chunkwise(qm, km, vm, ig, lf, C0, n0, m0)
    hm = rms_norm(jnp.transpose(hm, (0, 2, 1, 3)).astype(dt), ml_out_g) * jax.nn.sigmoid(o_pre).reshape(B, S, ML_H, ML_DH)
    h_a = hm.reshape(B, S, ML_W)

    q = rms_norm(q.reshape(B, S, NSA_H, NSA_DH), nsa_gq)
    qg = q.reshape(B, S, NSA_KV, NSA_R, NSA_DH)
    kv_shape = (B, S, NSA_KV, NSA_DH)
    kc, vc, vs, vw = (a.reshape(kv_shape) for a in (kc, vc, vs, vw))
    ks = rms_norm(ks.reshape(kv_shape), nsa_gk_slc)
    kw = rms_norm(kw.reshape(kv_shape), nsa_gk_win)
    gates = jax.nn.sigmoid(g_pre.reshape(B, S, NSA_H, 3) + gate_b)

    if past is None:
        T = S
        kch, vch = [to_chunks(kc)], [to_chunks(vc)]
    else:
        e, pt = past['e'], past['page_table']
        T = pos0 + S
        kch = [to_chunks(gather_pages(past['cmp_k'], e, pt)), to_chunks(kc)]
        vch = [to_chunks(gather_pages(past['cmp_v'], e, pt)), to_chunks(vc)]
    k_cmp = rms_norm(cmp_summaries(kch, T, pe_k, w1_k, w2_k), nsa_gk_cmp)
    v_cmp = cmp_summaries(vch, T, pe_v, w1_v, w2_v)
    n_cmp = k_cmp.shape[1]
    dist = tq[:, None] - (jnp.arange(n_cmp) * CMP_STRIDE + CMP_LEN - 1)[None, :]
    bias = rel_bias[t5_bucket(dist)].reshape(S, n_cmp, NSA_KV, NSA_R)
    o_cmp, p_cmp = gqa_attend(qg, k_cmp, v_cmp, dist >= 0, bias)

    NS = -(-T // SLC_BLOCK)
    imp = jnp.pad(jnp.sum(p_cmp, axis=2), ((0, 0), (0, 0), (0, 0), (1, SLC_RATIO * NS - n_cmp)))
    p_slc = sum(w * imp[..., k:k + SLC_RATIO * NS:SLC_RATIO] for k, w in enumerate(SLC_OVERLAP_W))
    tb = (tq // SLC_BLOCK)[:, None]
    jb = jnp.arange(NS)[None, :]
    forced = (jb == 0) | (jb == tb) | (jb == tb - 1)
    score = jnp.where(jb <= tb, p_slc + FORCE_SCORE * forced.astype(jnp.float32), -1.0)
    idx = jnp.transpose(lax.top_k(score, min(N_SELECT, NS))[1], (0, 2, 1, 3))
    b_i = jnp.arange(B)[:, None, None, None]
    g_i = jnp.arange(NSA_KV)[None, None, :, None]

    if past is None:
        pad = NS * SLC_BLOCK - S
        ks_b = jnp.pad(ks, ((0, 0), (0, pad), (0, 0), (0, 0))).reshape(B, NS, SLC_BLOCK, NSA_KV, NSA_DH)
        vs_b = jnp.pad(vs, ((0, 0), (0, pad), (0, 0), (0, 0))).reshape(B, NS, SLC_BLOCK, NSA_KV, NSA_DH)
        QB = SLC_QBLOCK if S % SLC_QBLOCK == 0 else S

        def slc_block(i):
            qb = lax.dynamic_slice_in_dim(qg, i * QB, QB, axis=1)
            ib = lax.dynamic_slice_in_dim(idx, i * QB, QB, axis=1)
            tqb = pos0 + i * QB + jnp.arange(QB)
            return slc_core(qb, tqb, ib, ks_b[b_i, ib, :, g_i], vs_b[b_i, ib, :, g_i], rel_bias)

        o_slc = jnp.moveaxis(lax.map(slc_block, jnp.arange(S // QB)), 0, 1).reshape(B, S, NSA_KV, NSA_R, NSA_DH)
    else:
        bpp = PAGE_SIZE // SLC_BLOCK
        nb_past = pos0 // SLC_BLOCK
        jp = jnp.minimum(idx, nb_past - 1)
        page = pt[b_i, jp // bpp]
        rows = (jp % bpp)[..., None] * SLC_BLOCK + jnp.arange(SLC_BLOCK)
        kg_past = past['slc_k'][e, page[..., None], rows, g_i[..., None]]
        vg_past = past['slc_v'][e, page[..., None], rows, g_i[..., None]]
        nbn = -(-S // SLC_BLOCK)
        padn = nbn * SLC_BLOCK - S
        ks_nb = jnp.pad(ks, ((0, 0), (0, padn), (0, 0), (0, 0))).reshape(B, nbn, SLC_BLOCK, NSA_KV, NSA_DH)
        vs_nb = jnp.pad(vs, ((0, 0), (0, padn), (0, 0), (0, 0))).reshape(B, nbn, SLC_BLOCK, NSA_KV, NSA_DH)
        jn = jnp.clip(idx - nb_past, 0, nbn - 1)
        is_past = (idx < nb_past)[..., None, None]
        kg = jnp.where(is_past, kg_past, ks_nb[b_i, jn, :, g_i])
        vg = jnp.where(is_past, vg_past, vs_nb[b_i, jn, :, g_i])
        o_slc = slc_core(qg, tq, idx, kg, vg, rel_bias)

    if past is None:
        QB = WIN_QBLOCK if S % WIN_QBLOCK == 0 else S
        span = QB + WINDOW
        kw_p = jnp.pad(kw, ((0, 0), (WINDOW, 0), (0, 0), (0, 0)))
        vw_p = jnp.pad(vw, ((0, 0), (WINDOW, 0), (0, 0), (0, 0)))

        def win_block(i):
            start = i * QB
            qb = lax.dynamic_slice_in_dim(qg, start, QB, axis=1)
            tqb = pos0 + start + jnp.arange(QB)
            kpos = pos0 + start - WINDOW + jnp.arange(span)
            d = tqb[:, None] - kpos[None, :]
            mask = (d >= 0) & (d <= WINDOW) & (kpos >= 0)[None, :]
            bw = rel_bias[t5_bucket(d)].reshape(QB, span, NSA_KV, NSA_R)
            kb = lax.dynamic_slice_in_dim(kw_p, start, span, axis=1)
            vb = lax.dynamic_slice_in_dim(vw_p, start, span, axis=1)
            return gqa_attend(qb, kb, vb, mask, bw)[0]

        o_win = jnp.moveaxis(lax.map(win_block, jnp.arange(S // QB)), 0, 1).reshape(B, S, NSA_KV, NSA_R, NSA_DH)
        nb = min(WINDOW, S)
        win_k_new, win_v_new = kw[:, S - nb:], vw[:, S - nb:]
    else:
        nbuf = past['win_k'].shape[1]
        k_all = jnp.concatenate([past['win_k'].astype(dt), kw], axis=1)
        v_all = jnp.concatenate([past['win_v'].astype(dt), vw], axis=1)
        kpos = jnp.concatenate([pos0 - nbuf + jnp.arange(nbuf), tq])
        d = tq[:, None] - kpos[None, :]
        mask = (d >= 0) & (d <= WINDOW)
        bw = rel_bias[t5_bucket(d)].reshape(S, nbuf + S, NSA_KV, NSA_R)
        o_win = gqa_attend(qg, k_all, v_all, mask, bw)[0]
        win_k_new, win_v_new = k_all[:, S:], v_all[:, S:]

    hs = (NSA_H, NSA_DH)
    o_b = (gates[..., 0:1] * o_cmp.reshape((B, S) + hs) + gates[..., 1:2] * o_slc.reshape((B, S) + hs)
           + gates[..., 2:3] * o_win.reshape((B, S) + hs))
    h_b = o_b.reshape(B, S, NSA_W)

    out = jnp.concatenate([h_a, h_b], axis=-1) @ w_out
    new = dict(C=C1.astype(dt), n=n1.astype(dt), m=m1.astype(dt), conv=conv_new,
               cmp_k=kc, cmp_v=vc, slc_k=ks, slc_v=vs, win_k=win_k_new, win_v=win_v_new)
    return out, new


def odd_mixer(xn, pos0, past, w_in, g_cq, w_uq, g_q, g_ckv, g_kr, w_uk, w_uv, w_out):
    B, S, _ = xn.shape
    cq, ckv, kr = split_cols(xn @ w_in, OD_SPLITS)
    tq = pos0 + jnp.arange(S)
    q = rms_norm((rms_norm(cq, g_cq) @ w_uq).reshape(B, S, MLA_H, MLA_NOPE + MLA_ROPE), g_q)
    q_nope = q[..., :MLA_NOPE]
    q_rope = apply_rope(q[..., MLA_NOPE:], tq)
    ckv = rms_norm(ckv, g_ckv)
    kr = apply_rope(rms_norm(kr, g_kr), tq)
    if past is None:
        k_nope = jnp.einsum('bsc,chn->bshn', ckv, w_uk)
        v = jnp.einsum('bsc,chv->bshv', ckv, w_uv)
        QB = MLA_QBLOCK if S % MLA_QBLOCK == 0 else S

        def blk(i):
            start = i * QB
            qn = lax.dynamic_slice_in_dim(q_nope, start, QB, axis=1)
            qr = lax.dynamic_slice_in_dim(q_rope, start, QB, axis=1)
            tqb = pos0 + start + jnp.arange(QB)
            s = (jnp.einsum('bqhn,bkhn->bhqk', qn, k_nope) + jnp.einsum('bqhr,bkr->bhqk', qr, kr)).astype(jnp.float32)
            s = jnp.where(tq[None, :] <= tqb[:, None], s * MLA_SCALE, NEG_INF)
            p = jax.nn.softmax(s, axis=-1)
            return jnp.einsum('bhqk,bkhv->bqhv', p.astype(v.dtype), v)

        o = jnp.moveaxis(lax.map(blk, jnp.arange(S // QB)), 0, 1).reshape(B, S, MLA_H * MLA_V)
    else:
        e, pt = past['e'], past['page_table']
        ckv_past = gather_pages(past['ckv'], e, pt).astype(ckv.dtype)
        kr_past = gather_pages(past['krope'], e, pt).astype(kr.dtype)
        P = ckv_past.shape[1]
        q_lat = jnp.einsum('bqhn,chn->bqhc', q_nope, w_uk)
        s_past = jnp.einsum('bqhc,btc->bhqt', q_lat, ckv_past) + jnp.einsum('bqhr,btr->bhqt', q_rope, kr_past)
        s_new = jnp.einsum('bqhc,bkc->bhqk', q_lat, ckv) + jnp.einsum('bqhr,bkr->bhqk', q_rope, kr)
        s_new = jnp.where(jnp.tril(jnp.ones((S, S), dtype=bool)), s_new.astype(jnp.float32) * MLA_SCALE, NEG_INF)
        s = jnp.concatenate([s_past.astype(jnp.float32) * MLA_SCALE, s_new], axis=-1)
        p = jax.nn.softmax(s, axis=-1).astype(ckv.dtype)
        o_lat = jnp.einsum('bhqt,btc->bqhc', p[..., :P], ckv_past) + jnp.einsum('bhqk,bkc->bqhc', p[..., P:], ckv)
        o = jnp.einsum('bqhc,chv->bqhv', o_lat, w_uv).reshape(B, S, MLA_H * MLA_V)
    return o @ w_out, dict(ckv=ckv, krope=kr)


def mem_kv(mem, g_mem, wk, wv, gk):
    B, M, _ = mem.shape
    m = rms_norm(mem, g_mem)
    k = rms_norm((m @ wk).reshape(B, M, XM_H, XM_DH), gk)
    v = (m @ wv).reshape(B, M, XM_H, XM_DH)
    return k, v


def mem_attend(xn, k, v, wq, gq, wo):
    B, S, _ = xn.shape
    q = rms_norm((xn @ wq).reshape(B, S, XM_H, XM_DH), gq)
    s = jnp.einsum('bshd,bmhd->bhsm', q, k.astype(q.dtype)).astype(jnp.float32) * (XM_DH ** -0.5)
    p = jax.nn.softmax(s, axis=-1)
    return jnp.einsum('bhsm,bmhd->bshd', p.astype(xn.dtype), v.astype(xn.dtype)).reshape(B, S, XM_W) @ wo


def stack_key(lst, name):
    return jnp.stack([d[name] for d in lst])


def setup_inputs(seed: int = 0) -> dict:
    key = jax.random.key(seed)
    keys = iter(jax.random.split(key, 96))

    def nrm(shape, scale=1.0):
        return jax.random.normal(next(keys), shape, jnp.float32) * scale

    def lin(shape, fan_in):
        return nrm(shape, fan_in ** -0.5)

    def gain(shape):
        return 1.0 + nrm(shape, 0.02)

    n_pages = PAST_LEN // PAGE_SIZE
    n_pool = (DEC_BATCH * n_pages * 5) // 4
    n_win = min(WINDOW, PAST_LEN)
    page_table = jax.random.permutation(next(keys), n_pool)[:DEC_BATCH * n_pages].reshape(DEC_BATCH, n_pages).astype(jnp.int32)
    kvp = (N_EVEN, n_pool, PAGE_SIZE, NSA_KV, NSA_DH)
    return {
        'x_prompt': nrm((BATCH, SEQ, D_MODEL)),
        'x_sample': nrm((DEC_BATCH, DEC_SEQ, D_MODEL)),
        'mem_prompt': nrm((BATCH, MEM_LEN, D_MODEL)),
        'state_ml_C': nrm((N_EVEN, DEC_BATCH, ML_H, ML_DH, ML_DH), 0.3),
        'state_ml_n': nrm((N_EVEN, DEC_BATCH, ML_H, ML_DH), 0.3),
        'state_ml_m': nrm((N_EVEN, DEC_BATCH, ML_H)),
        'state_ml_conv': nrm((N_EVEN, DEC_BATCH, CONV_K - 1, ML_W)),
        'cache_cmp_k': nrm(kvp),
        'cache_cmp_v': nrm(kvp),
        'cache_slc_k': nrm(kvp),
        'cache_slc_v': nrm(kvp),
        'cache_win_k': nrm((N_EVEN, DEC_BATCH, n_win, NSA_KV, NSA_DH)),
        'cache_win_v': nrm((N_EVEN, DEC_BATCH, n_win, NSA_KV, NSA_DH)),
        'cache_mla_ckv': nrm((N_ODD, n_pool, PAGE_SIZE, KV_LORA)),
        'cache_mla_krope': nrm((N_ODD, n_pool, PAGE_SIZE, MLA_ROPE)),
        'cache_mem_k': nrm((DEPTH, DEC_BATCH, MEM_LEN, XM_H, XM_DH)),
        'cache_mem_v': nrm((DEPTH, DEC_BATCH, MEM_LEN, XM_H, XM_DH)),
        'page_table': page_table,
        'rel_bias': nrm((REL_BUCKETS, NSA_H), 0.2),
        'ffn1_norm': gain((DEPTH, D_MODEL)),
        'ffn1_wg': lin((DEPTH, D_MODEL, D_FF), D_MODEL),
        'ffn1_wu': lin((DEPTH, D_MODEL, D_FF), D_MODEL),
        'ffn1_wd': lin((DEPTH, D_FF, D_MODEL), D_FF),
        'mix_norm': gain((DEPTH, D_MODEL)),
        'xm_norm': gain((DEPTH, D_MODEL)),
        'xm_mem_norm': gain((DEPTH, D_MODEL)),
        'xm_wq': lin((DEPTH, D_MODEL, XM_W), D_MODEL),
        'xm_wk': lin((DEPTH, D_MODEL, XM_W), D_MODEL),
        'xm_wv': lin((DEPTH, D_MODEL, XM_W), D_MODEL),
        'xm_wo': lin((DEPTH, XM_W, D_MODEL), XM_W),
        'xm_gq': gain((DEPTH, XM_DH)),
        'xm_gk': gain((DEPTH, XM_DH)),
        'ffn2_norm': gain((DEPTH, D_MODEL)),
        'ffn2_wg': lin((DEPTH, D_MODEL, D_FF), D_MODEL),
        'ffn2_wu': lin((DEPTH, D_MODEL, D_FF), D_MODEL),
        'ffn2_wd': lin((DEPTH, D_FF, D_MODEL), D_FF),
        'ev_w_in': lin((N_EVEN, D_MODEL, EV_IN), D_MODEL),
        'ev_w_out': lin((N_EVEN, ML_W + NSA_W, D_MODEL), ML_W + NSA_W),
        'ml_conv_w': lin((N_EVEN, CONV_K, ML_W), CONV_K),
        'ml_conv_b': nrm((N_EVEN, ML_W), 0.02),
        'ml_wq': lin((N_EVEN, ML_H, ML_DH, ML_DH), ML_DH),
        'ml_wk': lin((N_EVEN, ML_H, ML_DH, ML_DH), ML_DH),
        'ml_b_i': nrm((N_EVEN, ML_H), 0.1),
        'ml_b_f': 3.0 + nrm((N_EVEN, ML_H), 0.1),
        'ml_out_g': gain((N_EVEN, ML_H, ML_DH)),
        'nsa_gq': gain((N_EVEN, NSA_DH)),
        'nsa_gk_cmp': gain((N_EVEN, NSA_DH)),
        'nsa_gk_slc': gain((N_EVEN, NSA_DH)),
        'nsa_gk_win': gain((N_EVEN, NSA_DH)),
        'cmp_pe_k': nrm((N_EVEN, CMP_LEN, NSA_DH), 0.5),
        'cmp_w1_k': lin((N_EVEN, CMP_LEN, NSA_DH, CMP_HID), CMP_LEN * NSA_DH),
        'cmp_w2_k': lin((N_EVEN, CMP_HID, NSA_DH), CMP_HID),
        'cmp_pe_v': nrm((N_EVEN, CMP_LEN, NSA_DH), 0.5),
        'cmp_w1_v': lin((N_EVEN, CMP_LEN, NSA_DH, CMP_HID), CMP_LEN * NSA_DH),
        'cmp_w2_v': lin((N_EVEN, CMP_HID, NSA_DH), CMP_HID),
        'nsa_gate_b': nrm((N_EVEN, NSA_H, 3), 0.1),
        'od_w_in': lin((N_ODD, D_MODEL, OD_IN), D_MODEL),
        'mla_g_cq': gain((N_ODD, Q_LORA)),
        'mla_w_uq': lin((N_ODD, Q_LORA, MLA_H * (MLA_NOPE + MLA_ROPE)), Q_LORA),
        'mla_g_q': gain((N_ODD, MLA_NOPE + MLA_ROPE)),
        'mla_g_ckv': gain((N_ODD, KV_LORA)),
        'mla_g_kr': gain((N_ODD, MLA_ROPE)),
        'mla_w_uk': lin((N_ODD, KV_LORA, MLA_H, MLA_NOPE), KV_LORA),
        'mla_w_uv': lin((N_ODD, KV_LORA, MLA_H, MLA_V), KV_LORA),
        'od_w_out': lin((N_ODD, MLA_H * MLA_V, D_MODEL), MLA_H * MLA_V),
    }


def reference(x_prompt, x_sample, mem_prompt,
              state_ml_C, state_ml_n, state_ml_m, state_ml_conv,
              cache_cmp_k, cache_cmp_v, cache_slc_k, cache_slc_v, cache_win_k, cache_win_v,
              cache_mla_ckv, cache_mla_krope, cache_mem_k, cache_mem_v, page_table,
              rel_bias, ffn1_norm, ffn1_wg, ffn1_wu, ffn1_wd, mix_norm,
              xm_norm, xm_mem_norm, xm_wq, xm_wk, xm_wv, xm_wo, xm_gq, xm_gk,
              ffn2_norm, ffn2_wg, ffn2_wu, ffn2_wd,
              ev_w_in, ev_w_out, ml_conv_w, ml_conv_b, ml_wq, ml_wk, ml_b_i, ml_b_f, ml_out_g,
              nsa_gq, nsa_gk_cmp, nsa_gk_slc, nsa_gk_win, cmp_pe_k, cmp_w1_k, cmp_w2_k,
              cmp_pe_v, cmp_w1_v, cmp_w2_v, nsa_gate_b,
              od_w_in, mla_g_cq, mla_w_uq, mla_g_q, mla_g_ckv, mla_g_kr, mla_w_uk, mla_w_uv, od_w_out):
    past_len = page_table.shape[1] * PAGE_SIZE
    yp, ys = x_prompt, x_sample
    ev_p, ev_s, od_p, od_s, memk_p, memv_p = [], [], [], [], [], []
    for layer in range(DEPTH):
        yp = swiglu_half(yp, ffn1_norm[layer], ffn1_wg[layer], ffn1_wu[layer], ffn1_wd[layer])
        ys = swiglu_half(ys, ffn1_norm[layer], ffn1_wg[layer], ffn1_wu[layer], ffn1_wd[layer])
        if layer % 2 == 0:
            e = layer // 2
            ew = dict(w_in=ev_w_in[e], w_out=ev_w_out[e], conv_w=ml_conv_w[e], conv_b=ml_conv_b[e],
                      ml_wq=ml_wq[e], ml_wk=ml_wk[e], ml_b_i=ml_b_i[e], ml_b_f=ml_b_f[e], ml_out_g=ml_out_g[e],
                      nsa_gq=nsa_gq[e], nsa_gk_cmp=nsa_gk_cmp[e], nsa_gk_slc=nsa_gk_slc[e], nsa_gk_win=nsa_gk_win[e],
                      pe_k=cmp_pe_k[e], w1_k=cmp_w1_k[e], w2_k=cmp_w2_k[e],
                      pe_v=cmp_pe_v[e], w1_v=cmp_w1_v[e], w2_v=cmp_w2_v[e],
                      gate_b=nsa_gate_b[e], rel_bias=rel_bias)
            past = dict(e=e, page_table=page_table, C=state_ml_C[e], n=state_ml_n[e], m=state_ml_m[e],
                        conv=state_ml_conv[e], cmp_k=cache_cmp_k, cmp_v=cache_cmp_v,
                        slc_k=cache_slc_k, slc_v=cache_slc_v, win_k=cache_win_k[e], win_v=cache_win_v[e])
            hp, st_p = even_mixer(rms_norm(yp, mix_norm[layer]), 0, None, **ew)
            hs, st_s = even_mixer(rms_norm(ys, mix_norm[layer]), past_len, past, **ew)
            ev_p.append(st_p)
            ev_s.append(st_s)
        else:
            o = layer // 2
            ow = dict(w_in=od_w_in[o], g_cq=mla_g_cq[o], w_uq=mla_w_uq[o], g_q=mla_g_q[o], g_ckv=mla_g_ckv[o],
                      g_kr=mla_g_kr[o], w_uk=mla_w_uk[o], w_uv=mla_w_uv[o], w_out=od_w_out[o])
            past = dict(e=o, page_table=page_table, ckv=cache_mla_ckv, krope=cache_mla_krope)
            hp, st_p = odd_mixer(rms_norm(yp, mix_norm[layer]), 0, None, **ow)
            hs, st_s = odd_mixer(rms_norm(ys, mix_norm[layer]), past_len, past, **ow)
            od_p.append(st_p)
            od_s.append(st_s)
        yp = yp + hp
        ys = ys + hs
        mk, mv = mem_kv(mem_prompt, xm_mem_norm[layer], xm_wk[layer], xm_wv[layer], xm_gk[layer])
        memk_p.append(mk)
        memv_p.append(mv)
        yp = yp + mem_attend(rms_norm(yp, xm_norm[layer]), mk, mv, xm_wq[layer], xm_gq[layer], xm_wo[layer])
        ys = ys + mem_attend(rms_norm(ys, xm_norm[layer]), cache_mem_k[layer], cache_mem_v[layer],
                             xm_wq[layer], xm_gq[layer], xm_wo[layer])
        yp = swiglu_half(yp, ffn2_norm[layer], ffn2_wg[layer], ffn2_wu[layer], ffn2_wd[layer])
        ys = swiglu_half(ys, ffn2_norm[layer], ffn2_wg[layer], ffn2_wu[layer], ffn2_wd[layer])
    return (yp, ys,
            stack_key(ev_p, 'C'), stack_key(ev_p, 'n'), stack_key(ev_p, 'm'), stack_key(ev_p, 'conv'),
            stack_key(ev_p, 'cmp_k'), stack_key(ev_p, 'cmp_v'), stack_key(ev_p, 'slc_k'), stack_key(ev_p, 'slc_v'),
            stack_key(ev_p, 'win_k'), stack_key(ev_p, 'win_v'),
            stack_key(od_p, 'ckv'), stack_key(od_p, 'krope'),
            jnp.stack(memk_p), jnp.stack(memv_p),
            stack_key(ev_s, 'C'), stack_key(ev_s, 'n'), stack_key(ev_s, 'm'), stack_key(ev_s, 'conv'),
            stack_key(ev_s, 'cmp_k'), stack_key(ev_s, 'cmp_v'), stack_key(ev_s, 'slc_k'), stack_key(ev_s, 'slc_v'),
            stack_key(ev_s, 'win_k'), stack_key(ev_s, 'win_v'),
            stack_key(od_s, 'ckv'), stack_key(od_s, 'krope'))
```

```python
import functools
import math

import jax
import jax.numpy as jnp
import numpy as np
from jax import lax
from jax.experimental import pallas as pl
from jax.experimental.pallas import tpu as pltpu

D_MODEL = 1024
DEPTH = 2
PAGE_SIZE = 128
EPS = 1e-6
NEG_INF = -1e30
D_FF = 2816
ML_H = 4
ML_DH = 128
ML_W = ML_H * ML_DH
CONV_K = 4
MLSTM_CHUNK = 64
NSA_H = 8
NSA_KV = 2
NSA_R = NSA_H // NSA_KV
NSA_DH = 64
NSA_W = NSA_H * NSA_DH
NSA_KVW = NSA_KV * NSA_DH
NSA_SCALE = NSA_DH ** -0.5
CMP_STRIDE = 16
CMP_LEN = 2 * CMP_STRIDE
CMP_HID = 2 * NSA_DH
SLC_BLOCK = 64
SLC_RATIO = SLC_BLOCK // CMP_STRIDE
SLC_OVERLAP_W = (1.0, 2.0, 2.0, 2.0, 1.0)
N_SELECT = 16
WINDOW = 512
SLC_QBLOCK = 32
WIN_QBLOCK = 128
FORCE_SCORE = 1e6
REL_BUCKETS = 32
REL_MAX_DIST = 128
MLA_H = 16
MLA_NOPE = 64
MLA_ROPE = 32
MLA_V = 64
Q_LORA = 384
KV_LORA = 256
MLA_SCALE = (MLA_NOPE + MLA_ROPE) ** -0.5
MLA_QBLOCK = 128
ROPE_THETA = 10000.0
MEM_LEN = 256
XM_H = 4
XM_DH = 128
XM_W = XM_H * XM_DH
EV_SPLITS = (ML_W, ML_W, ML_W, ML_H, ML_H, NSA_W) + (NSA_KVW,) * 6 + (NSA_H * 3,)
OD_SPLITS = (Q_LORA, KV_LORA, MLA_ROPE)

V7X_LANES = 128
V7X_VMEM_LIMIT_BYTES = 56 * 1024 * 1024
FFN_CHUNK = 256
FFN_ROWS = 512


def _ffn_kernel(x_ref, g_ref, wg_ref, wu_ref, wd_ref, o_ref, acc_ref):
    x = x_ref[...]
    h = x * lax.rsqrt(jnp.mean(x * x, axis=-1, keepdims=True) + EPS) * g_ref[...]
    hb = h.astype(jnp.bfloat16)
    acc_ref[...] = jnp.zeros_like(acc_ref)

    def body(c, carry):
        cols = pl.ds(pl.multiple_of(c * FFN_CHUNK, FFN_CHUNK), FFN_CHUNK)
        gate = jnp.dot(hb, wg_ref[:, cols], preferred_element_type=jnp.float32)
        up = jnp.dot(hb, wu_ref[:, cols], preferred_element_type=jnp.float32)
        act = (gate * jax.nn.sigmoid(gate) * up).astype(jnp.bfloat16)
        acc_ref[...] += jnp.dot(act, wd_ref[cols, :], preferred_element_type=jnp.float32)
        return carry

    lax.fori_loop(0, D_FF // FFN_CHUNK, body, 0)
    o_ref[...] = x + 0.5 * acc_ref[...]


def ffn_half(x2d, g, wg, wu, wd):
    m = x2d.shape[0]
    tm = min(FFN_ROWS, m)
    assert m % tm == 0
    resident = functools.partial(pl.BlockSpec, pipeline_mode=pl.Buffered(1))
    return pl.pallas_call(
        _ffn_kernel,
        grid=(m // tm,),
        in_specs=[
            pl.BlockSpec((tm, D_MODEL), lambda i: (i, 0)),
            resident((1, D_MODEL), lambda i: (0, 0)),
            resident((D_MODEL, D_FF), lambda i: (0, 0)),
            resident((D_MODEL, D_FF), lambda i: (0, 0)),
            resident((D_FF, D_MODEL), lambda i: (0, 0)),
        ],
        out_specs=pl.BlockSpec((tm, D_MODEL), lambda i: (i, 0)),
        out_shape=jax.ShapeDtypeStruct((m, D_MODEL), jnp.float32),
        scratch_shapes=[pltpu.VMEM((tm, D_MODEL), jnp.float32)],
        compiler_params=pltpu.CompilerParams(
            dimension_semantics=("arbitrary",), vmem_limit_bytes=V7X_VMEM_LIMIT_BYTES),
        name="ffn_half",
    )(x2d, g.reshape(1, D_MODEL), wg, wu, wd)


def swiglu_half(x, g, wg, wu, wd):
    shp = x.shape
    return ffn_half(x.reshape(-1, D_MODEL), g, wg, wu, wd).reshape(shp)


def split_cols(a, sizes):
    idx = [int(s) for s in np.cumsum(sizes)[:-1]]
    return jnp.split(a, idx, axis=-1)


def rms_norm(x, g):
    xf = x.astype(jnp.float32)
    y = xf * lax.rsqrt(jnp.mean(xf * xf, axis=-1, keepdims=True) + EPS)
    return (y * g.astype(jnp.float32)).astype(x.dtype)


def t5_bucket(dist):
    n = jnp.maximum(dist, 0)
    exact = REL_BUCKETS // 2
    nf = jnp.maximum(n, exact).astype(jnp.float32)
    large = exact + (jnp.log(nf / exact) / math.log(REL_MAX_DIST / exact) * (REL_BUCKETS - exact)).astype(jnp.int32)
    return jnp.where(n < exact, n, jnp.minimum(large, REL_BUCKETS - 1))


def apply_rope(x, pos):
    half = x.shape[-1] // 2
    inv = ROPE_THETA ** (-jnp.arange(half, dtype=jnp.float32) / half)
    ang = pos.astype(jnp.float32)[:, None] * inv[None, :]
    ang = ang.reshape(ang.shape[:1] + (1,) * (x.ndim - 3) + (half,))
    cos, sin = jnp.cos(ang).astype(x.dtype), jnp.sin(ang).astype(x.dtype)
    x1, x2 = x[..., :half], x[..., half:]
    return jnp.concatenate([x1 * cos - x2 * sin, x1 * sin + x2 * cos], axis=-1)


def causal_conv(u, buf, w, b):
    S = u.shape[1]
    full = jnp.concatenate([buf.astype(u.dtype), u], axis=1)
    out = b + sum(full[:, j:j + S] * w[j] for j in range(CONV_K))
    return out, full[:, S:]


def gather_pages(pool, e, page_table):
    g = pool[e, page_table]
    return g.reshape((g.shape[0], g.shape[1] * g.shape[2]) + g.shape[3:])


def mlstm_chunkwise(q, k, v, i_pre, logf, C0, n0, m0):
    f32 = jnp.float32
    q, k, v, i_pre, logf = (a.astype(f32) for a in (q, k, v, i_pre, logf))
    B, H, S, D = q.shape
    L = MLSTM_CHUNK if S % MLSTM_CHUNK == 0 else S
    NC = S // L

    def chunks(a):
        return jnp.moveaxis(a.reshape((B, H, NC, L) + a.shape[3:]), 2, 0)

    causal = jnp.tril(jnp.ones((L, L), dtype=bool))

    def step(carry, inp):
        C, n, m = carry
        qc, kc, vc, ic, fc = inp
        b = jnp.cumsum(fc, axis=-1)
        g = b + m[..., None]
        dmat = jnp.where(causal, b[..., :, None] - b[..., None, :] + ic[..., None, :], -jnp.inf)
        mt = jnp.maximum(g, jnp.max(dmat, axis=-1))
        inter = jnp.exp(g - mt)
        sqk = jnp.einsum('bhtd,bhsd->bhts', qc, kc) * jnp.exp(dmat - mt[..., None])
        num = inter[..., None] * jnp.einsum('bhvd,bhtd->bhtv', C, qc) + jnp.einsum('bhts,bhsv->bhtv', sqk, vc)
        den = inter * jnp.einsum('bhd,bhtd->bht', n, qc) + jnp.sum(sqk, axis=-1)
        h = num / jnp.maximum(jnp.abs(den), jnp.exp(-mt))[..., None]
        b_end = b[..., -1]
        w_log = b_end[..., None] - b + ic
        m_new = jnp.maximum(b_end + m, jnp.max(w_log, axis=-1))
        decay = jnp.exp(b_end + m - m_new)
        w_in = jnp.exp(w_log - m_new[..., None])
        C_new = decay[..., None, None] * C + jnp.einsum('bhs,bhsv,bhsd->bhvd', w_in, vc, kc)
        n_new = decay[..., None] * n + jnp.einsum('bhs,bhsd->bhd', w_in, kc)
        return (C_new, n_new, m_new), h

    (C1, n1, m1), hs = lax.scan(step, (C0.astype(f32), n0.astype(f32), m0.astype(f32)),
                                tuple(chunks(a) for a in (q, k, v, i_pre, logf)))
    return jnp.moveaxis(hs, 0, 2).reshape(B, H, S, D), C1, n1, m1


def gqa_attend(q, k, v, mask, bias):
    s = jnp.einsum('bqgrd,bkgd->bgrqk', q, k).astype(jnp.float32) * NSA_SCALE
    s = s + jnp.transpose(bias, (2, 3, 0, 1)).astype(jnp.float32)
    s = jnp.where(mask, s, NEG_INF)
    p = jax.nn.softmax(s, axis=-1) * mask.astype(jnp.float32)
    o = jnp.einsum('bgrqk,bkgd->bqgrd', p.astype(v.dtype), v)
    return o, p


def slc_core(q, tq, idx, kg, vg, rel_bias):
    s = jnp.einsum('bqgrd,bqgnld->bqgnlr', q, kg).astype(jnp.float32) * NSA_SCALE
    pos = idx[..., None] * SLC_BLOCK + jnp.arange(SLC_BLOCK)
    dist = tq[None, :, None, None, None] - pos
    tbl = jnp.transpose(rel_bias.reshape(REL_BUCKETS, NSA_KV, NSA_R), (1, 0, 2))
    g_i = jnp.arange(NSA_KV)[None, None, :, None, None]
    s = s + tbl[g_i, t5_bucket(dist)].astype(jnp.float32)
    s = jnp.where((dist >= 0)[..., None], s, NEG_INF)
    B, Q, G, N, L, R = s.shape
    p = jax.nn.softmax(s.reshape(B, Q, G, N * L, R), axis=-2).reshape(s.shape)
    return jnp.einsum('bqgnlr,bqgnld->bqgrd', p.astype(vg.dtype), vg)


def to_chunks(a):
    B, T = a.shape[:2]
    pad = (-T) % CMP_STRIDE
    a = jnp.pad(a, ((0, 0), (0, pad), (0, 0), (0, 0)))
    return a.reshape((B, (T + pad) // CMP_STRIDE, CMP_STRIDE) + a.shape[2:])


def cmp_summaries(chunk_list, T, pe, w1, w2):
    lo = jnp.concatenate([jnp.einsum('bcjgd,jdh->bcgh', r, w1[:CMP_STRIDE]) for r in chunk_list], axis=1)
    hi = jnp.concatenate([jnp.einsum('bcjgd,jdh->bcgh', r, w1[CMP_STRIDE:]) for r in chunk_list], axis=1)
    n_cmp = (T - CMP_LEN) // CMP_STRIDE + 1
    hid = jax.nn.silu(lo[:, :n_cmp] + hi[:, 1:n_cmp + 1] + jnp.einsum('jd,jdh->h', pe, w1))
    return hid @ w2


def even_mixer(xn, pos0, past, w_in, w_out, conv_w, conv_b, ml_wq, ml_wk, ml_b_i, ml_b_f, ml_out_g,
               nsa_gq, nsa_gk_cmp, nsa_gk_slc, nsa_gk_win, pe_k, w1_k, w2_k, pe_v, w1_v, w2_v, gate_b, rel_bias):
    B, S, _ = xn.shape
    dt = xn.dtype
    (u, v_m, o_pre, i_pre, f_pre, q, kc, vc, ks, vs, kw, vw, g_pre) = split_cols(xn @ w_in, EV_SPLITS)
    tq = pos0 + jnp.arange(S)

    if past is None:
        conv_buf = jnp.zeros((B, CONV_K - 1, ML_W), dt)
        C0 = jnp.zeros((B, ML_H, ML_DH, ML_DH), jnp.float32)
        n0 = jnp.zeros((B, ML_H, ML_DH), jnp.float32)
        m0 = jnp.zeros((B, ML_H), jnp.float32)
    else:
        conv_buf, C0, n0, m0 = past['conv'], past['C'], past['n'], past['m']
    c, conv_new = causal_conv(u, conv_buf, conv_w, conv_b)
    ch = jax.nn.silu(c).reshape(B, S, ML_H, ML_DH)
    qm = jnp.einsum('bshd,hde->bhse', ch, ml_wq)
    km = jnp.einsum('bshd,hde->bhse', ch, ml_wk) * (ML_DH ** -0.5)
    vm = jnp.transpose(v_m.reshape(B, S, ML_H, ML_DH), (0, 2, 1, 3))
    ig = jnp.transpose(i_pre + ml_b_i, (0, 2, 1))
    lf = jax.nn.log_sigmoid(jnp.transpose(f_pre + ml_b_f, (0, 2, 1)).astype(jnp.float32))
    hm, C1, n1, m1 = mlstm_chunkwise(qm, km, vm, ig, lf, C0, n0, m0)
    hm = rms_norm(jnp.transpose(hm, (0, 2, 1, 3)).astype(dt), ml_out_g) * jax.nn.sigmoid(o_pre).reshape(B, S, ML_H, ML_DH)
    h_a = hm.reshape(B, S, ML_W)

    q = rms_norm(q.reshape(B, S, NSA_H, NSA_DH), nsa_gq)
    qg = q.reshape(B, S, NSA_KV, NSA_R, NSA_DH)
    kv_shape = (B, S, NSA_KV, NSA_DH)
    kc, vc, vs, vw = (a.reshape(kv_shape) for a in (kc, vc, vs, vw))
    ks = rms_norm(ks.reshape(kv_shape), nsa_gk_slc)
    kw = rms_norm(kw.reshape(kv_shape), nsa_gk_win)
    gates = jax.nn.sigmoid(g_pre.reshape(B, S, NSA_H, 3) + gate_b)

    if past is None:
        T = S
        kch, vch = [to_chunks(kc)], [to_chunks(vc)]
    else:
        e, pt = past['e'], past['page_table']
        T = pos0 + S
        kch = [to_chunks(gather_pages(past['cmp_k'], e, pt)), to_chunks(kc)]
        vch = [to_chunks(gather_pages(past['cmp_v'], e, pt)), to_chunks(vc)]
    k_cmp = rms_norm(cmp_summaries(kch, T, pe_k, w1_k, w2_k), nsa_gk_cmp)
    v_cmp = cmp_summaries(vch, T, pe_v, w1_v, w2_v)
    n_cmp = k_cmp.shape[1]
    dist = tq[:, None] - (jnp.arange(n_cmp) * CMP_STRIDE + CMP_LEN - 1)[None, :]
    bias = rel_bias[t5_bucket(dist)].reshape(S, n_cmp, NSA_KV, NSA_R)
    o_cmp, p_cmp = gqa_attend(qg, k_cmp, v_cmp, dist >= 0, bias)

    NS = -(-T // SLC_BLOCK)
    imp = jnp.pad(jnp.sum(p_cmp, axis=2), ((0, 0), (0, 0), (0, 0), (1, SLC_RATIO * NS - n_cmp)))
    p_slc = sum(w * imp[..., k:k + SLC_RATIO * NS:SLC_RATIO] for k, w in enumerate(SLC_OVERLAP_W))
    tb = (tq // SLC_BLOCK)[:, None]
    jb = jnp.arange(NS)[None, :]
    forced = (jb == 0) | (jb == tb) | (jb == tb - 1)
    score = jnp.where(jb <= tb, p_slc + FORCE_SCORE * forced.astype(jnp.float32), -1.0)
    idx = jnp.transpose(lax.top_k(score, min(N_SELECT, NS))[1], (0, 2, 1, 3))
    b_i = jnp.arange(B)[:, None, None, None]
    g_i = jnp.arange(NSA_KV)[None, None, :, None]

    if past is None:
        pad = NS * SLC_BLOCK - S
        ks_b = jnp.pad(ks, ((0, 0), (0, pad), (0, 0), (0, 0))).reshape(B, NS, SLC_BLOCK, NSA_KV, NSA_DH)
        vs_b = jnp.pad(vs, ((0, 0), (0, pad), (0, 0), (0, 0))).reshape(B, NS, SLC_BLOCK, NSA_KV, NSA_DH)
        QB = SLC_QBLOCK if S % SLC_QBLOCK == 0 else S

        def slc_block(i):
            qb = lax.dynamic_slice_in_dim(qg, i * QB, QB, axis=1)
            ib = lax.dynamic_slice_in_dim(idx, i * QB, QB, axis=1)
            tqb = pos0 + i * QB + jnp.arange(QB)
            return slc_core(qb, tqb, ib, ks_b[b_i, ib, :, g_i], vs_b[b_i, ib, :, g_i], rel_bias)

        o_slc = jnp.moveaxis(lax.map(slc_block, jnp.arange(S // QB)), 0, 1).reshape(B, S, NSA_KV, NSA_R, NSA_DH)
    else:
        bpp = PAGE_SIZE // SLC_BLOCK
        nb_past = pos0 // SLC_BLOCK
        jp = jnp.minimum(idx, nb_past - 1)
        page = pt[b_i, jp // bpp]
        rows = (jp % bpp)[..., None] * SLC_BLOCK + jnp.arange(SLC_BLOCK)
        kg_past = past['slc_k'][e, page[..., None], rows, g_i[..., None]]
        vg_past = past['slc_v'][e, page[..., None], rows, g_i[..., None]]
        nbn = -(-S // SLC_BLOCK)
        padn = nbn * SLC_BLOCK - S
        ks_nb = jnp.pad(ks, ((0, 0), (0, padn), (0, 0), (0, 0))).reshape(B, nbn, SLC_BLOCK, NSA_KV, NSA_DH)
        vs_nb = jnp.pad(vs, ((0, 0), (0, padn), (0, 0), (0, 0))).reshape(B, nbn, SLC_BLOCK, NSA_KV, NSA_DH)
        jn = jnp.clip(idx - nb_past, 0, nbn - 1)
        is_past = (idx < nb_past)[..., None, None]
        kg = jnp.where(is_past, kg_past, ks_nb[b_i, jn, :, g_i])
        vg = jnp.where(is_past, vg_past, vs_nb[b_i, jn, :, g_i])
        o_slc = slc_core(qg, tq, idx, kg, vg, rel_bias)

    if past is None:
        QB = WIN_QBLOCK if S % WIN_QBLOCK == 0 else S
        span = QB + WINDOW
        kw_p = jnp.pad(kw, ((0, 0), (WINDOW, 0), (0, 0), (0, 0)))
        vw_p = jnp.pad(vw, ((0, 0), (WINDOW, 0), (0, 0), (0, 0)))

        def win_block(i):
            start = i * QB
            qb = lax.dynamic_slice_in_dim(qg, start, QB, axis=1)
            tqb = pos0 + start + jnp.arange(QB)
            kpos = pos0 + start - WINDOW + jnp.arange(span)
            d = tqb[:, None] - kpos[None, :]
            mask = (d >= 0) & (d <= WINDOW) & (kpos >= 0)[None, :]
            bw = rel_bias[t5_bucket(d)].reshape(QB, span, NSA_KV, NSA_R)
            kb = lax.dynamic_slice_in_dim(kw_p, start, span, axis=1)
            vb = lax.dynamic_slice_in_dim(vw_p, start, span, axis=1)
            return gqa_attend(qb, kb, vb, mask, bw)[0]

        o_win = jnp.moveaxis(lax.map(win_block, jnp.arange(S // QB)), 0, 1).reshape(B, S, NSA_KV, NSA_R, NSA_DH)
        nb = min(WINDOW, S)
        win_k_new, win_v_new = kw[:, S - nb:], vw[:, S - nb:]
    else:
        nbuf = past['win_k'].shape[1]
        k_all = jnp.concatenate([past['win_k'].astype(dt), kw], axis=1)
        v_all = jnp.concatenate([past['win_v'].astype(dt), vw], axis=1)
        kpos = jnp.concatenate([pos0 - nbuf + jnp.arange(nbuf), tq])
        d = tq[:, None] - kpos[None, :]
        mask = (d >= 0) & (d <= WINDOW)
        bw = rel_bias[t5_bucket(d)].reshape(S, nbuf + S, NSA_KV, NSA_R)
        o_win = gqa_attend(qg, k_all, v_all, mask, bw)[0]
        win_k_new, win_v_new = k_all[:, S:], v_all[:, S:]

    hs = (NSA_H, NSA_DH)
    o_b = (gates[..., 0:1] * o_cmp.reshape((B, S) + hs) + gates[..., 1:2] * o_slc.reshape((B, S) + hs)
           + gates[..., 2:3] * o_win.reshape((B, S) + hs))
    h_b = o_b.reshape(B, S, NSA_W)

    out = jnp.concatenate([h_a, h_b], axis=-1) @ w_out
    new = dict(C=C1.astype(dt), n=n1.astype(dt), m=m1.astype(dt), conv=conv_new,
               cmp_k=kc, cmp_v=vc, slc_k=ks, slc_v=vs, win_k=win_k_new, win_v=win_v_new)
    return out, new


def odd_mixer(xn, pos0, past, w_in, g_cq, w_uq, g_q, g_ckv, g_kr, w_uk, w_uv, w_out):
    B, S, _ = xn.shape
    cq, ckv, kr = split_cols(xn @ w_in, OD_SPLITS)
    tq = pos0 + jnp.arange(S)
    q = rms_norm((rms_norm(cq, g_cq) @ w_uq).reshape(B, S, MLA_H, MLA_NOPE + MLA_ROPE), g_q)
    q_nope = q[..., :MLA_NOPE]
    q_rope = apply_rope(q[..., MLA_NOPE:], tq)
    ckv = rms_norm(ckv, g_ckv)
    kr = apply_rope(rms_norm(kr, g_kr), tq)
    if past is None:
        k_nope = jnp.einsum('bsc,chn->bshn', ckv, w_uk)
        v = jnp.einsum('bsc,chv->bshv', ckv, w_uv)
        QB = MLA_QBLOCK if S % MLA_QBLOCK == 0 else S

        def blk(i):
            start = i * QB
            qn = lax.dynamic_slice_in_dim(q_nope, start, QB, axis=1)
            qr = lax.dynamic_slice_in_dim(q_rope, start, QB, axis=1)
            tqb = pos0 + start + jnp.arange(QB)
            s = (jnp.einsum('bqhn,bkhn->bhqk', qn, k_nope) + jnp.einsum('bqhr,bkr->bhqk', qr, kr)).astype(jnp.float32)
            s = jnp.where(tq[None, :] <= tqb[:, None], s * MLA_SCALE, NEG_INF)
            p = jax.nn.softmax(s, axis=-1)
            return jnp.einsum('bhqk,bkhv->bqhv', p.astype(v.dtype), v)

        o = jnp.moveaxis(lax.map(blk, jnp.arange(S // QB)), 0, 1).reshape(B, S, MLA_H * MLA_V)
    else:
        e, pt = past['e'], past['page_table']
        ckv_past = gather_pages(past['ckv'], e, pt).astype(ckv.dtype)
        kr_past = gather_pages(past['krope'], e, pt).astype(kr.dtype)
        P = ckv_past.shape[1]
        q_lat = jnp.einsum('bqhn,chn->bqhc', q_nope, w_uk)
        s_past = jnp.einsum('bqhc,btc->bhqt', q_lat, ckv_past) + jnp.einsum('bqhr,btr->bhqt', q_rope, kr_past)
        s_new = jnp.einsum('bqhc,bkc->bhqk', q_lat, ckv) + jnp.einsum('bqhr,bkr->bhqk', q_rope, kr)
        s_new = jnp.where(jnp.tril(jnp.ones((S, S), dtype=bool)), s_new.astype(jnp.float32) * MLA_SCALE, NEG_INF)
        s = jnp.concatenate([s_past.astype(jnp.float32) * MLA_SCALE, s_new], axis=-1)
        p = jax.nn.softmax(s, axis=-1).astype(ckv.dtype)
        o_lat = jnp.einsum('bhqt,btc->bqhc', p[..., :P], ckv_past) + jnp.einsum('bhqk,bkc->bqhc', p[..., P:], ckv)
        o = jnp.einsum('bqhc,chv->bqhv', o_lat, w_uv).reshape(B, S, MLA_H * MLA_V)
    return o @ w_out, dict(ckv=ckv, krope=kr)


def mem_kv(mem, g_mem, wk, wv, gk):
    B, M, _ = mem.shape
    m = rms_norm(mem, g_mem)
    k = rms_norm((m @ wk).reshape(B, M, XM_H, XM_DH), gk)
    v = (m @ wv).reshape(B, M, XM_H, XM_DH)
    return k, v


def mem_attend(xn, k, v, wq, gq, wo):
    B, S, _ = xn.shape
    q = rms_norm((xn @ wq).reshape(B, S, XM_H, XM_DH), gq)
    s = jnp.einsum('bshd,bmhd->bhsm', q, k.astype(q.dtype)).astype(jnp.float32) * (XM_DH ** -0.5)
    p = jax.nn.softmax(s, axis=-1)
    return jnp.einsum('bhsm,bmhd->bshd', p.astype(xn.dtype), v.astype(xn.dtype)).reshape(B, S, XM_W) @ wo


def stack_key(lst, name):
    return jnp.stack([d[name] for d in lst])


def kernel(x_prompt, x_sample, mem_prompt,
           state_ml_C, state_ml_n, state_ml_m, state_ml_conv,
           cache_cmp_k, cache_cmp_v, cache_slc_k, cache_slc_v, cache_win_k, cache_win_v,
           cache_mla_ckv, cache_mla_krope, cache_mem_k, cache_mem_v, page_table,
           rel_bias, ffn1_norm, ffn1_wg, ffn1_wu, ffn1_wd, mix_norm,
           xm_norm, xm_mem_norm, xm_wq, xm_wk, xm_wv, xm_wo, xm_gq, xm_gk,
           ffn2_norm, ffn2_wg, ffn2_wu, ffn2_wd,
           ev_w_in, ev_w_out, ml_conv_w, ml_conv_b, ml_wq, ml_wk, ml_b_i, ml_b_f, ml_out_g,
           nsa_gq, nsa_gk_cmp, nsa_gk_slc, nsa_gk_win, cmp_pe_k, cmp_w1_k, cmp_w2_k,
           cmp_pe_v, cmp_w1_v, cmp_w2_v, nsa_gate_b,
           od_w_in, mla_g_cq, mla_w_uq, mla_g_q, mla_g_ckv, mla_g_kr, mla_w_uk, mla_w_uv, od_w_out):
    past_len = page_table.shape[1] * PAGE_SIZE
    bf = jnp.bfloat16
    yp, ys = x_prompt, x_sample
    ev_p, ev_s, od_p, od_s, memk_p, memv_p = [], [], [], [], [], []
    for layer in range(DEPTH):
        f1 = (ffn1_norm[layer], ffn1_wg[layer].astype(bf), ffn1_wu[layer].astype(bf), ffn1_wd[layer].astype(bf))
        yp = swiglu_half(yp, *f1)
        ys = swiglu_half(ys, *f1)
        if layer % 2 == 0:
            e = layer // 2
            ew = dict(w_in=ev_w_in[e], w_out=ev_w_out[e], conv_w=ml_conv_w[e], conv_b=ml_conv_b[e],
                      ml_wq=ml_wq[e], ml_wk=ml_wk[e], ml_b_i=ml_b_i[e], ml_b_f=ml_b_f[e], ml_out_g=ml_out_g[e],
                      nsa_gq=nsa_gq[e], nsa_gk_cmp=nsa_gk_cmp[e], nsa_gk_slc=nsa_gk_slc[e], nsa_gk_win=nsa_gk_win[e],
                      pe_k=cmp_pe_k[e], w1_k=cmp_w1_k[e], w2_k=cmp_w2_k[e],
                      pe_v=cmp_pe_v[e], w1_v=cmp_w1_v[e], w2_v=cmp_w2_v[e],
                      gate_b=nsa_gate_b[e], rel_bias=rel_bias)
            past = dict(e=e, page_table=page_table, C=state_ml_C[e], n=state_ml_n[e], m=state_ml_m[e],
                        conv=state_ml_conv[e], cmp_k=cache_cmp_k, cmp_v=cache_cmp_v,
                        slc_k=cache_slc_k, slc_v=cache_slc_v, win_k=cache_win_k[e], win_v=cache_win_v[e])
            hp, st_p = even_mixer(rms_norm(yp, mix_norm[layer]), 0, None, **ew)
            hs, st_s = even_mixer(rms_norm(ys, mix_norm[layer]), past_len, past, **ew)
            ev_p.append(st_p)
            ev_s.append(st_s)
        else:
            o = layer // 2
            ow = dict(w_in=od_w_in[o], g_cq=mla_g_cq[o], w_uq=mla_w_uq[o], g_q=mla_g_q[o], g_ckv=mla_g_ckv[o],
                      g_kr=mla_g_kr[o], w_uk=mla_w_uk[o], w_uv=mla_w_uv[o], w_out=od_w_out[o])
            past = dict(e=o, page_table=page_table, ckv=cache_mla_ckv, krope=cache_mla_krope)
            hp, st_p = odd_mixer(rms_norm(yp, mix_norm[layer]), 0, None, **ow)
            hs, st_s = odd_mixer(rms_norm(ys, mix_norm[layer]), past_len, past, **ow)
            od_p.append(st_p)
            od_s.append(st_s)
        yp = yp + hp
        ys = ys + hs
        mk, mv = mem_kv(mem_prompt, xm_mem_norm[layer], xm_wk[layer], xm_wv[layer], xm_gk[layer])
        memk_p.append(mk)
        memv_p.append(mv)
        yp = yp + mem_attend(rms_norm(yp, xm_norm[layer]), mk, mv, xm_wq[layer], xm_gq[layer], xm_wo[layer])
        ys = ys + mem_attend(rms_norm(ys, xm_norm[layer]), cache_mem_k[layer], cache_mem_v[layer],
                             xm_wq[layer], xm_gq[layer], xm_wo[layer])
        f2 = (ffn2_norm[layer], ffn2_wg[layer].astype(bf), ffn2_wu[layer].astype(bf), ffn2_wd[layer].astype(bf))
        yp = swiglu_half(yp, *f2)
        ys = swiglu_half(ys, *f2)
    return (yp, ys,
            stack_key(ev_p, 'C'), stack_key(ev_p, 'n'), stack_key(ev_p, 'm'), stack_key(ev_p, 'conv'),
            stack_key(ev_p, 'cmp_k'), stack_key(ev_p, 'cmp_v'), stack_key(ev_p, 'slc_k'), stack_key(ev_p, 'slc_v'),
            stack_key(ev_p, 'win_k'), stack_key(ev_p, 'win_v'),
            stack_key(od_p, 'ckv'), stack_key(od_p, 'krope'),
            jnp.stack(memk_p), jnp.stack(memv_p),
            stack_key(ev_s, 'C'), stack_key(ev_s, 'n'), stack_key(ev_s, 'm'), stack_key(ev_s, 'conv'),
            stack_key(ev_s, 'cmp_k'), stack_key(ev_s, 'cmp_v'), stack_key(ev_s, 'slc_k'), stack_key(ev_s, 'slc_v'),
            stack_key(ev_s, 'win_k'), stack_key(ev_s, 'win_v'),
            stack_key(od_s, 'ckv'), stack_key(od_s, 'krope'))
```

```python
import functools
import math

import jax
import jax.numpy as jnp
import numpy as np
from jax import lax
from jax.experimental import pallas as pl
from jax.experimental.pallas import tpu as pltpu

D_MODEL = 1024
DEPTH = 2
PAGE_SIZE = 128
EPS = 1e-6
NEG_INF = -1e30
D_FF = 2816
ML_H = 4
ML_DH = 128
ML_W = ML_H * ML_DH
CONV_K = 4
MLSTM_CHUNK = 64
NSA_H = 8
NSA_KV = 2
NSA_R = NSA_H // NSA_KV
NSA_DH = 64
NSA_W = NSA_H * NSA_DH
NSA_KVW = NSA_KV * NSA_DH
NSA_SCALE = NSA_DH ** -0.5
CMP_STRIDE = 16
CMP_LEN = 2 * CMP_STRIDE
CMP_HID = 2 * NSA_DH
SLC_BLOCK = 64
SLC_RATIO = SLC_BLOCK // CMP_STRIDE
SLC_OVERLAP_W = (1.0, 2.0, 2.0, 2.0, 1.0)
N_SELECT = 16
WINDOW = 512
SLC_QBLOCK = 32
WIN_QBLOCK = 128
FORCE_SCORE = 1e6
REL_BUCKETS = 32
REL_MAX_DIST = 128
MLA_H = 16
MLA_NOPE = 64
MLA_ROPE = 32
MLA_V = 64
Q_LORA = 384
KV_LORA = 256
MLA_SCALE = (MLA_NOPE + MLA_ROPE) ** -0.5
MLA_QBLOCK = 128
ROPE_THETA = 10000.0
MEM_LEN = 256
XM_H = 4
XM_DH = 128
XM_W = XM_H * XM_DH
EV_SPLITS = (ML_W, ML_W, ML_W, ML_H, ML_H, NSA_W) + (NSA_KVW,) * 6 + (NSA_H * 3,)
OD_SPLITS = (Q_LORA, KV_LORA, MLA_ROPE)

V7X_LANES = 128
V7X_VMEM_LIMIT_BYTES = 56 * 1024 * 1024
FFN_CHUNK = 256
FFN_ROWS = 512


def _ffn_kernel(x_ref, g_ref, wg_ref, wu_ref, wd_ref, o_ref, acc_ref):
    x = x_ref[...]
    h = x * lax.rsqrt(jnp.mean(x * x, axis=-1, keepdims=True) + EPS) * g_ref[...]
    hb = h.astype(jnp.bfloat16)
    acc_ref[...] = jnp.zeros_like(acc_ref)

    def body(c, carry):
        cols = pl.ds(pl.multiple_of(c * FFN_CHUNK, FFN_CHUNK), FFN_CHUNK)
        gate = jnp.dot(hb, wg_ref[:, cols], preferred_element_type=jnp.float32)
        up = jnp.dot(hb, wu_ref[:, cols], preferred_element_type=jnp.float32)
        act = (gate * jax.nn.sigmoid(gate) * up).astype(jnp.bfloat16)
        acc_ref[...] += jnp.dot(act, wd_ref[cols, :], preferred_element_type=jnp.float32)
        return carry

    lax.fori_loop(0, D_FF // FFN_CHUNK, body, 0)
    o_ref[...] = x + 0.5 * acc_ref[...]


def ffn_half(x2d, g, wg, wu, wd):
    m = x2d.shape[0]
    tm = min(FFN_ROWS, m)
    assert m % tm == 0
    resident = functools.partial(pl.BlockSpec, pipeline_mode=pl.Buffered(1))
    return pl.pallas_call(
        _ffn_kernel,
        grid=(m // tm,),
        in_specs=[
            pl.BlockSpec((tm, D_MODEL), lambda i: (i, 0)),
            resident((1, D_MODEL), lambda i: (0, 0)),
            resident((D_MODEL, D_FF), lambda i: (0, 0)),
            resident((D_MODEL, D_FF), lambda i: (0, 0)),
            resident((D_FF, D_MODEL), lambda i: (0, 0)),
        ],
        out_specs=pl.BlockSpec((tm, D_MODEL), lambda i: (i, 0)),
        out_shape=jax.ShapeDtypeStruct((m, D_MODEL), jnp.float32),
        scratch_shapes=[pltpu.VMEM((tm, D_MODEL), jnp.float32)],
        compiler_params=pltpu.CompilerParams(
            dimension_semantics=("arbitrary",), vmem_limit_bytes=V7X_VMEM_LIMIT_BYTES),
        name="ffn_half",
    )(x2d, g.reshape(1, D_MODEL), wg, wu, wd)


def swiglu_half(x, g, wg, wu, wd):
    shp = x.shape
    return ffn_half(x.reshape(-1, D_MODEL), g, wg, wu, wd).reshape(shp)


NSA_TQ = 128


def _nsa_prompt_kernel(q_ref, kc_ref, vc_ref, ks_ref, vs_ref, kw_ref, vw_ref, bc_ref, bn_ref, bfar_ref,
                       e_ref, mw_ref, gp_ref, gb_ref, o_ref, sel_ref, m_ref, l_ref, acc_ref, *, n_cmp, n_select):
    f32, bf16 = jnp.float32, jnp.bfloat16
    qi = pl.program_id(2)
    R, TQ, DH = q_ref.shape[2:]
    TK = TQ
    t0 = qi * TQ
    q2 = q_ref[0, 0].reshape(R * TQ, DH)
    t_col = t0 + lax.broadcasted_iota(jnp.int32, (TQ, 1), 0)

    def scores(k):
        s = lax.dot_general(q2, k, (((1,), (1,)), ((), ())), preferred_element_type=f32)
        return s.reshape(R, TQ, k.shape[0])

    ncp = kc_ref.shape[2]
    c_row = lax.broadcasted_iota(jnp.int32, (1, ncp), 1)
    mask_c = (t_col >= c_row * CMP_STRIDE + (CMP_LEN - 1)) & (c_row < n_cmp)
    s = jnp.where(mask_c[None], scores(kc_ref[0, 0]) + bc_ref[0], NEG_INF)
    e = jnp.exp(s - jnp.max(s, axis=-1, keepdims=True))
    p = e / jnp.sum(e, axis=-1, keepdims=True) * mask_c.astype(f32)[None]
    o_cmp = jnp.dot(p.reshape(R * TQ, ncp).astype(bf16), vc_ref[0, 0],
                    preferred_element_type=f32).reshape(R, TQ, DH)
    imp = jnp.sum(p, axis=0)

    mw = mw_ref[...]
    hi = imp.astype(bf16)
    r1 = imp - hi.astype(f32)
    mid = r1.astype(bf16)
    lo = (r1 - mid.astype(f32)).astype(bf16)
    p_slc = (jnp.dot(hi, mw, preferred_element_type=f32) + jnp.dot(mid, mw, preferred_element_type=f32)
             + jnp.dot(lo, mw, preferred_element_type=f32))
    ns = mw_ref.shape[1]
    jb = lax.broadcasted_iota(jnp.int32, (1, ns), 1)
    tb = jnp.right_shift(t_col, int(math.log2(SLC_BLOCK)))
    forced = (jb == 0) | (jb == tb) | (jb == tb - 1)
    score = jnp.where(jb <= tb, p_slc + FORCE_SCORE * forced.astype(f32), -1.0)
    rank = jnp.zeros((TQ, ns), f32)
    for j in range(ns):
        col = score[:, j:j + 1]
        rank = rank + ((col > score) | ((col == score) & (jb > j))).astype(f32)
    sel = (rank < n_select).astype(bf16)
    sel_tok = jnp.dot(sel, e_ref[...], preferred_element_type=f32)
    for kt in range(sel_ref.shape[0]):
        sel_ref[kt] = sel_tok[:, kt * TK:(kt + 1) * TK]

    def reset():
        m_ref[...] = jnp.full(m_ref.shape, NEG_INF, f32)
        l_ref[...] = jnp.zeros(l_ref.shape, f32)
        acc_ref[...] = jnp.zeros(acc_ref.shape, f32)

    def attend(k_ref, v_ref, kt, bias, selected):
        start = pl.multiple_of(kt * TK, TK)
        k = k_ref[0, 0, pl.ds(start, TK), :]
        v = v_ref[0, 0, pl.ds(start, TK), :]
        d = t_col - (start + lax.broadcasted_iota(jnp.int32, (1, TK), 1))
        if selected:
            msk = (d >= 0) & (sel_ref[kt] > 0.5)
        else:
            msk = (d >= 0) & (d <= WINDOW)
        s = jnp.where(msk[None], scores(k) + bias, NEG_INF)
        m_prev = m_ref[...]
        m_new = jnp.maximum(m_prev, jnp.max(s, axis=-1, keepdims=True))
        alpha = jnp.exp(m_prev - m_new)
        pe = jnp.where(msk[None], jnp.exp(s - m_new), 0.0)
        l_ref[...] = alpha * l_ref[...] + jnp.sum(pe, axis=-1, keepdims=True)
        pv = jnp.dot(pe.reshape(R * TQ, TK).astype(bf16), v, preferred_element_type=f32)
        acc_ref[...] = alpha * acc_ref[...] + pv.reshape(R, TQ, DH)
        m_ref[...] = m_new

    def branch(k_ref, v_ref, first_far, selected):
        reset()
        attend(k_ref, v_ref, qi, bn_ref[0, :, :, TK:], selected)

        @pl.when(qi >= 1)
        def _():
            attend(k_ref, v_ref, qi - 1, bn_ref[0, :, :, :TK], selected)

        def far(kt, carry):
            attend(k_ref, v_ref, kt, bfar_ref[0], selected)
            return carry

        lax.fori_loop(first_far, jnp.maximum(qi - 1, 0), far, 0)
        return acc_ref[...] / l_ref[...]

    o_slc = branch(ks_ref, vs_ref, 0, True)
    o_win = branch(kw_ref, vw_ref, jnp.maximum(qi - WINDOW // TK, 0), False)

    gates = jax.nn.sigmoid(gp_ref[0, 0] + gb_ref[0])
    outs = [gates[:, 3 * r:3 * r + 1] * o_cmp[r] + gates[:, 3 * r + 1:3 * r + 2] * o_slc[r]
            + gates[:, 3 * r + 2:3 * r + 3] * o_win[r] for r in range(R)]
    o_ref[0] = jnp.concatenate(outs, axis=-1)


def nsa_prompt(qn, k_cmp, v_cmp, ks, vs, kw, vw, g_pre, gate_b, rel_bias):
    f32, bf16 = jnp.float32, jnp.bfloat16
    B, S = qn.shape[:2]
    G, R, DH, TQ = NSA_KV, NSA_R, NSA_DH, NSA_TQ
    assert S % TQ == 0 and TQ == V7X_LANES and WINDOW % TQ == 0 and TQ % SLC_BLOCK == 0
    n_cmp = k_cmp.shape[1]
    ncp = S // CMP_STRIDE
    ns = S // SLC_BLOCK
    assert n_cmp == ncp - 1
    n_select = min(N_SELECT, ns)

    def kv_layout(a, n):
        a = jnp.pad(a, ((0, 0), (0, n - a.shape[1]), (0, 0), (0, 0)))
        return jnp.transpose(a, (0, 2, 1, 3)).astype(bf16)

    q5 = jnp.transpose((qn * NSA_SCALE).reshape(B, S, G, R, DH), (0, 2, 3, 1, 4)).astype(bf16)
    kc, vc = kv_layout(k_cmp, ncp), kv_layout(v_cmp, ncp)
    ks_t, vs_t, kw_t, vw_t = (kv_layout(a, S) for a in (ks, vs, kw, vw))

    t = jnp.arange(S)
    dist_c = t[:, None] - (jnp.arange(ncp) * CMP_STRIDE + CMP_LEN - 1)[None, :]
    bias_c = jnp.transpose(rel_bias[t5_bucket(dist_c)].reshape(S, ncp, G, R), (2, 3, 0, 1))
    d_near = jnp.arange(TQ)[:, None] + TQ - jnp.arange(2 * TQ)[None, :]
    bias_n = jnp.transpose(rel_bias[t5_bucket(d_near)].reshape(TQ, 2 * TQ, G, R), (2, 3, 0, 1))
    bias_f = jnp.broadcast_to(rel_bias[REL_BUCKETS - 1].reshape(G, R, 1, 1), (G, R, 1, TQ))
    expand = (jnp.arange(S)[None, :] // SLC_BLOCK == jnp.arange(ns)[:, None]).astype(bf16)
    c_i = jnp.arange(ncp)[:, None]
    j_i = jnp.arange(ns)[None, :]
    mw = sum(w * (c_i == SLC_RATIO * j_i + k - 1) for k, w in enumerate(SLC_OVERLAP_W)).astype(bf16)
    gp = jnp.transpose(g_pre.reshape(B, S, G, 3 * R), (0, 2, 1, 3))
    gb = gate_b.reshape(G, 1, 3 * R)

    kv_spec = lambda n: pl.BlockSpec((1, 1, n, DH), lambda b, g, i: (b, g, 0, 0))
    return pl.pallas_call(
        functools.partial(_nsa_prompt_kernel, n_cmp=n_cmp, n_select=n_select),
        grid=(B, G, S // TQ),
        in_specs=[
            pl.BlockSpec((1, 1, R, TQ, DH), lambda b, g, i: (b, g, 0, i, 0)),
            kv_spec(ncp), kv_spec(ncp), kv_spec(S), kv_spec(S), kv_spec(S), kv_spec(S),
            pl.BlockSpec((1, R, TQ, ncp), lambda b, g, i: (g, 0, i, 0)),
            pl.BlockSpec((1, R, TQ, 2 * TQ), lambda b, g, i: (g, 0, 0, 0)),
            pl.BlockSpec((1, R, 1, TQ), lambda b, g, i: (g, 0, 0, 0)),
            pl.BlockSpec((ns, S), lambda b, g, i: (0, 0)),
            pl.BlockSpec((ncp, ns), lambda b, g, i: (0, 0)),
            pl.BlockSpec((1, 1, TQ, 3 * R), lambda b, g, i: (b, g, i, 0)),
            pl.BlockSpec((1, 1, 3 * R), lambda b, g, i: (g, 0, 0)),
        ],
        out_specs=pl.BlockSpec((1, TQ, R * DH), lambda b, g, i: (b, i, g)),
        out_shape=jax.ShapeDtypeStruct((B, S, NSA_W), f32),
        scratch_shapes=[pltpu.VMEM((S // TQ, TQ, TQ), f32), pltpu.VMEM((R, TQ, 1), f32),
                        pltpu.VMEM((R, TQ, 1), f32), pltpu.VMEM((R, TQ, DH), f32)],
        compiler_params=pltpu.CompilerParams(
            dimension_semantics=("arbitrary", "arbitrary", "arbitrary"), vmem_limit_bytes=V7X_VMEM_LIMIT_BYTES),
        name="nsa_prompt",
    )(q5, kc, vc, ks_t, vs_t, kw_t, vw_t, bias_c, bias_n, bias_f, expand, mw, gp, gb)


def split_cols(a, sizes):
    idx = [int(s) for s in np.cumsum(sizes)[:-1]]
    return jnp.split(a, idx, axis=-1)


def rms_norm(x, g):
    xf = x.astype(jnp.float32)
    y = xf * lax.rsqrt(jnp.mean(xf * xf, axis=-1, keepdims=True) + EPS)
    return (y * g.astype(jnp.float32)).astype(x.dtype)


def t5_bucket(dist):
    n = jnp.maximum(dist, 0)
    exact = REL_BUCKETS // 2
    nf = jnp.maximum(n, exact).astype(jnp.float32)
    large = exact + (jnp.log(nf / exact) / math.log(REL_MAX_DIST / exact) * (REL_BUCKETS - exact)).astype(jnp.int32)
    return jnp.where(n < exact, n, jnp.minimum(large, REL_BUCKETS - 1))


def apply_rope(x, pos):
    half = x.shape[-1] // 2
    inv = ROPE_THETA ** (-jnp.arange(half, dtype=jnp.float32) / half)
    ang = pos.astype(jnp.float32)[:, None] * inv[None, :]
    ang = ang.reshape(ang.shape[:1] + (1,) * (x.ndim - 3) + (half,))
    cos, sin = jnp.cos(ang).astype(x.dtype), jnp.sin(ang).astype(x.dtype)
    x1, x2 = x[..., :half], x[..., half:]
    return jnp.concatenate([x1 * cos - x2 * sin, x1 * sin + x2 * cos], axis=-1)


def causal_conv(u, buf, w, b):
    S = u.shape[1]
    full = jnp.concatenate([buf.astype(u.dtype), u], axis=1)
    out = b + sum(full[:, j:j + S] * w[j] for j in range(CONV_K))
    return out, full[:, S:]


def gather_pages(pool, e, page_table):
    g = pool[e, page_table]
    return g.reshape((g.shape[0], g.shape[1] * g.shape[2]) + g.shape[3:])


def mlstm_chunkwise(q, k, v, i_pre, logf, C0, n0, m0):
    f32 = jnp.float32
    q, k, v, i_pre, logf = (a.astype(f32) for a in (q, k, v, i_pre, logf))
    B, H, S, D = q.shape
    L = MLSTM_CHUNK if S % MLSTM_CHUNK == 0 else S
    NC = S // L

    def chunks(a):
        return jnp.moveaxis(a.reshape((B, H, NC, L) + a.shape[3:]), 2, 0)

    causal = jnp.tril(jnp.ones((L, L), dtype=bool))

    def step(carry, inp):
        C, n, m = carry
        qc, kc, vc, ic, fc = inp
        b = jnp.cumsum(fc, axis=-1)
        g = b + m[..., None]
        dmat = jnp.where(causal, b[..., :, None] - b[..., None, :] + ic[..., None, :], -jnp.inf)
        mt = jnp.maximum(g, jnp.max(dmat, axis=-1))
        inter = jnp.exp(g - mt)
        sqk = jnp.einsum('bhtd,bhsd->bhts', qc, kc) * jnp.exp(dmat - mt[..., None])
        num = inter[..., None] * jnp.einsum('bhvd,bhtd->bhtv', C, qc) + jnp.einsum('bhts,bhsv->bhtv', sqk, vc)
        den = inter * jnp.einsum('bhd,bhtd->bht', n, qc) + jnp.sum(sqk, axis=-1)
        h = num / jnp.maximum(jnp.abs(den), jnp.exp(-mt))[..., None]
        b_end = b[..., -1]
        w_log = b_end[..., None] - b + ic
        m_new = jnp.maximum(b_end + m, jnp.max(w_log, axis=-1))
        decay = jnp.exp(b_end + m - m_new)
        w_in = jnp.exp(w_log - m_new[..., None])
        C_new = decay[..., None, None] * C + jnp.einsum('bhs,bhsv,bhsd->bhvd', w_in, vc, kc)
        n_new = decay[..., None] * n + jnp.einsum('bhs,bhsd->bhd', w_in, kc)
        return (C_new, n_new, m_new), h

    (C1, n1, m1), hs = lax.scan(step, (C0.astype(f32), n0.astype(f32), m0.astype(f32)),
                                tuple(chunks(a) for a in (q, k, v, i_pre, logf)))
    return jnp.moveaxis(hs, 0, 2).reshape(B, H, S, D), C1, n1, m1


def gqa_attend(q, k, v, mask, bias):
    s = jnp.einsum('bqgrd,bkgd->bgrqk', q, k).astype(jnp.float32) * NSA_SCALE
    s = s + jnp.transpose(bias, (2, 3, 0, 1)).astype(jnp.float32)
    s = jnp.where(mask, s, NEG_INF)
    p = jax.nn.softmax(s, axis=-1) * mask.astype(jnp.float32)
    o = jnp.einsum('bgrqk,bkgd->bqgrd', p.astype(v.dtype), v)
    return o, p


def slc_core(q, tq, idx, kg, vg, rel_bias):
    s = jnp.einsum('bqgrd,bqgnld->bqgnlr', q, kg).astype(jnp.float32) * NSA_SCALE
    pos = idx[..., None] * SLC_BLOCK + jnp.arange(SLC_BLOCK)
    dist = tq[None, :, None, None, None] - pos
    tbl = jnp.transpose(rel_bias.reshape(REL_BUCKETS, NSA_KV, NSA_R), (1, 0, 2))
    g_i = jnp.arange(NSA_KV)[None, None, :, None, None]
    s = s + tbl[g_i, t5_bucket(dist)].astype(jnp.float32)
    s = jnp.where((dist >= 0)[..., None], s, NEG_INF)
    B, Q, G, N, L, R = s.shape
    p = jax.nn.softmax(s.reshape(B, Q, G, N * L, R), axis=-2).reshape(s.shape)
    return jnp.einsum('bqgnlr,bqgnld->bqgrd', p.astype(vg.dtype), vg)


def to_chunks(a):
    B, T = a.shape[:2]
    pad = (-T) % CMP_STRIDE
    a = jnp.pad(a, ((0, 0), (0, pad), (0, 0), (0, 0)))
    return a.reshape((B, (T + pad) // CMP_STRIDE, CMP_STRIDE) + a.shape[2:])


def cmp_summaries(chunk_list, T, pe, w1, w2):
    lo = jnp.concatenate([jnp.einsum('bcjgd,jdh->bcgh', r, w1[:CMP_STRIDE]) for r in chunk_list], axis=1)
    hi = jnp.concatenate([jnp.einsum('bcjgd,jdh->bcgh', r, w1[CMP_STRIDE:]) for r in chunk_list], axis=1)
    n_cmp = (T - CMP_LEN) // CMP_STRIDE + 1
    hid = jax.nn.silu(lo[:, :n_cmp] + hi[:, 1:n_cmp + 1] + jnp.einsum('jd,jdh->h', pe, w1))
    return hid @ w2


def even_mixer(xn, pos0, past, w_in, w_out, conv_w, conv_b, ml_wq, ml_wk, ml_b_i, ml_b_f, ml_out_g,
               nsa_gq, nsa_gk_cmp, nsa_gk_slc, nsa_gk_win, pe_k, w1_k, w2_k, pe_v, w1_v, w2_v, gate_b, rel_bias):
    B, S, _ = xn.shape
    dt = xn.dtype
    (u, v_m, o_pre, i_pre, f_pre, q, kc, vc, ks, vs, kw, vw, g_pre) = split_cols(xn @ w_in, EV_SPLITS)
    tq = pos0 + jnp.arange(S)

    if past is None:
        conv_buf = jnp.zeros((B, CONV_K - 1, ML_W), dt)
        C0 = jnp.zeros((B, ML_H, ML_DH, ML_DH), jnp.float32)
        n0 = jnp.zeros((B, ML_H, ML_DH), jnp.float32)
        m0 = jnp.zeros((B, ML_H), jnp.float32)
    else:
        conv_buf, C0, n0, m0 = past['conv'], past['C'], past['n'], past['m']
    c, conv_new = causal_conv(u, conv_buf, conv_w, conv_b)
    ch = jax.nn.silu(c).reshape(B, S, ML_H, ML_DH)
    qm = jnp.einsum('bshd,hde->bhse', ch, ml_wq)
    km = jnp.einsum('bshd,hde->bhse', ch, ml_wk) * (ML_DH ** -0.5)
    vm = jnp.transpose(v_m.reshape(B, S, ML_H, ML_DH), (0, 2, 1, 3))
    ig = jnp.transpose(i_pre + ml_b_i, (0, 2, 1))
    lf = jax.nn.log_sigmoid(jnp.transpose(f_pre + ml_b_f, (0, 2, 1)).astype(jnp.float32))
    hm, C1, n1, m1 = mlstm_chunkwise(qm, km, vm, ig, lf, C0, n0, m0)
    hm = rms_norm(jnp.transpose(hm, (0, 2, 1, 3)).astype(dt), ml_out_g) * jax.nn.sigmoid(o_pre).reshape(B, S, ML_H, ML_DH)
    h_a = hm.reshape(B, S, ML_W)

    q = rms_norm(q.reshape(B, S, NSA_H, NSA_DH), nsa_gq)
    qg = q.reshape(B, S, NSA_KV, NSA_R, NSA_DH)
    kv_shape = (B, S, NSA_KV, NSA_DH)
    kc, vc, vs, vw = (a.reshape(kv_shape) for a in (kc, vc, vs, vw))
    ks = rms_norm(ks.reshape(kv_shape), nsa_gk_slc)
    kw = rms_norm(kw.reshape(kv_shape), nsa_gk_win)
    gates = jax.nn.sigmoid(g_pre.reshape(B, S, NSA_H, 3) + gate_b)

    if past is None:
        T = S
        kch, vch = [to_chunks(kc)], [to_chunks(vc)]
    else:
        e, pt = past['e'], past['page_table']
        T = pos0 + S
        kch = [to_chunks(gather_pages(past['cmp_k'], e, pt)), to_chunks(kc)]
        vch = [to_chunks(gather_pages(past['cmp_v'], e, pt)), to_chunks(vc)]
    k_cmp = rms_norm(cmp_summaries(kch, T, pe_k, w1_k, w2_k), nsa_gk_cmp)
    v_cmp = cmp_summaries(vch, T, pe_v, w1_v, w2_v)
    if past is None:
        h_b = nsa_prompt(q, k_cmp, v_cmp, ks, vs, kw, vw, g_pre, gate_b, rel_bias)
        nb = min(WINDOW, S)
        out = jnp.concatenate([h_a, h_b], axis=-1) @ w_out
        new = dict(C=C1.astype(dt), n=n1.astype(dt), m=m1.astype(dt), conv=conv_new,
                   cmp_k=kc, cmp_v=vc, slc_k=ks, slc_v=vs, win_k=kw[:, S - nb:], win_v=vw[:, S - nb:])
        return out, new
    n_cmp = k_cmp.shape[1]
    dist = tq[:, None] - (jnp.arange(n_cmp) * CMP_STRIDE + CMP_LEN - 1)[None, :]
    bias = rel_bias[t5_bucket(dist)].reshape(S, n_cmp, NSA_KV, NSA_R)
    o_cmp, p_cmp = gqa_attend(qg, k_cmp, v_cmp, dist >= 0, bias)

    NS = -(-T // SLC_BLOCK)
    imp = jnp.pad(jnp.sum(p_cmp, axis=2), ((0, 0), (0, 0), (0, 0), (1, SLC_RATIO * NS - n_cmp)))
    p_slc = sum(w * imp[..., k:k + SLC_RATIO * NS:SLC_RATIO] for k, w in enumerate(SLC_OVERLAP_W))
    tb = (tq // SLC_BLOCK)[:, None]
    jb = jnp.arange(NS)[None, :]
    forced = (jb == 0) | (jb == tb) | (jb == tb - 1)
    score = jnp.where(jb <= tb, p_slc + FORCE_SCORE * forced.astype(jnp.float32), -1.0)
    idx = jnp.transpose(lax.top_k(score, min(N_SELECT, NS))[1], (0, 2, 1, 3))
    b_i = jnp.arange(B)[:, None, None, None]
    g_i = jnp.arange(NSA_KV)[None, None, :, None]

    if past is None:
        pad = NS * SLC_BLOCK - S
        ks_b = jnp.pad(ks, ((0, 0), (0, pad), (0, 0), (0, 0))).reshape(B, NS, SLC_BLOCK, NSA_KV, NSA_DH)
        vs_b = jnp.pad(vs, ((0, 0), (0, pad), (0, 0), (0, 0))).reshape(B, NS, SLC_BLOCK, NSA_KV, NSA_DH)
        QB = SLC_QBLOCK if S % SLC_QBLOCK == 0 else S

        def slc_block(i):
            qb = lax.dynamic_slice_in_dim(qg, i * QB, QB, axis=1)
            ib = lax.dynamic_slice_in_dim(idx, i * QB, QB, axis=1)
            tqb = pos0 + i * QB + jnp.arange(QB)
            return slc_core(qb, tqb, ib, ks_b[b_i, ib, :, g_i], vs_b[b_i, ib, :, g_i], rel_bias)

        o_slc = jnp.moveaxis(lax.map(slc_block, jnp.arange(S // QB)), 0, 1).reshape(B, S, NSA_KV, NSA_R, NSA_DH)
    else:
        bpp = PAGE_SIZE // SLC_BLOCK
        nb_past = pos0 // SLC_BLOCK
        jp = jnp.minimum(idx, nb_past - 1)
        page = pt[b_i, jp // bpp]
        rows = (jp % bpp)[..., None] * SLC_BLOCK + jnp.arange(SLC_BLOCK)
        kg_past = past['slc_k'][e, page[..., None], rows, g_i[..., None]]
        vg_past = past['slc_v'][e, page[..., None], rows, g_i[..., None]]
        nbn = -(-S // SLC_BLOCK)
        padn = nbn * SLC_BLOCK - S
        ks_nb = jnp.pad(ks, ((0, 0), (0, padn), (0, 0), (0, 0))).reshape(B, nbn, SLC_BLOCK, NSA_KV, NSA_DH)
        vs_nb = jnp.pad(vs, ((0, 0), (0, padn), (0, 0), (0, 0))).reshape(B, nbn, SLC_BLOCK, NSA_KV, NSA_DH)
        jn = jnp.clip(idx - nb_past, 0, nbn - 1)
        is_past = (idx < nb_past)[..., None, None]
        kg = jnp.where(is_past, kg_past, ks_nb[b_i, jn, :, g_i])
        vg = jnp.where(is_past, vg_past, vs_nb[b_i, jn, :, g_i])
        o_slc = slc_core(qg, tq, idx, kg, vg, rel_bias)

    if past is None:
        QB = WIN_QBLOCK if S % WIN_QBLOCK == 0 else S
        span = QB + WINDOW
        kw_p = jnp.pad(kw, ((0, 0), (WINDOW, 0), (0, 0), (0, 0)))
        vw_p = jnp.pad(vw, ((0, 0), (WINDOW, 0), (0, 0), (0, 0)))

        def win_block(i):
            start = i * QB
            qb = lax.dynamic_slice_in_dim(qg, start, QB, axis=1)
            tqb = pos0 + start + jnp.arange(QB)
            kpos = pos0 + start - WINDOW + jnp.arange(span)
            d = tqb[:, None] - kpos[None, :]
            mask = (d >= 0) & (d <= WINDOW) & (kpos >= 0)[None, :]
            bw = rel_bias[t5_bucket(d)].reshape(QB, span, NSA_KV, NSA_R)
            kb = lax.dynamic_slice_in_dim(kw_p, start, span, axis=1)
            vb = lax.dynamic_slice_in_dim(vw_p, start, span, axis=1)
            return gqa_attend(qb, kb, vb, mask, bw)[0]

        o_win = jnp.moveaxis(lax.map(win_block, jnp.arange(S // QB)), 0, 1).reshape(B, S, NSA_KV, NSA_R, NSA_DH)
        nb = min(WINDOW, S)
        win_k_new, win_v_new = kw[:, S - nb:], vw[:, S - nb:]
    else:
        nbuf = past['win_k'].shape[1]
        k_all = jnp.concatenate([past['win_k'].astype(dt), kw], axis=1)
        v_all = jnp.concatenate([past['win_v'].astype(dt), vw], axis=1)
        kpos = jnp.concatenate([pos0 - nbuf + jnp.arange(nbuf), tq])
        d = tq[:, None] - kpos[None, :]
        mask = (d >= 0) & (d <= WINDOW)
        bw = rel_bias[t5_bucket(d)].reshape(S, nbuf + S, NSA_KV, NSA_R)
        o_win = gqa_attend(qg, k_all, v_all, mask, bw)[0]
        win_k_new, win_v_new = k_all[:, S:], v_all[:, S:]

    hs = (NSA_H, NSA_DH)
    o_b = (gates[..., 0:1] * o_cmp.reshape((B, S) + hs) + gates[..., 1:2] * o_slc.reshape((B, S) + hs)
           + gates[..., 2:3] * o_win.reshape((B, S) + hs))
    h_b = o_b.reshape(B, S, NSA_W)

    out = jnp.concatenate([h_a, h_b], axis=-1) @ w_out
    new = dict(C=C1.astype(dt), n=n1.astype(dt), m=m1.astype(dt), conv=conv_new,
               cmp_k=kc, cmp_v=vc, slc_k=ks, slc_v=vs, win_k=win_k_new, win_v=win_v_new)
    return out, new


def odd_mixer(xn, pos0, past, w_in, g_cq, w_uq, g_q, g_ckv, g_kr, w_uk, w_uv, w_out):
    B, S, _ = xn.shape
    cq, ckv, kr = split_cols(xn @ w_in, OD_SPLITS)
    tq = pos0 + jnp.arange(S)
    q = rms_norm((rms_norm(cq, g_cq) @ w_uq).reshape(B, S, MLA_H, MLA_NOPE + MLA_ROPE), g_q)
    q_nope = q[..., :MLA_NOPE]
    q_rope = apply_rope(q[..., MLA_NOPE:], tq)
    ckv = rms_norm(ckv, g_ckv)
    kr = apply_rope(rms_norm(kr, g_kr), tq)
    if past is None:
        k_nope = jnp.einsum('bsc,chn->bshn', ckv, w_uk)
        v = jnp.einsum('bsc,chv->bshv', ckv, w_uv)
        QB = MLA_QBLOCK if S % MLA_QBLOCK == 0 else S

        def blk(i):
            start = i * QB
            qn = lax.dynamic_slice_in_dim(q_nope, start, QB, axis=1)
            qr = lax.dynamic_slice_in_dim(q_rope, start, QB, axis=1)
            tqb = pos0 + start + jnp.arange(QB)
            s = (jnp.einsum('bqhn,bkhn->bhqk', qn, k_nope) + jnp.einsum('bqhr,bkr->bhqk', qr, kr)).astype(jnp.float32)
            s = jnp.where(tq[None, :] <= tqb[:, None], s * MLA_SCALE, NEG_INF)
            p = jax.nn.softmax(s, axis=-1)
            return jnp.einsum('bhqk,bkhv->bqhv', p.astype(v.dtype), v)

        o = jnp.moveaxis(lax.map(blk, jnp.arange(S // QB)), 0, 1).reshape(B, S, MLA_H * MLA_V)
    else:
        e, pt = past['e'], past['page_table']
        ckv_past = gather_pages(past['ckv'], e, pt).astype(ckv.dtype)
        kr_past = gather_pages(past['krope'], e, pt).astype(kr.dtype)
        P = ckv_past.shape[1]
        q_lat = jnp.einsum('bqhn,chn->bqhc', q_nope, w_uk)
        s_past = jnp.einsum('bqhc,btc->bhqt', q_lat, ckv_past) + jnp.einsum('bqhr,btr->bhqt', q_rope, kr_past)
        s_new = jnp.einsum('bqhc,bkc->bhqk', q_lat, ckv) + jnp.einsum('bqhr,bkr->bhqk', q_rope, kr)
        s_new = jnp.where(jnp.tril(jnp.ones((S, S), dtype=bool)), s_new.astype(jnp.float32) * MLA_SCALE, NEG_INF)
        s = jnp.concatenate([s_past.astype(jnp.float32) * MLA_SCALE, s_new], axis=-1)
        p = jax.nn.softmax(s, axis=-1).astype(ckv.dtype)
        o_lat = jnp.einsum('bhqt,btc->bqhc', p[..., :P], ckv_past) + jnp.einsum('bhqk,bkc->bqhc', p[..., P:], ckv)
        o = jnp.einsum('bqhc,chv->bqhv', o_lat, w_uv).reshape(B, S, MLA_H * MLA_V)
    return o @ w_out, dict(ckv=ckv, krope=kr)


def mem_kv(mem, g_mem, wk, wv, gk):
    B, M, _ = mem.shape
    m = rms_norm(mem, g_mem)
    k = rms_norm((m @ wk).reshape(B, M, XM_H, XM_DH), gk)
    v = (m @ wv).reshape(B, M, XM_H, XM_DH)
    return k, v


def mem_attend(xn, k, v, wq, gq, wo):
    B, S, _ = xn.shape
    q = rms_norm((xn @ wq).reshape(B, S, XM_H, XM_DH), gq)
    s = jnp.einsum('bshd,bmhd->bhsm', q, k.astype(q.dtype)).astype(jnp.float32) * (XM_DH ** -0.5)
    p = jax.nn.softmax(s, axis=-1)
    return jnp.einsum('bhsm,bmhd->bshd', p.astype(xn.dtype), v.astype(xn.dtype)).reshape(B, S, XM_W) @ wo


def stack_key(lst, name):
    return jnp.stack([d[name] for d in lst])


def kernel(x_prompt, x_sample, mem_prompt,
           state_ml_C, state_ml_n, state_ml_m, state_ml_conv,
           cache_cmp_k, cache_cmp_v, cache_slc_k, cache_slc_v, cache_win_k, cache_win_v,
           cache_mla_ckv, cache_mla_krope, cache_mem_k, cache_mem_v, page_table,
           rel_bias, ffn1_norm, ffn1_wg, ffn1_wu, ffn1_wd, mix_norm,
           xm_norm, xm_mem_norm, xm_wq, xm_wk, xm_wv, xm_wo, xm_gq, xm_gk,
           ffn2_norm, ffn2_wg, ffn2_wu, ffn2_wd,
           ev_w_in, ev_w_out, ml_conv_w, ml_conv_b, ml_wq, ml_wk, ml_b_i, ml_b_f, ml_out_g,
           nsa_gq, nsa_gk_cmp, nsa_gk_slc, nsa_gk_win, cmp_pe_k, cmp_w1_k, cmp_w2_k,
           cmp_pe_v, cmp_w1_v, cmp_w2_v, nsa_gate_b,
           od_w_in, mla_g_cq, mla_w_uq, mla_g_q, mla_g_ckv, mla_g_kr, mla_w_uk, mla_w_uv, od_w_out):
    past_len = page_table.shape[1] * PAGE_SIZE
    bf = jnp.bfloat16
    yp, ys = x_prompt, x_sample
    ev_p, ev_s, od_p, od_s, memk_p, memv_p = [], [], [], [], [], []
    for layer in range(DEPTH):
        f1 = (ffn1_norm[layer], ffn1_wg[layer].astype(bf), ffn1_wu[layer].astype(bf), ffn1_wd[layer].astype(bf))
        yp = swiglu_half(yp, *f1)
        ys = swiglu_half(ys, *f1)
        if layer % 2 == 0:
            e = layer // 2
            ew = dict(w_in=ev_w_in[e], w_out=ev_w_out[e], conv_w=ml_conv_w[e], conv_b=ml_conv_b[e],
                      ml_wq=ml_wq[e], ml_wk=ml_wk[e], ml_b_i=ml_b_i[e], ml_b_f=ml_b_f[e], ml_out_g=ml_out_g[e],
                      nsa_gq=nsa_gq[e], nsa_gk_cmp=nsa_gk_cmp[e], nsa_gk_slc=nsa_gk_slc[e], nsa_gk_win=nsa_gk_win[e],
                      pe_k=cmp_pe_k[e], w1_k=cmp_w1_k[e], w2_k=cmp_w2_k[e],
                      pe_v=cmp_pe_v[e], w1_v=cmp_w1_v[e], w2_v=cmp_w2_v[e],
                      gate_b=nsa_gate_b[e], rel_bias=rel_bias)
            past = dict(e=e, page_table=page_table, C=state_ml_C[e], n=state_ml_n[e], m=state_ml_m[e],
                        conv=state_ml_conv[e], cmp_k=cache_cmp_k, cmp_v=cache_cmp_v,
                        slc_k=cache_slc_k, slc_v=cache_slc_v, win_k=cache_win_k[e], win_v=cache_win_v[e])
            hp, st_p = even_mixer(rms_norm(yp, mix_norm[layer]), 0, None, **ew)
            hs, st_s = even_mixer(rms_norm(ys, mix_norm[layer]), past_len, past, **ew)
            ev_p.append(st_p)
            ev_s.append(st_s)
        else:
            o = layer // 2
            ow = dict(w_in=od_w_in[o], g_cq=mla_g_cq[o], w_uq=mla_w_uq[o], g_q=mla_g_q[o], g_ckv=mla_g_ckv[o],
                      g_kr=mla_g_kr[o], w_uk=mla_w_uk[o], w_uv=mla_w_uv[o], w_out=od_w_out[o])
            past = dict(e=o, page_table=page_table, ckv=cache_mla_ckv, krope=cache_mla_krope)
            hp, st_p = odd_mixer(rms_norm(yp, mix_norm[layer]), 0, None, **ow)
            hs, st_s = odd_mixer(rms_norm(ys, mix_norm[layer]), past_len, past, **ow)
            od_p.append(st_p)
            od_s.append(st_s)
        yp = yp + hp
        ys = ys + hs
        mk, mv = mem_kv(mem_prompt, xm_mem_norm[layer], xm_wk[layer], xm_wv[layer], xm_gk[layer])
        memk_p.append(mk)
        memv_p.append(mv)
        yp = yp + mem_attend(rms_norm(yp, xm_norm[layer]), mk, mv, xm_wq[layer], xm_gq[layer], xm_wo[layer])
        ys = ys + mem_attend(rms_norm(ys, xm_norm[layer]), cache_mem_k[layer], cache_mem_v[layer],
                             xm_wq[layer], xm_gq[layer], xm_wo[layer])
        f2 = (ffn2_norm[layer], ffn2_wg[layer].astype(bf), ffn2_wu[layer].astype(bf), ffn2_wd[layer].astype(bf))
        yp = swiglu_half(yp, *f2)
        ys = swiglu_half(ys, *f2)
    return (yp, ys,
            stack_key(ev_p, 'C'), stack_key(ev_p, 'n'), stack_key(ev_p, 'm'), stack_key(ev_p, 'conv'),
            stack_key(ev_p, 'cmp_k'), stack_key(ev_p, 'cmp_v'), stack_key(ev_p, 'slc_k'), stack_key(ev_p, 'slc_v'),
            stack_key(ev_p, 'win_k'), stack_key(ev_p, 'win_v'),
            stack_key(od_p, 'ckv'), stack_key(od_p, 'krope'),
            jnp.stack(memk_p), jnp.stack(memv_p),
            stack_key(ev_s, 'C'), stack_key(ev_s, 'n'), stack_key(ev_s, 'm'), stack_key(ev_s, 'conv'),
            stack_key(ev_s, 'cmp_k'), stack_key(ev_s, 'cmp_v'), stack_key(ev_s, 'slc_k'), stack_key(ev_s, 'slc_v'),
            stack_key(ev_s, 'win_k'), stack_key(ev_s, 'win_v'),
            stack_key(od_s, 'ckv'), stack_key(od_s, 'krope'))
```

```python
import functools
import math

import jax
import jax.numpy as jnp
import numpy as np
from jax import lax
from jax.experimental import pallas as pl
from jax.experimental.pallas import tpu as pltpu

D_MODEL = 1024
DEPTH = 2
PAGE_SIZE = 128
EPS = 1e-6
NEG_INF = -1e30
D_FF = 2816
ML_H = 4
ML_DH = 128
ML_W = ML_H * ML_DH
CONV_K = 4
MLSTM_CHUNK = 64
NSA_H = 8
NSA_KV = 2
NSA_R = NSA_H // NSA_KV
NSA_DH = 64
NSA_W = NSA_H * NSA_DH
NSA_KVW = NSA_KV * NSA_DH
NSA_SCALE = NSA_DH ** -0.5
CMP_STRIDE = 16
CMP_LEN = 2 * CMP_STRIDE
CMP_HID = 2 * NSA_DH
SLC_BLOCK = 64
SLC_RATIO = SLC_BLOCK // CMP_STRIDE
SLC_OVERLAP_W = (1.0, 2.0, 2.0, 2.0, 1.0)
N_SELECT = 16
WINDOW = 512
SLC_QBLOCK = 32
WIN_QBLOCK = 128
FORCE_SCORE = 1e6
REL_BUCKETS = 32
REL_MAX_DIST = 128
MLA_H = 16
MLA_NOPE = 64
MLA_ROPE = 32
MLA_V = 64
Q_LORA = 384
KV_LORA = 256
MLA_SCALE = (MLA_NOPE + MLA_ROPE) ** -0.5
MLA_QBLOCK = 128
ROPE_THETA = 10000.0
MEM_LEN = 256
XM_H = 4
XM_DH = 128
XM_W = XM_H * XM_DH
EV_SPLITS = (ML_W, ML_W, ML_W, ML_H, ML_H, NSA_W) + (NSA_KVW,) * 6 + (NSA_H * 3,)
OD_SPLITS = (Q_LORA, KV_LORA, MLA_ROPE)

V7X_LANES = 128
V7X_VMEM_LIMIT_BYTES = 56 * 1024 * 1024
FFN_CHUNK = 256
FFN_ROWS = 512


def _ffn_kernel(x_ref, g_ref, wg_ref, wu_ref, wd_ref, o_ref, acc_ref):
    x = x_ref[...]
    h = x * lax.rsqrt(jnp.mean(x * x, axis=-1, keepdims=True) + EPS) * g_ref[...]
    hb = h.astype(jnp.bfloat16)
    acc_ref[...] = jnp.zeros_like(acc_ref)

    def body(c, carry):
        cols = pl.ds(pl.multiple_of(c * FFN_CHUNK, FFN_CHUNK), FFN_CHUNK)
        gate = jnp.dot(hb, wg_ref[:, cols], preferred_element_type=jnp.float32)
        up = jnp.dot(hb, wu_ref[:, cols], preferred_element_type=jnp.float32)
        act = (gate * jax.nn.sigmoid(gate) * up).astype(jnp.bfloat16)
        acc_ref[...] += jnp.dot(act, wd_ref[cols, :], preferred_element_type=jnp.float32)
        return carry

    lax.fori_loop(0, D_FF // FFN_CHUNK, body, 0)
    o_ref[...] = x + 0.5 * acc_ref[...]


def ffn_half(x2d, g, wg, wu, wd):
    m = x2d.shape[0]
    tm = min(FFN_ROWS, m)
    assert m % tm == 0
    resident = functools.partial(pl.BlockSpec, pipeline_mode=pl.Buffered(1))
    return pl.pallas_call(
        _ffn_kernel,
        grid=(m // tm,),
        in_specs=[
            pl.BlockSpec((tm, D_MODEL), lambda i: (i, 0)),
            resident((1, D_MODEL), lambda i: (0, 0)),
            resident((D_MODEL, D_FF), lambda i: (0, 0)),
            resident((D_MODEL, D_FF), lambda i: (0, 0)),
            resident((D_FF, D_MODEL), lambda i: (0, 0)),
        ],
        out_specs=pl.BlockSpec((tm, D_MODEL), lambda i: (i, 0)),
        out_shape=jax.ShapeDtypeStruct((m, D_MODEL), jnp.float32),
        scratch_shapes=[pltpu.VMEM((tm, D_MODEL), jnp.float32)],
        compiler_params=pltpu.CompilerParams(
            dimension_semantics=("arbitrary",), vmem_limit_bytes=V7X_VMEM_LIMIT_BYTES),
        name="ffn_half",
    )(x2d, g.reshape(1, D_MODEL), wg, wu, wd)


def swiglu_half(x, g, wg, wu, wd):
    shp = x.shape
    return ffn_half(x.reshape(-1, D_MODEL), g, wg, wu, wd).reshape(shp)


NSA_TQ = 128


def _nsa_prompt_kernel(q_ref, kc_ref, vc_ref, ks_ref, vs_ref, kw_ref, vw_ref, bc_ref, bn_ref, bfar_ref,
                       e_ref, mw_ref, gp_ref, gb_ref, o_ref, sel_ref, m_ref, l_ref, acc_ref, *, n_cmp, n_select):
    f32, bf16 = jnp.float32, jnp.bfloat16
    qi = pl.program_id(2)
    R, TQ, DH = q_ref.shape[2:]
    TK = TQ
    t0 = qi * TQ
    q2 = q_ref[0, 0].reshape(R * TQ, DH)
    t_col = t0 + lax.broadcasted_iota(jnp.int32, (TQ, 1), 0)

    def scores(k):
        s = lax.dot_general(q2, k, (((1,), (1,)), ((), ())), preferred_element_type=f32)
        return s.reshape(R, TQ, k.shape[0])

    ncp = kc_ref.shape[2]
    c_row = lax.broadcasted_iota(jnp.int32, (1, ncp), 1)
    mask_c = (t_col >= c_row * CMP_STRIDE + (CMP_LEN - 1)) & (c_row < n_cmp)
    s = jnp.where(mask_c[None], scores(kc_ref[0, 0]) + bc_ref[0], NEG_INF)
    e = jnp.exp(s - jnp.max(s, axis=-1, keepdims=True))
    p = e / jnp.sum(e, axis=-1, keepdims=True) * mask_c.astype(f32)[None]
    o_cmp = jnp.dot(p.reshape(R * TQ, ncp).astype(bf16), vc_ref[0, 0],
                    preferred_element_type=f32).reshape(R, TQ, DH)
    imp = jnp.sum(p, axis=0)

    mw = mw_ref[...]
    hi = imp.astype(bf16)
    r1 = imp - hi.astype(f32)
    mid = r1.astype(bf16)
    lo = (r1 - mid.astype(f32)).astype(bf16)
    p_slc = (jnp.dot(hi, mw, preferred_element_type=f32) + jnp.dot(mid, mw, preferred_element_type=f32)
             + jnp.dot(lo, mw, preferred_element_type=f32))
    ns = mw_ref.shape[1]
    jb = lax.broadcasted_iota(jnp.int32, (1, ns), 1)
    tb = jnp.right_shift(t_col, int(math.log2(SLC_BLOCK)))
    forced = (jb == 0) | (jb == tb) | (jb == tb - 1)
    score = jnp.where(jb <= tb, p_slc + FORCE_SCORE * forced.astype(f32), -1.0)
    rank = jnp.zeros((TQ, ns), f32)
    for j in range(ns):
        col = score[:, j:j + 1]
        rank = rank + ((col > score) | ((col == score) & (jb > j))).astype(f32)
    sel = (rank < n_select).astype(bf16)
    sel_tok = jnp.dot(sel, e_ref[...], preferred_element_type=f32)
    for kt in range(sel_ref.shape[0]):
        sel_ref[kt] = sel_tok[:, kt * TK:(kt + 1) * TK]

    def reset():
        m_ref[...] = jnp.full(m_ref.shape, NEG_INF, f32)
        l_ref[...] = jnp.zeros(l_ref.shape, f32)
        acc_ref[...] = jnp.zeros(acc_ref.shape, f32)

    def attend(k_ref, v_ref, kt, bias, selected):
        start = pl.multiple_of(kt * TK, TK)
        k = k_ref[0, 0, pl.ds(start, TK), :]
        v = v_ref[0, 0, pl.ds(start, TK), :]
        d = t_col - (start + lax.broadcasted_iota(jnp.int32, (1, TK), 1))
        if selected:
            msk = (d >= 0) & (sel_ref[kt] > 0.5)
        else:
            msk = (d >= 0) & (d <= WINDOW)
        s = jnp.where(msk[None], scores(k) + bias, NEG_INF)
        m_prev = m_ref[...]
        m_new = jnp.maximum(m_prev, jnp.max(s, axis=-1, keepdims=True))
        alpha = jnp.exp(m_prev - m_new)
        pe = jnp.where(msk[None], jnp.exp(s - m_new), 0.0)
        l_ref[...] = alpha * l_ref[...] + jnp.sum(pe, axis=-1, keepdims=True)
        pv = jnp.dot(pe.reshape(R * TQ, TK).astype(bf16), v, preferred_element_type=f32)
        acc_ref[...] = alpha * acc_ref[...] + pv.reshape(R, TQ, DH)
        m_ref[...] = m_new

    def branch(k_ref, v_ref, first_far, selected):
        reset()
        attend(k_ref, v_ref, qi, bn_ref[0, :, :, TK:], selected)

        @pl.when(qi >= 1)
        def _():
            attend(k_ref, v_ref, qi - 1, bn_ref[0, :, :, :TK], selected)

        def far(kt, carry):
            attend(k_ref, v_ref, kt, bfar_ref[0], selected)
            return carry

        lax.fori_loop(first_far, jnp.maximum(qi - 1, 0), far, 0)
        return acc_ref[...] / l_ref[...]

    o_slc = branch(ks_ref, vs_ref, 0, True)
    o_win = branch(kw_ref, vw_ref, jnp.maximum(qi - WINDOW // TK, 0), False)

    gates = jax.nn.sigmoid(gp_ref[0, 0] + gb_ref[0])
    outs = [gates[:, 3 * r:3 * r + 1] * o_cmp[r] + gates[:, 3 * r + 1:3 * r + 2] * o_slc[r]
            + gates[:, 3 * r + 2:3 * r + 3] * o_win[r] for r in range(R)]
    o_ref[0] = jnp.concatenate(outs, axis=-1)


def nsa_prompt(qn, k_cmp, v_cmp, ks, vs, kw, vw, g_pre, gate_b, rel_bias):
    f32, bf16 = jnp.float32, jnp.bfloat16
    B, S = qn.shape[:2]
    G, R, DH, TQ = NSA_KV, NSA_R, NSA_DH, NSA_TQ
    assert S % TQ == 0 and TQ == V7X_LANES and WINDOW % TQ == 0 and TQ % SLC_BLOCK == 0
    n_cmp = k_cmp.shape[1]
    ncp = S // CMP_STRIDE
    ns = S // SLC_BLOCK
    assert n_cmp == ncp - 1
    n_select = min(N_SELECT, ns)

    def kv_layout(a, n):
        a = jnp.pad(a, ((0, 0), (0, n - a.shape[1]), (0, 0), (0, 0)))
        return jnp.transpose(a, (0, 2, 1, 3)).astype(bf16)

    q5 = jnp.transpose((qn * NSA_SCALE).reshape(B, S, G, R, DH), (0, 2, 3, 1, 4)).astype(bf16)
    kc, vc = kv_layout(k_cmp, ncp), kv_layout(v_cmp, ncp)
    ks_t, vs_t, kw_t, vw_t = (kv_layout(a, S) for a in (ks, vs, kw, vw))

    t = jnp.arange(S)
    dist_c = t[:, None] - (jnp.arange(ncp) * CMP_STRIDE + CMP_LEN - 1)[None, :]
    bias_c = jnp.transpose(rel_bias[t5_bucket(dist_c)].reshape(S, ncp, G, R), (2, 3, 0, 1))
    d_near = jnp.arange(TQ)[:, None] + TQ - jnp.arange(2 * TQ)[None, :]
    bias_n = jnp.transpose(rel_bias[t5_bucket(d_near)].reshape(TQ, 2 * TQ, G, R), (2, 3, 0, 1))
    bias_f = jnp.broadcast_to(rel_bias[REL_BUCKETS - 1].reshape(G, R, 1, 1), (G, R, 1, TQ))
    expand = (jnp.arange(S)[None, :] // SLC_BLOCK == jnp.arange(ns)[:, None]).astype(bf16)
    c_i = jnp.arange(ncp)[:, None]
    j_i = jnp.arange(ns)[None, :]
    mw = sum(w * (c_i == SLC_RATIO * j_i + k - 1) for k, w in enumerate(SLC_OVERLAP_W)).astype(bf16)
    gp = jnp.transpose(g_pre.reshape(B, S, G, 3 * R), (0, 2, 1, 3))
    gb = gate_b.reshape(G, 1, 3 * R)

    kv_spec = lambda n: pl.BlockSpec((1, 1, n, DH), lambda b, g, i: (b, g, 0, 0))
    return pl.pallas_call(
        functools.partial(_nsa_prompt_kernel, n_cmp=n_cmp, n_select=n_select),
        grid=(B, G, S // TQ),
        in_specs=[
            pl.BlockSpec((1, 1, R, TQ, DH), lambda b, g, i: (b, g, 0, i, 0)),
            kv_spec(ncp), kv_spec(ncp), kv_spec(S), kv_spec(S), kv_spec(S), kv_spec(S),
            pl.BlockSpec((1, R, TQ, ncp), lambda b, g, i: (g, 0, i, 0)),
            pl.BlockSpec((1, R, TQ, 2 * TQ), lambda b, g, i: (g, 0, 0, 0)),
            pl.BlockSpec((1, R, 1, TQ), lambda b, g, i: (g, 0, 0, 0)),
            pl.BlockSpec((ns, S), lambda b, g, i: (0, 0)),
            pl.BlockSpec((ncp, ns), lambda b, g, i: (0, 0)),
            pl.BlockSpec((1, 1, TQ, 3 * R), lambda b, g, i: (b, g, i, 0)),
            pl.BlockSpec((1, 1, 3 * R), lambda b, g, i: (g, 0, 0)),
        ],
        out_specs=pl.BlockSpec((1, TQ, R * DH), lambda b, g, i: (b, i, g)),
        out_shape=jax.ShapeDtypeStruct((B, S, NSA_W), f32),
        scratch_shapes=[pltpu.VMEM((S // TQ, TQ, TQ), f32), pltpu.VMEM((R, TQ, 1), f32),
                        pltpu.VMEM((R, TQ, 1), f32), pltpu.VMEM((R, TQ, DH), f32)],
        compiler_params=pltpu.CompilerParams(
            dimension_semantics=("arbitrary", "arbitrary", "arbitrary"), vmem_limit_bytes=V7X_VMEM_LIMIT_BYTES),
        name="nsa_prompt",
    )(q5, kc, vc, ks_t, vs_t, kw_t, vw_t, bias_c, bias_n, bias_f, expand, mw, gp, gb)


MLA_TQ = 512
MLA_TK = 256
MLA_HEADS_PER_STEP = 2


def _mla_prompt_kernel(q_ref, k_ref, v_ref, o_ref, m_ref, l_ref, acc_ref):
    f32, bf16 = jnp.float32, jnp.bfloat16
    qi = pl.program_id(2)
    HP, TQ = q_ref.shape[1:3]
    TK = MLA_TK
    row = qi * TQ + lax.broadcasted_iota(jnp.int32, (TQ, 1), 0)
    outs = []
    for h in range(HP):
        q = q_ref[0, h]
        m_ref[...] = jnp.full(m_ref.shape, NEG_INF, f32)
        l_ref[...] = jnp.zeros(l_ref.shape, f32)
        acc_ref[...] = jnp.zeros(acc_ref.shape, f32)

        def attend(kt, masked):
            start = pl.multiple_of(kt * TK, TK)
            k = k_ref[0, h, pl.ds(start, TK), :]
            v = v_ref[0, h, pl.ds(start, TK), :]
            s = lax.dot_general(q, k, (((1,), (1,)), ((), ())), preferred_element_type=f32) * MLA_SCALE
            if masked:
                s = jnp.where(row >= start + lax.broadcasted_iota(jnp.int32, (1, TK), 1), s, NEG_INF)
            m_prev = m_ref[...]
            m_new = jnp.maximum(m_prev, jnp.max(s, axis=-1, keepdims=True))
            alpha = jnp.exp(m_prev - m_new)
            pe = jnp.exp(s - m_new)
            l_ref[...] = alpha * l_ref[...] + jnp.sum(pe, axis=-1, keepdims=True)
            acc_ref[...] = alpha * acc_ref[...] + jnp.dot(pe.astype(bf16), v, preferred_element_type=f32)
            m_ref[...] = m_new

        def full_tile(kt, carry):
            attend(kt, False)
            return carry

        lax.fori_loop(0, qi * (TQ // TK), full_tile, 0)
        for j in range(TQ // TK):
            attend(qi * (TQ // TK) + j, True)
        outs.append(acc_ref[...] / l_ref[...])
    o_ref[0] = jnp.concatenate(outs, axis=-1)


def mla_prompt_attention(q_cat, k_cat, v):
    B, H, S, DQ = q_cat.shape
    HP, TQ = MLA_HEADS_PER_STEP, min(MLA_TQ, S)
    assert S % TQ == 0 and TQ % MLA_TK == 0 and H % HP == 0 and HP * MLA_V == V7X_LANES
    return pl.pallas_call(
        _mla_prompt_kernel,
        grid=(B, H // HP, S // TQ),
        in_specs=[
            pl.BlockSpec((1, HP, TQ, DQ), lambda b, h, i: (b, h, i, 0)),
            pl.BlockSpec((1, HP, S, DQ), lambda b, h, i: (b, h, 0, 0)),
            pl.BlockSpec((1, HP, S, MLA_V), lambda b, h, i: (b, h, 0, 0)),
        ],
        out_specs=pl.BlockSpec((1, TQ, HP * MLA_V), lambda b, h, i: (b, i, h)),
        out_shape=jax.ShapeDtypeStruct((B, S, H * MLA_V), jnp.float32),
        scratch_shapes=[pltpu.VMEM((TQ, 1), jnp.float32), pltpu.VMEM((TQ, 1), jnp.float32),
                        pltpu.VMEM((TQ, MLA_V), jnp.float32)],
        compiler_params=pltpu.CompilerParams(
            dimension_semantics=("arbitrary", "arbitrary", "arbitrary"), vmem_limit_bytes=V7X_VMEM_LIMIT_BYTES),
        name="mla_prompt",
    )(q_cat, k_cat, v)


def _paged_copy(pool_ref, page, buf_ref, slot, p, sem_ref):
    return pltpu.make_async_copy(pool_ref.at[page], buf_ref.at[slot, p], sem_ref.at[slot])


def _paged_pipeline(pt_ref, pools, bufs, sems):
    b = pl.program_id(0)
    n_pages = pt_ref.shape[1]

    def start(seq, slot):
        def body(p, carry):
            for pool, buf, sem in zip(pools, bufs, sems):
                _paged_copy(pool, pt_ref[seq, p], buf, slot, p, sem).start()
            return carry
        lax.fori_loop(0, n_pages, body, 0)

    @pl.when(b == 0)
    def _():
        start(0, 0)

    @pl.when(b + 1 < pl.num_programs(0))
    def _():
        start(b + 1, (b + 1) % 2)

    slot = b % 2

    def wait_body(p, carry):
        for pool, buf, sem in zip(pools, bufs, sems):
            _paged_copy(pool, 0, buf, slot, p, sem).wait()
        return carry
    lax.fori_loop(0, n_pages, wait_body, 0)
    return slot


def _softmax_rows(s, valid):
    e = jnp.exp(s - jnp.max(s, axis=-1, keepdims=True))
    return e / jnp.sum(e, axis=-1, keepdims=True) * valid


def _nsa_decode_cmp_kernel(pt_ref, qbd_ref, wk_ref, wv_ref, pek_ref, pev_ref, w2k_ref, w2v_ref, gk_ref,
                           bc_ref, mw_ref, kpool_ref, vpool_ref, ocmp_ref, sel_ref,
                           kbuf, vbuf, xrm_ref, ksem, vsem, *, n_cmp, n_blocks, n_select):
    f32, bf16 = jnp.float32, jnp.bfloat16
    slot = _paged_pipeline(pt_ref, (kpool_ref, vpool_ref), (kbuf, vbuf), (ksem, vsem))
    n_pages = pt_ref.shape[1]
    n_chunk = n_pages * (PAGE_SIZE // CMP_STRIDE)
    G, DH, HID = NSA_KV, NSA_DH, CMP_HID

    def summaries(buf, w_ref, pe_ref, w2_ref):
        def xpose(p, carry):
            xrm_ref[pl.ds(pl.multiple_of(p * PAGE_SIZE, PAGE_SIZE), PAGE_SIZE), :] = buf[slot, p].T
            return carry
        lax.fori_loop(0, n_pages, xpose, 0)
        acc = jnp.zeros((n_chunk, 2 * G * HID), f32)
        for j in range(CMP_STRIDE):
            rows = xrm_ref[pl.ds(j, n_chunk, stride=CMP_STRIDE), :].astype(bf16)
            acc = acc + jnp.dot(rows, w_ref[j], preferred_element_type=f32)
        lo, hi = acc[:, :G * HID], acc[:, G * HID:]
        pre = lo + pltpu.roll(hi, n_chunk - 1, 0) + pe_ref[...]
        hid = pre * jax.nn.sigmoid(pre)
        return jnp.dot(hid.astype(bf16), w2_ref[...], preferred_element_type=f32)

    k_sum = summaries(kbuf, wk_ref, pek_ref, w2k_ref)
    lane = lax.broadcasted_iota(jnp.int32, (1, G * DH), 1)
    sq = k_sum * k_sum
    s_all = jnp.sum(sq, axis=-1, keepdims=True)
    s_g0 = jnp.sum(jnp.where(lane < DH, sq, 0.0), axis=-1, keepdims=True)
    ms = jnp.where(lane < DH, s_g0, s_all - s_g0) * (1.0 / DH)
    k_cmp = (k_sum * lax.rsqrt(ms + EPS) * gk_ref[...]).astype(bf16)
    v_cmp = summaries(vbuf, wv_ref, pev_ref, w2v_ref).astype(bf16)

    qbd = qbd_ref[0]
    H = qbd.shape[0]
    c_row = lax.broadcasted_iota(jnp.int32, (1, n_chunk), 1)
    valid = (c_row < n_cmp).astype(f32)
    s = lax.dot_general(qbd, k_cmp, (((1,), (1,)), ((), ())), preferred_element_type=f32) + bc_ref[...]
    p = _softmax_rows(jnp.where(valid > 0.5, s, NEG_INF), valid)
    ocmp_ref[0] = jnp.dot(p.astype(bf16), v_cmp, preferred_element_type=f32)

    R = H // G
    head = lax.broadcasted_iota(jnp.int32, (H, 1), 0)
    imp = jnp.where(head < R, jnp.sum(p[:R], axis=0, keepdims=True), jnp.sum(p[R:], axis=0, keepdims=True))
    mw = mw_ref[...]
    hi_p = imp.astype(bf16)
    r1 = imp - hi_p.astype(f32)
    mid_p = r1.astype(bf16)
    lo_p = (r1 - mid_p.astype(f32)).astype(bf16)
    p_slc = (jnp.dot(hi_p, mw, preferred_element_type=f32) + jnp.dot(mid_p, mw, preferred_element_type=f32)
             + jnp.dot(lo_p, mw, preferred_element_type=f32))
    nsp = mw_ref.shape[1]
    tb = n_blocks - 1
    jb = lax.broadcasted_iota(jnp.int32, (1, nsp), 1)
    forced = (jb == 0) | (jb == tb) | (jb == tb - 1)
    score = jnp.where(jb <= tb, p_slc + FORCE_SCORE * forced.astype(f32), -1.0)
    j_col = lax.broadcasted_iota(jnp.int32, (nsp, 1), 0)
    sels = []
    for g in range(G):
        row = score[g * R:g * R + 1, :]
        col = jnp.broadcast_to(row, (nsp, nsp)).T
        beats = (col > row) | ((col == row) & (j_col < jb))
        rank = jnp.sum(beats.astype(f32), axis=0, keepdims=True)
        sels.append(jnp.broadcast_to((rank < n_select).astype(f32), (R, nsp)))
    sel_ref[0] = jnp.concatenate(sels, axis=0)


def _nsa_decode_attn_kernel(pt_ref, qbd_ref, sel_ref, ocmp_ref, gate_ref, new_ref, e_ref, bs_ref, bw_ref, b0_ref,
                            kw_ref, vw_ref, kpool_ref, vpool_ref, o_ref, kbuf, vbuf, s_ref, ksem, vsem):
    f32, bf16 = jnp.float32, jnp.bfloat16
    slot = _paged_pipeline(pt_ref, (kpool_ref, vpool_ref), (kbuf, vbuf), (ksem, vsem))
    n_pages = pt_ref.shape[1]
    qbd = qbd_ref[0]
    qf = qbd.astype(f32)
    new = new_ref[0]
    b0 = b0_ref[...]

    def new_score(k_row):
        return jnp.sum(qf * k_row.astype(bf16).astype(f32), axis=-1, keepdims=True) + b0

    for p in range(n_pages):
        s_ref[:, p * PAGE_SIZE:(p + 1) * PAGE_SIZE] = jnp.dot(qbd, kbuf[slot, p].astype(bf16),
                                                              preferred_element_type=f32)
    n_blk_past = e_ref.shape[0]
    sel_tok = jnp.dot(sel_ref[0][:, :n_blk_past].astype(bf16), e_ref[...], preferred_element_type=f32)
    s_past = jnp.where(sel_tok > 0.5, s_ref[...] + bs_ref[...], NEG_INF)
    s_new = new_score(new[0:1])
    m = jnp.maximum(jnp.max(s_past, axis=-1, keepdims=True), s_new)
    e_new = jnp.exp(s_new - m)
    s_ref[...] = jnp.exp(s_past - m)
    denom = jnp.sum(s_ref[...], axis=-1, keepdims=True) + e_new
    acc = e_new.astype(bf16).astype(f32) * new[1:2].astype(bf16).astype(f32)
    for p in range(n_pages):
        pe = s_ref[:, p * PAGE_SIZE:(p + 1) * PAGE_SIZE].astype(bf16)
        acc = acc + lax.dot_general(pe, vbuf[slot, p].astype(bf16), (((1,), (1,)), ((), ())),
                                    preferred_element_type=f32)
    o_slc = acc / denom

    s_w = jnp.dot(qbd, kw_ref[0].astype(bf16), preferred_element_type=f32) + bw_ref[...]
    s_wn = new_score(new[2:3])
    m = jnp.maximum(jnp.max(s_w, axis=-1, keepdims=True), s_wn)
    e_w, e_wn = jnp.exp(s_w - m), jnp.exp(s_wn - m)
    denom = jnp.sum(e_w, axis=-1, keepdims=True) + e_wn
    acc = (lax.dot_general(e_w.astype(bf16), vw_ref[0].astype(bf16), (((1,), (1,)), ((), ())),
                           preferred_element_type=f32)
           + e_wn.astype(bf16).astype(f32) * new[3:4].astype(bf16).astype(f32))
    o_win = acc / denom

    gates = jax.nn.sigmoid(gate_ref[0])
    o_ref[0] = gates[:, 0:1] * ocmp_ref[0] + gates[:, 1:2] * o_slc + gates[:, 2:3] * o_win


def _block_diag_heads(x):
    B, H, DH = x.shape
    g_of_h = jnp.arange(H) // (H // NSA_KV)
    onehot = (g_of_h[:, None] == jnp.arange(NSA_KV)[None, :]).astype(x.dtype)
    return (x[:, :, None, :] * onehot[None, :, :, None]).reshape(B, H, NSA_KV * DH)


def nsa_decode(qn, ks, vs, kw, vw, g_pre, gate_b, rel_bias, page_table, cmp_k_pool, cmp_v_pool,
               slc_k_pool, slc_v_pool, win_k, win_v, pe_k, w1_k, w2_k, pe_v, w1_v, w2_v, gk_cmp):
    f32, bf16 = jnp.float32, jnp.bfloat16
    B, n_pages = page_table.shape
    G, R, DH, H = NSA_KV, NSA_R, NSA_DH, NSA_H
    GD = G * DH
    past = n_pages * PAGE_SIZE
    n_chunk = past // CMP_STRIDE
    n_cmp = (past + 1 - CMP_LEN) // CMP_STRIDE + 1
    n_blocks = -(-(past + 1) // SLC_BLOCK)
    n_blk_past = past // SLC_BLOCK
    nsp = -(-n_blocks // V7X_LANES) * V7X_LANES
    n_win = win_k.shape[1]
    assert GD == V7X_LANES and PAGE_SIZE == V7X_LANES and n_blocks >= N_SELECT and n_win == WINDOW

    def pool_view(pool):
        return jnp.transpose(pool, (0, 2, 3, 1)).reshape(pool.shape[0], GD, PAGE_SIZE)
    kc_pool, vc_pool, ks_pool, vs_pool = (pool_view(a) for a in (cmp_k_pool, cmp_v_pool, slc_k_pool, slc_v_pool))
    kw_t = jnp.transpose(win_k, (0, 2, 3, 1)).reshape(B, GD, n_win)
    vw_t = jnp.transpose(win_v, (0, 2, 3, 1)).reshape(B, GD, n_win)

    qbd = _block_diag_heads((qn[:, 0] * NSA_SCALE)).astype(bf16)
    eye = jnp.eye(G, dtype=f32)

    def chunk_weights(w1):
        def bd(w):
            return jnp.einsum('jdh,gk->jgdkh', w, eye).reshape(CMP_STRIDE, GD, G * CMP_HID)
        return jnp.concatenate([bd(w1[:CMP_STRIDE]), bd(w1[CMP_STRIDE:])], axis=-1).astype(bf16)

    def pe_term(pe, w1):
        return jnp.tile(jnp.einsum('jd,jdh->h', pe, w1), G).reshape(1, G * CMP_HID)

    def w2_bd(w2):
        return jnp.einsum('hd,gk->ghkd', w2, eye).reshape(G * CMP_HID, GD).astype(bf16)

    c_i = jnp.arange(n_chunk)
    bias_c = rel_bias[t5_bucket(past - (c_i * CMP_STRIDE + CMP_LEN - 1))].T
    j_i = jnp.arange(nsp)[None, :]
    mw = sum(w * ((c_i[:, None] == SLC_RATIO * j_i + k - 1) & (j_i < n_blocks))
             for k, w in enumerate(SLC_OVERLAP_W)).astype(bf16)
    tok = jnp.arange(past)
    expand = (tok[None, :] // SLC_BLOCK == jnp.arange(n_blk_past)[:, None]).astype(bf16)
    bias_s = rel_bias[t5_bucket(past - tok)].T
    bias_w = rel_bias[t5_bucket(n_win - jnp.arange(n_win))].T
    bias_0 = rel_bias[t5_bucket(jnp.zeros((1,), jnp.int32))].T
    new_rows = jnp.stack([a.reshape(B, GD) for a in (ks, vs, kw, vw)], axis=1)
    gate_in = g_pre.reshape(B, H, 3) + gate_b

    const = lambda shape: pl.BlockSpec(shape, lambda b, pt: (0,) * len(shape))
    per_seq = lambda shape: pl.BlockSpec((1,) + shape, lambda b, pt: (b,) + (0,) * len(shape))
    any_spec = pl.BlockSpec(memory_space=pl.ANY)
    page_buf = pltpu.VMEM((2, n_pages, GD, PAGE_SIZE), f32)
    params = pltpu.CompilerParams(dimension_semantics=("arbitrary",), vmem_limit_bytes=V7X_VMEM_LIMIT_BYTES)

    o_cmp, sel = pl.pallas_call(
        functools.partial(_nsa_decode_cmp_kernel, n_cmp=n_cmp, n_blocks=n_blocks, n_select=min(N_SELECT, n_blocks)),
        grid_spec=pltpu.PrefetchScalarGridSpec(
            num_scalar_prefetch=1, grid=(B,),
            in_specs=[per_seq((H, GD)),
                      const((CMP_STRIDE, GD, 2 * G * CMP_HID)), const((CMP_STRIDE, GD, 2 * G * CMP_HID)),
                      const((1, G * CMP_HID)), const((1, G * CMP_HID)),
                      const((G * CMP_HID, GD)), const((G * CMP_HID, GD)), const((1, GD)),
                      const((H, n_chunk)), const((n_chunk, nsp)), any_spec, any_spec],
            out_specs=[per_seq((H, GD)), per_seq((H, nsp))],
            scratch_shapes=[page_buf, page_buf, pltpu.VMEM((past, GD), f32),
                            pltpu.SemaphoreType.DMA((2,)), pltpu.SemaphoreType.DMA((2,))]),
        out_shape=[jax.ShapeDtypeStruct((B, H, GD), f32), jax.ShapeDtypeStruct((B, H, nsp), f32)],
        compiler_params=params, name="nsa_decode_cmp",
    )(page_table, qbd, chunk_weights(w1_k), chunk_weights(w1_v), pe_term(pe_k, w1_k), pe_term(pe_v, w1_v),
      w2_bd(w2_k), w2_bd(w2_v), jnp.tile(gk_cmp, G).reshape(1, GD), bias_c, mw, kc_pool, vc_pool)

    out = pl.pallas_call(
        _nsa_decode_attn_kernel,
        grid_spec=pltpu.PrefetchScalarGridSpec(
            num_scalar_prefetch=1, grid=(B,),
            in_specs=[per_seq((H, GD)), per_seq((H, nsp)), per_seq((H, GD)), per_seq((H, 3)), per_seq((4, GD)),
                      const((n_blk_past, past)), const((H, past)), const((H, n_win)), const((H, 1)),
                      per_seq((GD, n_win)), per_seq((GD, n_win)), any_spec, any_spec],
            out_specs=per_seq((H, GD)),
            scratch_shapes=[page_buf, page_buf, pltpu.VMEM((H, past), f32),
                            pltpu.SemaphoreType.DMA((2,)), pltpu.SemaphoreType.DMA((2,))]),
        out_shape=jax.ShapeDtypeStruct((B, H, GD), f32),
        compiler_params=params, name="nsa_decode_attn",
    )(page_table, qbd, sel, o_cmp, gate_in, new_rows, expand, bias_s, bias_w, bias_0, kw_t, vw_t, ks_pool, vs_pool)

    out = out.reshape(B, G, R, G, DH)
    h_b = jnp.stack([out[:, g, :, g, :] for g in range(G)], axis=1)
    return h_b.reshape(B, 1, NSA_W)


def _mla_decode_kernel(pt_ref, ql_ref, qr_ref, new_ref, newr_ref, cpool_ref, rpool_ref, o_ref,
                       cbuf, rbuf, cb16, s_ref, csem, rsem):
    f32, bf16 = jnp.float32, jnp.bfloat16
    slot = _paged_pipeline(pt_ref, (cpool_ref, rpool_ref), (cbuf, rbuf), (csem, rsem))
    n_pages = pt_ref.shape[1]
    ql, qr = ql_ref[0], qr_ref[0]
    for p in range(n_pages):
        c16 = cbuf[slot, p].astype(bf16)
        cb16[p] = c16
        s_ref[:, p * PAGE_SIZE:(p + 1) * PAGE_SIZE] = (
            lax.dot_general(ql, c16, (((1,), (1,)), ((), ())), preferred_element_type=f32)
            + jnp.dot(qr, rbuf[slot, p].astype(bf16), preferred_element_type=f32)) * MLA_SCALE
    c_new = new_ref[0].astype(bf16).astype(f32)
    r_new = newr_ref[0].astype(bf16).astype(f32)
    s_new = (jnp.sum(ql.astype(f32) * c_new, axis=-1, keepdims=True)
             + jnp.sum(qr.astype(f32) * r_new, axis=-1, keepdims=True)) * MLA_SCALE
    s_past = s_ref[...]
    m = jnp.maximum(jnp.max(s_past, axis=-1, keepdims=True), s_new)
    e_new = jnp.exp(s_new - m)
    s_ref[...] = jnp.exp(s_past - m)
    denom = jnp.sum(s_ref[...], axis=-1, keepdims=True) + e_new
    inv = 1.0 / denom
    acc = (e_new * inv).astype(bf16).astype(f32) * c_new
    for p in range(n_pages):
        pe = (s_ref[:, p * PAGE_SIZE:(p + 1) * PAGE_SIZE] * inv).astype(bf16)
        acc = acc + jnp.dot(pe, cb16[p], preferred_element_type=f32)
    o_ref[0] = acc


def mla_decode_attention(q_lat, q_rope, ckv_new, kr_new, page_table, ckv_pool, krope_pool):
    f32, bf16 = jnp.float32, jnp.bfloat16
    B, n_pages = page_table.shape
    H = q_lat.shape[1]
    past = n_pages * PAGE_SIZE
    rpool_t = jnp.transpose(krope_pool, (0, 2, 1))
    per_seq = lambda shape: pl.BlockSpec((1,) + shape, lambda b, pt: (b,) + (0,) * len(shape))
    any_spec = pl.BlockSpec(memory_space=pl.ANY)
    return pl.pallas_call(
        _mla_decode_kernel,
        grid_spec=pltpu.PrefetchScalarGridSpec(
            num_scalar_prefetch=1, grid=(B,),
            in_specs=[per_seq((H, KV_LORA)), per_seq((H, MLA_ROPE)), per_seq((1, KV_LORA)), per_seq((1, MLA_ROPE)),
                      any_spec, any_spec],
            out_specs=per_seq((H, KV_LORA)),
            scratch_shapes=[pltpu.VMEM((2, n_pages, PAGE_SIZE, KV_LORA), f32),
                            pltpu.VMEM((2, n_pages, MLA_ROPE, PAGE_SIZE), f32),
                            pltpu.VMEM((n_pages, PAGE_SIZE, KV_LORA), bf16),
                            pltpu.VMEM((H, past), f32),
                            pltpu.SemaphoreType.DMA((2,)), pltpu.SemaphoreType.DMA((2,))]),
        out_shape=jax.ShapeDtypeStruct((B, H, KV_LORA), f32),
        compiler_params=pltpu.CompilerParams(dimension_semantics=("arbitrary",),
                                             vmem_limit_bytes=V7X_VMEM_LIMIT_BYTES),
        name="mla_decode",
    )(page_table, q_lat.astype(bf16), q_rope.astype(bf16), ckv_new.reshape(B, 1, KV_LORA),
      kr_new.reshape(B, 1, MLA_ROPE), ckv_pool, rpool_t)


def split_cols(a, sizes):
    idx = [int(s) for s in np.cumsum(sizes)[:-1]]
    return jnp.split(a, idx, axis=-1)


def rms_norm(x, g):
    xf = x.astype(jnp.float32)
    y = xf * lax.rsqrt(jnp.mean(xf * xf, axis=-1, keepdims=True) + EPS)
    return (y * g.astype(jnp.float32)).astype(x.dtype)


def t5_bucket(dist):
    n = jnp.maximum(dist, 0)
    exact = REL_BUCKETS // 2
    nf = jnp.maximum(n, exact).astype(jnp.float32)
    large = exact + (jnp.log(nf / exact) / math.log(REL_MAX_DIST / exact) * (REL_BUCKETS - exact)).astype(jnp.int32)
    return jnp.where(n < exact, n, jnp.minimum(large, REL_BUCKETS - 1))


def apply_rope(x, pos):
    half = x.shape[-1] // 2
    inv = ROPE_THETA ** (-jnp.arange(half, dtype=jnp.float32) / half)
    ang = pos.astype(jnp.float32)[:, None] * inv[None, :]
    ang = ang.reshape(ang.shape[:1] + (1,) * (x.ndim - 3) + (half,))
    cos, sin = jnp.cos(ang).astype(x.dtype), jnp.sin(ang).astype(x.dtype)
    x1, x2 = x[..., :half], x[..., half:]
    return jnp.concatenate([x1 * cos - x2 * sin, x1 * sin + x2 * cos], axis=-1)


def causal_conv(u, buf, w, b):
    S = u.shape[1]
    full = jnp.concatenate([buf.astype(u.dtype), u], axis=1)
    out = b + sum(full[:, j:j + S] * w[j] for j in range(CONV_K))
    return out, full[:, S:]


def mlstm_chunkwise(q, k, v, i_pre, logf, C0, n0, m0):
    f32 = jnp.float32
    q, k, v, i_pre, logf = (a.astype(f32) for a in (q, k, v, i_pre, logf))
    B, H, S, D = q.shape
    L = MLSTM_CHUNK if S % MLSTM_CHUNK == 0 else S
    NC = S // L

    def chunks(a):
        return jnp.moveaxis(a.reshape((B, H, NC, L) + a.shape[3:]), 2, 0)

    causal = jnp.tril(jnp.ones((L, L), dtype=bool))

    def step(carry, inp):
        C, n, m = carry
        qc, kc, vc, ic, fc = inp
        b = jnp.cumsum(fc, axis=-1)
        g = b + m[..., None]
        dmat = jnp.where(causal, b[..., :, None] - b[..., None, :] + ic[..., None, :], -jnp.inf)
        mt = jnp.maximum(g, jnp.max(dmat, axis=-1))
        inter = jnp.exp(g - mt)
        sqk = jnp.einsum('bhtd,bhsd->bhts', qc, kc) * jnp.exp(dmat - mt[..., None])
        num = inter[..., None] * jnp.einsum('bhvd,bhtd->bhtv', C, qc) + jnp.einsum('bhts,bhsv->bhtv', sqk, vc)
        den = inter * jnp.einsum('bhd,bhtd->bht', n, qc) + jnp.sum(sqk, axis=-1)
        h = num / jnp.maximum(jnp.abs(den), jnp.exp(-mt))[..., None]
        b_end = b[..., -1]
        w_log = b_end[..., None] - b + ic
        m_new = jnp.maximum(b_end + m, jnp.max(w_log, axis=-1))
        decay = jnp.exp(b_end + m - m_new)
        w_in = jnp.exp(w_log - m_new[..., None])
        C_new = decay[..., None, None] * C + jnp.einsum('bhs,bhsv,bhsd->bhvd', w_in, vc, kc)
        n_new = decay[..., None] * n + jnp.einsum('bhs,bhsd->bhd', w_in, kc)
        return (C_new, n_new, m_new), h

    (C1, n1, m1), hs = lax.scan(step, (C0.astype(f32), n0.astype(f32), m0.astype(f32)),
                                tuple(chunks(a) for a in (q, k, v, i_pre, logf)))
    return jnp.moveaxis(hs, 0, 2).reshape(B, H, S, D), C1, n1, m1


def to_chunks(a):
    B, T = a.shape[:2]
    pad = (-T) % CMP_STRIDE
    a = jnp.pad(a, ((0, 0), (0, pad), (0, 0), (0, 0)))
    return a.reshape((B, (T + pad) // CMP_STRIDE, CMP_STRIDE) + a.shape[2:])


def cmp_summaries(chunk_list, T, pe, w1, w2):
    lo = jnp.concatenate([jnp.einsum('bcjgd,jdh->bcgh', r, w1[:CMP_STRIDE]) for r in chunk_list], axis=1)
    hi = jnp.concatenate([jnp.einsum('bcjgd,jdh->bcgh', r, w1[CMP_STRIDE:]) for r in chunk_list], axis=1)
    n_cmp = (T - CMP_LEN) // CMP_STRIDE + 1
    hid = jax.nn.silu(lo[:, :n_cmp] + hi[:, 1:n_cmp + 1] + jnp.einsum('jd,jdh->h', pe, w1))
    return hid @ w2


def even_mixer(xn, pos0, past, w_in, w_out, conv_w, conv_b, ml_wq, ml_wk, ml_b_i, ml_b_f, ml_out_g,
               nsa_gq, nsa_gk_cmp, nsa_gk_slc, nsa_gk_win, pe_k, w1_k, w2_k, pe_v, w1_v, w2_v, gate_b, rel_bias):
    B, S, _ = xn.shape
    dt = xn.dtype
    (u, v_m, o_pre, i_pre, f_pre, q, kc, vc, ks, vs, kw, vw, g_pre) = split_cols(xn @ w_in, EV_SPLITS)

    if past is None:
        conv_buf = jnp.zeros((B, CONV_K - 1, ML_W), dt)
        C0 = jnp.zeros((B, ML_H, ML_DH, ML_DH), jnp.float32)
        n0 = jnp.zeros((B, ML_H, ML_DH), jnp.float32)
        m0 = jnp.zeros((B, ML_H), jnp.float32)
    else:
        conv_buf, C0, n0, m0 = past['conv'], past['C'], past['n'], past['m']
    c, conv_new = causal_conv(u, conv_buf, conv_w, conv_b)
    ch = jax.nn.silu(c).reshape(B, S, ML_H, ML_DH)
    qm = jnp.einsum('bshd,hde->bhse', ch, ml_wq)
    km = jnp.einsum('bshd,hde->bhse', ch, ml_wk) * (ML_DH ** -0.5)
    vm = jnp.transpose(v_m.reshape(B, S, ML_H, ML_DH), (0, 2, 1, 3))
    ig = jnp.transpose(i_pre + ml_b_i, (0, 2, 1))
    lf = jax.nn.log_sigmoid(jnp.transpose(f_pre + ml_b_f, (0, 2, 1)).astype(jnp.float32))
    hm, C1, n1, m1 = mlstm_chunkwise(qm, km, vm, ig, lf, C0, n0, m0)
    hm = rms_norm(jnp.transpose(hm, (0, 2, 1, 3)).astype(dt), ml_out_g) * jax.nn.sigmoid(o_pre).reshape(B, S, ML_H, ML_DH)
    h_a = hm.reshape(B, S, ML_W)

    q = rms_norm(q.reshape(B, S, NSA_H, NSA_DH), nsa_gq)
    kv_shape = (B, S, NSA_KV, NSA_DH)
    kc, vc, vs, vw = (a.reshape(kv_shape) for a in (kc, vc, vs, vw))
    ks = rms_norm(ks.reshape(kv_shape), nsa_gk_slc)
    kw = rms_norm(kw.reshape(kv_shape), nsa_gk_win)

    if past is None:
        k_cmp = rms_norm(cmp_summaries([to_chunks(kc)], S, pe_k, w1_k, w2_k), nsa_gk_cmp)
        v_cmp = cmp_summaries([to_chunks(vc)], S, pe_v, w1_v, w2_v)
        h_b = nsa_prompt(q, k_cmp, v_cmp, ks, vs, kw, vw, g_pre, gate_b, rel_bias)
        nb = min(WINDOW, S)
        win_k_new, win_v_new = kw[:, S - nb:], vw[:, S - nb:]
    else:
        assert S == 1
        e = past['e']
        h_b = nsa_decode(q, ks, vs, kw, vw, g_pre, gate_b, rel_bias, past['page_table'],
                         past['cmp_k'][e], past['cmp_v'][e], past['slc_k'][e], past['slc_v'][e],
                         past['win_k'], past['win_v'], pe_k, w1_k, w2_k, pe_v, w1_v, w2_v, nsa_gk_cmp)
        win_k_new = jnp.concatenate([past['win_k'][:, S:], kw], axis=1)
        win_v_new = jnp.concatenate([past['win_v'][:, S:], vw], axis=1)
    out = jnp.concatenate([h_a, h_b], axis=-1) @ w_out
    new = dict(C=C1.astype(dt), n=n1.astype(dt), m=m1.astype(dt), conv=conv_new,
               cmp_k=kc, cmp_v=vc, slc_k=ks, slc_v=vs, win_k=win_k_new, win_v=win_v_new)
    return out, new


def odd_mixer(xn, pos0, past, w_in, g_cq, w_uq, g_q, g_ckv, g_kr, w_uk, w_uv, w_out):
    B, S, _ = xn.shape
    cq, ckv, kr = split_cols(xn @ w_in, OD_SPLITS)
    tq = pos0 + jnp.arange(S)
    q = rms_norm((rms_norm(cq, g_cq) @ w_uq).reshape(B, S, MLA_H, MLA_NOPE + MLA_ROPE), g_q)
    q_nope = q[..., :MLA_NOPE]
    q_rope = apply_rope(q[..., MLA_NOPE:], tq)
    ckv = rms_norm(ckv, g_ckv)
    kr = apply_rope(rms_norm(kr, g_kr), tq)
    if past is None:
        k_nope = jnp.einsum('bsc,chn->bshn', ckv, w_uk)
        v = jnp.einsum('bsc,chv->bshv', ckv, w_uv)
        bf16 = jnp.bfloat16
        q_cat = jnp.transpose(jnp.concatenate([q_nope, q_rope], axis=-1), (0, 2, 1, 3)).astype(bf16)
        kr_h = jnp.broadcast_to(kr[:, :, None, :], (B, S, MLA_H, MLA_ROPE))
        k_cat = jnp.transpose(jnp.concatenate([k_nope, kr_h], axis=-1), (0, 2, 1, 3)).astype(bf16)
        o = mla_prompt_attention(q_cat, k_cat, jnp.transpose(v, (0, 2, 1, 3)).astype(bf16))
    else:
        assert S == 1
        e = past['e']
        q_lat = jnp.einsum('bqhn,chn->bqhc', q_nope, w_uk)
        o_lat = mla_decode_attention(q_lat[:, 0], q_rope[:, 0], ckv[:, 0], kr[:, 0], past['page_table'],
                                     past['ckv'][e], past['krope'][e])[:, None]
        o = jnp.einsum('bqhc,chv->bqhv', o_lat, w_uv).reshape(B, S, MLA_H * MLA_V)
    return o @ w_out, dict(ckv=ckv, krope=kr)


def mem_kv(mem, g_mem, wk, wv, gk):
    B, M, _ = mem.shape
    m = rms_norm(mem, g_mem)
    k = rms_norm((m @ wk).reshape(B, M, XM_H, XM_DH), gk)
    v = (m @ wv).reshape(B, M, XM_H, XM_DH)
    return k, v


def mem_attend(xn, k, v, wq, gq, wo):
    B, S, _ = xn.shape
    q = rms_norm((xn @ wq).reshape(B, S, XM_H, XM_DH), gq)
    s = jnp.einsum('bshd,bmhd->bhsm', q, k.astype(q.dtype)).astype(jnp.float32) * (XM_DH ** -0.5)
    p = jax.nn.softmax(s, axis=-1)
    return jnp.einsum('bhsm,bmhd->bshd', p.astype(xn.dtype), v.astype(xn.dtype)).reshape(B, S, XM_W) @ wo


def stack_key(lst, name):
    return jnp.stack([d[name] for d in lst])


def kernel(x_prompt, x_sample, mem_prompt,
           state_ml_C, state_ml_n, state_ml_m, state_ml_conv,
           cache_cmp_k, cache_cmp_v, cache_slc_k, cache_slc_v, cache_win_k, cache_win_v,
           cache_mla_ckv, cache_mla_krope, cache_mem_k, cache_mem_v, page_table,
           rel_bias, ffn1_norm, ffn1_wg, ffn1_wu, ffn1_wd, mix_norm,
           xm_norm, xm_mem_norm, xm_wq, xm_wk, xm_wv, xm_wo, xm_gq, xm_gk,
           ffn2_norm, ffn2_wg, ffn2_wu, ffn2_wd,
           ev_w_in, ev_w_out, ml_conv_w, ml_conv_b, ml_wq, ml_wk, ml_b_i, ml_b_f, ml_out_g,
           nsa_gq, nsa_gk_cmp, nsa_gk_slc, nsa_gk_win, cmp_pe_k, cmp_w1_k, cmp_w2_k,
           cmp_pe_v, cmp_w1_v, cmp_w2_v, nsa_gate_b,
           od_w_in, mla_g_cq, mla_w_uq, mla_g_q, mla_g_ckv, mla_g_kr, mla_w_uk, mla_w_uv, od_w_out):
    past_len = page_table.shape[1] * PAGE_SIZE
    bf = jnp.bfloat16
    ffn_w = [[(n[layer], wg[layer].astype(bf), wu[layer].astype(bf), wd[layer].astype(bf))
              for n, wg, wu, wd in ((ffn1_norm, ffn1_wg, ffn1_wu, ffn1_wd), (ffn2_norm, ffn2_wg, ffn2_wu, ffn2_wd))]
             for layer in range(DEPTH)]

    def run_group(y, prompt):
        ev, od, memk, memv = [], [], [], []
        for layer in range(DEPTH):
            y = swiglu_half(y, *ffn_w[layer][0])
            xn = rms_norm(y, mix_norm[layer])
            if layer % 2 == 0:
                e = layer // 2
                ew = dict(w_in=ev_w_in[e], w_out=ev_w_out[e], conv_w=ml_conv_w[e], conv_b=ml_conv_b[e],
                          ml_wq=ml_wq[e], ml_wk=ml_wk[e], ml_b_i=ml_b_i[e], ml_b_f=ml_b_f[e], ml_out_g=ml_out_g[e],
                          nsa_gq=nsa_gq[e], nsa_gk_cmp=nsa_gk_cmp[e], nsa_gk_slc=nsa_gk_slc[e],
                          nsa_gk_win=nsa_gk_win[e], pe_k=cmp_pe_k[e], w1_k=cmp_w1_k[e], w2_k=cmp_w2_k[e],
                          pe_v=cmp_pe_v[e], w1_v=cmp_w1_v[e], w2_v=cmp_w2_v[e],
                          gate_b=nsa_gate_b[e], rel_bias=rel_bias)
                past = None if prompt else dict(
                    e=e, page_table=page_table, C=state_ml_C[e], n=state_ml_n[e], m=state_ml_m[e],
                    conv=state_ml_conv[e], cmp_k=cache_cmp_k, cmp_v=cache_cmp_v,
                    slc_k=cache_slc_k, slc_v=cache_slc_v, win_k=cache_win_k[e], win_v=cache_win_v[e])
                h, st = even_mixer(xn, 0 if prompt else past_len, past, **ew)
                ev.append(st)
            else:
                o = layer // 2
                ow = dict(w_in=od_w_in[o], g_cq=mla_g_cq[o], w_uq=mla_w_uq[o], g_q=mla_g_q[o], g_ckv=mla_g_ckv[o],
                          g_kr=mla_g_kr[o], w_uk=mla_w_uk[o], w_uv=mla_w_uv[o], w_out=od_w_out[o])
                past = None if prompt else dict(e=o, page_table=page_table, ckv=cache_mla_ckv, krope=cache_mla_krope)
                h, st = odd_mixer(xn, 0 if prompt else past_len, past, **ow)
                od.append(st)
            y = y + h
            if prompt:
                mk, mv = mem_kv(mem_prompt, xm_mem_norm[layer], xm_wk[layer], xm_wv[layer], xm_gk[layer])
                memk.append(mk)
                memv.append(mv)
            else:
                mk, mv = cache_mem_k[layer], cache_mem_v[layer]
            y = y + mem_attend(rms_norm(y, xm_norm[layer]), mk, mv, xm_wq[layer], xm_gq[layer], xm_wo[layer])
            y = swiglu_half(y, *ffn_w[layer][1])
        return y, ev, od, memk, memv

    ys, ev_s, od_s, _, _ = run_group(x_sample, False)
    yp, ev_p, od_p, memk_p, memv_p = run_group(x_prompt, True)
    return (yp, ys,
            stack_key(ev_p, 'C'), stack_key(ev_p, 'n'), stack_key(ev_p, 'm'), stack_key(ev_p, 'conv'),
            stack_key(ev_p, 'cmp_k'), stack_key(ev_p, 'cmp_v'), stack_key(ev_p, 'slc_k'), stack_key(ev_p, 'slc_v'),
            stack_key(ev_p, 'win_k'), stack_key(ev_p, 'win_v'),
            stack_key(od_p, 'ckv'), stack_key(od_p, 'krope'),
            jnp.stack(memk_p), jnp.stack(memv_p),
            stack_key(ev_s, 'C'), stack_key(ev_s, 'n'), stack_key(ev_s, 'm'), stack_key(ev_s, 'conv'),
            stack_key(ev_s, 'cmp_k'), stack_key(ev_s, 'cmp_v'), stack_key(ev_s, 'slc_k'), stack_key(ev_s, 'slc_v'),
            stack_key(ev_s, 'win_k'), stack_key(ev_s, 'win_v'),
            stack_key(od_s, 'ckv'), stack_key(od_s, 'krope'))
```

```python
import functools
import math

import jax
import jax.numpy as jnp
import numpy as np
from jax import lax
from jax.experimental import pallas as pl
from jax.experimental.pallas import tpu as pltpu

D_MODEL = 1024
DEPTH = 2
PAGE_SIZE = 128
EPS = 1e-6
NEG_INF = -1e30
D_FF = 2816
ML_H = 4
ML_DH = 128
ML_W = ML_H * ML_DH
CONV_K = 4
MLSTM_CHUNK = 64
NSA_H = 8
NSA_KV = 2
NSA_R = NSA_H // NSA_KV
NSA_DH = 64
NSA_W = NSA_H * NSA_DH
NSA_KVW = NSA_KV * NSA_DH
NSA_SCALE = NSA_DH ** -0.5
CMP_STRIDE = 16
CMP_LEN = 2 * CMP_STRIDE
CMP_HID = 2 * NSA_DH
SLC_BLOCK = 64
SLC_RATIO = SLC_BLOCK // CMP_STRIDE
SLC_OVERLAP_W = (1.0, 2.0, 2.0, 2.0, 1.0)
N_SELECT = 16
WINDOW = 512
SLC_QBLOCK = 32
WIN_QBLOCK = 128
FORCE_SCORE = 1e6
REL_BUCKETS = 32
REL_MAX_DIST = 128
MLA_H = 16
MLA_NOPE = 64
MLA_ROPE = 32
MLA_V = 64
Q_LORA = 384
KV_LORA = 256
MLA_SCALE = (MLA_NOPE + MLA_ROPE) ** -0.5
MLA_QBLOCK = 128
ROPE_THETA = 10000.0
MEM_LEN = 256
XM_H = 4
XM_DH = 128
XM_W = XM_H * XM_DH
EV_SPLITS = (ML_W, ML_W, ML_W, ML_H, ML_H, NSA_W) + (NSA_KVW,) * 6 + (NSA_H * 3,)
OD_SPLITS = (Q_LORA, KV_LORA, MLA_ROPE)

V7X_LANES = 128
V7X_VMEM_LIMIT_BYTES = 56 * 1024 * 1024
FFN_CHUNK = 256
FFN_ROWS = 512


def _ffn_kernel(x_ref, g_ref, wg_ref, wu_ref, wd_ref, o_ref, acc_ref):
    x = x_ref[...]
    h = x * lax.rsqrt(jnp.mean(x * x, axis=-1, keepdims=True) + EPS) * g_ref[...]
    hb = h.astype(jnp.bfloat16)
    acc_ref[...] = jnp.zeros_like(acc_ref)

    def body(c, carry):
        cols = pl.ds(pl.multiple_of(c * FFN_CHUNK, FFN_CHUNK), FFN_CHUNK)
        gate = jnp.dot(hb, wg_ref[:, cols], preferred_element_type=jnp.float32)
        up = jnp.dot(hb, wu_ref[:, cols], preferred_element_type=jnp.float32)
        act = (gate * jax.nn.sigmoid(gate) * up).astype(jnp.bfloat16)
        acc_ref[...] += jnp.dot(act, wd_ref[cols, :], preferred_element_type=jnp.float32)
        return carry

    lax.fori_loop(0, D_FF // FFN_CHUNK, body, 0)
    o_ref[...] = x + 0.5 * acc_ref[...]


def ffn_half(x2d, g, wg, wu, wd):
    m = x2d.shape[0]
    tm = min(FFN_ROWS, m)
    assert m % tm == 0
    resident = functools.partial(pl.BlockSpec, pipeline_mode=pl.Buffered(1))
    return pl.pallas_call(
        _ffn_kernel,
        grid=(m // tm,),
        in_specs=[
            pl.BlockSpec((tm, D_MODEL), lambda i: (i, 0)),
            resident((1, D_MODEL), lambda i: (0, 0)),
            resident((D_MODEL, D_FF), lambda i: (0, 0)),
            resident((D_MODEL, D_FF), lambda i: (0, 0)),
            resident((D_FF, D_MODEL), lambda i: (0, 0)),
        ],
        out_specs=pl.BlockSpec((tm, D_MODEL), lambda i: (i, 0)),
        out_shape=jax.ShapeDtypeStruct((m, D_MODEL), jnp.float32),
        scratch_shapes=[pltpu.VMEM((tm, D_MODEL), jnp.float32)],
        compiler_params=pltpu.CompilerParams(
            dimension_semantics=("arbitrary",), vmem_limit_bytes=V7X_VMEM_LIMIT_BYTES),
        name="ffn_half",
    )(x2d, g.reshape(1, D_MODEL), wg, wu, wd)


def swiglu_half(x, g, wg, wu, wd):
    shp = x.shape
    return ffn_half(x.reshape(-1, D_MODEL), g, wg, wu, wd).reshape(shp)


NSA_TQ = 256
NSA_NEAR = 128


def _nsa_prompt_kernel(q_ref, kc_ref, vc_ref, ks_ref, vs_ref, kw_ref, vw_ref, bc_ref, bn_ref,
                       e_ref, mw_ref, gp_ref, gb_ref, o_ref, *, n_cmp, n_select):
    f32, bf16 = jnp.float32, jnp.bfloat16
    qi = pl.program_id(2)
    R, TQ, DH = q_ref.shape[2:]
    S = ks_ref.shape[2]
    t0 = qi * TQ
    q2 = q_ref[0, 0].reshape(R * TQ, DH)
    t_col = t0 + lax.broadcasted_iota(jnp.int32, (TQ, 1), 0)

    def scores(k):
        s = lax.dot_general(q2, k, (((1,), (1,)), ((), ())), preferred_element_type=f32)
        return s.reshape(R, TQ, k.shape[0])

    ncp = kc_ref.shape[2]
    c_row = lax.broadcasted_iota(jnp.int32, (1, ncp), 1)
    mask_c = (t_col >= c_row * CMP_STRIDE + (CMP_LEN - 1)) & (c_row < n_cmp)
    s = jnp.where(mask_c[None], scores(kc_ref[0, 0]) + bc_ref[0], NEG_INF)
    e = jnp.exp(s - jnp.max(s, axis=-1, keepdims=True))
    p = e / jnp.sum(e, axis=-1, keepdims=True) * mask_c.astype(f32)[None]
    o_cmp = jnp.dot(p.reshape(R * TQ, ncp).astype(bf16), vc_ref[0, 0],
                    preferred_element_type=f32).reshape(R, TQ, DH)
    imp = jnp.sum(p, axis=0)

    mw = mw_ref[...]
    hi = imp.astype(bf16)
    r1 = imp - hi.astype(f32)
    mid = r1.astype(bf16)
    lo = (r1 - mid.astype(f32)).astype(bf16)
    p_slc = (jnp.dot(hi, mw, preferred_element_type=f32) + jnp.dot(mid, mw, preferred_element_type=f32)
             + jnp.dot(lo, mw, preferred_element_type=f32))
    ns = mw_ref.shape[1]
    jb = lax.broadcasted_iota(jnp.int32, (1, ns), 1)
    tb = jnp.right_shift(t_col, int(math.log2(SLC_BLOCK)))
    forced = (jb == 0) | (jb == tb) | (jb == tb - 1)
    score = jnp.where(jb <= tb, p_slc + FORCE_SCORE * forced.astype(f32), -1.0)
    rank = jnp.zeros((TQ, ns), f32)
    for j in range(ns):
        col = score[:, j:j + 1]
        rank = rank + ((col > score) | ((col == score) & (jb > j))).astype(f32)
    sel = (rank < n_select).astype(bf16)
    gates = jax.nn.sigmoid(gp_ref[0, 0] + gb_ref[0])
    q = q_ref[0, 0]

    def attend(r, k, v, add_mask, first_key, t_start):
        s = lax.dot_general(q[r], k, (((1,), (1,)), ((), ())), preferred_element_type=f32) + add_mask
        near_lo = max(t_start - NSA_NEAR, first_key)
        band = bn_ref[0, r][:, near_lo - (t_start - NSA_NEAR):]
        cut = near_lo - first_key
        near = s[:, cut:] + band
        s = near if cut == 0 else jnp.concatenate([s[:, :cut], near], axis=1)
        e = jnp.exp(s - jnp.max(s, axis=-1, keepdims=True))
        return jnp.dot(e.astype(bf16), v, preferred_element_type=f32) / jnp.sum(e, axis=-1, keepdims=True)

    for c in range(S // TQ):
        @pl.when(qi == c)
        def _(c=c):
            t_start, n_keys = c * TQ, (c + 1) * TQ
            row = t_start + lax.broadcasted_iota(jnp.int32, (TQ, 1), 0)
            col = lax.broadcasted_iota(jnp.int32, (1, n_keys), 1)
            sel_tok = jnp.dot(sel, e_ref[:, :n_keys], preferred_element_type=f32)
            mask_s = jnp.where((sel_tok > 0.5) & (col <= row), 0.0, NEG_INF)
            w_lo = max(t_start - WINDOW, 0)
            col_w = w_lo + lax.broadcasted_iota(jnp.int32, (1, n_keys - w_lo), 1)
            mask_w = jnp.where((col_w <= row) & (row - col_w <= WINDOW), 0.0, NEG_INF)
            ks, vs = ks_ref[0, 0, :n_keys, :], vs_ref[0, 0, :n_keys, :]
            kw, vw = kw_ref[0, 0, w_lo:n_keys, :], vw_ref[0, 0, w_lo:n_keys, :]
            outs = []
            for r in range(R):
                o_slc = attend(r, ks, vs, mask_s, 0, t_start)
                o_win = attend(r, kw, vw, mask_w, w_lo, t_start)
                outs.append(gates[:, 3 * r:3 * r + 1] * o_cmp[r] + gates[:, 3 * r + 1:3 * r + 2] * o_slc
                            + gates[:, 3 * r + 2:3 * r + 3] * o_win)
            o_ref[0] = jnp.concatenate(outs, axis=-1)


def _bias_lookup(rel_bias, dist):
    bucket = t5_bucket(dist)[..., None]
    out = jnp.zeros(dist.shape + rel_bias.shape[1:], rel_bias.dtype)
    for b in range(REL_BUCKETS):
        out = jnp.where(bucket == b, rel_bias[b], out)
    return out


def nsa_prompt(qn, k_cmp, v_cmp, ks, vs, kw, vw, g_pre, gate_b, rel_bias):
    f32, bf16 = jnp.float32, jnp.bfloat16
    B, S = qn.shape[:2]
    G, R, DH, TQ = NSA_KV, NSA_R, NSA_DH, NSA_TQ
    assert S % TQ == 0 and TQ % V7X_LANES == 0 and WINDOW % V7X_LANES == 0 and NSA_NEAR % V7X_LANES == 0
    n_cmp = k_cmp.shape[1]
    ncp = S // CMP_STRIDE
    ns = S // SLC_BLOCK
    assert n_cmp == ncp - 1
    n_select = min(N_SELECT, ns)

    def kv_layout(a, n):
        a = jnp.pad(a, ((0, 0), (0, n - a.shape[1]), (0, 0), (0, 0)))
        return jnp.transpose(a, (0, 2, 1, 3)).astype(bf16)

    q5 = jnp.transpose((qn * NSA_SCALE).reshape(B, S, G, R, DH), (0, 2, 3, 1, 4)).astype(bf16)
    kc, vc = kv_layout(k_cmp, ncp), kv_layout(v_cmp, ncp)
    ks_t, vs_t, kw_t, vw_t = (kv_layout(a, S) for a in (ks, vs, kw, vw))

    half = REL_BUCKETS // 2
    assert half + int(math.log(NSA_NEAR / half) / math.log(REL_MAX_DIST / half) * half) >= REL_BUCKETS - 1
    t = jnp.arange(S)
    dist_c = t[:, None] - (jnp.arange(ncp) * CMP_STRIDE + CMP_LEN - 1)[None, :]
    bias_c = jnp.transpose(_bias_lookup(rel_bias, dist_c).reshape(S, ncp, G, R), (2, 3, 0, 1))
    d_near = jnp.arange(TQ)[:, None] + NSA_NEAR - jnp.arange(TQ + NSA_NEAR)[None, :]
    bias_n = jnp.transpose((rel_bias[t5_bucket(d_near)] - rel_bias[REL_BUCKETS - 1]).reshape(
        TQ, TQ + NSA_NEAR, G, R), (2, 3, 0, 1))
    expand = (jnp.arange(S)[None, :] // SLC_BLOCK == jnp.arange(ns)[:, None]).astype(bf16)
    c_i = jnp.arange(ncp)[:, None]
    j_i = jnp.arange(ns)[None, :]
    mw = sum(w * (c_i == SLC_RATIO * j_i + k - 1) for k, w in enumerate(SLC_OVERLAP_W)).astype(bf16)
    gp = jnp.transpose(g_pre.reshape(B, S, G, 3 * R), (0, 2, 1, 3))
    gb = gate_b.reshape(G, 1, 3 * R)

    kv_spec = lambda n: pl.BlockSpec((1, 1, n, DH), lambda b, g, i: (b, g, 0, 0))
    return pl.pallas_call(
        functools.partial(_nsa_prompt_kernel, n_cmp=n_cmp, n_select=n_select),
        grid=(B, G, S // TQ),
        in_specs=[
            pl.BlockSpec((1, 1, R, TQ, DH), lambda b, g, i: (b, g, 0, i, 0)),
            kv_spec(ncp), kv_spec(ncp), kv_spec(S), kv_spec(S), kv_spec(S), kv_spec(S),
            pl.BlockSpec((1, R, TQ, ncp), lambda b, g, i: (g, 0, i, 0)),
            pl.BlockSpec((1, R, TQ, TQ + NSA_NEAR), lambda b, g, i: (g, 0, 0, 0)),
            pl.BlockSpec((ns, S), lambda b, g, i: (0, 0)),
            pl.BlockSpec((ncp, ns), lambda b, g, i: (0, 0)),
            pl.BlockSpec((1, 1, TQ, 3 * R), lambda b, g, i: (b, g, i, 0)),
            pl.BlockSpec((1, 1, 3 * R), lambda b, g, i: (g, 0, 0)),
        ],
        out_specs=pl.BlockSpec((1, TQ, R * DH), lambda b, g, i: (b, i, g)),
        out_shape=jax.ShapeDtypeStruct((B, S, NSA_W), f32),
        compiler_params=pltpu.CompilerParams(
            dimension_semantics=("arbitrary", "arbitrary", "arbitrary"), vmem_limit_bytes=V7X_VMEM_LIMIT_BYTES),
        name="nsa_prompt",
    )(q5, kc, vc, ks_t, vs_t, kw_t, vw_t, bias_c, bias_n, expand, mw, gp, gb)


MLA_TQ = 512
MLA_HEADS_PER_STEP = 2


def _mla_prompt_kernel(q_ref, k_ref, v_ref, o_ref, s_ref):
    f32, bf16 = jnp.float32, jnp.bfloat16
    qi = pl.program_id(2)
    HP, TQ = q_ref.shape[1:3]
    S = k_ref.shape[2]
    causal = lax.broadcasted_iota(jnp.int32, (TQ, 1), 0) >= lax.broadcasted_iota(jnp.int32, (1, TQ), 1)

    for c in range(S // TQ):
        @pl.when(qi == c)
        def _(c=c):
            n_keys = (c + 1) * TQ
            outs = []
            for h in range(HP):
                s_ref[h, :, :n_keys] = lax.dot_general(
                    q_ref[0, h], k_ref[0, h, :n_keys, :], (((1,), (1,)), ((), ())),
                    preferred_element_type=f32) * MLA_SCALE
                s_ref[h, :, n_keys - TQ:n_keys] = jnp.where(causal, s_ref[h, :, n_keys - TQ:n_keys], NEG_INF)
                s = s_ref[h, :, :n_keys]
                e = jnp.exp(s - jnp.max(s, axis=-1, keepdims=True))
                pv = jnp.dot(e.astype(bf16), v_ref[0, h, :n_keys, :], preferred_element_type=f32)
                outs.append(pv / jnp.sum(e, axis=-1, keepdims=True))
            o_ref[0] = jnp.concatenate(outs, axis=-1)


def mla_prompt_attention(q_cat, k_cat, v):
    B, H, S, DQ = q_cat.shape
    HP, TQ = MLA_HEADS_PER_STEP, min(MLA_TQ, S)
    assert S % TQ == 0 and H % HP == 0 and HP * MLA_V == V7X_LANES
    return pl.pallas_call(
        _mla_prompt_kernel,
        grid=(B, H // HP, S // TQ),
        in_specs=[
            pl.BlockSpec((1, HP, TQ, DQ), lambda b, h, i: (b, h, i, 0)),
            pl.BlockSpec((1, HP, S, DQ), lambda b, h, i: (b, h, 0, 0)),
            pl.BlockSpec((1, HP, S, MLA_V), lambda b, h, i: (b, h, 0, 0)),
        ],
        out_specs=pl.BlockSpec((1, TQ, HP * MLA_V), lambda b, h, i: (b, i, h)),
        out_shape=jax.ShapeDtypeStruct((B, S, H * MLA_V), jnp.float32),
        scratch_shapes=[pltpu.VMEM((HP, TQ, S), jnp.float32)],
        compiler_params=pltpu.CompilerParams(
            dimension_semantics=("arbitrary", "arbitrary", "arbitrary"), vmem_limit_bytes=V7X_VMEM_LIMIT_BYTES),
        name="mla_prompt",
    )(q_cat, k_cat, v)


DECODE_XPOSE_UNROLL = 8


def _paged_copy(pool_ref, page, buf_ref, slot, p, sem_ref):
    return pltpu.make_async_copy(pool_ref.at[page], buf_ref.at[slot, p], sem_ref.at[slot])


def _paged_pipeline(pt_ref, pools, bufs, sems):
    b = pl.program_id(0)
    n_pages = pt_ref.shape[1]

    def start(seq, slot):
        def body(p, carry):
            for pool, buf, sem in zip(pools, bufs, sems):
                _paged_copy(pool, pt_ref[seq, p], buf, slot, p, sem).start()
            return carry
        lax.fori_loop(0, n_pages, body, 0)

    @pl.when(b == 0)
    def _():
        start(0, 0)

    @pl.when(b + 1 < pl.num_programs(0))
    def _():
        start(b + 1, (b + 1) % 2)

    slot = b % 2

    def wait_body(p, carry):
        for pool, buf, sem in zip(pools, bufs, sems):
            _paged_copy(pool, 0, buf, slot, p, sem).wait()
        return carry
    lax.fori_loop(0, n_pages, wait_body, 0)
    return slot


def _softmax_rows(s, valid):
    e = jnp.exp(s - jnp.max(s, axis=-1, keepdims=True))
    return e / jnp.sum(e, axis=-1, keepdims=True) * valid


def _nsa_decode_cmp_kernel(pt_ref, qbd_ref, wk_ref, wv_ref, pek_ref, pev_ref, w2k_ref, w2v_ref, gk_ref,
                           bc_ref, mw_ref, kpool_ref, vpool_ref, ocmp_ref, sel_ref,
                           kbuf, vbuf, xrm_ref, ksem, vsem, *, n_cmp, n_blocks, n_select):
    f32, bf16 = jnp.float32, jnp.bfloat16
    slot = _paged_pipeline(pt_ref, (kpool_ref, vpool_ref), (kbuf, vbuf), (ksem, vsem))
    n_pages = pt_ref.shape[1]
    n_chunk = n_pages * (PAGE_SIZE // CMP_STRIDE)
    G, DH, HID = NSA_KV, NSA_DH, CMP_HID

    def summaries(buf, w_ref, pe_ref, w2_ref):
        def xpose(i, carry):
            for u in range(DECODE_XPOSE_UNROLL):
                p = i * DECODE_XPOSE_UNROLL + u
                xrm_ref[pl.ds(pl.multiple_of(p * PAGE_SIZE, PAGE_SIZE), PAGE_SIZE), :] = buf[slot, p].T
            return carry
        lax.fori_loop(0, n_pages // DECODE_XPOSE_UNROLL, xpose, 0)
        acc = jnp.zeros((n_chunk, 2 * G * HID), f32)
        for j in range(0, CMP_STRIDE, 2):
            rows = jnp.concatenate([xrm_ref[pl.ds(j + u, n_chunk, stride=CMP_STRIDE), :].astype(bf16)
                                    for u in range(2)], axis=1)
            acc = acc + jnp.dot(rows, w_ref[j // 2], preferred_element_type=f32)
        lo, hi = acc[:, :G * HID], acc[:, G * HID:]
        pre = lo + pltpu.roll(hi, n_chunk - 1, 0) + pe_ref[...]
        hid = pre * jax.nn.sigmoid(pre)
        return jnp.dot(hid.astype(bf16), w2_ref[...], preferred_element_type=f32)

    k_sum = summaries(kbuf, wk_ref, pek_ref, w2k_ref)
    lane = lax.broadcasted_iota(jnp.int32, (1, G * DH), 1)
    sq = k_sum * k_sum
    s_all = jnp.sum(sq, axis=-1, keepdims=True)
    s_g0 = jnp.sum(jnp.where(lane < DH, sq, 0.0), axis=-1, keepdims=True)
    ms = jnp.where(lane < DH, s_g0, s_all - s_g0) * (1.0 / DH)
    k_cmp = (k_sum * lax.rsqrt(ms + EPS) * gk_ref[...]).astype(bf16)
    v_cmp = summaries(vbuf, wv_ref, pev_ref, w2v_ref).astype(bf16)

    qbd = qbd_ref[0]
    H = qbd.shape[0]
    c_row = lax.broadcasted_iota(jnp.int32, (1, n_chunk), 1)
    valid = (c_row < n_cmp).astype(f32)
    s = lax.dot_general(qbd, k_cmp, (((1,), (1,)), ((), ())), preferred_element_type=f32) + bc_ref[...]
    p = _softmax_rows(jnp.where(valid > 0.5, s, NEG_INF), valid)
    ocmp_ref[0] = jnp.dot(p.astype(bf16), v_cmp, preferred_element_type=f32)

    R = H // G
    head = lax.broadcasted_iota(jnp.int32, (H, 1), 0)
    imp = jnp.where(head < R, jnp.sum(p[:R], axis=0, keepdims=True), jnp.sum(p[R:], axis=0, keepdims=True))
    mw = mw_ref[...]
    hi_p = imp.astype(bf16)
    r1 = imp - hi_p.astype(f32)
    mid_p = r1.astype(bf16)
    lo_p = (r1 - mid_p.astype(f32)).astype(bf16)
    p_slc = (jnp.dot(hi_p, mw, preferred_element_type=f32) + jnp.dot(mid_p, mw, preferred_element_type=f32)
             + jnp.dot(lo_p, mw, preferred_element_type=f32))
    nsp = mw_ref.shape[1]
    tb = n_blocks - 1
    jb = lax.broadcasted_iota(jnp.int32, (1, nsp), 1)
    forced = (jb == 0) | (jb == tb) | (jb == tb - 1)
    score = jnp.where(jb <= tb, p_slc + FORCE_SCORE * forced.astype(f32), -1.0)
    j_col = lax.broadcasted_iota(jnp.int32, (nsp, 1), 0)
    sels = []
    for g in range(G):
        row = score[g * R:g * R + 1, :]
        col = jnp.broadcast_to(row, (nsp, nsp)).T
        beats = (col > row) | ((col == row) & (j_col < jb))
        rank = jnp.sum(beats.astype(f32), axis=0, keepdims=True)
        sels.append(jnp.broadcast_to((rank < n_select).astype(f32), (R, nsp)))
    sel_ref[0] = jnp.concatenate(sels, axis=0)


def _nsa_decode_attn_kernel(pt_ref, qbd_ref, sel_ref, ocmp_ref, gate_ref, new_ref, e_ref, bs_ref, bw_ref, b0_ref,
                            kw_ref, vw_ref, kpool_ref, vpool_ref, o_ref, kbuf, vbuf, s_ref, ksem, vsem):
    f32, bf16 = jnp.float32, jnp.bfloat16
    slot = _paged_pipeline(pt_ref, (kpool_ref, vpool_ref), (kbuf, vbuf), (ksem, vsem))
    n_pages = pt_ref.shape[1]
    qbd = qbd_ref[0]
    qf = qbd.astype(f32)
    new = new_ref[0]
    b0 = b0_ref[...]

    def new_score(k_row):
        return jnp.sum(qf * k_row.astype(bf16).astype(f32), axis=-1, keepdims=True) + b0

    for p in range(n_pages):
        s_ref[:, p * PAGE_SIZE:(p + 1) * PAGE_SIZE] = jnp.dot(qbd, kbuf[slot, p].astype(bf16),
                                                              preferred_element_type=f32)
    n_blk_past = e_ref.shape[0]
    sel_tok = jnp.dot(sel_ref[0][:, :n_blk_past].astype(bf16), e_ref[...], preferred_element_type=f32)
    s_past = jnp.where(sel_tok > 0.5, s_ref[...] + bs_ref[...], NEG_INF)
    s_new = new_score(new[0:1])
    m = jnp.maximum(jnp.max(s_past, axis=-1, keepdims=True), s_new)
    e_new = jnp.exp(s_new - m)
    s_ref[...] = jnp.exp(s_past - m)
    denom = jnp.sum(s_ref[...], axis=-1, keepdims=True) + e_new
    acc = e_new.astype(bf16).astype(f32) * new[1:2].astype(bf16).astype(f32)
    for p in range(n_pages):
        pe = s_ref[:, p * PAGE_SIZE:(p + 1) * PAGE_SIZE].astype(bf16)
        acc = acc + lax.dot_general(pe, vbuf[slot, p].astype(bf16), (((1,), (1,)), ((), ())),
                                    preferred_element_type=f32)
    o_slc = acc / denom

    s_w = jnp.dot(qbd, kw_ref[0].astype(bf16), preferred_element_type=f32) + bw_ref[...]
    s_wn = new_score(new[2:3])
    m = jnp.maximum(jnp.max(s_w, axis=-1, keepdims=True), s_wn)
    e_w, e_wn = jnp.exp(s_w - m), jnp.exp(s_wn - m)
    denom = jnp.sum(e_w, axis=-1, keepdims=True) + e_wn
    acc = (lax.dot_general(e_w.astype(bf16), vw_ref[0].astype(bf16), (((1,), (1,)), ((), ())),
                           preferred_element_type=f32)
           + e_wn.astype(bf16).astype(f32) * new[3:4].astype(bf16).astype(f32))
    o_win = acc / denom

    gates = jax.nn.sigmoid(gate_ref[0])
    o_ref[0] = gates[:, 0:1] * ocmp_ref[0] + gates[:, 1:2] * o_slc + gates[:, 2:3] * o_win


def _block_diag_heads(x):
    B, H, DH = x.shape
    g_of_h = jnp.arange(H) // (H // NSA_KV)
    onehot = (g_of_h[:, None] == jnp.arange(NSA_KV)[None, :]).astype(x.dtype)
    return (x[:, :, None, :] * onehot[None, :, :, None]).reshape(B, H, NSA_KV * DH)


def nsa_decode(qn, ks, vs, kw, vw, g_pre, gate_b, rel_bias, page_table, cmp_k_pool, cmp_v_pool,
               slc_k_pool, slc_v_pool, win_k, win_v, pe_k, w1_k, w2_k, pe_v, w1_v, w2_v, gk_cmp):
    f32, bf16 = jnp.float32, jnp.bfloat16
    B, n_pages = page_table.shape
    G, R, DH, H = NSA_KV, NSA_R, NSA_DH, NSA_H
    GD = G * DH
    past = n_pages * PAGE_SIZE
    n_chunk = past // CMP_STRIDE
    n_cmp = (past + 1 - CMP_LEN) // CMP_STRIDE + 1
    n_blocks = -(-(past + 1) // SLC_BLOCK)
    n_blk_past = past // SLC_BLOCK
    nsp = -(-n_blocks // V7X_LANES) * V7X_LANES
    n_win = win_k.shape[1]
    assert GD == V7X_LANES and PAGE_SIZE == V7X_LANES and n_blocks >= N_SELECT and n_win == WINDOW

    def pool_view(pool):
        return jnp.transpose(pool, (0, 2, 3, 1)).reshape(pool.shape[0], GD, PAGE_SIZE)
    kc_pool, vc_pool, ks_pool, vs_pool = (pool_view(a) for a in (cmp_k_pool, cmp_v_pool, slc_k_pool, slc_v_pool))
    kw_t = jnp.transpose(win_k, (0, 2, 3, 1)).reshape(B, GD, n_win)
    vw_t = jnp.transpose(win_v, (0, 2, 3, 1)).reshape(B, GD, n_win)

    qbd = _block_diag_heads((qn[:, 0] * NSA_SCALE)).astype(bf16)
    eye = jnp.eye(G, dtype=f32)

    def chunk_weights(w1):
        def bd(w):
            return jnp.einsum('jdh,gk->jgdkh', w, eye).reshape(CMP_STRIDE, GD, G * CMP_HID)
        w = jnp.concatenate([bd(w1[:CMP_STRIDE]), bd(w1[CMP_STRIDE:])], axis=-1)
        return w.reshape(CMP_STRIDE // 2, 2 * GD, 2 * G * CMP_HID).astype(bf16)

    def pe_term(pe, w1):
        return jnp.tile(jnp.einsum('jd,jdh->h', pe, w1), G).reshape(1, G * CMP_HID)

    def w2_bd(w2):
        return jnp.einsum('hd,gk->ghkd', w2, eye).reshape(G * CMP_HID, GD).astype(bf16)

    c_i = jnp.arange(n_chunk)
    bias_c = rel_bias[t5_bucket(past - (c_i * CMP_STRIDE + CMP_LEN - 1))].T
    j_i = jnp.arange(nsp)[None, :]
    mw = sum(w * ((c_i[:, None] == SLC_RATIO * j_i + k - 1) & (j_i < n_blocks))
             for k, w in enumerate(SLC_OVERLAP_W)).astype(bf16)
    tok = jnp.arange(past)
    expand = (tok[None, :] // SLC_BLOCK == jnp.arange(n_blk_past)[:, None]).astype(bf16)
    bias_s = rel_bias[t5_bucket(past - tok)].T
    bias_w = rel_bias[t5_bucket(n_win - jnp.arange(n_win))].T
    bias_0 = rel_bias[t5_bucket(jnp.zeros((1,), jnp.int32))].T
    new_rows = jnp.stack([a.reshape(B, GD) for a in (ks, vs, kw, vw)], axis=1)
    gate_in = g_pre.reshape(B, H, 3) + gate_b

    const = lambda shape: pl.BlockSpec(shape, lambda b, pt: (0,) * len(shape))
    per_seq = lambda shape: pl.BlockSpec((1,) + shape, lambda b, pt: (b,) + (0,) * len(shape))
    any_spec = pl.BlockSpec(memory_space=pl.ANY)
    page_buf = pltpu.VMEM((2, n_pages, GD, PAGE_SIZE), f32)
    params = pltpu.CompilerParams(dimension_semantics=("arbitrary",), vmem_limit_bytes=V7X_VMEM_LIMIT_BYTES)

    o_cmp, sel = pl.pallas_call(
        functools.partial(_nsa_decode_cmp_kernel, n_cmp=n_cmp, n_blocks=n_blocks, n_select=min(N_SELECT, n_blocks)),
        grid_spec=pltpu.PrefetchScalarGridSpec(
            num_scalar_prefetch=1, grid=(B,),
            in_specs=[per_seq((H, GD)),
                      const((CMP_STRIDE // 2, 2 * GD, 2 * G * CMP_HID)),
                      const((CMP_STRIDE // 2, 2 * GD, 2 * G * CMP_HID)),
                      const((1, G * CMP_HID)), const((1, G * CMP_HID)),
                      const((G * CMP_HID, GD)), const((G * CMP_HID, GD)), const((1, GD)),
                      const((H, n_chunk)), const((n_chunk, nsp)), any_spec, any_spec],
            out_specs=[per_seq((H, GD)), per_seq((H, nsp))],
            scratch_shapes=[page_buf, page_buf, pltpu.VMEM((past, GD), f32),
                            pltpu.SemaphoreType.DMA((2,)), pltpu.SemaphoreType.DMA((2,))]),
        out_shape=[jax.ShapeDtypeStruct((B, H, GD), f32), jax.ShapeDtypeStruct((B, H, nsp), f32)],
        compiler_params=params, name="nsa_decode_cmp",
    )(page_table, qbd, chunk_weights(w1_k), chunk_weights(w1_v), pe_term(pe_k, w1_k), pe_term(pe_v, w1_v),
      w2_bd(w2_k), w2_bd(w2_v), jnp.tile(gk_cmp, G).reshape(1, GD), bias_c, mw, kc_pool, vc_pool)

    out = pl.pallas_call(
        _nsa_decode_attn_kernel,
        grid_spec=pltpu.PrefetchScalarGridSpec(
            num_scalar_prefetch=1, grid=(B,),
            in_specs=[per_seq((H, GD)), per_seq((H, nsp)), per_seq((H, GD)), per_seq((H, 3)), per_seq((4, GD)),
                      const((n_blk_past, past)), const((H, past)), const((H, n_win)), const((H, 1)),
                      per_seq((GD, n_win)), per_seq((GD, n_win)), any_spec, any_spec],
            out_specs=per_seq((H, GD)),
            scratch_shapes=[page_buf, page_buf, pltpu.VMEM((H, past), f32),
                            pltpu.SemaphoreType.DMA((2,)), pltpu.SemaphoreType.DMA((2,))]),
        out_shape=jax.ShapeDtypeStruct((B, H, GD), f32),
        compiler_params=params, name="nsa_decode_attn",
    )(page_table, qbd, sel, o_cmp, gate_in, new_rows, expand, bias_s, bias_w, bias_0, kw_t, vw_t, ks_pool, vs_pool)

    out = out.reshape(B, G, R, G, DH)
    h_b = jnp.stack([out[:, g, :, g, :] for g in range(G)], axis=1)
    return h_b.reshape(B, 1, NSA_W)


def _mla_decode_kernel(pt_ref, ql_ref, qr_ref, new_ref, newr_ref, cpool_ref, rpool_ref, o_ref,
                       cbuf, rbuf, cb16, s_ref, csem, rsem):
    f32, bf16 = jnp.float32, jnp.bfloat16
    slot = _paged_pipeline(pt_ref, (cpool_ref, rpool_ref), (cbuf, rbuf), (csem, rsem))
    n_pages = pt_ref.shape[1]
    ql, qr = ql_ref[0], qr_ref[0]
    for p in range(n_pages):
        c16 = cbuf[slot, p].astype(bf16)
        cb16[p] = c16
        s_ref[:, p * PAGE_SIZE:(p + 1) * PAGE_SIZE] = (
            lax.dot_general(ql, c16, (((1,), (1,)), ((), ())), preferred_element_type=f32)
            + jnp.dot(qr, rbuf[slot, p].astype(bf16), preferred_element_type=f32)) * MLA_SCALE
    c_new = new_ref[0].astype(bf16).astype(f32)
    r_new = newr_ref[0].astype(bf16).astype(f32)
    s_new = (jnp.sum(ql.astype(f32) * c_new, axis=-1, keepdims=True)
             + jnp.sum(qr.astype(f32) * r_new, axis=-1, keepdims=True)) * MLA_SCALE
    s_past = s_ref[...]
    m = jnp.maximum(jnp.max(s_past, axis=-1, keepdims=True), s_new)
    e_new = jnp.exp(s_new - m)
    s_ref[...] = jnp.exp(s_past - m)
    denom = jnp.sum(s_ref[...], axis=-1, keepdims=True) + e_new
    inv = 1.0 / denom
    acc = (e_new * inv).astype(bf16).astype(f32) * c_new
    for p in range(n_pages):
        pe = (s_ref[:, p * PAGE_SIZE:(p + 1) * PAGE_SIZE] * inv).astype(bf16)
        acc = acc + jnp.dot(pe, cb16[p], preferred_element_type=f32)
    o_ref[0] = acc


def mla_decode_attention(q_lat, q_rope, ckv_new, kr_new, page_table, ckv_pool, krope_pool):
    f32, bf16 = jnp.float32, jnp.bfloat16
    B, n_pages = page_table.shape
    H = q_lat.shape[1]
    past = n_pages * PAGE_SIZE
    rpool_t = jnp.transpose(krope_pool, (0, 2, 1))
    per_seq = lambda shape: pl.BlockSpec((1,) + shape, lambda b, pt: (b,) + (0,) * len(shape))
    any_spec = pl.BlockSpec(memory_space=pl.ANY)
    return pl.pallas_call(
        _mla_decode_kernel,
        grid_spec=pltpu.PrefetchScalarGridSpec(
            num_scalar_prefetch=1, grid=(B,),
            in_specs=[per_seq((H, KV_LORA)), per_seq((H, MLA_ROPE)), per_seq((1, KV_LORA)), per_seq((1, MLA_ROPE)),
                      any_spec, any_spec],
            out_specs=per_seq((H, KV_LORA)),
            scratch_shapes=[pltpu.VMEM((2, n_pages, PAGE_SIZE, KV_LORA), f32),
                            pltpu.VMEM((2, n_pages, MLA_ROPE, PAGE_SIZE), f32),
                            pltpu.VMEM((n_pages, PAGE_SIZE, KV_LORA), bf16),
                            pltpu.VMEM((H, past), f32),
                            pltpu.SemaphoreType.DMA((2,)), pltpu.SemaphoreType.DMA((2,))]),
        out_shape=jax.ShapeDtypeStruct((B, H, KV_LORA), f32),
        compiler_params=pltpu.CompilerParams(dimension_semantics=("arbitrary",),
                                             vmem_limit_bytes=V7X_VMEM_LIMIT_BYTES),
        name="mla_decode",
    )(page_table, q_lat.astype(bf16), q_rope.astype(bf16), ckv_new.reshape(B, 1, KV_LORA),
      kr_new.reshape(B, 1, MLA_ROPE), ckv_pool, rpool_t)


def split_cols(a, sizes):
    idx = [int(s) for s in np.cumsum(sizes)[:-1]]
    return jnp.split(a, idx, axis=-1)


def rms_norm(x, g):
    xf = x.astype(jnp.float32)
    y = xf * lax.rsqrt(jnp.mean(xf * xf, axis=-1, keepdims=True) + EPS)
    return (y * g.astype(jnp.float32)).astype(x.dtype)


def t5_bucket(dist):
    n = jnp.maximum(dist, 0)
    exact = REL_BUCKETS // 2
    nf = jnp.maximum(n, exact).astype(jnp.float32)
    large = exact + (jnp.log(nf / exact) / math.log(REL_MAX_DIST / exact) * (REL_BUCKETS - exact)).astype(jnp.int32)
    return jnp.where(n < exact, n, jnp.minimum(large, REL_BUCKETS - 1))


def apply_rope(x, pos):
    half = x.shape[-1] // 2
    inv = ROPE_THETA ** (-jnp.arange(half, dtype=jnp.float32) / half)
    ang = pos.astype(jnp.float32)[:, None] * inv[None, :]
    ang = ang.reshape(ang.shape[:1] + (1,) * (x.ndim - 3) + (half,))
    cos, sin = jnp.cos(ang).astype(x.dtype), jnp.sin(ang).astype(x.dtype)
    x1, x2 = x[..., :half], x[..., half:]
    return jnp.concatenate([x1 * cos - x2 * sin, x1 * sin + x2 * cos], axis=-1)


def causal_conv(u, buf, w, b):
    S = u.shape[1]
    full = jnp.concatenate([buf.astype(u.dtype), u], axis=1)
    out = b + sum(full[:, j:j + S] * w[j] for j in range(CONV_K))
    return out, full[:, S:]


def mlstm_chunkwise(q, k, v, i_pre, logf, C0, n0, m0):
    f32 = jnp.float32
    q, k, v, i_pre, logf = (a.astype(f32) for a in (q, k, v, i_pre, logf))
    B, H, S, D = q.shape
    L = MLSTM_CHUNK if S % MLSTM_CHUNK == 0 else S
    NC = S // L

    def chunks(a):
        return jnp.moveaxis(a.reshape((B, H, NC, L) + a.shape[3:]), 2, 0)

    causal = jnp.tril(jnp.ones((L, L), dtype=bool))

    def step(carry, inp):
        C, n, m = carry
        qc, kc, vc, ic, fc = inp
        b = jnp.cumsum(fc, axis=-1)
        g = b + m[..., None]
        dmat = jnp.where(causal, b[..., :, None] - b[..., None, :] + ic[..., None, :], -jnp.inf)
        mt = jnp.maximum(g, jnp.max(dmat, axis=-1))
        inter = jnp.exp(g - mt)
        sqk = jnp.einsum('bhtd,bhsd->bhts', qc, kc) * jnp.exp(dmat - mt[..., None])
        num = inter[..., None] * jnp.einsum('bhvd,bhtd->bhtv', C, qc) + jnp.einsum('bhts,bhsv->bhtv', sqk, vc)
        den = inter * jnp.einsum('bhd,bhtd->bht', n, qc) + jnp.sum(sqk, axis=-1)
        h = num / jnp.maximum(jnp.abs(den), jnp.exp(-mt))[..., None]
        b_end = b[..., -1]
        w_log = b_end[..., None] - b + ic
        m_new = jnp.maximum(b_end + m, jnp.max(w_log, axis=-1))
        decay = jnp.exp(b_end + m - m_new)
        w_in = jnp.exp(w_log - m_new[..., None])
        C_new = decay[..., None, None] * C + jnp.einsum('bhs,bhsv,bhsd->bhvd', w_in, vc, kc)
        n_new = decay[..., None] * n + jnp.einsum('bhs,bhsd->bhd', w_in, kc)
        return (C_new, n_new, m_new), h

    (C1, n1, m1), hs = lax.scan(step, (C0.astype(f32), n0.astype(f32), m0.astype(f32)),
                                tuple(chunks(a) for a in (q, k, v, i_pre, logf)))
    return jnp.moveaxis(hs, 0, 2).reshape(B, H, S, D), C1, n1, m1


def to_chunks(a):
    B, T = a.shape[:2]
    pad = (-T) % CMP_STRIDE
    a = jnp.pad(a, ((0, 0), (0, pad), (0, 0), (0, 0)))
    return a.reshape((B, (T + pad) // CMP_STRIDE, CMP_STRIDE) + a.shape[2:])


def cmp_summaries(chunk_list, T, pe, w1, w2):
    lo = jnp.concatenate([jnp.einsum('bcjgd,jdh->bcgh', r, w1[:CMP_STRIDE]) for r in chunk_list], axis=1)
    hi = jnp.concatenate([jnp.einsum('bcjgd,jdh->bcgh', r, w1[CMP_STRIDE:]) for r in chunk_list], axis=1)
    n_cmp = (T - CMP_LEN) // CMP_STRIDE + 1
    hid = jax.nn.silu(lo[:, :n_cmp] + hi[:, 1:n_cmp + 1] + jnp.einsum('jd,jdh->h', pe, w1))
    return hid @ w2


def even_mixer(xn, pos0, past, w_in, w_out, conv_w, conv_b, ml_wq, ml_wk, ml_b_i, ml_b_f, ml_out_g,
               nsa_gq, nsa_gk_cmp, nsa_gk_slc, nsa_gk_win, pe_k, w1_k, w2_k, pe_v, w1_v, w2_v, gate_b, rel_bias):
    B, S, _ = xn.shape
    dt = xn.dtype
    (u, v_m, o_pre, i_pre, f_pre, q, kc, vc, ks, vs, kw, vw, g_pre) = split_cols(xn @ w_in, EV_SPLITS)

    if past is None:
        conv_buf = jnp.zeros((B, CONV_K - 1, ML_W), dt)
        C0 = jnp.zeros((B, ML_H, ML_DH, ML_DH), jnp.float32)
        n0 = jnp.zeros((B, ML_H, ML_DH), jnp.float32)
        m0 = jnp.zeros((B, ML_H), jnp.float32)
    else:
        conv_buf, C0, n0, m0 = past['conv'], past['C'], past['n'], past['m']
    c, conv_new = causal_conv(u, conv_buf, conv_w, conv_b)
    ch = jax.nn.silu(c).reshape(B, S, ML_H, ML_DH)
    qm = jnp.einsum('bshd,hde->bhse', ch, ml_wq)
    km = jnp.einsum('bshd,hde->bhse', ch, ml_wk) * (ML_DH ** -0.5)
    vm = jnp.transpose(v_m.reshape(B, S, ML_H, ML_DH), (0, 2, 1, 3))
    ig = jnp.transpose(i_pre + ml_b_i, (0, 2, 1))
    lf = jax.nn.log_sigmoid(jnp.transpose(f_pre + ml_b_f, (0, 2, 1)).astype(jnp.float32))
    hm, C1, n1, m1 = mlstm_chunkwise(qm, km, vm, ig, lf, C0, n0, m0)
    hm = rms_norm(jnp.transpose(hm, (0, 2, 1, 3)).astype(dt), ml_out_g) * jax.nn.sigmoid(o_pre).reshape(B, S, ML_H, ML_DH)
    h_a = hm.reshape(B, S, ML_W)

    q = rms_norm(q.reshape(B, S, NSA_H, NSA_DH), nsa_gq)
    kv_shape = (B, S, NSA_KV, NSA_DH)
    kc, vc, vs, vw = (a.reshape(kv_shape) for a in (kc, vc, vs, vw))
    ks = rms_norm(ks.reshape(kv_shape), nsa_gk_slc)
    kw = rms_norm(kw.reshape(kv_shape), nsa_gk_win)

    if past is None:
        k_cmp = rms_norm(cmp_summaries([to_chunks(kc)], S, pe_k, w1_k, w2_k), nsa_gk_cmp)
        v_cmp = cmp_summaries([to_chunks(vc)], S, pe_v, w1_v, w2_v)
        h_b = nsa_prompt(q, k_cmp, v_cmp, ks, vs, kw, vw, g_pre, gate_b, rel_bias)
        nb = min(WINDOW, S)
        win_k_new, win_v_new = kw[:, S - nb:], vw[:, S - nb:]
    else:
        assert S == 1
        e = past['e']
        h_b = nsa_decode(q, ks, vs, kw, vw, g_pre, gate_b, rel_bias, past['page_table'],
                         past['cmp_k'][e], past['cmp_v'][e], past['slc_k'][e], past['slc_v'][e],
                         past['win_k'], past['win_v'], pe_k, w1_k, w2_k, pe_v, w1_v, w2_v, nsa_gk_cmp)
        win_k_new = jnp.concatenate([past['win_k'][:, S:], kw], axis=1)
        win_v_new = jnp.concatenate([past['win_v'][:, S:], vw], axis=1)
    out = jnp.concatenate([h_a, h_b], axis=-1) @ w_out
    new = dict(C=C1.astype(dt), n=n1.astype(dt), m=m1.astype(dt), conv=conv_new,
               cmp_k=kc, cmp_v=vc, slc_k=ks, slc_v=vs, win_k=win_k_new, win_v=win_v_new)
    return out, new


def odd_mixer(xn, pos0, past, w_in, g_cq, w_uq, g_q, g_ckv, g_kr, w_uk, w_uv, w_out):
    B, S, _ = xn.shape
    cq, ckv, kr = split_cols(xn @ w_in, OD_SPLITS)
    tq = pos0 + jnp.arange(S)
    q = rms_norm((rms_norm(cq, g_cq) @ w_uq).reshape(B, S, MLA_H, MLA_NOPE + MLA_ROPE), g_q)
    q_nope = q[..., :MLA_NOPE]
    q_rope = apply_rope(q[..., MLA_NOPE:], tq)
    ckv = rms_norm(ckv, g_ckv)
    kr = apply_rope(rms_norm(kr, g_kr), tq)
    if past is None:
        k_nope = jnp.einsum('bsc,chn->bshn', ckv, w_uk)
        v = jnp.einsum('bsc,chv->bshv', ckv, w_uv)
        bf16 = jnp.bfloat16
        q_cat = jnp.transpose(jnp.concatenate([q_nope, q_rope], axis=-1), (0, 2, 1, 3)).astype(bf16)
        kr_h = jnp.broadcast_to(kr[:, :, None, :], (B, S, MLA_H, MLA_ROPE))
        k_cat = jnp.transpose(jnp.concatenate([k_nope, kr_h], axis=-1), (0, 2, 1, 3)).astype(bf16)
        o = mla_prompt_attention(q_cat, k_cat, jnp.transpose(v, (0, 2, 1, 3)).astype(bf16))
    else:
        assert S == 1
        e = past['e']
        q_lat = jnp.einsum('bqhn,chn->bqhc', q_nope, w_uk)
        o_lat = mla_decode_attention(q_lat[:, 0], q_rope[:, 0], ckv[:, 0], kr[:, 0], past['page_table'],
                                     past['ckv'][e], past['krope'][e])[:, None]
        o = jnp.einsum('bqhc,chv->bqhv', o_lat, w_uv).reshape(B, S, MLA_H * MLA_V)
    return o @ w_out, dict(ckv=ckv, krope=kr)


def mem_kv(mem, g_mem, wk, wv, gk):
    B, M, _ = mem.shape
    m = rms_norm(mem, g_mem)
    k = rms_norm((m @ wk).reshape(B, M, XM_H, XM_DH), gk)
    v = (m @ wv).reshape(B, M, XM_H, XM_DH)
    return k, v


def mem_attend(xn, k, v, wq, gq, wo):
    B, S, _ = xn.shape
    q = rms_norm((xn @ wq).reshape(B, S, XM_H, XM_DH), gq)
    s = jnp.einsum('bshd,bmhd->bhsm', q, k.astype(q.dtype)).astype(jnp.float32) * (XM_DH ** -0.5)
    p = jax.nn.softmax(s, axis=-1)
    return jnp.einsum('bhsm,bmhd->bshd', p.astype(xn.dtype), v.astype(xn.dtype)).reshape(B, S, XM_W) @ wo


def stack_key(lst, name):
    return jnp.stack([d[name] for d in lst])


def kernel(x_prompt, x_sample, mem_prompt,
           state_ml_C, state_ml_n, state_ml_m, state_ml_conv,
           cache_cmp_k, cache_cmp_v, cache_slc_k, cache_slc_v, cache_win_k, cache_win_v,
           cache_mla_ckv, cache_mla_krope, cache_mem_k, cache_mem_v, page_table,
           rel_bias, ffn1_norm, ffn1_wg, ffn1_wu, ffn1_wd, mix_norm,
           xm_norm, xm_mem_norm, xm_wq, xm_wk, xm_wv, xm_wo, xm_gq, xm_gk,
           ffn2_norm, ffn2_wg, ffn2_wu, ffn2_wd,
           ev_w_in, ev_w_out, ml_conv_w, ml_conv_b, ml_wq, ml_wk, ml_b_i, ml_b_f, ml_out_g,
           nsa_gq, nsa_gk_cmp, nsa_gk_slc, nsa_gk_win, cmp_pe_k, cmp_w1_k, cmp_w2_k,
           cmp_pe_v, cmp_w1_v, cmp_w2_v, nsa_gate_b,
           od_w_in, mla_g_cq, mla_w_uq, mla_g_q, mla_g_ckv, mla_g_kr, mla_w_uk, mla_w_uv, od_w_out):
    past_len = page_table.shape[1] * PAGE_SIZE
    bf = jnp.bfloat16
    ffn_w = [[(n[layer], wg[layer].astype(bf), wu[layer].astype(bf), wd[layer].astype(bf))
              for n, wg, wu, wd in ((ffn1_norm, ffn1_wg, ffn1_wu, ffn1_wd), (ffn2_norm, ffn2_wg, ffn2_wu, ffn2_wd))]
             for layer in range(DEPTH)]

    def run_group(y, prompt):
        ev, od, memk, memv = [], [], [], []
        for layer in range(DEPTH):
            y = swiglu_half(y, *ffn_w[layer][0])
            xn = rms_norm(y, mix_norm[layer])
            if layer % 2 == 0:
                e = layer // 2
                ew = dict(w_in=ev_w_in[e], w_out=ev_w_out[e], conv_w=ml_conv_w[e], conv_b=ml_conv_b[e],
                          ml_wq=ml_wq[e], ml_wk=ml_wk[e], ml_b_i=ml_b_i[e], ml_b_f=ml_b_f[e], ml_out_g=ml_out_g[e],
                          nsa_gq=nsa_gq[e], nsa_gk_cmp=nsa_gk_cmp[e], nsa_gk_slc=nsa_gk_slc[e],
                          nsa_gk_win=nsa_gk_win[e], pe_k=cmp_pe_k[e], w1_k=cmp_w1_k[e], w2_k=cmp_w2_k[e],
                          pe_v=cmp_pe_v[e], w1_v=cmp_w1_v[e], w2_v=cmp_w2_v[e],
                          gate_b=nsa_gate_b[e], rel_bias=rel_bias)
                past = None if prompt else dict(
                    e=e, page_table=page_table, C=state_ml_C[e], n=state_ml_n[e], m=state_ml_m[e],
                    conv=state_ml_conv[e], cmp_k=cache_cmp_k, cmp_v=cache_cmp_v,
                    slc_k=cache_slc_k, slc_v=cache_slc_v, win_k=cache_win_k[e], win_v=cache_win_v[e])
                h, st = even_mixer(xn, 0 if prompt else past_len, past, **ew)
                ev.append(st)
            else:
                o = layer // 2
                ow = dict(w_in=od_w_in[o], g_cq=mla_g_cq[o], w_uq=mla_w_uq[o], g_q=mla_g_q[o], g_ckv=mla_g_ckv[o],
                          g_kr=mla_g_kr[o], w_uk=mla_w_uk[o], w_uv=mla_w_uv[o], w_out=od_w_out[o])
                past = None if prompt else dict(e=o, page_table=page_table, ckv=cache_mla_ckv, krope=cache_mla_krope)
                h, st = odd_mixer(xn, 0 if prompt else past_len, past, **ow)
                od.append(st)
            y = y + h
            if prompt:
                mk, mv = mem_kv(mem_prompt, xm_mem_norm[layer], xm_wk[layer], xm_wv[layer], xm_gk[layer])
                memk.append(mk)
                memv.append(mv)
            else:
                mk, mv = cache_mem_k[layer], cache_mem_v[layer]
            y = y + mem_attend(rms_norm(y, xm_norm[layer]), mk, mv, xm_wq[layer], xm_gq[layer], xm_wo[layer])
            y = swiglu_half(y, *ffn_w[layer][1])
        return y, ev, od, memk, memv

    ys, ev_s, od_s, _, _ = run_group(x_sample, False)
    yp, ev_p, od_p, memk_p, memv_p = run_group(x_prompt, True)
    return (yp, ys,
            stack_key(ev_p, 'C'), stack_key(ev_p, 'n'), stack_key(ev_p, 'm'), stack_key(ev_p, 'conv'),
            stack_key(ev_p, 'cmp_k'), stack_key(ev_p, 'cmp_v'), stack_key(ev_p, 'slc_k'), stack_key(ev_p, 'slc_v'),
            stack_key(ev_p, 'win_k'), stack_key(ev_p, 'win_v'),
            stack_key(od_p, 'ckv'), stack_key(od_p, 'krope'),
            jnp.stack(memk_p), jnp.stack(memv_p),
            stack_key(ev_s, 'C'), stack_key(ev_s, 'n'), stack_key(ev_s, 'm'), stack_key(ev_s, 'conv'),
            stack_key(ev_s, 'cmp_k'), stack_key(ev_s, 'cmp_v'), stack_key(ev_s, 'slc_k'), stack_key(ev_s, 'slc_v'),
            stack_key(ev_s, 'win_k'), stack_key(ev_s, 'win_v'),
            stack_key(od_s, 'ckv'), stack_key(od_s, 'krope'))
```

```python
import functools
import math

import jax
import jax.numpy as jnp
import numpy as np
from jax import lax
from jax.experimental import pallas as pl
from jax.experimental.pallas import tpu as pltpu

D_MODEL = 1024
DEPTH = 2
PAGE_SIZE = 128
EPS = 1e-6
NEG_INF = -1e30
D_FF = 2816
ML_H = 4
ML_DH = 128
ML_W = ML_H * ML_DH
CONV_K = 4
MLSTM_CHUNK = 64
NSA_H = 8
NSA_KV = 2
NSA_R = NSA_H // NSA_KV
NSA_DH = 64
NSA_W = NSA_H * NSA_DH
NSA_KVW = NSA_KV * NSA_DH
NSA_SCALE = NSA_DH ** -0.5
CMP_STRIDE = 16
CMP_LEN = 2 * CMP_STRIDE
CMP_HID = 2 * NSA_DH
SLC_BLOCK = 64
SLC_RATIO = SLC_BLOCK // CMP_STRIDE
SLC_OVERLAP_W = (1.0, 2.0, 2.0, 2.0, 1.0)
N_SELECT = 16
WINDOW = 512
SLC_QBLOCK = 32
WIN_QBLOCK = 128
FORCE_SCORE = 1e6
REL_BUCKETS = 32
REL_MAX_DIST = 128
MLA_H = 16
MLA_NOPE = 64
MLA_ROPE = 32
MLA_V = 64
Q_LORA = 384
KV_LORA = 256
MLA_SCALE = (MLA_NOPE + MLA_ROPE) ** -0.5
MLA_QBLOCK = 128
ROPE_THETA = 10000.0
MEM_LEN = 256
XM_H = 4
XM_DH = 128
XM_W = XM_H * XM_DH
EV_SPLITS = (ML_W, ML_W, ML_W, ML_H, ML_H, NSA_W) + (NSA_KVW,) * 6 + (NSA_H * 3,)
OD_SPLITS = (Q_LORA, KV_LORA, MLA_ROPE)

V7X_LANES = 128
V7X_VMEM_LIMIT_BYTES = 56 * 1024 * 1024
FFN_CHUNK = 256
FFN_ROWS = 512


def _ffn_kernel(x_ref, g_ref, wg_ref, wu_ref, wd_ref, o_ref, act_ref):
    x = x_ref[...]
    h = x * lax.rsqrt(jnp.mean(x * x, axis=-1, keepdims=True) + EPS) * g_ref[...]
    hb = h.astype(jnp.bfloat16)
    for c in range(D_FF // FFN_CHUNK):
        cols = slice(c * FFN_CHUNK, (c + 1) * FFN_CHUNK)
        gate = jnp.dot(hb, wg_ref[:, cols], preferred_element_type=jnp.float32)
        up = jnp.dot(hb, wu_ref[:, cols], preferred_element_type=jnp.float32)
        act_ref[:, cols] = (gate * jax.nn.sigmoid(gate) * up).astype(jnp.bfloat16)
    o_ref[...] = x + 0.5 * jnp.dot(act_ref[...], wd_ref[...], preferred_element_type=jnp.float32)


def ffn_half(x2d, g, wg, wu, wd):
    m = x2d.shape[0]
    tm = min(FFN_ROWS, m)
    assert m % tm == 0
    resident = functools.partial(pl.BlockSpec, pipeline_mode=pl.Buffered(1))
    return pl.pallas_call(
        _ffn_kernel,
        grid=(m // tm,),
        in_specs=[
            pl.BlockSpec((tm, D_MODEL), lambda i: (i, 0)),
            resident((1, D_MODEL), lambda i: (0, 0)),
            resident((D_MODEL, D_FF), lambda i: (0, 0)),
            resident((D_MODEL, D_FF), lambda i: (0, 0)),
            resident((D_FF, D_MODEL), lambda i: (0, 0)),
        ],
        out_specs=pl.BlockSpec((tm, D_MODEL), lambda i: (i, 0)),
        out_shape=jax.ShapeDtypeStruct((m, D_MODEL), jnp.float32),
        scratch_shapes=[pltpu.VMEM((tm, D_FF), jnp.bfloat16)],
        compiler_params=pltpu.CompilerParams(
            dimension_semantics=("arbitrary",), vmem_limit_bytes=V7X_VMEM_LIMIT_BYTES),
        name="ffn_half",
    )(x2d, g.reshape(1, D_MODEL), wg, wu, wd)


def swiglu_half(x, g, wg, wu, wd):
    shp = x.shape
    return ffn_half(x.reshape(-1, D_MODEL), g, wg, wu, wd).reshape(shp)


NSA_TQ = 256
NSA_NEAR = 128


def _nsa_prompt_kernel(q_ref, kc_ref, vc_ref, ks_ref, vs_ref, kw_ref, vw_ref, bc_ref, bn_ref,
                       e_ref, mw_ref, gp_ref, gb_ref, o_ref, *, n_cmp, n_select):
    f32, bf16 = jnp.float32, jnp.bfloat16
    qi = pl.program_id(2)
    R, TQ, DH = q_ref.shape[2:]
    S = ks_ref.shape[2]
    t0 = qi * TQ
    q2 = q_ref[0, 0].reshape(R * TQ, DH)
    t_col = t0 + lax.broadcasted_iota(jnp.int32, (TQ, 1), 0)

    def scores(k):
        s = lax.dot_general(q2, k, (((1,), (1,)), ((), ())), preferred_element_type=f32)
        return s.reshape(R, TQ, k.shape[0])

    ncp = kc_ref.shape[2]
    c_row = lax.broadcasted_iota(jnp.int32, (1, ncp), 1)
    mask_c = (t_col >= c_row * CMP_STRIDE + (CMP_LEN - 1)) & (c_row < n_cmp)
    s = jnp.where(mask_c[None], scores(kc_ref[0, 0]) + bc_ref[0], NEG_INF)
    e = jnp.exp(s - jnp.max(s, axis=-1, keepdims=True))
    p = e / jnp.sum(e, axis=-1, keepdims=True) * mask_c.astype(f32)[None]
    o_cmp = jnp.dot(p.reshape(R * TQ, ncp).astype(bf16), vc_ref[0, 0],
                    preferred_element_type=f32).reshape(R, TQ, DH)
    imp = jnp.sum(p, axis=0)

    mw = mw_ref[...]
    hi = imp.astype(bf16)
    r1 = imp - hi.astype(f32)
    mid = r1.astype(bf16)
    lo = (r1 - mid.astype(f32)).astype(bf16)
    p_slc = (jnp.dot(hi, mw, preferred_element_type=f32) + jnp.dot(mid, mw, preferred_element_type=f32)
             + jnp.dot(lo, mw, preferred_element_type=f32))
    ns = mw_ref.shape[1]
    jb = lax.broadcasted_iota(jnp.int32, (1, ns), 1)
    tb = jnp.right_shift(t_col, int(math.log2(SLC_BLOCK)))
    forced = (jb == 0) | (jb == tb) | (jb == tb - 1)
    score = jnp.where(jb <= tb, p_slc + FORCE_SCORE * forced.astype(f32), -1.0)
    rank = jnp.zeros((TQ, ns), f32)
    for j in range(ns):
        col = score[:, j:j + 1]
        rank = rank + ((col > score) | ((col == score) & (jb > j))).astype(f32)
    sel = (rank < n_select).astype(bf16)
    gates = jax.nn.sigmoid(gp_ref[0, 0] + gb_ref[0])
    q = q_ref[0, 0]

    def attend(r, k, v, add_mask, first_key, t_start):
        s = lax.dot_general(q[r], k, (((1,), (1,)), ((), ())), preferred_element_type=f32) + add_mask
        near_lo = max(t_start - NSA_NEAR, first_key)
        band = bn_ref[0, r][:, near_lo - (t_start - NSA_NEAR):]
        cut = near_lo - first_key
        near = s[:, cut:] + band
        s = near if cut == 0 else jnp.concatenate([s[:, :cut], near], axis=1)
        e = jnp.exp(s - jnp.max(s, axis=-1, keepdims=True))
        return jnp.dot(e.astype(bf16), v, preferred_element_type=f32) / jnp.sum(e, axis=-1, keepdims=True)

    for c in range(S // TQ):
        @pl.when(qi == c)
        def _(c=c):
            t_start, n_keys = c * TQ, (c + 1) * TQ
            row = t_start + lax.broadcasted_iota(jnp.int32, (TQ, 1), 0)
            col = lax.broadcasted_iota(jnp.int32, (1, n_keys), 1)
            sel_tok = jnp.dot(sel, e_ref[:, :n_keys], preferred_element_type=f32)
            mask_s = jnp.where((sel_tok > 0.5) & (col <= row), 0.0, NEG_INF)
            w_lo = max(t_start - WINDOW, 0)
            col_w = w_lo + lax.broadcasted_iota(jnp.int32, (1, n_keys - w_lo), 1)
            mask_w = jnp.where((col_w <= row) & (row - col_w <= WINDOW), 0.0, NEG_INF)
            ks, vs = ks_ref[0, 0, :n_keys, :], vs_ref[0, 0, :n_keys, :]
            kw, vw = kw_ref[0, 0, w_lo:n_keys, :], vw_ref[0, 0, w_lo:n_keys, :]
            outs = []
            for r in range(R):
                o_slc = attend(r, ks, vs, mask_s, 0, t_start)
                o_win = attend(r, kw, vw, mask_w, w_lo, t_start)
                outs.append(gates[:, 3 * r:3 * r + 1] * o_cmp[r] + gates[:, 3 * r + 1:3 * r + 2] * o_slc
                            + gates[:, 3 * r + 2:3 * r + 3] * o_win)
            o_ref[0] = jnp.concatenate(outs, axis=-1)


def _bias_lookup(rel_bias, dist):
    bucket = t5_bucket(dist)[..., None]
    out = jnp.zeros(dist.shape + rel_bias.shape[1:], rel_bias.dtype)
    for b in range(REL_BUCKETS):
        out = jnp.where(bucket == b, rel_bias[b], out)
    return out


def nsa_prompt(qn, k_cmp, v_cmp, ks, vs, kw, vw, g_pre, gate_b, rel_bias):
    f32, bf16 = jnp.float32, jnp.bfloat16
    B, S = qn.shape[:2]
    G, R, DH, TQ = NSA_KV, NSA_R, NSA_DH, NSA_TQ
    assert S % TQ == 0 and TQ % V7X_LANES == 0 and WINDOW % V7X_LANES == 0 and NSA_NEAR % V7X_LANES == 0
    n_cmp = k_cmp.shape[1]
    ncp = S // CMP_STRIDE
    ns = S // SLC_BLOCK
    assert n_cmp == ncp - 1
    n_select = min(N_SELECT, ns)

    def kv_layout(a, n):
        a = jnp.pad(a, ((0, 0), (0, n - a.shape[1]), (0, 0), (0, 0)))
        return jnp.transpose(a, (0, 2, 1, 3)).astype(bf16)

    q5 = jnp.transpose((qn * NSA_SCALE).reshape(B, S, G, R, DH), (0, 2, 3, 1, 4)).astype(bf16)
    kc, vc = kv_layout(k_cmp, ncp), kv_layout(v_cmp, ncp)
    ks_t, vs_t, kw_t, vw_t = (kv_layout(a, S) for a in (ks, vs, kw, vw))

    half = REL_BUCKETS // 2
    assert half + int(math.log(NSA_NEAR / half) / math.log(REL_MAX_DIST / half) * half) >= REL_BUCKETS - 1
    t = jnp.arange(S)
    dist_c = t[:, None] - (jnp.arange(ncp) * CMP_STRIDE + CMP_LEN - 1)[None, :]
    bias_c = jnp.transpose(_bias_lookup(rel_bias, dist_c).reshape(S, ncp, G, R), (2, 3, 0, 1))
    d_near = jnp.arange(TQ)[:, None] + NSA_NEAR - jnp.arange(TQ + NSA_NEAR)[None, :]
    bias_n = jnp.transpose((_bias_lookup(rel_bias, d_near) - rel_bias[REL_BUCKETS - 1]).reshape(
        TQ, TQ + NSA_NEAR, G, R), (2, 3, 0, 1))
    expand = (jnp.arange(S)[None, :] // SLC_BLOCK == jnp.arange(ns)[:, None]).astype(bf16)
    c_i = jnp.arange(ncp)[:, None]
    j_i = jnp.arange(ns)[None, :]
    mw = sum(w * (c_i == SLC_RATIO * j_i + k - 1) for k, w in enumerate(SLC_OVERLAP_W)).astype(bf16)
    gp = jnp.transpose(g_pre.reshape(B, S, G, 3 * R), (0, 2, 1, 3))
    gb = gate_b.reshape(G, 1, 3 * R)

    kv_spec = lambda n: pl.BlockSpec((1, 1, n, DH), lambda b, g, i: (b, g, 0, 0))
    return pl.pallas_call(
        functools.partial(_nsa_prompt_kernel, n_cmp=n_cmp, n_select=n_select),
        grid=(B, G, S // TQ),
        in_specs=[
            pl.BlockSpec((1, 1, R, TQ, DH), lambda b, g, i: (b, g, 0, i, 0)),
            kv_spec(ncp), kv_spec(ncp), kv_spec(S), kv_spec(S), kv_spec(S), kv_spec(S),
            pl.BlockSpec((1, R, TQ, ncp), lambda b, g, i: (g, 0, i, 0)),
            pl.BlockSpec((1, R, TQ, TQ + NSA_NEAR), lambda b, g, i: (g, 0, 0, 0)),
            pl.BlockSpec((ns, S), lambda b, g, i: (0, 0)),
            pl.BlockSpec((ncp, ns), lambda b, g, i: (0, 0)),
            pl.BlockSpec((1, 1, TQ, 3 * R), lambda b, g, i: (b, g, i, 0)),
            pl.BlockSpec((1, 1, 3 * R), lambda b, g, i: (g, 0, 0)),
        ],
        out_specs=pl.BlockSpec((1, TQ, R * DH), lambda b, g, i: (b, i, g)),
        out_shape=jax.ShapeDtypeStruct((B, S, NSA_W), f32),
        compiler_params=pltpu.CompilerParams(
            dimension_semantics=("arbitrary", "arbitrary", "arbitrary"), vmem_limit_bytes=V7X_VMEM_LIMIT_BYTES),
        name="nsa_prompt",
    )(q5, kc, vc, ks_t, vs_t, kw_t, vw_t, bias_c, bias_n, expand, mw, gp, gb)


MLA_TQ = 512
MLA_HEADS_PER_STEP = 2


def _mla_prompt_kernel(q_ref, ckv_ref, kr_ref, wuk_ref, wuv_ref, o_ref, k_ref, v_ref, s_ref):
    f32, bf16 = jnp.float32, jnp.bfloat16
    qi = pl.program_id(2)
    HP, TQ = q_ref.shape[1:3]
    S = ckv_ref.shape[1]
    causal = lax.broadcasted_iota(jnp.int32, (TQ, 1), 0) >= lax.broadcasted_iota(jnp.int32, (1, TQ), 1)

    @pl.when(qi == 0)
    def _():
        ckv = ckv_ref[0]
        for h in range(HP):
            k_nope = jnp.dot(ckv, wuk_ref[h], preferred_element_type=f32).astype(bf16)
            k_ref[h] = jnp.concatenate([k_nope, kr_ref[0]], axis=-1)
            v_ref[h] = jnp.dot(ckv, wuv_ref[h], preferred_element_type=f32).astype(bf16)

    for c in range(S // TQ):
        @pl.when(qi == c)
        def _(c=c):
            n_keys = (c + 1) * TQ
            outs = []
            for h in range(HP):
                s_ref[h, :, :n_keys] = lax.dot_general(
                    q_ref[0, h], k_ref[h, :n_keys, :], (((1,), (1,)), ((), ())),
                    preferred_element_type=f32) * MLA_SCALE
                s_ref[h, :, n_keys - TQ:n_keys] = jnp.where(causal, s_ref[h, :, n_keys - TQ:n_keys], NEG_INF)
                s = s_ref[h, :, :n_keys]
                e = jnp.exp(s - jnp.max(s, axis=-1, keepdims=True))
                pv = jnp.dot(e.astype(bf16), v_ref[h, :n_keys, :], preferred_element_type=f32)
                outs.append(pv / jnp.sum(e, axis=-1, keepdims=True))
            o_ref[0] = jnp.concatenate(outs, axis=-1)


def mla_prompt_attention(q_cat, ckv, kr, w_uk, w_uv):
    B, H, S, DQ = q_cat.shape
    HP, TQ = MLA_HEADS_PER_STEP, min(MLA_TQ, S)
    assert S % TQ == 0 and H % HP == 0 and HP * MLA_V == V7X_LANES
    return pl.pallas_call(
        _mla_prompt_kernel,
        grid=(B, H // HP, S // TQ),
        in_specs=[
            pl.BlockSpec((1, HP, TQ, DQ), lambda b, h, i: (b, h, i, 0)),
            pl.BlockSpec((1, S, KV_LORA), lambda b, h, i: (b, 0, 0)),
            pl.BlockSpec((1, S, MLA_ROPE), lambda b, h, i: (b, 0, 0)),
            pl.BlockSpec((HP, KV_LORA, MLA_NOPE), lambda b, h, i: (h, 0, 0)),
            pl.BlockSpec((HP, KV_LORA, MLA_V), lambda b, h, i: (h, 0, 0)),
        ],
        out_specs=pl.BlockSpec((1, TQ, HP * MLA_V), lambda b, h, i: (b, i, h)),
        out_shape=jax.ShapeDtypeStruct((B, S, H * MLA_V), jnp.float32),
        scratch_shapes=[pltpu.VMEM((HP, S, DQ), jnp.bfloat16), pltpu.VMEM((HP, S, MLA_V), jnp.bfloat16),
                        pltpu.VMEM((HP, TQ, S), jnp.float32)],
        compiler_params=pltpu.CompilerParams(
            dimension_semantics=("arbitrary", "arbitrary", "arbitrary"), vmem_limit_bytes=V7X_VMEM_LIMIT_BYTES),
        name="mla_prompt",
    )(q_cat, ckv, kr, w_uk, w_uv)


MEM_ROWS = 512


def _mem_attn_kernel(x_ref, g_ref, wq_ref, gq_ref, k_ref, v_ref, wo_ref, o_ref):
    f32, bf16 = jnp.float32, jnp.bfloat16
    x = x_ref[0]
    xn = (x * lax.rsqrt(jnp.mean(x * x, axis=-1, keepdims=True) + EPS) * g_ref[...]).astype(bf16)
    q = jnp.dot(xn, wq_ref[...], preferred_element_type=f32)
    k, v = k_ref[0], v_ref[0]
    outs = []
    for h in range(XM_H):
        cols = slice(h * XM_DH, (h + 1) * XM_DH)
        qh = q[:, cols]
        qh = (qh * lax.rsqrt(jnp.mean(qh * qh, axis=-1, keepdims=True) + EPS) * gq_ref[...]).astype(bf16)
        s = lax.dot_general(qh, k[:, cols], (((1,), (1,)), ((), ())), preferred_element_type=f32) * (XM_DH ** -0.5)
        e = jnp.exp(s - jnp.max(s, axis=-1, keepdims=True))
        p = (e / jnp.sum(e, axis=-1, keepdims=True)).astype(bf16)
        outs.append(jnp.dot(p, v[:, cols], preferred_element_type=f32))
    o = jnp.concatenate(outs, axis=-1).astype(bf16)
    o_ref[0] = x + jnp.dot(o, wo_ref[...], preferred_element_type=f32)


def mem_attention(y, g, wq, gq, k, v, wo):
    bf16 = jnp.bfloat16
    B, S, D = y.shape
    M = k.shape[1]
    ts = min(MEM_ROWS, S)
    assert S % ts == 0
    const = lambda shape: pl.BlockSpec(shape, lambda b, i: (0,) * len(shape), pipeline_mode=pl.Buffered(1))
    kv_spec = pl.BlockSpec((1, M, XM_W), lambda b, i: (b, 0, 0))
    return pl.pallas_call(
        _mem_attn_kernel,
        grid=(B, S // ts),
        in_specs=[pl.BlockSpec((1, ts, D), lambda b, i: (b, i, 0)), const((1, D)), const((D, XM_W)),
                  const((1, XM_DH)), kv_spec, kv_spec, const((XM_W, D))],
        out_specs=pl.BlockSpec((1, ts, D), lambda b, i: (b, i, 0)),
        out_shape=jax.ShapeDtypeStruct((B, S, D), jnp.float32),
        compiler_params=pltpu.CompilerParams(
            dimension_semantics=("arbitrary", "arbitrary"), vmem_limit_bytes=V7X_VMEM_LIMIT_BYTES),
        name="mem_attn",
    )(y, g.reshape(1, D), wq.astype(bf16), gq.reshape(1, XM_DH), k.reshape(B, M, XM_W).astype(bf16),
      v.reshape(B, M, XM_W).astype(bf16), wo.astype(bf16))


DECODE_XPOSE_UNROLL = 8


def _paged_copy(pool_ref, page, buf_ref, slot, p, sem_ref):
    return pltpu.make_async_copy(pool_ref.at[page], buf_ref.at[slot, p], sem_ref.at[slot])


def _paged_pipeline(pt_ref, pools, bufs, sems):
    b = pl.program_id(0)
    n_pages = pt_ref.shape[1]

    def start(seq, slot):
        def body(p, carry):
            for pool, buf, sem in zip(pools, bufs, sems):
                _paged_copy(pool, pt_ref[seq, p], buf, slot, p, sem).start()
            return carry
        lax.fori_loop(0, n_pages, body, 0)

    @pl.when(b == 0)
    def _():
        start(0, 0)

    @pl.when(b + 1 < pl.num_programs(0))
    def _():
        start(b + 1, (b + 1) % 2)

    slot = b % 2

    def wait_body(p, carry):
        for pool, buf, sem in zip(pools, bufs, sems):
            _paged_copy(pool, 0, buf, slot, p, sem).wait()
        return carry
    lax.fori_loop(0, n_pages, wait_body, 0)
    return slot


def _softmax_rows(s, valid):
    e = jnp.exp(s - jnp.max(s, axis=-1, keepdims=True))
    return e / jnp.sum(e, axis=-1, keepdims=True) * valid


def _nsa_decode_cmp_kernel(pt_ref, qbd_ref, wk_ref, wv_ref, pek_ref, pev_ref, w2k_ref, w2v_ref, gk_ref,
                           bc_ref, mw_ref, kpool_ref, vpool_ref, ocmp_ref, sel_ref,
                           kbuf, vbuf, xrm_ref, ksem, vsem, *, n_cmp, n_blocks, n_select):
    f32, bf16 = jnp.float32, jnp.bfloat16
    slot = _paged_pipeline(pt_ref, (kpool_ref, vpool_ref), (kbuf, vbuf), (ksem, vsem))
    n_pages = pt_ref.shape[1]
    n_chunk = n_pages * (PAGE_SIZE // CMP_STRIDE)
    G, DH, HID = NSA_KV, NSA_DH, CMP_HID

    def summaries(buf, w_ref, pe_ref, w2_ref):
        def xpose(i, carry):
            for u in range(DECODE_XPOSE_UNROLL):
                p = i * DECODE_XPOSE_UNROLL + u
                xrm_ref[pl.ds(pl.multiple_of(p * PAGE_SIZE, PAGE_SIZE), PAGE_SIZE), :] = buf[slot, p].T
            return carry
        lax.fori_loop(0, n_pages // DECODE_XPOSE_UNROLL, xpose, 0)
        acc = jnp.zeros((n_chunk, 2 * G * HID), f32)
        for j in range(0, CMP_STRIDE, 2):
            rows = jnp.concatenate([xrm_ref[pl.ds(j + u, n_chunk, stride=CMP_STRIDE), :].astype(bf16)
                                    for u in range(2)], axis=1)
            acc = acc + jnp.dot(rows, w_ref[j // 2], preferred_element_type=f32)
        lo, hi = acc[:, :G * HID], acc[:, G * HID:]
        pre = lo + pltpu.roll(hi, n_chunk - 1, 0) + pe_ref[...]
        hid = pre * jax.nn.sigmoid(pre)
        return jnp.dot(hid.astype(bf16), w2_ref[...], preferred_element_type=f32)

    k_sum = summaries(kbuf, wk_ref, pek_ref, w2k_ref)
    lane = lax.broadcasted_iota(jnp.int32, (1, G * DH), 1)
    sq = k_sum * k_sum
    s_all = jnp.sum(sq, axis=-1, keepdims=True)
    s_g0 = jnp.sum(jnp.where(lane < DH, sq, 0.0), axis=-1, keepdims=True)
    ms = jnp.where(lane < DH, s_g0, s_all - s_g0) * (1.0 / DH)
    k_cmp = (k_sum * lax.rsqrt(ms + EPS) * gk_ref[...]).astype(bf16)
    v_cmp = summaries(vbuf, wv_ref, pev_ref, w2v_ref).astype(bf16)

    qbd = qbd_ref[0]
    H = qbd.shape[0]
    c_row = lax.broadcasted_iota(jnp.int32, (1, n_chunk), 1)
    valid = (c_row < n_cmp).astype(f32)
    s = lax.dot_general(qbd, k_cmp, (((1,), (1,)), ((), ())), preferred_element_type=f32) + bc_ref[...]
    p = _softmax_rows(jnp.where(valid > 0.5, s, NEG_INF), valid)
    ocmp_ref[0] = jnp.dot(p.astype(bf16), v_cmp, preferred_element_type=f32)

    R = H // G
    head = lax.broadcasted_iota(jnp.int32, (H, 1), 0)
    imp = jnp.where(head < R, jnp.sum(p[:R], axis=0, keepdims=True), jnp.sum(p[R:], axis=0, keepdims=True))
    mw = mw_ref[...]
    hi_p = imp.astype(bf16)
    r1 = imp - hi_p.astype(f32)
    mid_p = r1.astype(bf16)
    lo_p = (r1 - mid_p.astype(f32)).astype(bf16)
    p_slc = (jnp.dot(hi_p, mw, preferred_element_type=f32) + jnp.dot(mid_p, mw, preferred_element_type=f32)
             + jnp.dot(lo_p, mw, preferred_element_type=f32))
    nsp = mw_ref.shape[1]
    tb = n_blocks - 1
    jb = lax.broadcasted_iota(jnp.int32, (1, nsp), 1)
    forced = (jb == 0) | (jb == tb) | (jb == tb - 1)
    score = jnp.where(jb <= tb, p_slc + FORCE_SCORE * forced.astype(f32), -1.0)
    j_col = lax.broadcasted_iota(jnp.int32, (nsp, 1), 0)
    sels = []
    for g in range(G):
        row = score[g * R:g * R + 1, :]
        col = jnp.broadcast_to(row, (nsp, nsp)).T
        beats = (col > row) | ((col == row) & (j_col < jb))
        rank = jnp.sum(beats.astype(f32), axis=0, keepdims=True)
        sels.append(jnp.broadcast_to((rank < n_select).astype(f32), (R, nsp)))
    sel_ref[0] = jnp.concatenate(sels, axis=0)


def _nsa_decode_attn_kernel(pt_ref, qbd_ref, sel_ref, ocmp_ref, gate_ref, new_ref, e_ref, bs_ref, bw_ref, b0_ref,
                            kw_ref, vw_ref, kpool_ref, vpool_ref, o_ref, kbuf, vbuf, s_ref, ksem, vsem):
    f32, bf16 = jnp.float32, jnp.bfloat16
    slot = _paged_pipeline(pt_ref, (kpool_ref, vpool_ref), (kbuf, vbuf), (ksem, vsem))
    n_pages = pt_ref.shape[1]
    qbd = qbd_ref[0]
    qf = qbd.astype(f32)
    new = new_ref[0]
    b0 = b0_ref[...]

    def new_score(k_row):
        return jnp.sum(qf * k_row.astype(bf16).astype(f32), axis=-1, keepdims=True) + b0

    for p in range(n_pages):
        s_ref[:, p * PAGE_SIZE:(p + 1) * PAGE_SIZE] = jnp.dot(qbd, kbuf[slot, p].astype(bf16),
                                                              preferred_element_type=f32)
    n_blk_past = e_ref.shape[0]
    sel_tok = jnp.dot(sel_ref[0][:, :n_blk_past].astype(bf16), e_ref[...], preferred_element_type=f32)
    s_past = jnp.where(sel_tok > 0.5, s_ref[...] + bs_ref[...], NEG_INF)
    s_new = new_score(new[0:1])
    m = jnp.maximum(jnp.max(s_past, axis=-1, keepdims=True), s_new)
    e_new = jnp.exp(s_new - m)
    s_ref[...] = jnp.exp(s_past - m)
    denom = jnp.sum(s_ref[...], axis=-1, keepdims=True) + e_new
    acc = e_new.astype(bf16).astype(f32) * new[1:2].astype(bf16).astype(f32)
    for p in range(n_pages):
        pe = s_ref[:, p * PAGE_SIZE:(p + 1) * PAGE_SIZE].astype(bf16)
        acc = acc + lax.dot_general(pe, vbuf[slot, p].astype(bf16), (((1,), (1,)), ((), ())),
                                    preferred_element_type=f32)
    o_slc = acc / denom

    s_w = jnp.dot(qbd, kw_ref[0].astype(bf16), preferred_element_type=f32) + bw_ref[...]
    s_wn = new_score(new[2:3])
    m = jnp.maximum(jnp.max(s_w, axis=-1, keepdims=True), s_wn)
    e_w, e_wn = jnp.exp(s_w - m), jnp.exp(s_wn - m)
    denom = jnp.sum(e_w, axis=-1, keepdims=True) + e_wn
    acc = (lax.dot_general(e_w.astype(bf16), vw_ref[0].astype(bf16), (((1,), (1,)), ((), ())),
                           preferred_element_type=f32)
           + e_wn.astype(bf16).astype(f32) * new[3:4].astype(bf16).astype(f32))
    o_win = acc / denom

    gates = jax.nn.sigmoid(gate_ref[0])
    o_ref[0] = gates[:, 0:1] * ocmp_ref[0] + gates[:, 1:2] * o_slc + gates[:, 2:3] * o_win


def _block_diag_heads(x):
    B, H, DH = x.shape
    g_of_h = jnp.arange(H) // (H // NSA_KV)
    onehot = (g_of_h[:, None] == jnp.arange(NSA_KV)[None, :]).astype(x.dtype)
    return (x[:, :, None, :] * onehot[None, :, :, None]).reshape(B, H, NSA_KV * DH)


def nsa_decode(qn, ks, vs, kw, vw, g_pre, gate_b, rel_bias, page_table, cmp_k_pool, cmp_v_pool,
               slc_k_pool, slc_v_pool, win_k, win_v, pe_k, w1_k, w2_k, pe_v, w1_v, w2_v, gk_cmp):
    f32, bf16 = jnp.float32, jnp.bfloat16
    B, n_pages = page_table.shape
    G, R, DH, H = NSA_KV, NSA_R, NSA_DH, NSA_H
    GD = G * DH
    past = n_pages * PAGE_SIZE
    n_chunk = past // CMP_STRIDE
    n_cmp = (past + 1 - CMP_LEN) // CMP_STRIDE + 1
    n_blocks = -(-(past + 1) // SLC_BLOCK)
    n_blk_past = past // SLC_BLOCK
    nsp = -(-n_blocks // V7X_LANES) * V7X_LANES
    n_win = win_k.shape[1]
    assert GD == V7X_LANES and PAGE_SIZE == V7X_LANES and n_blocks >= N_SELECT and n_win == WINDOW

    def pool_view(pool):
        return jnp.transpose(pool, (0, 2, 3, 1)).reshape(pool.shape[0], GD, PAGE_SIZE)
    kc_pool, vc_pool, ks_pool, vs_pool = (pool_view(a) for a in (cmp_k_pool, cmp_v_pool, slc_k_pool, slc_v_pool))
    kw_t = jnp.transpose(win_k, (0, 2, 3, 1)).reshape(B, GD, n_win)
    vw_t = jnp.transpose(win_v, (0, 2, 3, 1)).reshape(B, GD, n_win)

    qbd = _block_diag_heads((qn[:, 0] * NSA_SCALE)).astype(bf16)
    eye = jnp.eye(G, dtype=f32)

    def chunk_weights(w1):
        def bd(w):
            return jnp.einsum('jdh,gk->jgdkh', w, eye).reshape(CMP_STRIDE, GD, G * CMP_HID)
        w = jnp.concatenate([bd(w1[:CMP_STRIDE]), bd(w1[CMP_STRIDE:])], axis=-1)
        return w.reshape(CMP_STRIDE // 2, 2 * GD, 2 * G * CMP_HID).astype(bf16)

    def pe_term(pe, w1):
        return jnp.tile(jnp.einsum('jd,jdh->h', pe, w1), G).reshape(1, G * CMP_HID)

    def w2_bd(w2):
        return jnp.einsum('hd,gk->ghkd', w2, eye).reshape(G * CMP_HID, GD).astype(bf16)

    c_i = jnp.arange(n_chunk)
    bias_c = _bias_lookup(rel_bias, past - (c_i * CMP_STRIDE + CMP_LEN - 1)).T
    j_i = jnp.arange(nsp)[None, :]
    mw = sum(w * ((c_i[:, None] == SLC_RATIO * j_i + k - 1) & (j_i < n_blocks))
             for k, w in enumerate(SLC_OVERLAP_W)).astype(bf16)
    tok = jnp.arange(past)
    expand = (tok[None, :] // SLC_BLOCK == jnp.arange(n_blk_past)[:, None]).astype(bf16)
    bias_s = _bias_lookup(rel_bias, past - tok).T
    bias_w = _bias_lookup(rel_bias, n_win - jnp.arange(n_win)).T
    bias_0 = _bias_lookup(rel_bias, jnp.zeros((1,), jnp.int32)).T
    new_rows = jnp.stack([a.reshape(B, GD) for a in (ks, vs, kw, vw)], axis=1)
    gate_in = g_pre.reshape(B, H, 3) + gate_b

    const = lambda shape: pl.BlockSpec(shape, lambda b, pt: (0,) * len(shape))
    per_seq = lambda shape: pl.BlockSpec((1,) + shape, lambda b, pt: (b,) + (0,) * len(shape))
    any_spec = pl.BlockSpec(memory_space=pl.ANY)
    page_buf = pltpu.VMEM((2, n_pages, GD, PAGE_SIZE), f32)
    params = pltpu.CompilerParams(dimension_semantics=("arbitrary",), vmem_limit_bytes=V7X_VMEM_LIMIT_BYTES)

    o_cmp, sel = pl.pallas_call(
        functools.partial(_nsa_decode_cmp_kernel, n_cmp=n_cmp, n_blocks=n_blocks, n_select=min(N_SELECT, n_blocks)),
        grid_spec=pltpu.PrefetchScalarGridSpec(
            num_scalar_prefetch=1, grid=(B,),
            in_specs=[per_seq((H, GD)),
                      const((CMP_STRIDE // 2, 2 * GD, 2 * G * CMP_HID)),
                      const((CMP_STRIDE // 2, 2 * GD, 2 * G * CMP_HID)),
                      const((1, G * CMP_HID)), const((1, G * CMP_HID)),
                      const((G * CMP_HID, GD)), const((G * CMP_HID, GD)), const((1, GD)),
                      const((H, n_chunk)), const((n_chunk, nsp)), any_spec, any_spec],
            out_specs=[per_seq((H, GD)), per_seq((H, nsp))],
            scratch_shapes=[page_buf, page_buf, pltpu.VMEM((past, GD), f32),
                            pltpu.SemaphoreType.DMA((2,)), pltpu.SemaphoreType.DMA((2,))]),
        out_shape=[jax.ShapeDtypeStruct((B, H, GD), f32), jax.ShapeDtypeStruct((B, H, nsp), f32)],
        compiler_params=params, name="nsa_decode_cmp",
    )(page_table, qbd, chunk_weights(w1_k), chunk_weights(w1_v), pe_term(pe_k, w1_k), pe_term(pe_v, w1_v),
      w2_bd(w2_k), w2_bd(w2_v), jnp.tile(gk_cmp, G).reshape(1, GD), bias_c, mw, kc_pool, vc_pool)

    out = pl.pallas_call(
        _nsa_decode_attn_kernel,
        grid_spec=pltpu.PrefetchScalarGridSpec(
            num_scalar_prefetch=1, grid=(B,),
            in_specs=[per_seq((H, GD)), per_seq((H, nsp)), per_seq((H, GD)), per_seq((H, 3)), per_seq((4, GD)),
                      const((n_blk_past, past)), const((H, past)), const((H, n_win)), const((H, 1)),
                      per_seq((GD, n_win)), per_seq((GD, n_win)), any_spec, any_spec],
            out_specs=per_seq((H, GD)),
            scratch_shapes=[page_buf, page_buf, pltpu.VMEM((H, past), f32),
                            pltpu.SemaphoreType.DMA((2,)), pltpu.SemaphoreType.DMA((2,))]),
        out_shape=jax.ShapeDtypeStruct((B, H, GD), f32),
        compiler_params=params, name="nsa_decode_attn",
    )(page_table, qbd, sel, o_cmp, gate_in, new_rows, expand, bias_s, bias_w, bias_0, kw_t, vw_t, ks_pool, vs_pool)

    out = out.reshape(B, G, R, G, DH)
    h_b = jnp.stack([out[:, g, :, g, :] for g in range(G)], axis=1)
    return h_b.reshape(B, 1, NSA_W)


def _mla_decode_kernel(pt_ref, ql_ref, qr_ref, new_ref, newr_ref, cpool_ref, rpool_ref, o_ref,
                       cbuf, rbuf, cb16, s_ref, csem, rsem):
    f32, bf16 = jnp.float32, jnp.bfloat16
    slot = _paged_pipeline(pt_ref, (cpool_ref, rpool_ref), (cbuf, rbuf), (csem, rsem))
    n_pages = pt_ref.shape[1]
    ql, qr = ql_ref[0], qr_ref[0]
    for p in range(n_pages):
        c16 = cbuf[slot, p].astype(bf16)
        cb16[p] = c16
        s_ref[:, p * PAGE_SIZE:(p + 1) * PAGE_SIZE] = (
            lax.dot_general(ql, c16, (((1,), (1,)), ((), ())), preferred_element_type=f32)
            + jnp.dot(qr, rbuf[slot, p].astype(bf16), preferred_element_type=f32)) * MLA_SCALE
    c_new = new_ref[0].astype(bf16).astype(f32)
    r_new = newr_ref[0].astype(bf16).astype(f32)
    s_new = (jnp.sum(ql.astype(f32) * c_new, axis=-1, keepdims=True)
             + jnp.sum(qr.astype(f32) * r_new, axis=-1, keepdims=True)) * MLA_SCALE
    s_past = s_ref[...]
    m = jnp.maximum(jnp.max(s_past, axis=-1, keepdims=True), s_new)
    e_new = jnp.exp(s_new - m)
    s_ref[...] = jnp.exp(s_past - m)
    denom = jnp.sum(s_ref[...], axis=-1, keepdims=True) + e_new
    inv = 1.0 / denom
    acc = (e_new * inv).astype(bf16).astype(f32) * c_new
    for p in range(n_pages):
        pe = (s_ref[:, p * PAGE_SIZE:(p + 1) * PAGE_SIZE] * inv).astype(bf16)
        acc = acc + jnp.dot(pe, cb16[p], preferred_element_type=f32)
    o_ref[0] = acc


def mla_decode_attention(q_lat, q_rope, ckv_new, kr_new, page_table, ckv_pool, krope_pool):
    f32, bf16 = jnp.float32, jnp.bfloat16
    B, n_pages = page_table.shape
    H = q_lat.shape[1]
    past = n_pages * PAGE_SIZE
    rpool_t = jnp.transpose(krope_pool, (0, 2, 1))
    per_seq = lambda shape: pl.BlockSpec((1,) + shape, lambda b, pt: (b,) + (0,) * len(shape))
    any_spec = pl.BlockSpec(memory_space=pl.ANY)
    return pl.pallas_call(
        _mla_decode_kernel,
        grid_spec=pltpu.PrefetchScalarGridSpec(
            num_scalar_prefetch=1, grid=(B,),
            in_specs=[per_seq((H, KV_LORA)), per_seq((H, MLA_ROPE)), per_seq((1, KV_LORA)), per_seq((1, MLA_ROPE)),
                      any_spec, any_spec],
            out_specs=per_seq((H, KV_LORA)),
            scratch_shapes=[pltpu.VMEM((2, n_pages, PAGE_SIZE, KV_LORA), f32),
                            pltpu.VMEM((2, n_pages, MLA_ROPE, PAGE_SIZE), f32),
                            pltpu.VMEM((n_pages, PAGE_SIZE, KV_LORA), bf16),
                            pltpu.VMEM((H, past), f32),
                            pltpu.SemaphoreType.DMA((2,)), pltpu.SemaphoreType.DMA((2,))]),
        out_shape=jax.ShapeDtypeStruct((B, H, KV_LORA), f32),
        compiler_params=pltpu.CompilerParams(dimension_semantics=("arbitrary",),
                                             vmem_limit_bytes=V7X_VMEM_LIMIT_BYTES),
        name="mla_decode",
    )(page_table, q_lat.astype(bf16), q_rope.astype(bf16), ckv_new.reshape(B, 1, KV_LORA),
      kr_new.reshape(B, 1, MLA_ROPE), ckv_pool, rpool_t)


def split_cols(a, sizes):
    idx = [int(s) for s in np.cumsum(sizes)[:-1]]
    return jnp.split(a, idx, axis=-1)


def rms_norm(x, g):
    xf = x.astype(jnp.float32)
    y = xf * lax.rsqrt(jnp.mean(xf * xf, axis=-1, keepdims=True) + EPS)
    return (y * g.astype(jnp.float32)).astype(x.dtype)


def t5_bucket(dist):
    n = jnp.maximum(dist, 0)
    exact = REL_BUCKETS // 2
    nf = jnp.maximum(n, exact).astype(jnp.float32)
    large = exact + (jnp.log(nf / exact) / math.log(REL_MAX_DIST / exact) * (REL_BUCKETS - exact)).astype(jnp.int32)
    return jnp.where(n < exact, n, jnp.minimum(large, REL_BUCKETS - 1))


def apply_rope(x, pos):
    half = x.shape[-1] // 2
    inv = ROPE_THETA ** (-jnp.arange(half, dtype=jnp.float32) / half)
    ang = pos.astype(jnp.float32)[:, None] * inv[None, :]
    ang = ang.reshape(ang.shape[:1] + (1,) * (x.ndim - 3) + (half,))
    cos, sin = jnp.cos(ang).astype(x.dtype), jnp.sin(ang).astype(x.dtype)
    x1, x2 = x[..., :half], x[..., half:]
    return jnp.concatenate([x1 * cos - x2 * sin, x1 * sin + x2 * cos], axis=-1)


def causal_conv(u, buf, w, b):
    S = u.shape[1]
    full = jnp.concatenate([buf.astype(u.dtype), u], axis=1)
    out = b + sum(full[:, j:j + S] * w[j] for j in range(CONV_K))
    return out, full[:, S:]


def mlstm_chunkwise(q, k, v, i_pre, logf, C0, n0, m0):
    f32 = jnp.float32
    q, k, v, i_pre, logf = (a.astype(f32) for a in (q, k, v, i_pre, logf))
    B, H, S, D = q.shape
    L = MLSTM_CHUNK if S % MLSTM_CHUNK == 0 else S
    NC = S // L

    def chunks(a):
        return jnp.moveaxis(a.reshape((B, H, NC, L) + a.shape[3:]), 2, 0)

    causal = jnp.tril(jnp.ones((L, L), dtype=bool))

    def step(carry, inp):
        C, n, m = carry
        qc, kc, vc, ic, fc = inp
        b = jnp.cumsum(fc, axis=-1)
        g = b + m[..., None]
        dmat = jnp.where(causal, b[..., :, None] - b[..., None, :] + ic[..., None, :], -jnp.inf)
        mt = jnp.maximum(g, jnp.max(dmat, axis=-1))
        inter = jnp.exp(g - mt)
        sqk = jnp.einsum('bhtd,bhsd->bhts', qc, kc) * jnp.exp(dmat - mt[..., None])
        num = inter[..., None] * jnp.einsum('bhvd,bhtd->bhtv', C, qc) + jnp.einsum('bhts,bhsv->bhtv', sqk, vc)
        den = inter * jnp.einsum('bhd,bhtd->bht', n, qc) + jnp.sum(sqk, axis=-1)
        h = num / jnp.maximum(jnp.abs(den), jnp.exp(-mt))[..., None]
        b_end = b[..., -1]
        w_log = b_end[..., None] - b + ic
        m_new = jnp.maximum(b_end + m, jnp.max(w_log, axis=-1))
        decay = jnp.exp(b_end + m - m_new)
        w_in = jnp.exp(w_log - m_new[..., None])
        C_new = decay[..., None, None] * C + jnp.einsum('bhs,bhsv,bhsd->bhvd', w_in, vc, kc)
        n_new = decay[..., None] * n + jnp.einsum('bhs,bhsd->bhd', w_in, kc)
        return (C_new, n_new, m_new), h

    (C1, n1, m1), hs = lax.scan(step, (C0.astype(f32), n0.astype(f32), m0.astype(f32)),
                                tuple(chunks(a) for a in (q, k, v, i_pre, logf)))
    return jnp.moveaxis(hs, 0, 2).reshape(B, H, S, D), C1, n1, m1


def to_chunks(a):
    B, T = a.shape[:2]
    pad = (-T) % CMP_STRIDE
    a = jnp.pad(a, ((0, 0), (0, pad), (0, 0), (0, 0)))
    return a.reshape((B, (T + pad) // CMP_STRIDE, CMP_STRIDE) + a.shape[2:])


def cmp_summaries(chunk_list, T, pe, w1, w2):
    lo = jnp.concatenate([jnp.einsum('bcjgd,jdh->bcgh', r, w1[:CMP_STRIDE]) for r in chunk_list], axis=1)
    hi = jnp.concatenate([jnp.einsum('bcjgd,jdh->bcgh', r, w1[CMP_STRIDE:]) for r in chunk_list], axis=1)
    n_cmp = (T - CMP_LEN) // CMP_STRIDE + 1
    hid = jax.nn.silu(lo[:, :n_cmp] + hi[:, 1:n_cmp + 1] + jnp.einsum('jd,jdh->h', pe, w1))
    return hid @ w2


def even_mixer(xn, pos0, past, w_in, w_out, conv_w, conv_b, ml_wq, ml_wk, ml_b_i, ml_b_f, ml_out_g,
               nsa_gq, nsa_gk_cmp, nsa_gk_slc, nsa_gk_win, pe_k, w1_k, w2_k, pe_v, w1_v, w2_v, gate_b, rel_bias):
    B, S, _ = xn.shape
    dt = xn.dtype
    (u, v_m, o_pre, i_pre, f_pre, q, kc, vc, ks, vs, kw, vw, g_pre) = split_cols(xn @ w_in, EV_SPLITS)

    if past is None:
        conv_buf = jnp.zeros((B, CONV_K - 1, ML_W), dt)
        C0 = jnp.zeros((B, ML_H, ML_DH, ML_DH), jnp.float32)
        n0 = jnp.zeros((B, ML_H, ML_DH), jnp.float32)
        m0 = jnp.zeros((B, ML_H), jnp.float32)
    else:
        conv_buf, C0, n0, m0 = past['conv'], past['C'], past['n'], past['m']
    c, conv_new = causal_conv(u, conv_buf, conv_w, conv_b)
    ch = jax.nn.silu(c).reshape(B, S, ML_H, ML_DH)
    qm = jnp.einsum('bshd,hde->bhse', ch, ml_wq)
    km = jnp.einsum('bshd,hde->bhse', ch, ml_wk) * (ML_DH ** -0.5)
    vm = jnp.transpose(v_m.reshape(B, S, ML_H, ML_DH), (0, 2, 1, 3))
    ig = jnp.transpose(i_pre + ml_b_i, (0, 2, 1))
    lf = jax.nn.log_sigmoid(jnp.transpose(f_pre + ml_b_f, (0, 2, 1)).astype(jnp.float32))
    hm, C1, n1, m1 = mlstm_chunkwise(qm, km, vm, ig, lf, C0, n0, m0)
    hm = rms_norm(jnp.transpose(hm, (0, 2, 1, 3)).astype(dt), ml_out_g) * jax.nn.sigmoid(o_pre).reshape(B, S, ML_H, ML_DH)
    h_a = hm.reshape(B, S, ML_W)

    q = rms_norm(q.reshape(B, S, NSA_H, NSA_DH), nsa_gq)
    kv_shape = (B, S, NSA_KV, NSA_DH)
    kc, vc, vs, vw = (a.reshape(kv_shape) for a in (kc, vc, vs, vw))
    ks = rms_norm(ks.reshape(kv_shape), nsa_gk_slc)
    kw = rms_norm(kw.reshape(kv_shape), nsa_gk_win)

    if past is None:
        k_cmp = rms_norm(cmp_summaries([to_chunks(kc)], S, pe_k, w1_k, w2_k), nsa_gk_cmp)
        v_cmp = cmp_summaries([to_chunks(vc)], S, pe_v, w1_v, w2_v)
        h_b = nsa_prompt(q, k_cmp, v_cmp, ks, vs, kw, vw, g_pre, gate_b, rel_bias)
        nb = min(WINDOW, S)
        win_k_new, win_v_new = kw[:, S - nb:], vw[:, S - nb:]
    else:
        assert S == 1
        e = past['e']
        h_b = nsa_decode(q, ks, vs, kw, vw, g_pre, gate_b, rel_bias, past['page_table'],
                         past['cmp_k'][e], past['cmp_v'][e], past['slc_k'][e], past['slc_v'][e],
                         past['win_k'], past['win_v'], pe_k, w1_k, w2_k, pe_v, w1_v, w2_v, nsa_gk_cmp)
        win_k_new = jnp.concatenate([past['win_k'][:, S:], kw], axis=1)
        win_v_new = jnp.concatenate([past['win_v'][:, S:], vw], axis=1)
    out = jnp.concatenate([h_a, h_b], axis=-1) @ w_out
    new = dict(C=C1.astype(dt), n=n1.astype(dt), m=m1.astype(dt), conv=conv_new,
               cmp_k=kc, cmp_v=vc, slc_k=ks, slc_v=vs, win_k=win_k_new, win_v=win_v_new)
    return out, new


def odd_mixer(xn, pos0, past, w_in, g_cq, w_uq, g_q, g_ckv, g_kr, w_uk, w_uv, w_out):
    B, S, _ = xn.shape
    cq, ckv, kr = split_cols(xn @ w_in, OD_SPLITS)
    tq = pos0 + jnp.arange(S)
    q = rms_norm((rms_norm(cq, g_cq) @ w_uq).reshape(B, S, MLA_H, MLA_NOPE + MLA_ROPE), g_q)
    q_nope = q[..., :MLA_NOPE]
    q_rope = apply_rope(q[..., MLA_NOPE:], tq)
    ckv = rms_norm(ckv, g_ckv)
    kr = apply_rope(rms_norm(kr, g_kr), tq)
    if past is None:
        bf16 = jnp.bfloat16
        q_cat = jnp.transpose(jnp.concatenate([q_nope, q_rope], axis=-1), (0, 2, 1, 3)).astype(bf16)
        o = mla_prompt_attention(q_cat, ckv.astype(bf16), kr.astype(bf16),
                                 jnp.transpose(w_uk, (1, 0, 2)).astype(bf16),
                                 jnp.transpose(w_uv, (1, 0, 2)).astype(bf16))
    else:
        assert S == 1
        e = past['e']
        q_lat = jnp.einsum('bqhn,chn->bqhc', q_nope, w_uk)
        o_lat = mla_decode_attention(q_lat[:, 0], q_rope[:, 0], ckv[:, 0], kr[:, 0], past['page_table'],
                                     past['ckv'][e], past['krope'][e])[:, None]
        o = jnp.einsum('bqhc,chv->bqhv', o_lat, w_uv).reshape(B, S, MLA_H * MLA_V)
    return o @ w_out, dict(ckv=ckv, krope=kr)


def mem_kv(mem, g_mem, wk, wv, gk):
    B, M, _ = mem.shape
    m = rms_norm(mem, g_mem)
    k = rms_norm((m @ wk).reshape(B, M, XM_H, XM_DH), gk)
    v = (m @ wv).reshape(B, M, XM_H, XM_DH)
    return k, v


def mem_attend(xn, k, v, wq, gq, wo):
    B, S, _ = xn.shape
    q = rms_norm((xn @ wq).reshape(B, S, XM_H, XM_DH), gq)
    s = jnp.einsum('bshd,bmhd->bhsm', q, k.astype(q.dtype)).astype(jnp.float32) * (XM_DH ** -0.5)
    p = jax.nn.softmax(s, axis=-1)
    return jnp.einsum('bhsm,bmhd->bshd', p.astype(xn.dtype), v.astype(xn.dtype)).reshape(B, S, XM_W) @ wo


def stack_key(lst, name):
    return jnp.stack([d[name] for d in lst])


def kernel(x_prompt, x_sample, mem_prompt,
           state_ml_C, state_ml_n, state_ml_m, state_ml_conv,
           cache_cmp_k, cache_cmp_v, cache_slc_k, cache_slc_v, cache_win_k, cache_win_v,
           cache_mla_ckv, cache_mla_krope, cache_mem_k, cache_mem_v, page_table,
           rel_bias, ffn1_norm, ffn1_wg, ffn1_wu, ffn1_wd, mix_norm,
           xm_norm, xm_mem_norm, xm_wq, xm_wk, xm_wv, xm_wo, xm_gq, xm_gk,
           ffn2_norm, ffn2_wg, ffn2_wu, ffn2_wd,
           ev_w_in, ev_w_out, ml_conv_w, ml_conv_b, ml_wq, ml_wk, ml_b_i, ml_b_f, ml_out_g,
           nsa_gq, nsa_gk_cmp, nsa_gk_slc, nsa_gk_win, cmp_pe_k, cmp_w1_k, cmp_w2_k,
           cmp_pe_v, cmp_w1_v, cmp_w2_v, nsa_gate_b,
           od_w_in, mla_g_cq, mla_w_uq, mla_g_q, mla_g_ckv, mla_g_kr, mla_w_uk, mla_w_uv, od_w_out):
    past_len = page_table.shape[1] * PAGE_SIZE
    bf = jnp.bfloat16
    ffn_w = [[(n[layer], wg[layer].astype(bf), wu[layer].astype(bf), wd[layer].astype(bf))
              for n, wg, wu, wd in ((ffn1_norm, ffn1_wg, ffn1_wu, ffn1_wd), (ffn2_norm, ffn2_wg, ffn2_wu, ffn2_wd))]
             for layer in range(DEPTH)]

    def run_group(y, prompt):
        ev, od, memk, memv = [], [], [], []
        for layer in range(DEPTH):
            y = swiglu_half(y, *ffn_w[layer][0])
            xn = rms_norm(y, mix_norm[layer])
            if layer % 2 == 0:
                e = layer // 2
                ew = dict(w_in=ev_w_in[e], w_out=ev_w_out[e], conv_w=ml_conv_w[e], conv_b=ml_conv_b[e],
                          ml_wq=ml_wq[e], ml_wk=ml_wk[e], ml_b_i=ml_b_i[e], ml_b_f=ml_b_f[e], ml_out_g=ml_out_g[e],
                          nsa_gq=nsa_gq[e], nsa_gk_cmp=nsa_gk_cmp[e], nsa_gk_slc=nsa_gk_slc[e],
                          nsa_gk_win=nsa_gk_win[e], pe_k=cmp_pe_k[e], w1_k=cmp_w1_k[e], w2_k=cmp_w2_k[e],
                          pe_v=cmp_pe_v[e], w1_v=cmp_w1_v[e], w2_v=cmp_w2_v[e],
                          gate_b=nsa_gate_b[e], rel_bias=rel_bias)
                past = None if prompt else dict(
                    e=e, page_table=page_table, C=state_ml_C[e], n=state_ml_n[e], m=state_ml_m[e],
                    conv=state_ml_conv[e], cmp_k=cache_cmp_k, cmp_v=cache_cmp_v,
                    slc_k=cache_slc_k, slc_v=cache_slc_v, win_k=cache_win_k[e], win_v=cache_win_v[e])
                h, st = even_mixer(xn, 0 if prompt else past_len, past, **ew)
                ev.append(st)
            else:
                o = layer // 2
                ow = dict(w_in=od_w_in[o], g_cq=mla_g_cq[o], w_uq=mla_w_uq[o], g_q=mla_g_q[o], g_ckv=mla_g_ckv[o],
                          g_kr=mla_g_kr[o], w_uk=mla_w_uk[o], w_uv=mla_w_uv[o], w_out=od_w_out[o])
                past = None if prompt else dict(e=o, page_table=page_table, ckv=cache_mla_ckv, krope=cache_mla_krope)
                h, st = odd_mixer(xn, 0 if prompt else past_len, past, **ow)
                od.append(st)
            y = y + h
            if prompt:
                mk, mv = mem_kv(mem_prompt, xm_mem_norm[layer], xm_wk[layer], xm_wv[layer], xm_gk[layer])
                memk.append(mk)
                memv.append(mv)
                y = mem_attention(y, xm_norm[layer], xm_wq[layer], xm_gq[layer], mk, mv, xm_wo[layer])
            else:
                y = y + mem_attend(rms_norm(y, xm_norm[layer]), cache_mem_k[layer], cache_mem_v[layer],
                                   xm_wq[layer], xm_gq[layer], xm_wo[layer])
            y = swiglu_half(y, *ffn_w[layer][1])
        return y, ev, od, memk, memv

    ys, ev_s, od_s, _, _ = run_group(x_sample, False)
    yp, ev_p, od_p, memk_p, memv_p = run_group(x_prompt, True)
    return (yp, ys,
            stack_key(ev_p, 'C'), stack_key(ev_p, 'n'), stack_key(ev_p, 'm'), stack_key(ev_p, 'conv'),
            stack_key(ev_p, 'cmp_k'), stack_key(ev_p, 'cmp_v'), stack_key(ev_p, 'slc_k'), stack_key(ev_p, 'slc_v'),
            stack_key(ev_p, 'win_k'), stack_key(ev_p, 'win_v'),
            stack_key(od_p, 'ckv'), stack_key(od_p, 'krope'),
            jnp.stack(memk_p), jnp.stack(memv_p),
            stack_key(ev_s, 'C'), stack_key(ev_s, 'n'), stack_key(ev_s, 'm'), stack_key(ev_s, 'conv'),
            stack_key(ev_s, 'cmp_k'), stack_key(ev_s, 'cmp_v'), stack_key(ev_s, 'slc_k'), stack_key(ev_s, 'slc_v'),
            stack_key(ev_s, 'win_k'), stack_key(ev_s, 'win_v'),
            stack_key(od_s, 'ckv'), stack_key(od_s, 'krope'))
```

```python
import functools
import math

import jax
import jax.numpy as jnp
import numpy as np
from jax import lax
from jax.experimental import pallas as pl
from jax.experimental.pallas import tpu as pltpu

D_MODEL = 1024
DEPTH = 2
PAGE_SIZE = 128
EPS = 1e-6
NEG_INF = -1e30
D_FF = 2816
ML_H = 4
ML_DH = 128
ML_W = ML_H * ML_DH
CONV_K = 4
MLSTM_CHUNK = 64
NSA_H = 8
NSA_KV = 2
NSA_R = NSA_H // NSA_KV
NSA_DH = 64
NSA_W = NSA_H * NSA_DH
NSA_KVW = NSA_KV * NSA_DH
NSA_SCALE = NSA_DH ** -0.5
CMP_STRIDE = 16
CMP_LEN = 2 * CMP_STRIDE
CMP_HID = 2 * NSA_DH
SLC_BLOCK = 64
SLC_RATIO = SLC_BLOCK // CMP_STRIDE
SLC_OVERLAP_W = (1.0, 2.0, 2.0, 2.0, 1.0)
N_SELECT = 16
WINDOW = 512
SLC_QBLOCK = 32
WIN_QBLOCK = 128
FORCE_SCORE = 1e6
REL_BUCKETS = 32
REL_MAX_DIST = 128
MLA_H = 16
MLA_NOPE = 64
MLA_ROPE = 32
MLA_V = 64
Q_LORA = 384
KV_LORA = 256
MLA_SCALE = (MLA_NOPE + MLA_ROPE) ** -0.5
MLA_QBLOCK = 128
ROPE_THETA = 10000.0
MEM_LEN = 256
XM_H = 4
XM_DH = 128
XM_W = XM_H * XM_DH
EV_SPLITS = (ML_W, ML_W, ML_W, ML_H, ML_H, NSA_W) + (NSA_KVW,) * 6 + (NSA_H * 3,)
OD_SPLITS = (Q_LORA, KV_LORA, MLA_ROPE)

V7X_LANES = 128
V7X_VMEM_LIMIT_BYTES = 56 * 1024 * 1024
FFN_CHUNK = 256
FFN_ROWS = 512


def _ffn_kernel(x_ref, g_ref, wg_ref, wu_ref, wd_ref, o_ref, act_ref):
    x = x_ref[...]
    h = x * lax.rsqrt(jnp.mean(x * x, axis=-1, keepdims=True) + EPS) * g_ref[...]
    hb = h.astype(jnp.bfloat16)
    for c in range(D_FF // FFN_CHUNK):
        cols = slice(c * FFN_CHUNK, (c + 1) * FFN_CHUNK)
        gate = jnp.dot(hb, wg_ref[:, cols], preferred_element_type=jnp.float32)
        up = jnp.dot(hb, wu_ref[:, cols], preferred_element_type=jnp.float32)
        act_ref[:, cols] = (gate * jax.nn.sigmoid(gate) * up).astype(jnp.bfloat16)
    o_ref[...] = x + 0.5 * jnp.dot(act_ref[...], wd_ref[...], preferred_element_type=jnp.float32)


def ffn_half(x2d, g, wg, wu, wd):
    m = x2d.shape[0]
    tm = min(FFN_ROWS, m)
    assert m % tm == 0
    resident = functools.partial(pl.BlockSpec, pipeline_mode=pl.Buffered(1))
    return pl.pallas_call(
        _ffn_kernel,
        grid=(m // tm,),
        in_specs=[
            pl.BlockSpec((tm, D_MODEL), lambda i: (i, 0)),
            resident((1, D_MODEL), lambda i: (0, 0)),
            resident((D_MODEL, D_FF), lambda i: (0, 0)),
            resident((D_MODEL, D_FF), lambda i: (0, 0)),
            resident((D_FF, D_MODEL), lambda i: (0, 0)),
        ],
        out_specs=pl.BlockSpec((tm, D_MODEL), lambda i: (i, 0)),
        out_shape=jax.ShapeDtypeStruct((m, D_MODEL), jnp.float32),
        scratch_shapes=[pltpu.VMEM((tm, D_FF), jnp.bfloat16)],
        compiler_params=pltpu.CompilerParams(
            dimension_semantics=("arbitrary",), vmem_limit_bytes=V7X_VMEM_LIMIT_BYTES),
        name="ffn_half",
    )(x2d, g.reshape(1, D_MODEL), wg, wu, wd)


def swiglu_half(x, g, wg, wu, wd):
    shp = x.shape
    return ffn_half(x.reshape(-1, D_MODEL), g, wg, wu, wd).reshape(shp)


NSA_TQ = 256
NSA_NEAR = 128


def _nsa_prompt_kernel(q_ref, kc_ref, vc_ref, ks_ref, vs_ref, kw_ref, vw_ref, bc_ref, bn_ref,
                       e_ref, mw_ref, gp_ref, gb_ref, o_ref, *, n_cmp, n_select):
    f32, bf16 = jnp.float32, jnp.bfloat16
    qi = pl.program_id(2)
    R, TQ, DH = q_ref.shape[2:]
    S = ks_ref.shape[2]
    t0 = qi * TQ
    q2 = q_ref[0, 0].reshape(R * TQ, DH)
    t_col = t0 + lax.broadcasted_iota(jnp.int32, (TQ, 1), 0)

    def scores(k):
        s = lax.dot_general(q2, k, (((1,), (1,)), ((), ())), preferred_element_type=f32)
        return s.reshape(R, TQ, k.shape[0])

    ncp = kc_ref.shape[2]
    c_row = lax.broadcasted_iota(jnp.int32, (1, ncp), 1)
    mask_c = (t_col >= c_row * CMP_STRIDE + (CMP_LEN - 1)) & (c_row < n_cmp)
    s = jnp.where(mask_c[None], scores(kc_ref[0, 0]) + bc_ref[0], NEG_INF)
    e = jnp.exp(s - jnp.max(s, axis=-1, keepdims=True))
    p = e / jnp.sum(e, axis=-1, keepdims=True) * mask_c.astype(f32)[None]
    o_cmp = jnp.dot(p.reshape(R * TQ, ncp).astype(bf16), vc_ref[0, 0],
                    preferred_element_type=f32).reshape(R, TQ, DH)
    imp = jnp.sum(p, axis=0)

    mw = mw_ref[...]
    hi = imp.astype(bf16)
    r1 = imp - hi.astype(f32)
    mid = r1.astype(bf16)
    lo = (r1 - mid.astype(f32)).astype(bf16)
    p_slc = (jnp.dot(hi, mw, preferred_element_type=f32) + jnp.dot(mid, mw, preferred_element_type=f32)
             + jnp.dot(lo, mw, preferred_element_type=f32))
    ns = mw_ref.shape[1]
    jb = lax.broadcasted_iota(jnp.int32, (1, ns), 1)
    tb = jnp.right_shift(t_col, int(math.log2(SLC_BLOCK)))
    forced = (jb == 0) | (jb == tb) | (jb == tb - 1)
    score = jnp.where(jb <= tb, p_slc + FORCE_SCORE * forced.astype(f32), -1.0)
    rank = jnp.zeros((TQ, ns), f32)
    for j in range(ns):
        col = score[:, j:j + 1]
        rank = rank + ((col > score) | ((col == score) & (jb > j))).astype(f32)
    sel = (rank < n_select).astype(bf16)
    gates = jax.nn.sigmoid(gp_ref[0, 0] + gb_ref[0])
    q = q_ref[0, 0]

    def attend(r, k, v, add_mask, first_key, t_start):
        s = lax.dot_general(q[r], k, (((1,), (1,)), ((), ())), preferred_element_type=f32) + add_mask
        near_lo = max(t_start - NSA_NEAR, first_key)
        band = bn_ref[0, r][:, near_lo - (t_start - NSA_NEAR):]
        cut = near_lo - first_key
        near = s[:, cut:] + band
        s = near if cut == 0 else jnp.concatenate([s[:, :cut], near], axis=1)
        e = jnp.exp(s - jnp.max(s, axis=-1, keepdims=True))
        return jnp.dot(e.astype(bf16), v, preferred_element_type=f32) / jnp.sum(e, axis=-1, keepdims=True)

    for c in range(S // TQ):
        @pl.when(qi == c)
        def _(c=c):
            t_start, n_keys = c * TQ, (c + 1) * TQ
            row = t_start + lax.broadcasted_iota(jnp.int32, (TQ, 1), 0)
            col = lax.broadcasted_iota(jnp.int32, (1, n_keys), 1)
            sel_tok = jnp.dot(sel, e_ref[:, :n_keys], preferred_element_type=f32)
            mask_s = jnp.where((sel_tok > 0.5) & (col <= row), 0.0, NEG_INF)
            w_lo = max(t_start - WINDOW, 0)
            col_w = w_lo + lax.broadcasted_iota(jnp.int32, (1, n_keys - w_lo), 1)
            mask_w = jnp.where((col_w <= row) & (row - col_w <= WINDOW), 0.0, NEG_INF)
            ks, vs = ks_ref[0, 0, :n_keys, :], vs_ref[0, 0, :n_keys, :]
            kw, vw = kw_ref[0, 0, w_lo:n_keys, :], vw_ref[0, 0, w_lo:n_keys, :]
            outs = []
            for r in range(R):
                o_slc = attend(r, ks, vs, mask_s, 0, t_start)
                o_win = attend(r, kw, vw, mask_w, w_lo, t_start)
                outs.append(gates[:, 3 * r:3 * r + 1] * o_cmp[r] + gates[:, 3 * r + 1:3 * r + 2] * o_slc
                            + gates[:, 3 * r + 2:3 * r + 3] * o_win)
            o_ref[0] = jnp.concatenate(outs, axis=-1)


def _bias_lookup(rel_bias, dist):
    bucket = t5_bucket(dist)[..., None]
    out = jnp.zeros(dist.shape + rel_bias.shape[1:], rel_bias.dtype)
    for b in range(REL_BUCKETS):
        out = jnp.where(bucket == b, rel_bias[b], out)
    return out


def nsa_prompt(qn, k_cmp, v_cmp, ks, vs, kw, vw, g_pre, gate_b, rel_bias):
    f32, bf16 = jnp.float32, jnp.bfloat16
    B, S = qn.shape[:2]
    G, R, DH, TQ = NSA_KV, NSA_R, NSA_DH, NSA_TQ
    assert S % TQ == 0 and TQ % V7X_LANES == 0 and WINDOW % V7X_LANES == 0 and NSA_NEAR % V7X_LANES == 0
    n_cmp = k_cmp.shape[1]
    ncp = S // CMP_STRIDE
    ns = S // SLC_BLOCK
    assert n_cmp == ncp - 1
    n_select = min(N_SELECT, ns)

    def kv_layout(a, n):
        a = jnp.pad(a, ((0, 0), (0, n - a.shape[1]), (0, 0), (0, 0)))
        return jnp.transpose(a, (0, 2, 1, 3)).astype(bf16)

    q5 = jnp.transpose((qn * NSA_SCALE).reshape(B, S, G, R, DH), (0, 2, 3, 1, 4)).astype(bf16)
    kc, vc = kv_layout(k_cmp, ncp), kv_layout(v_cmp, ncp)
    ks_t, vs_t, kw_t, vw_t = (kv_layout(a, S) for a in (ks, vs, kw, vw))

    half = REL_BUCKETS // 2
    assert half + int(math.log(NSA_NEAR / half) / math.log(REL_MAX_DIST / half) * half) >= REL_BUCKETS - 1
    t = jnp.arange(S)
    dist_c = t[:, None] - (jnp.arange(ncp) * CMP_STRIDE + CMP_LEN - 1)[None, :]
    bias_c = jnp.transpose(_bias_lookup(rel_bias, dist_c).reshape(S, ncp, G, R), (2, 3, 0, 1))
    d_near = jnp.arange(TQ)[:, None] + NSA_NEAR - jnp.arange(TQ + NSA_NEAR)[None, :]
    bias_n = jnp.transpose((_bias_lookup(rel_bias, d_near) - rel_bias[REL_BUCKETS - 1]).reshape(
        TQ, TQ + NSA_NEAR, G, R), (2, 3, 0, 1))
    expand = (jnp.arange(S)[None, :] // SLC_BLOCK == jnp.arange(ns)[:, None]).astype(bf16)
    c_i = jnp.arange(ncp)[:, None]
    j_i = jnp.arange(ns)[None, :]
    mw = sum(w * (c_i == SLC_RATIO * j_i + k - 1) for k, w in enumerate(SLC_OVERLAP_W)).astype(bf16)
    gp = jnp.transpose(g_pre.reshape(B, S, G, 3 * R), (0, 2, 1, 3))
    gb = gate_b.reshape(G, 1, 3 * R)

    kv_spec = lambda n: pl.BlockSpec((1, 1, n, DH), lambda b, g, i: (b, g, 0, 0))
    return pl.pallas_call(
        functools.partial(_nsa_prompt_kernel, n_cmp=n_cmp, n_select=n_select),
        grid=(B, G, S // TQ),
        in_specs=[
            pl.BlockSpec((1, 1, R, TQ, DH), lambda b, g, i: (b, g, 0, i, 0)),
            kv_spec(ncp), kv_spec(ncp), kv_spec(S), kv_spec(S), kv_spec(S), kv_spec(S),
            pl.BlockSpec((1, R, TQ, ncp), lambda b, g, i: (g, 0, i, 0)),
            pl.BlockSpec((1, R, TQ, TQ + NSA_NEAR), lambda b, g, i: (g, 0, 0, 0)),
            pl.BlockSpec((ns, S), lambda b, g, i: (0, 0)),
            pl.BlockSpec((ncp, ns), lambda b, g, i: (0, 0)),
            pl.BlockSpec((1, 1, TQ, 3 * R), lambda b, g, i: (b, g, i, 0)),
            pl.BlockSpec((1, 1, 3 * R), lambda b, g, i: (g, 0, 0)),
        ],
        out_specs=pl.BlockSpec((1, TQ, R * DH), lambda b, g, i: (b, i, g)),
        out_shape=jax.ShapeDtypeStruct((B, S, NSA_W), f32),
        compiler_params=pltpu.CompilerParams(
            dimension_semantics=("arbitrary", "arbitrary", "arbitrary"), vmem_limit_bytes=V7X_VMEM_LIMIT_BYTES),
        name="nsa_prompt",
    )(q5, kc, vc, ks_t, vs_t, kw_t, vw_t, bias_c, bias_n, expand, mw, gp, gb)


MLA_TQ = 512
MLA_HEADS_PER_STEP = 2


def _mla_prompt_kernel(cq_ref, wuq_ref, gq_ref, ra_ref, rm_ref, rp_ref, ckv_ref, kr_ref, wuk_ref, wuv_ref,
                       o_ref, q_ref, k_ref, v_ref, s_ref):
    f32, bf16 = jnp.float32, jnp.bfloat16
    qi = pl.program_id(2)
    HP, S, DP = k_ref.shape
    TQ = cq_ref.shape[1]
    causal = lax.broadcasted_iota(jnp.int32, (TQ, 1), 0) >= lax.broadcasted_iota(jnp.int32, (1, TQ), 1)

    @pl.when(qi == 0)
    def _():
        ckv = ckv_ref[0]
        pad = jnp.zeros((S, DP - MLA_NOPE - MLA_ROPE), bf16)
        for h in range(HP):
            k_nope = jnp.dot(ckv, wuk_ref[h], preferred_element_type=f32).astype(bf16)
            k_ref[h] = jnp.concatenate([k_nope, kr_ref[0], pad], axis=-1)
            v_ref[h] = jnp.dot(ckv, wuv_ref[h], preferred_element_type=f32).astype(bf16)

    for h in range(HP):
        q = jnp.dot(cq_ref[0], wuq_ref[h], preferred_element_type=f32)
        ms = jnp.sum(q * q, axis=-1, keepdims=True) * (1.0 / (MLA_NOPE + MLA_ROPE))
        q = q * lax.rsqrt(ms + EPS) * gq_ref[...]
        q = (q * ra_ref[...] + pltpu.roll(q, DP - MLA_ROPE // 2, 1) * rm_ref[...]
             + pltpu.roll(q, MLA_ROPE // 2, 1) * rp_ref[...])
        q_ref[h] = q.astype(bf16)

    for c in range(S // TQ):
        @pl.when(qi == c)
        def _(c=c):
            n_keys = (c + 1) * TQ
            outs = []
            for h in range(HP):
                s_ref[h, :, :n_keys] = lax.dot_general(
                    q_ref[h], k_ref[h, :n_keys, :], (((1,), (1,)), ((), ())),
                    preferred_element_type=f32) * MLA_SCALE
                s_ref[h, :, n_keys - TQ:n_keys] = jnp.where(causal, s_ref[h, :, n_keys - TQ:n_keys], NEG_INF)
                s = s_ref[h, :, :n_keys]
                e = jnp.exp(s - jnp.max(s, axis=-1, keepdims=True))
                pv = jnp.dot(e.astype(bf16), v_ref[h, :n_keys, :], preferred_element_type=f32)
                outs.append(pv / jnp.sum(e, axis=-1, keepdims=True))
            o_ref[0] = jnp.concatenate(outs, axis=-1)


def mla_prompt_attention(cq, w_uq, g_q, pos, ckv, kr, w_uk, w_uv):
    f32, bf16 = jnp.float32, jnp.bfloat16
    B, S, _ = cq.shape
    H, DQ, DP = MLA_H, MLA_NOPE + MLA_ROPE, V7X_LANES
    HP, TQ = MLA_HEADS_PER_STEP, min(MLA_TQ, S)
    half = MLA_ROPE // 2
    assert S % TQ == 0 and H % HP == 0 and HP * MLA_V == V7X_LANES and DQ <= DP
    wuq_p = jnp.pad(jnp.transpose(w_uq.reshape(Q_LORA, H, DQ), (1, 0, 2)), ((0, 0), (0, 0), (0, DP - DQ))).astype(bf16)
    gq_p = jnp.pad(g_q, (0, DP - DQ)).reshape(1, DP)
    inv = ROPE_THETA ** (-jnp.arange(half, dtype=f32) / half)
    ang = pos.astype(f32)[:, None] * inv[None, :]
    cos, sin, zero = jnp.cos(ang), jnp.sin(ang), jnp.zeros((S, half), f32)
    lead = lambda v: jnp.full((S, MLA_NOPE), v, f32)
    tail = jnp.zeros((S, DP - DQ), f32)
    rope_a = jnp.concatenate([lead(1.0), cos, cos, tail], axis=1)
    rope_m = jnp.concatenate([lead(0.0), -sin, zero, tail], axis=1)
    rope_p = jnp.concatenate([lead(0.0), zero, sin, tail], axis=1)
    row_tab = pl.BlockSpec((TQ, DP), lambda b, h, i: (i, 0))
    return pl.pallas_call(
        _mla_prompt_kernel,
        grid=(B, H // HP, S // TQ),
        in_specs=[
            pl.BlockSpec((1, TQ, Q_LORA), lambda b, h, i: (b, i, 0)),
            pl.BlockSpec((HP, Q_LORA, DP), lambda b, h, i: (h, 0, 0)),
            pl.BlockSpec((1, DP), lambda b, h, i: (0, 0)),
            row_tab, row_tab, row_tab,
            pl.BlockSpec((1, S, KV_LORA), lambda b, h, i: (b, 0, 0)),
            pl.BlockSpec((1, S, MLA_ROPE), lambda b, h, i: (b, 0, 0)),
            pl.BlockSpec((HP, KV_LORA, MLA_NOPE), lambda b, h, i: (h, 0, 0)),
            pl.BlockSpec((HP, KV_LORA, MLA_V), lambda b, h, i: (h, 0, 0)),
        ],
        out_specs=pl.BlockSpec((1, TQ, HP * MLA_V), lambda b, h, i: (b, i, h)),
        out_shape=jax.ShapeDtypeStruct((B, S, H * MLA_V), jnp.float32),
        scratch_shapes=[pltpu.VMEM((HP, TQ, DP), bf16), pltpu.VMEM((HP, S, DP), bf16),
                        pltpu.VMEM((HP, S, MLA_V), bf16), pltpu.VMEM((HP, TQ, S), f32)],
        compiler_params=pltpu.CompilerParams(
            dimension_semantics=("arbitrary", "arbitrary", "arbitrary"), vmem_limit_bytes=V7X_VMEM_LIMIT_BYTES),
        name="mla_prompt",
    )(cq, wuq_p, gq_p, rope_a, rope_m, rope_p, ckv, kr, w_uk, w_uv)


def _split3(x):
    bf16, f32 = jnp.bfloat16, jnp.float32
    hi = x.astype(bf16)
    r = x - hi.astype(f32)
    mid = r.astype(bf16)
    return hi, mid, (r - mid.astype(f32)).astype(bf16)


def _mlstm_kernel(q_ref, k_ref, v_ref, ir_ref, fr_ref, it_ref, ft_ref, tri_ref, h_ref, c_ref, n_ref, m_ref):
    f32, bf16 = jnp.float32, jnp.bfloat16
    H, NC, L = ir_ref.shape[1:]
    DH = ML_DH
    tri = tri_ref[...]
    causal = (tri > 0.5)[None]

    def bmm(a, b, ca, cb):
        return lax.dot_general(a, b, (((ca,), (cb,)), ((0,), (0,))), preferred_element_type=f32)

    def heads(ref, rows):
        return jnp.stack([ref[0, rows, h * DH:(h + 1) * DH] for h in range(H)], axis=0)

    b_rows = jnp.stack([sum(lax.dot_general(p, tri, (((1,), (1,)), ((), ())), preferred_element_type=f32)
                            for p in _split3(fr_ref[0, h])) for h in range(H)], axis=0)
    b_cols = jnp.stack([sum(jnp.dot(tri, p, preferred_element_type=f32) for p in _split3(ft_ref[0, h]))
                        for h in range(H)], axis=0)
    i_rows, i_cols = ir_ref[0], it_ref[0]
    C = jnp.zeros((H, DH, DH), f32)
    n = jnp.zeros((H, 1, DH), f32)
    m = jnp.zeros((H, 1, 1), f32)
    for c in range(NC):
        rows = slice(c * L, (c + 1) * L)
        q = heads(q_ref, rows).astype(bf16)
        k = heads(k_ref, rows).astype(bf16)
        v = heads(v_ref, rows)
        b_row, i_row = b_rows[:, c:c + 1, :], i_rows[:, c:c + 1, :]
        b_col, i_col = b_cols[:, :, c:c + 1], i_cols[:, :, c:c + 1]
        g = b_col + m
        dmat = jnp.where(causal, b_col - b_row + i_row, -jnp.inf)
        mt = jnp.maximum(g, jnp.max(dmat, axis=-1, keepdims=True))
        inter = jnp.exp(g - mt)
        sqk = bmm(q, k, 2, 2) * jnp.exp(dmat - mt)
        num = inter * bmm(q, C.astype(bf16), 2, 2) + bmm(sqk.astype(bf16), v.astype(bf16), 2, 1)
        nq = jnp.sum(q.astype(f32) * n.astype(bf16).astype(f32), axis=-1, keepdims=True)
        den = inter * nq + jnp.sum(sqk, axis=-1, keepdims=True)
        hs = num / jnp.maximum(jnp.abs(den), jnp.exp(-mt))
        for h in range(H):
            h_ref[0, rows, h * DH:(h + 1) * DH] = hs[h]
        b_end = b_row[:, :, L - 1:L]
        w_row = b_end - b_row + i_row
        m_new = jnp.maximum(b_end + m, jnp.max(w_row, axis=-1, keepdims=True))
        decay = jnp.exp(b_end + m - m_new)
        w_col = jnp.exp(b_end - b_col + i_col - m_new)
        C = decay * C + bmm((w_col * v).astype(bf16), k, 1, 1)
        n = decay * n + bmm(jnp.exp(w_row - m_new).astype(bf16), k, 2, 1)
        m = m_new
    c_ref[0] = C
    n_ref[0] = n
    m_ref[0] = jnp.broadcast_to(m, (H, 1, V7X_LANES))


def mlstm_prompt(q, k, v, i_pre, logf):
    f32 = jnp.float32
    B, S, W = q.shape
    H, L = ML_H, MLSTM_CHUNK
    assert S % L == 0 and W == H * ML_DH
    NC = S // L
    rows = lambda a: jnp.transpose(a, (0, 2, 1)).reshape(B, H, NC, L)
    cols = lambda a: jnp.transpose(rows(a), (0, 1, 3, 2))
    tri = jnp.tril(jnp.ones((L, L), jnp.bfloat16))
    seq = pl.BlockSpec((1, S, W), lambda b: (b, 0, 0))
    gate_r = pl.BlockSpec((1, H, NC, L), lambda b: (b, 0, 0, 0))
    gate_c = pl.BlockSpec((1, H, L, NC), lambda b: (b, 0, 0, 0))
    hs, C, n, m = pl.pallas_call(
        _mlstm_kernel,
        grid=(B,),
        in_specs=[seq, seq, seq, gate_r, gate_r, gate_c, gate_c, pl.BlockSpec((L, L), lambda b: (0, 0))],
        out_specs=[seq, pl.BlockSpec((1, H, ML_DH, ML_DH), lambda b: (b, 0, 0, 0)),
                   pl.BlockSpec((1, H, 1, ML_DH), lambda b: (b, 0, 0, 0)),
                   pl.BlockSpec((1, H, 1, V7X_LANES), lambda b: (b, 0, 0, 0))],
        out_shape=[jax.ShapeDtypeStruct((B, S, W), f32), jax.ShapeDtypeStruct((B, H, ML_DH, ML_DH), f32),
                   jax.ShapeDtypeStruct((B, H, 1, ML_DH), f32), jax.ShapeDtypeStruct((B, H, 1, V7X_LANES), f32)],
        compiler_params=pltpu.CompilerParams(dimension_semantics=("arbitrary",),
                                             vmem_limit_bytes=V7X_VMEM_LIMIT_BYTES),
        name="mlstm_prompt",
    )(q, k, v, rows(i_pre), rows(logf), cols(i_pre), cols(logf), tri)
    return hs, C, n[:, :, 0], m[:, :, 0, 0]


MEM_ROWS = 512


def _mem_attn_kernel(x_ref, g_ref, wq_ref, gq_ref, k_ref, v_ref, wo_ref, o_ref):
    f32, bf16 = jnp.float32, jnp.bfloat16
    x = x_ref[0]
    xn = (x * lax.rsqrt(jnp.mean(x * x, axis=-1, keepdims=True) + EPS) * g_ref[...]).astype(bf16)
    q = jnp.dot(xn, wq_ref[...], preferred_element_type=f32)
    k, v = k_ref[0], v_ref[0]
    outs = []
    for h in range(XM_H):
        cols = slice(h * XM_DH, (h + 1) * XM_DH)
        qh = q[:, cols]
        qh = (qh * lax.rsqrt(jnp.mean(qh * qh, axis=-1, keepdims=True) + EPS) * gq_ref[...]).astype(bf16)
        s = lax.dot_general(qh, k[:, cols], (((1,), (1,)), ((), ())), preferred_element_type=f32) * (XM_DH ** -0.5)
        e = jnp.exp(s - jnp.max(s, axis=-1, keepdims=True))
        p = (e / jnp.sum(e, axis=-1, keepdims=True)).astype(bf16)
        outs.append(jnp.dot(p, v[:, cols], preferred_element_type=f32))
    o = jnp.concatenate(outs, axis=-1).astype(bf16)
    o_ref[0] = x + jnp.dot(o, wo_ref[...], preferred_element_type=f32)


def mem_attention(y, g, wq, gq, k, v, wo):
    bf16 = jnp.bfloat16
    B, S, D = y.shape
    M = k.shape[1]
    ts = min(MEM_ROWS, S)
    assert S % ts == 0
    const = lambda shape: pl.BlockSpec(shape, lambda b, i: (0,) * len(shape), pipeline_mode=pl.Buffered(1))
    kv_spec = pl.BlockSpec((1, M, XM_W), lambda b, i: (b, 0, 0))
    return pl.pallas_call(
        _mem_attn_kernel,
        grid=(B, S // ts),
        in_specs=[pl.BlockSpec((1, ts, D), lambda b, i: (b, i, 0)), const((1, D)), const((D, XM_W)),
                  const((1, XM_DH)), kv_spec, kv_spec, const((XM_W, D))],
        out_specs=pl.BlockSpec((1, ts, D), lambda b, i: (b, i, 0)),
        out_shape=jax.ShapeDtypeStruct((B, S, D), jnp.float32),
        compiler_params=pltpu.CompilerParams(
            dimension_semantics=("arbitrary", "arbitrary"), vmem_limit_bytes=V7X_VMEM_LIMIT_BYTES),
        name="mem_attn",
    )(y, g.reshape(1, D), wq.astype(bf16), gq.reshape(1, XM_DH), k.reshape(B, M, XM_W).astype(bf16),
      v.reshape(B, M, XM_W).astype(bf16), wo.astype(bf16))


DECODE_XPOSE_UNROLL = 8


def _paged_copy(pool_ref, page, buf_ref, slot, p, sem_ref):
    return pltpu.make_async_copy(pool_ref.at[page], buf_ref.at[slot, p], sem_ref.at[slot])


def _paged_pipeline(pt_ref, pools, bufs, sems):
    b = pl.program_id(0)
    n_pages = pt_ref.shape[1]

    def start(seq, slot):
        def body(p, carry):
            for pool, buf, sem in zip(pools, bufs, sems):
                _paged_copy(pool, pt_ref[seq, p], buf, slot, p, sem).start()
            return carry
        lax.fori_loop(0, n_pages, body, 0)

    @pl.when(b == 0)
    def _():
        start(0, 0)

    @pl.when(b + 1 < pl.num_programs(0))
    def _():
        start(b + 1, (b + 1) % 2)

    slot = b % 2

    def wait_body(p, carry):
        for pool, buf, sem in zip(pools, bufs, sems):
            _paged_copy(pool, 0, buf, slot, p, sem).wait()
        return carry
    lax.fori_loop(0, n_pages, wait_body, 0)
    return slot


def _softmax_rows(s, valid):
    e = jnp.exp(s - jnp.max(s, axis=-1, keepdims=True))
    return e / jnp.sum(e, axis=-1, keepdims=True) * valid


def _nsa_decode_cmp_kernel(pt_ref, qbd_ref, wk_ref, wv_ref, pek_ref, pev_ref, w2k_ref, w2v_ref, gk_ref,
                           bc_ref, mw_ref, kpool_ref, vpool_ref, ocmp_ref, sel_ref,
                           kbuf, vbuf, xrm_ref, ksem, vsem, *, n_cmp, n_blocks, n_select):
    f32, bf16 = jnp.float32, jnp.bfloat16
    slot = _paged_pipeline(pt_ref, (kpool_ref, vpool_ref), (kbuf, vbuf), (ksem, vsem))
    n_pages = pt_ref.shape[1]
    n_chunk = n_pages * (PAGE_SIZE // CMP_STRIDE)
    G, DH, HID = NSA_KV, NSA_DH, CMP_HID

    def summaries(buf, w_ref, pe_ref, w2_ref):
        def xpose(i, carry):
            for u in range(DECODE_XPOSE_UNROLL):
                p = i * DECODE_XPOSE_UNROLL + u
                xrm_ref[pl.ds(pl.multiple_of(p * PAGE_SIZE, PAGE_SIZE), PAGE_SIZE), :] = buf[slot, p].T
            return carry
        lax.fori_loop(0, n_pages // DECODE_XPOSE_UNROLL, xpose, 0)
        acc = jnp.zeros((n_chunk, 2 * G * HID), f32)
        for j in range(0, CMP_STRIDE, 2):
            rows = jnp.concatenate([xrm_ref[pl.ds(j + u, n_chunk, stride=CMP_STRIDE), :].astype(bf16)
                                    for u in range(2)], axis=1)
            acc = acc + jnp.dot(rows, w_ref[j // 2], preferred_element_type=f32)
        lo, hi = acc[:, :G * HID], acc[:, G * HID:]
        pre = lo + pltpu.roll(hi, n_chunk - 1, 0) + pe_ref[...]
        hid = pre * jax.nn.sigmoid(pre)
        return jnp.dot(hid.astype(bf16), w2_ref[...], preferred_element_type=f32)

    k_sum = summaries(kbuf, wk_ref, pek_ref, w2k_ref)
    lane = lax.broadcasted_iota(jnp.int32, (1, G * DH), 1)
    sq = k_sum * k_sum
    s_all = jnp.sum(sq, axis=-1, keepdims=True)
    s_g0 = jnp.sum(jnp.where(lane < DH, sq, 0.0), axis=-1, keepdims=True)
    ms = jnp.where(lane < DH, s_g0, s_all - s_g0) * (1.0 / DH)
    k_cmp = (k_sum * lax.rsqrt(ms + EPS) * gk_ref[...]).astype(bf16)
    v_cmp = summaries(vbuf, wv_ref, pev_ref, w2v_ref).astype(bf16)

    qbd = qbd_ref[0]
    H = qbd.shape[0]
    c_row = lax.broadcasted_iota(jnp.int32, (1, n_chunk), 1)
    valid = (c_row < n_cmp).astype(f32)
    s = lax.dot_general(qbd, k_cmp, (((1,), (1,)), ((), ())), preferred_element_type=f32) + bc_ref[...]
    p = _softmax_rows(jnp.where(valid > 0.5, s, NEG_INF), valid)
    ocmp_ref[0] = jnp.dot(p.astype(bf16), v_cmp, preferred_element_type=f32)

    R = H // G
    head = lax.broadcasted_iota(jnp.int32, (H, 1), 0)
    imp = jnp.where(head < R, jnp.sum(p[:R], axis=0, keepdims=True), jnp.sum(p[R:], axis=0, keepdims=True))
    mw = mw_ref[...]
    hi_p = imp.astype(bf16)
    r1 = imp - hi_p.astype(f32)
    mid_p = r1.astype(bf16)
    lo_p = (r1 - mid_p.astype(f32)).astype(bf16)
    p_slc = (jnp.dot(hi_p, mw, preferred_element_type=f32) + jnp.dot(mid_p, mw, preferred_element_type=f32)
             + jnp.dot(lo_p, mw, preferred_element_type=f32))
    nsp = mw_ref.shape[1]
    tb = n_blocks - 1
    jb = lax.broadcasted_iota(jnp.int32, (1, nsp), 1)
    forced = (jb == 0) | (jb == tb) | (jb == tb - 1)
    score = jnp.where(jb <= tb, p_slc + FORCE_SCORE * forced.astype(f32), -1.0)
    j_col = lax.broadcasted_iota(jnp.int32, (nsp, 1), 0)
    sels = []
    for g in range(G):
        row = score[g * R:g * R + 1, :]
        col = jnp.broadcast_to(row, (nsp, nsp)).T
        beats = (col > row) | ((col == row) & (j_col < jb))
        rank = jnp.sum(beats.astype(f32), axis=0, keepdims=True)
        sels.append(jnp.broadcast_to((rank < n_select).astype(f32), (R, nsp)))
    sel_ref[0] = jnp.concatenate(sels, axis=0)


def _nsa_decode_attn_kernel(pt_ref, qbd_ref, sel_ref, ocmp_ref, gate_ref, new_ref, e_ref, bs_ref, bw_ref, b0_ref,
                            kw_ref, vw_ref, kpool_ref, vpool_ref, o_ref, kbuf, vbuf, s_ref, ksem, vsem):
    f32, bf16 = jnp.float32, jnp.bfloat16
    slot = _paged_pipeline(pt_ref, (kpool_ref, vpool_ref), (kbuf, vbuf), (ksem, vsem))
    n_pages = pt_ref.shape[1]
    qbd = qbd_ref[0]
    qf = qbd.astype(f32)
    new = new_ref[0]
    b0 = b0_ref[...]

    def new_score(k_row):
        return jnp.sum(qf * k_row.astype(bf16).astype(f32), axis=-1, keepdims=True) + b0

    for p in range(n_pages):
        s_ref[:, p * PAGE_SIZE:(p + 1) * PAGE_SIZE] = jnp.dot(qbd, kbuf[slot, p].astype(bf16),
                                                              preferred_element_type=f32)
    n_blk_past = e_ref.shape[0]
    sel_tok = jnp.dot(sel_ref[0][:, :n_blk_past].astype(bf16), e_ref[...], preferred_element_type=f32)
    s_past = jnp.where(sel_tok > 0.5, s_ref[...] + bs_ref[...], NEG_INF)
    s_new = new_score(new[0:1])
    m = jnp.maximum(jnp.max(s_past, axis=-1, keepdims=True), s_new)
    e_new = jnp.exp(s_new - m)
    s_ref[...] = jnp.exp(s_past - m)
    denom = jnp.sum(s_ref[...], axis=-1, keepdims=True) + e_new
    acc = e_new.astype(bf16).astype(f32) * new[1:2].astype(bf16).astype(f32)
    for p in range(n_pages):
        pe = s_ref[:, p * PAGE_SIZE:(p + 1) * PAGE_SIZE].astype(bf16)
        acc = acc + lax.dot_general(pe, vbuf[slot, p].astype(bf16), (((1,), (1,)), ((), ())),
                                    preferred_element_type=f32)
    o_slc = acc / denom

    s_w = jnp.dot(qbd, kw_ref[0].astype(bf16), preferred_element_type=f32) + bw_ref[...]
    s_wn = new_score(new[2:3])
    m = jnp.maximum(jnp.max(s_w, axis=-1, keepdims=True), s_wn)
    e_w, e_wn = jnp.exp(s_w - m), jnp.exp(s_wn - m)
    denom = jnp.sum(e_w, axis=-1, keepdims=True) + e_wn
    acc = (lax.dot_general(e_w.astype(bf16), vw_ref[0].astype(bf16), (((1,), (1,)), ((), ())),
                           preferred_element_type=f32)
           + e_wn.astype(bf16).astype(f32) * new[3:4].astype(bf16).astype(f32))
    o_win = acc / denom

    gates = jax.nn.sigmoid(gate_ref[0])
    o_ref[0] = gates[:, 0:1] * ocmp_ref[0] + gates[:, 1:2] * o_slc + gates[:, 2:3] * o_win


def _block_diag_heads(x):
    B, H, DH = x.shape
    g_of_h = jnp.arange(H) // (H // NSA_KV)
    onehot = (g_of_h[:, None] == jnp.arange(NSA_KV)[None, :]).astype(x.dtype)
    return (x[:, :, None, :] * onehot[None, :, :, None]).reshape(B, H, NSA_KV * DH)


def nsa_decode(qn, ks, vs, kw, vw, g_pre, gate_b, rel_bias, page_table, cmp_k_pool, cmp_v_pool,
               slc_k_pool, slc_v_pool, win_k, win_v, pe_k, w1_k, w2_k, pe_v, w1_v, w2_v, gk_cmp):
    f32, bf16 = jnp.float32, jnp.bfloat16
    B, n_pages = page_table.shape
    G, R, DH, H = NSA_KV, NSA_R, NSA_DH, NSA_H
    GD = G * DH
    past = n_pages * PAGE_SIZE
    n_chunk = past // CMP_STRIDE
    n_cmp = (past + 1 - CMP_LEN) // CMP_STRIDE + 1
    n_blocks = -(-(past + 1) // SLC_BLOCK)
    n_blk_past = past // SLC_BLOCK
    nsp = -(-n_blocks // V7X_LANES) * V7X_LANES
    n_win = win_k.shape[1]
    assert GD == V7X_LANES and PAGE_SIZE == V7X_LANES and n_blocks >= N_SELECT and n_win == WINDOW

    def pool_view(pool):
        return jnp.transpose(pool, (0, 2, 3, 1)).reshape(pool.shape[0], GD, PAGE_SIZE)
    kc_pool, vc_pool, ks_pool, vs_pool = (pool_view(a) for a in (cmp_k_pool, cmp_v_pool, slc_k_pool, slc_v_pool))
    kw_t = jnp.transpose(win_k, (0, 2, 3, 1)).reshape(B, GD, n_win)
    vw_t = jnp.transpose(win_v, (0, 2, 3, 1)).reshape(B, GD, n_win)

    qbd = _block_diag_heads((qn[:, 0] * NSA_SCALE)).astype(bf16)
    eye = jnp.eye(G, dtype=f32)

    def chunk_weights(w1):
        def bd(w):
            return jnp.einsum('jdh,gk->jgdkh', w, eye).reshape(CMP_STRIDE, GD, G * CMP_HID)
        w = jnp.concatenate([bd(w1[:CMP_STRIDE]), bd(w1[CMP_STRIDE:])], axis=-1)
        return w.reshape(CMP_STRIDE // 2, 2 * GD, 2 * G * CMP_HID).astype(bf16)

    def pe_term(pe, w1):
        return jnp.tile(jnp.einsum('jd,jdh->h', pe, w1), G).reshape(1, G * CMP_HID)

    def w2_bd(w2):
        return jnp.einsum('hd,gk->ghkd', w2, eye).reshape(G * CMP_HID, GD).astype(bf16)

    c_i = jnp.arange(n_chunk)
    bias_c = _bias_lookup(rel_bias, past - (c_i * CMP_STRIDE + CMP_LEN - 1)).T
    j_i = jnp.arange(nsp)[None, :]
    mw = sum(w * ((c_i[:, None] == SLC_RATIO * j_i + k - 1) & (j_i < n_blocks))
             for k, w in enumerate(SLC_OVERLAP_W)).astype(bf16)
    tok = jnp.arange(past)
    expand = (tok[None, :] // SLC_BLOCK == jnp.arange(n_blk_past)[:, None]).astype(bf16)
    bias_s = _bias_lookup(rel_bias, past - tok).T
    bias_w = _bias_lookup(rel_bias, n_win - jnp.arange(n_win)).T
    bias_0 = _bias_lookup(rel_bias, jnp.zeros((1,), jnp.int32)).T
    new_rows = jnp.stack([a.reshape(B, GD) for a in (ks, vs, kw, vw)], axis=1)
    gate_in = g_pre.reshape(B, H, 3) + gate_b

    const = lambda shape: pl.BlockSpec(shape, lambda b, pt: (0,) * len(shape))
    per_seq = lambda shape: pl.BlockSpec((1,) + shape, lambda b, pt: (b,) + (0,) * len(shape))
    any_spec = pl.BlockSpec(memory_space=pl.ANY)
    page_buf = pltpu.VMEM((2, n_pages, GD, PAGE_SIZE), f32)
    params = pltpu.CompilerParams(dimension_semantics=("arbitrary",), vmem_limit_bytes=V7X_VMEM_LIMIT_BYTES)

    o_cmp, sel = pl.pallas_call(
        functools.partial(_nsa_decode_cmp_kernel, n_cmp=n_cmp, n_blocks=n_blocks, n_select=min(N_SELECT, n_blocks)),
        grid_spec=pltpu.PrefetchScalarGridSpec(
            num_scalar_prefetch=1, grid=(B,),
            in_specs=[per_seq((H, GD)),
                      const((CMP_STRIDE // 2, 2 * GD, 2 * G * CMP_HID)),
                      const((CMP_STRIDE // 2, 2 * GD, 2 * G * CMP_HID)),
                      const((1, G * CMP_HID)), const((1, G * CMP_HID)),
                      const((G * CMP_HID, GD)), const((G * CMP_HID, GD)), const((1, GD)),
                      const((H, n_chunk)), const((n_chunk, nsp)), any_spec, any_spec],
            out_specs=[per_seq((H, GD)), per_seq((H, nsp))],
            scratch_shapes=[page_buf, page_buf, pltpu.VMEM((past, GD), f32),
                            pltpu.SemaphoreType.DMA((2,)), pltpu.SemaphoreType.DMA((2,))]),
        out_shape=[jax.ShapeDtypeStruct((B, H, GD), f32), jax.ShapeDtypeStruct((B, H, nsp), f32)],
        compiler_params=params, name="nsa_decode_cmp",
    )(page_table, qbd, chunk_weights(w1_k), chunk_weights(w1_v), pe_term(pe_k, w1_k), pe_term(pe_v, w1_v),
      w2_bd(w2_k), w2_bd(w2_v), jnp.tile(gk_cmp, G).reshape(1, GD), bias_c, mw, kc_pool, vc_pool)

    out = pl.pallas_call(
        _nsa_decode_attn_kernel,
        grid_spec=pltpu.PrefetchScalarGridSpec(
            num_scalar_prefetch=1, grid=(B,),
            in_specs=[per_seq((H, GD)), per_seq((H, nsp)), per_seq((H, GD)), per_seq((H, 3)), per_seq((4, GD)),
                      const((n_blk_past, past)), const((H, past)), const((H, n_win)), const((H, 1)),
                      per_seq((GD, n_win)), per_seq((GD, n_win)), any_spec, any_spec],
            out_specs=per_seq((H, GD)),
            scratch_shapes=[page_buf, page_buf, pltpu.VMEM((H, past), f32),
                            pltpu.SemaphoreType.DMA((2,)), pltpu.SemaphoreType.DMA((2,))]),
        out_shape=jax.ShapeDtypeStruct((B, H, GD), f32),
        compiler_params=params, name="nsa_decode_attn",
    )(page_table, qbd, sel, o_cmp, gate_in, new_rows, expand, bias_s, bias_w, bias_0, kw_t, vw_t, ks_pool, vs_pool)

    out = out.reshape(B, G, R, G, DH)
    h_b = jnp.stack([out[:, g, :, g, :] for g in range(G)], axis=1)
    return h_b.reshape(B, 1, NSA_W)


def _mla_decode_kernel(pt_ref, ql_ref, qr_ref, new_ref, newr_ref, cpool_ref, rpool_ref, o_ref,
                       cbuf, rbuf, cb16, s_ref, csem, rsem):
    f32, bf16 = jnp.float32, jnp.bfloat16
    slot = _paged_pipeline(pt_ref, (cpool_ref, rpool_ref), (cbuf, rbuf), (csem, rsem))
    n_pages = pt_ref.shape[1]
    ql, qr = ql_ref[0], qr_ref[0]
    for p in range(n_pages):
        c16 = cbuf[slot, p].astype(bf16)
        cb16[p] = c16
        s_ref[:, p * PAGE_SIZE:(p + 1) * PAGE_SIZE] = (
            lax.dot_general(ql, c16, (((1,), (1,)), ((), ())), preferred_element_type=f32)
            + jnp.dot(qr, rbuf[slot, p].astype(bf16), preferred_element_type=f32)) * MLA_SCALE
    c_new = new_ref[0].astype(bf16).astype(f32)
    r_new = newr_ref[0].astype(bf16).astype(f32)
    s_new = (jnp.sum(ql.astype(f32) * c_new, axis=-1, keepdims=True)
             + jnp.sum(qr.astype(f32) * r_new, axis=-1, keepdims=True)) * MLA_SCALE
    s_past = s_ref[...]
    m = jnp.maximum(jnp.max(s_past, axis=-1, keepdims=True), s_new)
    e_new = jnp.exp(s_new - m)
    s_ref[...] = jnp.exp(s_past - m)
    denom = jnp.sum(s_ref[...], axis=-1, keepdims=True) + e_new
    inv = 1.0 / denom
    acc = (e_new * inv).astype(bf16).astype(f32) * c_new
    for p in range(n_pages):
        pe = (s_ref[:, p * PAGE_SIZE:(p + 1) * PAGE_SIZE] * inv).astype(bf16)
        acc = acc + jnp.dot(pe, cb16[p], preferred_element_type=f32)
    o_ref[0] = acc


def mla_decode_attention(q_lat, q_rope, ckv_new, kr_new, page_table, ckv_pool, krope_pool):
    f32, bf16 = jnp.float32, jnp.bfloat16
    B, n_pages = page_table.shape
    H = q_lat.shape[1]
    past = n_pages * PAGE_SIZE
    rpool_t = jnp.transpose(krope_pool, (0, 2, 1))
    per_seq = lambda shape: pl.BlockSpec((1,) + shape, lambda b, pt: (b,) + (0,) * len(shape))
    any_spec = pl.BlockSpec(memory_space=pl.ANY)
    return pl.pallas_call(
        _mla_decode_kernel,
        grid_spec=pltpu.PrefetchScalarGridSpec(
            num_scalar_prefetch=1, grid=(B,),
            in_specs=[per_seq((H, KV_LORA)), per_seq((H, MLA_ROPE)), per_seq((1, KV_LORA)), per_seq((1, MLA_ROPE)),
                      any_spec, any_spec],
            out_specs=per_seq((H, KV_LORA)),
            scratch_shapes=[pltpu.VMEM((2, n_pages, PAGE_SIZE, KV_LORA), f32),
                            pltpu.VMEM((2, n_pages, MLA_ROPE, PAGE_SIZE), f32),
                            pltpu.VMEM((n_pages, PAGE_SIZE, KV_LORA), bf16),
                            pltpu.VMEM((H, past), f32),
                            pltpu.SemaphoreType.DMA((2,)), pltpu.SemaphoreType.DMA((2,))]),
        out_shape=jax.ShapeDtypeStruct((B, H, KV_LORA), f32),
        compiler_params=pltpu.CompilerParams(dimension_semantics=("arbitrary",),
                                             vmem_limit_bytes=V7X_VMEM_LIMIT_BYTES),
        name="mla_decode",
    )(page_table, q_lat.astype(bf16), q_rope.astype(bf16), ckv_new.reshape(B, 1, KV_LORA),
      kr_new.reshape(B, 1, MLA_ROPE), ckv_pool, rpool_t)


def split_cols(a, sizes):
    idx = [int(s) for s in np.cumsum(sizes)[:-1]]
    return jnp.split(a, idx, axis=-1)


def rms_norm(x, g):
    xf = x.astype(jnp.float32)
    y = xf * lax.rsqrt(jnp.mean(xf * xf, axis=-1, keepdims=True) + EPS)
    return (y * g.astype(jnp.float32)).astype(x.dtype)


def t5_bucket(dist):
    n = jnp.maximum(dist, 0)
    exact = REL_BUCKETS // 2
    nf = jnp.maximum(n, exact).astype(jnp.float32)
    large = exact + (jnp.log(nf / exact) / math.log(REL_MAX_DIST / exact) * (REL_BUCKETS - exact)).astype(jnp.int32)
    return jnp.where(n < exact, n, jnp.minimum(large, REL_BUCKETS - 1))


def apply_rope(x, pos):
    half = x.shape[-1] // 2
    inv = ROPE_THETA ** (-jnp.arange(half, dtype=jnp.float32) / half)
    ang = pos.astype(jnp.float32)[:, None] * inv[None, :]
    ang = ang.reshape(ang.shape[:1] + (1,) * (x.ndim - 3) + (half,))
    cos, sin = jnp.cos(ang).astype(x.dtype), jnp.sin(ang).astype(x.dtype)
    x1, x2 = x[..., :half], x[..., half:]
    return jnp.concatenate([x1 * cos - x2 * sin, x1 * sin + x2 * cos], axis=-1)


def causal_conv(u, buf, w, b):
    S = u.shape[1]
    full = jnp.concatenate([buf.astype(u.dtype), u], axis=1)
    out = b + sum(full[:, j:j + S] * w[j] for j in range(CONV_K))
    return out, full[:, S:]


def mlstm_chunkwise(q, k, v, i_pre, logf, C0, n0, m0):
    f32 = jnp.float32
    q, k, v, i_pre, logf = (a.astype(f32) for a in (q, k, v, i_pre, logf))
    B, H, S, D = q.shape
    L = MLSTM_CHUNK if S % MLSTM_CHUNK == 0 else S
    NC = S // L

    def chunks(a):
        return jnp.moveaxis(a.reshape((B, H, NC, L) + a.shape[3:]), 2, 0)

    causal = jnp.tril(jnp.ones((L, L), dtype=bool))

    def step(carry, inp):
        C, n, m = carry
        qc, kc, vc, ic, fc = inp
        b = jnp.cumsum(fc, axis=-1)
        g = b + m[..., None]
        dmat = jnp.where(causal, b[..., :, None] - b[..., None, :] + ic[..., None, :], -jnp.inf)
        mt = jnp.maximum(g, jnp.max(dmat, axis=-1))
        inter = jnp.exp(g - mt)
        sqk = jnp.einsum('bhtd,bhsd->bhts', qc, kc) * jnp.exp(dmat - mt[..., None])
        num = inter[..., None] * jnp.einsum('bhvd,bhtd->bhtv', C, qc) + jnp.einsum('bhts,bhsv->bhtv', sqk, vc)
        den = inter * jnp.einsum('bhd,bhtd->bht', n, qc) + jnp.sum(sqk, axis=-1)
        h = num / jnp.maximum(jnp.abs(den), jnp.exp(-mt))[..., None]
        b_end = b[..., -1]
        w_log = b_end[..., None] - b + ic
        m_new = jnp.maximum(b_end + m, jnp.max(w_log, axis=-1))
        decay = jnp.exp(b_end + m - m_new)
        w_in = jnp.exp(w_log - m_new[..., None])
        C_new = decay[..., None, None] * C + jnp.einsum('bhs,bhsv,bhsd->bhvd', w_in, vc, kc)
        n_new = decay[..., None] * n + jnp.einsum('bhs,bhsd->bhd', w_in, kc)
        return (C_new, n_new, m_new), h

    (C1, n1, m1), hs = lax.scan(step, (C0.astype(f32), n0.astype(f32), m0.astype(f32)),
                                tuple(chunks(a) for a in (q, k, v, i_pre, logf)))
    return jnp.moveaxis(hs, 0, 2).reshape(B, H, S, D), C1, n1, m1


def to_chunks(a):
    B, T = a.shape[:2]
    pad = (-T) % CMP_STRIDE
    a = jnp.pad(a, ((0, 0), (0, pad), (0, 0), (0, 0)))
    return a.reshape((B, (T + pad) // CMP_STRIDE, CMP_STRIDE) + a.shape[2:])


def cmp_summaries(chunk_list, T, pe, w1, w2):
    lo = jnp.concatenate([jnp.einsum('bcjgd,jdh->bcgh', r, w1[:CMP_STRIDE]) for r in chunk_list], axis=1)
    hi = jnp.concatenate([jnp.einsum('bcjgd,jdh->bcgh', r, w1[CMP_STRIDE:]) for r in chunk_list], axis=1)
    n_cmp = (T - CMP_LEN) // CMP_STRIDE + 1
    hid = jax.nn.silu(lo[:, :n_cmp] + hi[:, 1:n_cmp + 1] + jnp.einsum('jd,jdh->h', pe, w1))
    return hid @ w2


def even_mixer(xn, pos0, past, w_in, w_out, conv_w, conv_b, ml_wq, ml_wk, ml_b_i, ml_b_f, ml_out_g,
               nsa_gq, nsa_gk_cmp, nsa_gk_slc, nsa_gk_win, pe_k, w1_k, w2_k, pe_v, w1_v, w2_v, gate_b, rel_bias):
    B, S, _ = xn.shape
    dt = xn.dtype
    (u, v_m, o_pre, i_pre, f_pre, q, kc, vc, ks, vs, kw, vw, g_pre) = split_cols(xn @ w_in, EV_SPLITS)

    if past is None:
        conv_buf = jnp.zeros((B, CONV_K - 1, ML_W), dt)
        C0 = jnp.zeros((B, ML_H, ML_DH, ML_DH), jnp.float32)
        n0 = jnp.zeros((B, ML_H, ML_DH), jnp.float32)
        m0 = jnp.zeros((B, ML_H), jnp.float32)
    else:
        conv_buf, C0, n0, m0 = past['conv'], past['C'], past['n'], past['m']
    c, conv_new = causal_conv(u, conv_buf, conv_w, conv_b)
    ch = jax.nn.silu(c).reshape(B, S, ML_H, ML_DH)
    if past is None:
        qm = jnp.einsum('bshd,hde->bshe', ch, ml_wq).reshape(B, S, ML_W)
        km = (jnp.einsum('bshd,hde->bshe', ch, ml_wk) * (ML_DH ** -0.5)).reshape(B, S, ML_W)
        hm, C1, n1, m1 = mlstm_prompt(qm, km, v_m, i_pre + ml_b_i, jax.nn.log_sigmoid(f_pre + ml_b_f))
        hm = hm.reshape(B, S, ML_H, ML_DH)
    else:
        qm = jnp.einsum('bshd,hde->bhse', ch, ml_wq)
        km = jnp.einsum('bshd,hde->bhse', ch, ml_wk) * (ML_DH ** -0.5)
        vm = jnp.transpose(v_m.reshape(B, S, ML_H, ML_DH), (0, 2, 1, 3))
        ig = jnp.transpose(i_pre + ml_b_i, (0, 2, 1))
        lf = jax.nn.log_sigmoid(jnp.transpose(f_pre + ml_b_f, (0, 2, 1)).astype(jnp.float32))
        hm, C1, n1, m1 = mlstm_chunkwise(qm, km, vm, ig, lf, C0, n0, m0)
        hm = jnp.transpose(hm, (0, 2, 1, 3))
    hm = rms_norm(hm.astype(dt), ml_out_g) * jax.nn.sigmoid(o_pre).reshape(B, S, ML_H, ML_DH)
    h_a = hm.reshape(B, S, ML_W)

    q = rms_norm(q.reshape(B, S, NSA_H, NSA_DH), nsa_gq)
    kv_shape = (B, S, NSA_KV, NSA_DH)
    kc, vc, vs, vw = (a.reshape(kv_shape) for a in (kc, vc, vs, vw))
    ks = rms_norm(ks.reshape(kv_shape), nsa_gk_slc)
    kw = rms_norm(kw.reshape(kv_shape), nsa_gk_win)

    if past is None:
        k_cmp = rms_norm(cmp_summaries([to_chunks(kc)], S, pe_k, w1_k, w2_k), nsa_gk_cmp)
        v_cmp = cmp_summaries([to_chunks(vc)], S, pe_v, w1_v, w2_v)
        h_b = nsa_prompt(q, k_cmp, v_cmp, ks, vs, kw, vw, g_pre, gate_b, rel_bias)
        nb = min(WINDOW, S)
        win_k_new, win_v_new = kw[:, S - nb:], vw[:, S - nb:]
    else:
        assert S == 1
        e = past['e']
        h_b = nsa_decode(q, ks, vs, kw, vw, g_pre, gate_b, rel_bias, past['page_table'],
                         past['cmp_k'][e], past['cmp_v'][e], past['slc_k'][e], past['slc_v'][e],
                         past['win_k'], past['win_v'], pe_k, w1_k, w2_k, pe_v, w1_v, w2_v, nsa_gk_cmp)
        win_k_new = jnp.concatenate([past['win_k'][:, S:], kw], axis=1)
        win_v_new = jnp.concatenate([past['win_v'][:, S:], vw], axis=1)
    out = jnp.concatenate([h_a, h_b], axis=-1) @ w_out
    new = dict(C=C1.astype(dt), n=n1.astype(dt), m=m1.astype(dt), conv=conv_new,
               cmp_k=kc, cmp_v=vc, slc_k=ks, slc_v=vs, win_k=win_k_new, win_v=win_v_new)
    return out, new


def odd_mixer(xn, pos0, past, w_in, g_cq, w_uq, g_q, g_ckv, g_kr, w_uk, w_uv, w_out):
    B, S, _ = xn.shape
    cq, ckv, kr = split_cols(xn @ w_in, OD_SPLITS)
    tq = pos0 + jnp.arange(S)
    ckv = rms_norm(ckv, g_ckv)
    kr = apply_rope(rms_norm(kr, g_kr), tq)
    if past is None:
        bf16 = jnp.bfloat16
        o = mla_prompt_attention(rms_norm(cq, g_cq).astype(bf16), w_uq, g_q, tq, ckv.astype(bf16), kr.astype(bf16),
                                 jnp.transpose(w_uk, (1, 0, 2)).astype(bf16),
                                 jnp.transpose(w_uv, (1, 0, 2)).astype(bf16))
    else:
        assert S == 1
        e = past['e']
        q = rms_norm((rms_norm(cq, g_cq) @ w_uq).reshape(B, S, MLA_H, MLA_NOPE + MLA_ROPE), g_q)
        q_nope = q[..., :MLA_NOPE]
        q_rope = apply_rope(q[..., MLA_NOPE:], tq)
        q_lat = jnp.einsum('bqhn,chn->bqhc', q_nope, w_uk)
        o_lat = mla_decode_attention(q_lat[:, 0], q_rope[:, 0], ckv[:, 0], kr[:, 0], past['page_table'],
                                     past['ckv'][e], past['krope'][e])[:, None]
        o = jnp.einsum('bqhc,chv->bqhv', o_lat, w_uv).reshape(B, S, MLA_H * MLA_V)
    return o @ w_out, dict(ckv=ckv, krope=kr)


def mem_kv(mem, g_mem, wk, wv, gk):
    B, M, _ = mem.shape
    m = rms_norm(mem, g_mem)
    k = rms_norm((m @ wk).reshape(B, M, XM_H, XM_DH), gk)
    v = (m @ wv).reshape(B, M, XM_H, XM_DH)
    return k, v


def mem_attend(xn, k, v, wq, gq, wo):
    B, S, _ = xn.shape
    q = rms_norm((xn @ wq).reshape(B, S, XM_H, XM_DH), gq)
    s = jnp.einsum('bshd,bmhd->bhsm', q, k.astype(q.dtype)).astype(jnp.float32) * (XM_DH ** -0.5)
    p = jax.nn.softmax(s, axis=-1)
    return jnp.einsum('bhsm,bmhd->bshd', p.astype(xn.dtype), v.astype(xn.dtype)).reshape(B, S, XM_W) @ wo


def stack_key(lst, name):
    return jnp.stack([d[name] for d in lst])


def kernel(x_prompt, x_sample, mem_prompt,
           state_ml_C, state_ml_n, state_ml_m, state_ml_conv,
           cache_cmp_k, cache_cmp_v, cache_slc_k, cache_slc_v, cache_win_k, cache_win_v,
           cache_mla_ckv, cache_mla_krope, cache_mem_k, cache_mem_v, page_table,
           rel_bias, ffn1_norm, ffn1_wg, ffn1_wu, ffn1_wd, mix_norm,
           xm_norm, xm_mem_norm, xm_wq, xm_wk, xm_wv, xm_wo, xm_gq, xm_gk,
           ffn2_norm, ffn2_wg, ffn2_wu, ffn2_wd,
           ev_w_in, ev_w_out, ml_conv_w, ml_conv_b, ml_wq, ml_wk, ml_b_i, ml_b_f, ml_out_g,
           nsa_gq, nsa_gk_cmp, nsa_gk_slc, nsa_gk_win, cmp_pe_k, cmp_w1_k, cmp_w2_k,
           cmp_pe_v, cmp_w1_v, cmp_w2_v, nsa_gate_b,
           od_w_in, mla_g_cq, mla_w_uq, mla_g_q, mla_g_ckv, mla_g_kr, mla_w_uk, mla_w_uv, od_w_out):
    past_len = page_table.shape[1] * PAGE_SIZE
    bf = jnp.bfloat16
    ffn_w = [[(n[layer], wg[layer].astype(bf), wu[layer].astype(bf), wd[layer].astype(bf))
              for n, wg, wu, wd in ((ffn1_norm, ffn1_wg, ffn1_wu, ffn1_wd), (ffn2_norm, ffn2_wg, ffn2_wu, ffn2_wd))]
             for layer in range(DEPTH)]

    def run_group(y, prompt):
        ev, od, memk, memv = [], [], [], []
        for layer in range(DEPTH):
            y = swiglu_half(y, *ffn_w[layer][0])
            xn = rms_norm(y, mix_norm[layer])
            if layer % 2 == 0:
                e = layer // 2
                ew = dict(w_in=ev_w_in[e], w_out=ev_w_out[e], conv_w=ml_conv_w[e], conv_b=ml_conv_b[e],
                          ml_wq=ml_wq[e], ml_wk=ml_wk[e], ml_b_i=ml_b_i[e], ml_b_f=ml_b_f[e], ml_out_g=ml_out_g[e],
                          nsa_gq=nsa_gq[e], nsa_gk_cmp=nsa_gk_cmp[e], nsa_gk_slc=nsa_gk_slc[e],
                          nsa_gk_win=nsa_gk_win[e], pe_k=cmp_pe_k[e], w1_k=cmp_w1_k[e], w2_k=cmp_w2_k[e],
                          pe_v=cmp_pe_v[e], w1_v=cmp_w1_v[e], w2_v=cmp_w2_v[e],
                          gate_b=nsa_gate_b[e], rel_bias=rel_bias)
                past = None if prompt else dict(
                    e=e, page_table=page_table, C=state_ml_C[e], n=state_ml_n[e], m=state_ml_m[e],
                    conv=state_ml_conv[e], cmp_k=cache_cmp_k, cmp_v=cache_cmp_v,
                    slc_k=cache_slc_k, slc_v=cache_slc_v, win_k=cache_win_k[e], win_v=cache_win_v[e])
                h, st = even_mixer(xn, 0 if prompt else past_len, past, **ew)
                ev.append(st)
            else:
                o = layer // 2
                ow = dict(w_in=od_w_in[o], g_cq=mla_g_cq[o], w_uq=mla_w_uq[o], g_q=mla_g_q[o], g_ckv=mla_g_ckv[o],
                          g_kr=mla_g_kr[o], w_uk=mla_w_uk[o], w_uv=mla_w_uv[o], w_out=od_w_out[o])
                past = None if prompt else dict(e=o, page_table=page_table, ckv=cache_mla_ckv, krope=cache_mla_krope)
                h, st = odd_mixer(xn, 0 if prompt else past_len, past, **ow)
                od.append(st)
            y = y + h
            if prompt:
                mk, mv = mem_kv(mem_prompt, xm_mem_norm[layer], xm_wk[layer], xm_wv[layer], xm_gk[layer])
                memk.append(mk)
                memv.append(mv)
                y = mem_attention(y, xm_norm[layer], xm_wq[layer], xm_gq[layer], mk, mv, xm_wo[layer])
            else:
                y = y + mem_attend(rms_norm(y, xm_norm[layer]), cache_mem_k[layer], cache_mem_v[layer],
                                   xm_wq[layer], xm_gq[layer], xm_wo[layer])
            y = swiglu_half(y, *ffn_w[layer][1])
        return y, ev, od, memk, memv

    ys, ev_s, od_s, _, _ = run_group(x_sample, False)
    yp, ev_p, od_p, memk_p, memv_p = run_group(x_prompt, True)
    return (yp, ys,
            stack_key(ev_p, 'C'), stack_key(ev_p, 'n'), stack_key(ev_p, 'm'), stack_key(ev_p, 'conv'),
            stack_key(ev_p, 'cmp_k'), stack_key(ev_p, 'cmp_v'), stack_key(ev_p, 'slc_k'), stack_key(ev_p, 'slc_v'),
            stack_key(ev_p, 'win_k'), stack_key(ev_p, 'win_v'),
            stack_key(od_p, 'ckv'), stack_key(od_p, 'krope'),
            jnp.stack(memk_p), jnp.stack(memv_p),
            stack_key(ev_s, 'C'), stack_key(ev_s, 'n'), stack_key(ev_s, 'm'), stack_key(ev_s, 'conv'),
            stack_key(ev_s, 'cmp_k'), stack_key(ev_s, 'cmp_v'), stack_key(ev_s, 'slc_k'), stack_key(ev_s, 'slc_v'),
            stack_key(ev_s, 'win_k'), stack_key(ev_s, 'win_v'),
            stack_key(od_s, 'ckv'), stack_key(od_s, 'krope'))
```

```python
import functools
import math

import jax
import jax.numpy as jnp
import numpy as np
from jax import lax
from jax.experimental import pallas as pl
from jax.experimental.pallas import tpu as pltpu

D_MODEL = 1024
DEPTH = 2
PAGE_SIZE = 128
EPS = 1e-6
NEG_INF = -1e30
D_FF = 2816
ML_H = 4
ML_DH = 128
ML_W = ML_H * ML_DH
CONV_K = 4
MLSTM_CHUNK = 64
NSA_H = 8
NSA_KV = 2
NSA_R = NSA_H // NSA_KV
NSA_DH = 64
NSA_W = NSA_H * NSA_DH
NSA_KVW = NSA_KV * NSA_DH
NSA_SCALE = NSA_DH ** -0.5
CMP_STRIDE = 16
CMP_LEN = 2 * CMP_STRIDE
CMP_HID = 2 * NSA_DH
SLC_BLOCK = 64
SLC_RATIO = SLC_BLOCK // CMP_STRIDE
SLC_OVERLAP_W = (1.0, 2.0, 2.0, 2.0, 1.0)
N_SELECT = 16
WINDOW = 512
SLC_QBLOCK = 32
WIN_QBLOCK = 128
FORCE_SCORE = 1e6
REL_BUCKETS = 32
REL_MAX_DIST = 128
MLA_H = 16
MLA_NOPE = 64
MLA_ROPE = 32
MLA_V = 64
Q_LORA = 384
KV_LORA = 256
MLA_SCALE = (MLA_NOPE + MLA_ROPE) ** -0.5
MLA_QBLOCK = 128
ROPE_THETA = 10000.0
MEM_LEN = 256
XM_H = 4
XM_DH = 128
XM_W = XM_H * XM_DH
EV_SPLITS = (ML_W, ML_W, ML_W, ML_H, ML_H, NSA_W) + (NSA_KVW,) * 6 + (NSA_H * 3,)
OD_SPLITS = (Q_LORA, KV_LORA, MLA_ROPE)

V7X_LANES = 128
V7X_VMEM_LIMIT_BYTES = 56 * 1024 * 1024
FFN_CHUNK = 256
FFN_ROWS = 512


def _ffn_kernel(x_ref, g_ref, wg_ref, wu_ref, wd_ref, o_ref, act_ref):
    x = x_ref[...]
    h = x * lax.rsqrt(jnp.mean(x * x, axis=-1, keepdims=True) + EPS) * g_ref[...]
    hb = h.astype(jnp.bfloat16)
    for c in range(D_FF // FFN_CHUNK):
        cols = slice(c * FFN_CHUNK, (c + 1) * FFN_CHUNK)
        gate = jnp.dot(hb, wg_ref[:, cols], preferred_element_type=jnp.float32)
        up = jnp.dot(hb, wu_ref[:, cols], preferred_element_type=jnp.float32)
        act_ref[:, cols] = (gate * jax.nn.sigmoid(gate) * up).astype(jnp.bfloat16)
    o_ref[...] = x + 0.5 * jnp.dot(act_ref[...], wd_ref[...], preferred_element_type=jnp.float32)


def ffn_half(x2d, g, wg, wu, wd):
    m = x2d.shape[0]
    tm = min(FFN_ROWS, m)
    assert m % tm == 0
    resident = functools.partial(pl.BlockSpec, pipeline_mode=pl.Buffered(1))
    return pl.pallas_call(
        _ffn_kernel,
        grid=(m // tm,),
        in_specs=[
            pl.BlockSpec((tm, D_MODEL), lambda i: (i, 0)),
            resident((1, D_MODEL), lambda i: (0, 0)),
            resident((D_MODEL, D_FF), lambda i: (0, 0)),
            resident((D_MODEL, D_FF), lambda i: (0, 0)),
            resident((D_FF, D_MODEL), lambda i: (0, 0)),
        ],
        out_specs=pl.BlockSpec((tm, D_MODEL), lambda i: (i, 0)),
        out_shape=jax.ShapeDtypeStruct((m, D_MODEL), jnp.float32),
        scratch_shapes=[pltpu.VMEM((tm, D_FF), jnp.bfloat16)],
        compiler_params=pltpu.CompilerParams(
            dimension_semantics=("arbitrary",), vmem_limit_bytes=V7X_VMEM_LIMIT_BYTES),
        name="ffn_half",
    )(x2d, g.reshape(1, D_MODEL), wg, wu, wd)


def swiglu_half(x, g, wg, wu, wd):
    shp = x.shape
    return ffn_half(x.reshape(-1, D_MODEL), g, wg, wu, wd).reshape(shp)


NSA_TQ = 256
NSA_NEAR = 128


def _nsa_prompt_kernel(q_ref, kc_ref, vc_ref, ks_ref, vs_ref, kw_ref, vw_ref, bc_ref, bn_ref,
                       e_ref, mw_ref, gp_ref, gb_ref, o_ref, *, n_cmp, n_select):
    f32, bf16 = jnp.float32, jnp.bfloat16
    qi = pl.program_id(2)
    R, TQ, DH = q_ref.shape[2:]
    S = ks_ref.shape[2]
    t0 = qi * TQ
    q2 = q_ref[0, 0].reshape(R * TQ, DH)
    t_col = t0 + lax.broadcasted_iota(jnp.int32, (TQ, 1), 0)

    def scores(k):
        s = lax.dot_general(q2, k, (((1,), (1,)), ((), ())), preferred_element_type=f32)
        return s.reshape(R, TQ, k.shape[0])

    ncp = kc_ref.shape[2]
    c_row = lax.broadcasted_iota(jnp.int32, (1, ncp), 1)
    mask_c = (t_col >= c_row * CMP_STRIDE + (CMP_LEN - 1)) & (c_row < n_cmp)
    s = jnp.where(mask_c[None], scores(kc_ref[0, 0]) + bc_ref[0], NEG_INF)
    e = jnp.exp(s - jnp.max(s, axis=-1, keepdims=True))
    p = e / jnp.sum(e, axis=-1, keepdims=True) * mask_c.astype(f32)[None]
    o_cmp = jnp.dot(p.reshape(R * TQ, ncp).astype(bf16), vc_ref[0, 0],
                    preferred_element_type=f32).reshape(R, TQ, DH)
    imp = jnp.sum(p, axis=0)

    mw = mw_ref[...]
    hi = imp.astype(bf16)
    r1 = imp - hi.astype(f32)
    mid = r1.astype(bf16)
    lo = (r1 - mid.astype(f32)).astype(bf16)
    p_slc = (jnp.dot(hi, mw, preferred_element_type=f32) + jnp.dot(mid, mw, preferred_element_type=f32)
             + jnp.dot(lo, mw, preferred_element_type=f32))
    ns = mw_ref.shape[1]
    jb = lax.broadcasted_iota(jnp.int32, (1, ns), 1)
    tb = jnp.right_shift(t_col, int(math.log2(SLC_BLOCK)))
    forced = (jb == 0) | (jb == tb) | (jb == tb - 1)
    score = jnp.where(jb <= tb, p_slc + FORCE_SCORE * forced.astype(f32), -1.0)
    rank = jnp.zeros((TQ, ns), f32)
    for j in range(ns):
        col = score[:, j:j + 1]
        rank = rank + ((col > score) | ((col == score) & (jb > j))).astype(f32)
    sel = (rank < n_select).astype(bf16)
    gates = jax.nn.sigmoid(gp_ref[0, 0] + gb_ref[0])
    q = q_ref[0, 0]

    def attend(r, k, v, add_mask, first_key, t_start):
        s = lax.dot_general(q[r], k, (((1,), (1,)), ((), ())), preferred_element_type=f32) + add_mask
        near_lo = max(t_start - NSA_NEAR, first_key)
        band = bn_ref[0, r][:, near_lo - (t_start - NSA_NEAR):]
        cut = near_lo - first_key
        near = s[:, cut:] + band
        s = near if cut == 0 else jnp.concatenate([s[:, :cut], near], axis=1)
        e = jnp.exp(s - jnp.max(s, axis=-1, keepdims=True))
        return jnp.dot(e.astype(bf16), v, preferred_element_type=f32) / jnp.sum(e, axis=-1, keepdims=True)

    for c in range(S // TQ):
        @pl.when(qi == c)
        def _(c=c):
            t_start, n_keys = c * TQ, (c + 1) * TQ
            row = t_start + lax.broadcasted_iota(jnp.int32, (TQ, 1), 0)
            col = lax.broadcasted_iota(jnp.int32, (1, n_keys), 1)
            sel_tok = jnp.dot(sel, e_ref[:, :n_keys], preferred_element_type=f32)
            mask_s = jnp.where((sel_tok > 0.5) & (col <= row), 0.0, NEG_INF)
            w_lo = max(t_start - WINDOW, 0)
            col_w = w_lo + lax.broadcasted_iota(jnp.int32, (1, n_keys - w_lo), 1)
            mask_w = jnp.where((col_w <= row) & (row - col_w <= WINDOW), 0.0, NEG_INF)
            ks, vs = ks_ref[0, 0, :n_keys, :], vs_ref[0, 0, :n_keys, :]
            kw, vw = kw_ref[0, 0, w_lo:n_keys, :], vw_ref[0, 0, w_lo:n_keys, :]
            outs = []
            for r in range(R):
                o_slc = attend(r, ks, vs, mask_s, 0, t_start)
                o_win = attend(r, kw, vw, mask_w, w_lo, t_start)
                outs.append(gates[:, 3 * r:3 * r + 1] * o_cmp[r] + gates[:, 3 * r + 1:3 * r + 2] * o_slc
                            + gates[:, 3 * r + 2:3 * r + 3] * o_win)
            o_ref[0] = jnp.concatenate(outs, axis=-1)


def _bias_lookup(rel_bias, dist):
    bucket = t5_bucket(dist)[..., None]
    out = jnp.zeros(dist.shape + rel_bias.shape[1:], rel_bias.dtype)
    for b in range(REL_BUCKETS):
        out = jnp.where(bucket == b, rel_bias[b], out)
    return out


def nsa_prompt(qn, k_cmp, v_cmp, ks, vs, kw, vw, g_pre, gate_b, rel_bias):
    f32, bf16 = jnp.float32, jnp.bfloat16
    B, S = qn.shape[:2]
    G, R, DH, TQ = NSA_KV, NSA_R, NSA_DH, NSA_TQ
    assert S % TQ == 0 and TQ % V7X_LANES == 0 and WINDOW % V7X_LANES == 0 and NSA_NEAR % V7X_LANES == 0
    n_cmp = k_cmp.shape[1]
    ncp = S // CMP_STRIDE
    ns = S // SLC_BLOCK
    assert n_cmp == ncp - 1
    n_select = min(N_SELECT, ns)

    def kv_layout(a, n):
        a = jnp.pad(a, ((0, 0), (0, n - a.shape[1]), (0, 0), (0, 0)))
        return jnp.transpose(a, (0, 2, 1, 3)).astype(bf16)

    q5 = jnp.transpose((qn * NSA_SCALE).reshape(B, S, G, R, DH), (0, 2, 3, 1, 4)).astype(bf16)
    kc, vc = kv_layout(k_cmp, ncp), kv_layout(v_cmp, ncp)
    ks_t, vs_t, kw_t, vw_t = (kv_layout(a, S) for a in (ks, vs, kw, vw))

    half = REL_BUCKETS // 2
    assert half + int(math.log(NSA_NEAR / half) / math.log(REL_MAX_DIST / half) * half) >= REL_BUCKETS - 1
    t = jnp.arange(S)
    dist_c = t[:, None] - (jnp.arange(ncp) * CMP_STRIDE + CMP_LEN - 1)[None, :]
    bias_c = jnp.transpose(_bias_lookup(rel_bias, dist_c).reshape(S, ncp, G, R), (2, 3, 0, 1))
    d_near = jnp.arange(TQ)[:, None] + NSA_NEAR - jnp.arange(TQ + NSA_NEAR)[None, :]
    bias_n = jnp.transpose((_bias_lookup(rel_bias, d_near) - rel_bias[REL_BUCKETS - 1]).reshape(
        TQ, TQ + NSA_NEAR, G, R), (2, 3, 0, 1))
    expand = (jnp.arange(S)[None, :] // SLC_BLOCK == jnp.arange(ns)[:, None]).astype(bf16)
    c_i = jnp.arange(ncp)[:, None]
    j_i = jnp.arange(ns)[None, :]
    mw = sum(w * (c_i == SLC_RATIO * j_i + k - 1) for k, w in enumerate(SLC_OVERLAP_W)).astype(bf16)
    gp = jnp.transpose(g_pre.reshape(B, S, G, 3 * R), (0, 2, 1, 3))
    gb = gate_b.reshape(G, 1, 3 * R)

    kv_spec = lambda n: pl.BlockSpec((1, 1, n, DH), lambda b, g, i: (b, g, 0, 0))
    return pl.pallas_call(
        functools.partial(_nsa_prompt_kernel, n_cmp=n_cmp, n_select=n_select),
        grid=(B, G, S // TQ),
        in_specs=[
            pl.BlockSpec((1, 1, R, TQ, DH), lambda b, g, i: (b, g, 0, i, 0)),
            kv_spec(ncp), kv_spec(ncp), kv_spec(S), kv_spec(S), kv_spec(S), kv_spec(S),
            pl.BlockSpec((1, R, TQ, ncp), lambda b, g, i: (g, 0, i, 0)),
            pl.BlockSpec((1, R, TQ, TQ + NSA_NEAR), lambda b, g, i: (g, 0, 0, 0)),
            pl.BlockSpec((ns, S), lambda b, g, i: (0, 0)),
            pl.BlockSpec((ncp, ns), lambda b, g, i: (0, 0)),
            pl.BlockSpec((1, 1, TQ, 3 * R), lambda b, g, i: (b, g, i, 0)),
            pl.BlockSpec((1, 1, 3 * R), lambda b, g, i: (g, 0, 0)),
        ],
        out_specs=pl.BlockSpec((1, TQ, R * DH), lambda b, g, i: (b, i, g)),
        out_shape=jax.ShapeDtypeStruct((B, S, NSA_W), f32),
        compiler_params=pltpu.CompilerParams(
            dimension_semantics=("arbitrary", "arbitrary", "arbitrary"), vmem_limit_bytes=V7X_VMEM_LIMIT_BYTES),
        name="nsa_prompt",
    )(q5, kc, vc, ks_t, vs_t, kw_t, vw_t, bias_c, bias_n, expand, mw, gp, gb)


MLA_TQ = 512
MLA_HEADS_PER_STEP = 2


def _mla_prompt_kernel(cq_ref, wuq_ref, ra_ref, rm_ref, rp_ref, ckv_ref, kr_ref, wuk_ref, wuv_ref,
                       o_ref, q_ref, k_ref, v_ref, s_ref):
    f32, bf16 = jnp.float32, jnp.bfloat16
    qi = pl.program_id(2)
    HP, S, DP = k_ref.shape
    TQ = cq_ref.shape[1]
    causal = lax.broadcasted_iota(jnp.int32, (TQ, 1), 0) >= lax.broadcasted_iota(jnp.int32, (1, TQ), 1)

    @pl.when(qi == 0)
    def _():
        ckv = ckv_ref[0]
        pad = jnp.zeros((S, DP - MLA_NOPE - MLA_ROPE), bf16)
        for h in range(HP):
            k_nope = jnp.dot(ckv, wuk_ref[h], preferred_element_type=f32).astype(bf16)
            k_ref[h] = jnp.concatenate([k_nope, kr_ref[0], pad], axis=-1)
            v_ref[h] = jnp.dot(ckv, wuv_ref[h], preferred_element_type=f32).astype(bf16)

    for h in range(HP):
        q3 = jnp.dot(cq_ref[0], wuq_ref[h], preferred_element_type=f32)
        q = q3[:, :DP]
        ms = jnp.sum(q * q, axis=-1, keepdims=True) * (1.0 / (MLA_NOPE + MLA_ROPE))
        q = (q * ra_ref[...] + q3[:, DP:2 * DP] * rm_ref[...] + q3[:, 2 * DP:] * rp_ref[...]) * lax.rsqrt(ms + EPS)
        q_ref[h] = q.astype(bf16)

    for c in range(S // TQ):
        @pl.when(qi == c)
        def _(c=c):
            n_keys = (c + 1) * TQ
            outs = []
            for h in range(HP):
                s_ref[h, :, :n_keys] = lax.dot_general(
                    q_ref[h], k_ref[h, :n_keys, :], (((1,), (1,)), ((), ())),
                    preferred_element_type=f32) * MLA_SCALE
                s_ref[h, :, n_keys - TQ:n_keys] = jnp.where(causal, s_ref[h, :, n_keys - TQ:n_keys], NEG_INF)
                s = s_ref[h, :, :n_keys]
                e = jnp.exp(s - jnp.max(s, axis=-1, keepdims=True))
                pv = jnp.dot(e.astype(bf16), v_ref[h, :n_keys, :], preferred_element_type=f32)
                outs.append(pv / jnp.sum(e, axis=-1, keepdims=True))
            o_ref[0] = jnp.concatenate(outs, axis=-1)


def mla_prompt_attention(cq, w_uq, g_q, pos, ckv, kr, w_uk, w_uv):
    f32, bf16 = jnp.float32, jnp.bfloat16
    B, S, _ = cq.shape
    H, DQ, DP = MLA_H, MLA_NOPE + MLA_ROPE, V7X_LANES
    HP, TQ = MLA_HEADS_PER_STEP, min(MLA_TQ, S)
    half = MLA_ROPE // 2
    assert S % TQ == 0 and H % HP == 0 and HP * MLA_V == V7X_LANES and DQ <= DP
    w = jnp.transpose(w_uq.reshape(Q_LORA, H, DQ), (1, 0, 2))
    zw = lambda n: jnp.zeros((H, Q_LORA, n), f32)
    w_left = jnp.concatenate([zw(MLA_NOPE), w[..., MLA_NOPE + half:], zw(half)], axis=-1)
    w_right = jnp.concatenate([zw(MLA_NOPE + half), w[..., MLA_NOPE:MLA_NOPE + half]], axis=-1)
    pad_w = lambda a: jnp.pad(a, ((0, 0), (0, 0), (0, DP - DQ)))
    wuq_p = jnp.concatenate([pad_w(w), pad_w(w_left), pad_w(w_right)], axis=-1).astype(bf16)
    inv = ROPE_THETA ** (-jnp.arange(half, dtype=f32) / half)
    ang = pos.astype(f32)[:, None] * inv[None, :]
    cos, sin, zero = jnp.cos(ang), jnp.sin(ang), jnp.zeros((S, half), f32)
    g_nope, g_lo, g_hi = g_q[:MLA_NOPE], g_q[MLA_NOPE:MLA_NOPE + half], g_q[MLA_NOPE + half:]
    tail = jnp.zeros((S, DP - DQ), f32)
    lead0 = jnp.zeros((S, MLA_NOPE), f32)
    rope_a = jnp.concatenate([jnp.broadcast_to(g_nope, (S, MLA_NOPE)), cos * g_lo, cos * g_hi, tail], axis=1)
    rope_m = jnp.concatenate([lead0, -sin * g_hi, zero, tail], axis=1)
    rope_p = jnp.concatenate([lead0, zero, sin * g_lo, tail], axis=1)
    row_tab = pl.BlockSpec((TQ, DP), lambda b, h, i: (i, 0))
    return pl.pallas_call(
        _mla_prompt_kernel,
        grid=(B, H // HP, S // TQ),
        in_specs=[
            pl.BlockSpec((1, TQ, Q_LORA), lambda b, h, i: (b, i, 0)),
            pl.BlockSpec((HP, Q_LORA, 3 * DP), lambda b, h, i: (h, 0, 0)),
            row_tab, row_tab, row_tab,
            pl.BlockSpec((1, S, KV_LORA), lambda b, h, i: (b, 0, 0)),
            pl.BlockSpec((1, S, MLA_ROPE), lambda b, h, i: (b, 0, 0)),
            pl.BlockSpec((HP, KV_LORA, MLA_NOPE), lambda b, h, i: (h, 0, 0)),
            pl.BlockSpec((HP, KV_LORA, MLA_V), lambda b, h, i: (h, 0, 0)),
        ],
        out_specs=pl.BlockSpec((1, TQ, HP * MLA_V), lambda b, h, i: (b, i, h)),
        out_shape=jax.ShapeDtypeStruct((B, S, H * MLA_V), jnp.float32),
        scratch_shapes=[pltpu.VMEM((HP, TQ, DP), bf16), pltpu.VMEM((HP, S, DP), bf16),
                        pltpu.VMEM((HP, S, MLA_V), bf16), pltpu.VMEM((HP, TQ, S), f32)],
        compiler_params=pltpu.CompilerParams(
            dimension_semantics=("arbitrary", "arbitrary", "arbitrary"), vmem_limit_bytes=V7X_VMEM_LIMIT_BYTES),
        name="mla_prompt",
    )(cq, wuq_p, rope_a, rope_m, rope_p, ckv, kr, w_uk, w_uv)


def _split3(x):
    bf16, f32 = jnp.bfloat16, jnp.float32
    hi = x.astype(bf16)
    r = x - hi.astype(f32)
    mid = r.astype(bf16)
    return hi, mid, (r - mid.astype(f32)).astype(bf16)


def _mlstm_kernel(u_ref, v_ref, og_ref, cw_ref, cb_ref, wq_ref, wk_ref, g_ref, ir_ref, fr_ref, it_ref, ft_ref,
                  tri_ref, h_ref, c_ref, n_ref, m_ref):
    f32, bf16 = jnp.float32, jnp.bfloat16
    H, NC, L = ir_ref.shape[1:]
    DH = ML_DH
    PRE = 8
    tri = tri_ref[...]
    causal = (tri > 0.5)[None]

    def bmm(a, b, ca, cb):
        return lax.dot_general(a, b, (((ca,), (cb,)), ((0,), (0,))), preferred_element_type=f32)

    def split_heads(x):
        return jnp.stack([x[:, h * DH:(h + 1) * DH] for h in range(H)], axis=0)

    def conv_act(c):
        cur = u_ref[0, c * L:(c + 1) * L, :]
        prev = jnp.zeros((PRE, cur.shape[1]), f32) if c == 0 else u_ref[0, c * L - PRE:c * L, :]
        full = jnp.concatenate([prev, cur], axis=0)
        acc = cb_ref[...]
        for j in range(CONV_K):
            lo = PRE - (CONV_K - 1) + j
            acc = acc + full[lo:lo + L, :] * cw_ref[j:j + 1, :]
        return acc * jax.nn.sigmoid(acc)

    b_rows = jnp.stack([sum(lax.dot_general(p, tri, (((1,), (1,)), ((), ())), preferred_element_type=f32)
                            for p in _split3(fr_ref[0, h])) for h in range(H)], axis=0)
    b_cols = jnp.stack([sum(jnp.dot(tri, p, preferred_element_type=f32) for p in _split3(ft_ref[0, h]))
                        for h in range(H)], axis=0)
    i_rows, i_cols = ir_ref[0], it_ref[0]
    C = jnp.zeros((H, DH, DH), f32)
    n = jnp.zeros((H, 1, DH), f32)
    m = jnp.zeros((H, 1, 1), f32)
    for c in range(NC):
        rows = slice(c * L, (c + 1) * L)
        ch = split_heads(conv_act(c)).astype(bf16)
        q = bmm(ch, wq_ref[...], 2, 1).astype(bf16)
        k = (bmm(ch, wk_ref[...], 2, 1) * (DH ** -0.5)).astype(bf16)
        v = split_heads(v_ref[0, rows, :])
        b_row, i_row = b_rows[:, c:c + 1, :], i_rows[:, c:c + 1, :]
        b_col, i_col = b_cols[:, :, c:c + 1], i_cols[:, :, c:c + 1]
        g = b_col + m
        dmat = jnp.where(causal, b_col - b_row + i_row, -jnp.inf)
        mt = jnp.maximum(g, jnp.max(dmat, axis=-1, keepdims=True))
        inter = jnp.exp(g - mt)
        sqk = bmm(q, k, 2, 2) * jnp.exp(dmat - mt)
        num = inter * bmm(q, C.astype(bf16), 2, 2) + bmm(sqk.astype(bf16), v.astype(bf16), 2, 1)
        nq = jnp.sum(q.astype(f32) * n.astype(bf16).astype(f32), axis=-1, keepdims=True)
        den = inter * nq + jnp.sum(sqk, axis=-1, keepdims=True)
        hs = num / jnp.maximum(jnp.abs(den), jnp.exp(-mt))
        hs = hs * lax.rsqrt(jnp.mean(hs * hs, axis=-1, keepdims=True) + EPS)
        gate = jax.nn.sigmoid(og_ref[0, rows, :])
        for h in range(H):
            lanes = slice(h * DH, (h + 1) * DH)
            h_ref[0, rows, lanes] = hs[h] * g_ref[:, lanes] * gate[:, lanes]
        b_end = b_row[:, :, L - 1:L]
        w_row = b_end - b_row + i_row
        m_new = jnp.maximum(b_end + m, jnp.max(w_row, axis=-1, keepdims=True))
        decay = jnp.exp(b_end + m - m_new)
        w_col = jnp.exp(b_end - b_col + i_col - m_new)
        C = decay * C + bmm((w_col * v).astype(bf16), k, 1, 1)
        n = decay * n + bmm(jnp.exp(w_row - m_new).astype(bf16), k, 2, 1)
        m = m_new
    c_ref[0] = C
    n_ref[0] = n
    m_ref[0] = jnp.broadcast_to(m, (H, 1, V7X_LANES))


def mlstm_prompt(u, v, o_pre, conv_w, conv_b, wq, wk, out_g, i_pre, logf):
    f32, bf16 = jnp.float32, jnp.bfloat16
    B, S, W = u.shape
    H, L = ML_H, MLSTM_CHUNK
    assert S % L == 0 and W == H * ML_DH and L % 8 == 0 and CONV_K - 1 <= 8
    const = lambda shape: pl.BlockSpec(shape, lambda b: (0,) * len(shape))
    NC = S // L
    rows = lambda a: jnp.transpose(a, (0, 2, 1)).reshape(B, H, NC, L)
    cols = lambda a: jnp.transpose(rows(a), (0, 1, 3, 2))
    tri = jnp.tril(jnp.ones((L, L), jnp.bfloat16))
    seq = pl.BlockSpec((1, S, W), lambda b: (b, 0, 0))
    gate_r = pl.BlockSpec((1, H, NC, L), lambda b: (b, 0, 0, 0))
    gate_c = pl.BlockSpec((1, H, L, NC), lambda b: (b, 0, 0, 0))
    hs, C, n, m = pl.pallas_call(
        _mlstm_kernel,
        grid=(B,),
        in_specs=[seq, seq, seq, const((CONV_K, W)), const((1, W)), const((H, ML_DH, ML_DH)),
                  const((H, ML_DH, ML_DH)), const((1, W)), gate_r, gate_r, gate_c, gate_c, const((L, L))],
        out_specs=[seq, pl.BlockSpec((1, H, ML_DH, ML_DH), lambda b: (b, 0, 0, 0)),
                   pl.BlockSpec((1, H, 1, ML_DH), lambda b: (b, 0, 0, 0)),
                   pl.BlockSpec((1, H, 1, V7X_LANES), lambda b: (b, 0, 0, 0))],
        out_shape=[jax.ShapeDtypeStruct((B, S, W), f32), jax.ShapeDtypeStruct((B, H, ML_DH, ML_DH), f32),
                   jax.ShapeDtypeStruct((B, H, 1, ML_DH), f32), jax.ShapeDtypeStruct((B, H, 1, V7X_LANES), f32)],
        compiler_params=pltpu.CompilerParams(dimension_semantics=("arbitrary",),
                                             vmem_limit_bytes=V7X_VMEM_LIMIT_BYTES),
        name="mlstm_prompt",
    )(u, v, o_pre, conv_w, conv_b.reshape(1, W), wq.astype(bf16), wk.astype(bf16), out_g.reshape(1, W),
      rows(i_pre), rows(logf), cols(i_pre), cols(logf), tri)
    return hs, C, n[:, :, 0], m[:, :, 0, 0]


MEM_ROWS = 512


def _mem_attn_kernel(x_ref, g_ref, wq_ref, gq_ref, k_ref, v_ref, wo_ref, o_ref):
    f32, bf16 = jnp.float32, jnp.bfloat16
    x = x_ref[0]
    xn = (x * lax.rsqrt(jnp.mean(x * x, axis=-1, keepdims=True) + EPS) * g_ref[...]).astype(bf16)
    q = jnp.dot(xn, wq_ref[...], preferred_element_type=f32)
    k, v = k_ref[0], v_ref[0]
    outs = []
    for h in range(XM_H):
        cols = slice(h * XM_DH, (h + 1) * XM_DH)
        qh = q[:, cols]
        qh = (qh * lax.rsqrt(jnp.mean(qh * qh, axis=-1, keepdims=True) + EPS) * gq_ref[...]).astype(bf16)
        s = lax.dot_general(qh, k[:, cols], (((1,), (1,)), ((), ())), preferred_element_type=f32) * (XM_DH ** -0.5)
        e = jnp.exp(s - jnp.max(s, axis=-1, keepdims=True))
        p = (e / jnp.sum(e, axis=-1, keepdims=True)).astype(bf16)
        outs.append(jnp.dot(p, v[:, cols], preferred_element_type=f32))
    o = jnp.concatenate(outs, axis=-1).astype(bf16)
    o_ref[0] = x + jnp.dot(o, wo_ref[...], preferred_element_type=f32)


def mem_attention(y, g, wq, gq, k, v, wo):
    bf16 = jnp.bfloat16
    B, S, D = y.shape
    M = k.shape[1]
    ts = min(MEM_ROWS, S)
    assert S % ts == 0
    const = lambda shape: pl.BlockSpec(shape, lambda b, i: (0,) * len(shape), pipeline_mode=pl.Buffered(1))
    kv_spec = pl.BlockSpec((1, M, XM_W), lambda b, i: (b, 0, 0))
    return pl.pallas_call(
        _mem_attn_kernel,
        grid=(B, S // ts),
        in_specs=[pl.BlockSpec((1, ts, D), lambda b, i: (b, i, 0)), const((1, D)), const((D, XM_W)),
                  const((1, XM_DH)), kv_spec, kv_spec, const((XM_W, D))],
        out_specs=pl.BlockSpec((1, ts, D), lambda b, i: (b, i, 0)),
        out_shape=jax.ShapeDtypeStruct((B, S, D), jnp.float32),
        compiler_params=pltpu.CompilerParams(
            dimension_semantics=("arbitrary", "arbitrary"), vmem_limit_bytes=V7X_VMEM_LIMIT_BYTES),
        name="mem_attn",
    )(y, g.reshape(1, D), wq.astype(bf16), gq.reshape(1, XM_DH), k.reshape(B, M, XM_W).astype(bf16),
      v.reshape(B, M, XM_W).astype(bf16), wo.astype(bf16))


DECODE_XPOSE_UNROLL = 8


def _paged_copy(pool_ref, page, buf_ref, slot, p, sem_ref):
    return pltpu.make_async_copy(pool_ref.at[page], buf_ref.at[slot, p], sem_ref.at[slot])


def _paged_pipeline(pt_ref, pools, bufs, sems):
    b = pl.program_id(0)
    n_pages = pt_ref.shape[1]

    def start(seq, slot):
        def body(p, carry):
            for pool, buf, sem in zip(pools, bufs, sems):
                _paged_copy(pool, pt_ref[seq, p], buf, slot, p, sem).start()
            return carry
        lax.fori_loop(0, n_pages, body, 0)

    @pl.when(b == 0)
    def _():
        start(0, 0)

    @pl.when(b + 1 < pl.num_programs(0))
    def _():
        start(b + 1, (b + 1) % 2)

    slot = b % 2

    def wait_body(p, carry):
        for pool, buf, sem in zip(pools, bufs, sems):
            _paged_copy(pool, 0, buf, slot, p, sem).wait()
        return carry
    lax.fori_loop(0, n_pages, wait_body, 0)
    return slot


def _softmax_rows(s, valid):
    e = jnp.exp(s - jnp.max(s, axis=-1, keepdims=True))
    return e / jnp.sum(e, axis=-1, keepdims=True) * valid


def _nsa_decode_cmp_kernel(pt_ref, qbd_ref, wk_ref, wv_ref, pek_ref, pev_ref, w2k_ref, w2v_ref, gk_ref,
                           bc_ref, mw_ref, kpool_ref, vpool_ref, ocmp_ref, sel_ref,
                           kbuf, vbuf, xrm_ref, ksem, vsem, *, n_cmp, n_blocks, n_select):
    f32, bf16 = jnp.float32, jnp.bfloat16
    slot = _paged_pipeline(pt_ref, (kpool_ref, vpool_ref), (kbuf, vbuf), (ksem, vsem))
    n_pages = pt_ref.shape[1]
    n_chunk = n_pages * (PAGE_SIZE // CMP_STRIDE)
    G, DH, HID = NSA_KV, NSA_DH, CMP_HID

    def summaries(buf, w_ref, pe_ref, w2_ref):
        def xpose(i, carry):
            for u in range(DECODE_XPOSE_UNROLL):
                p = i * DECODE_XPOSE_UNROLL + u
                xrm_ref[pl.ds(pl.multiple_of(p * PAGE_SIZE, PAGE_SIZE), PAGE_SIZE), :] = buf[slot, p].T
            return carry
        lax.fori_loop(0, n_pages // DECODE_XPOSE_UNROLL, xpose, 0)
        acc = jnp.zeros((n_chunk, 2 * G * HID), f32)
        for j in range(0, CMP_STRIDE, 2):
            rows = jnp.concatenate([xrm_ref[pl.ds(j + u, n_chunk, stride=CMP_STRIDE), :].astype(bf16)
                                    for u in range(2)], axis=1)
            acc = acc + jnp.dot(rows, w_ref[j // 2], preferred_element_type=f32)
        lo, hi = acc[:, :G * HID], acc[:, G * HID:]
        pre = lo + pltpu.roll(hi, n_chunk - 1, 0) + pe_ref[...]
        hid = pre * jax.nn.sigmoid(pre)
        return jnp.dot(hid.astype(bf16), w2_ref[...], preferred_element_type=f32)

    k_sum = summaries(kbuf, wk_ref, pek_ref, w2k_ref)
    lane = lax.broadcasted_iota(jnp.int32, (1, G * DH), 1)
    sq = k_sum * k_sum
    s_all = jnp.sum(sq, axis=-1, keepdims=True)
    s_g0 = jnp.sum(jnp.where(lane < DH, sq, 0.0), axis=-1, keepdims=True)
    ms = jnp.where(lane < DH, s_g0, s_all - s_g0) * (1.0 / DH)
    k_cmp = (k_sum * lax.rsqrt(ms + EPS) * gk_ref[...]).astype(bf16)
    v_cmp = summaries(vbuf, wv_ref, pev_ref, w2v_ref).astype(bf16)

    qbd = qbd_ref[0]
    H = qbd.shape[0]
    c_row = lax.broadcasted_iota(jnp.int32, (1, n_chunk), 1)
    valid = (c_row < n_cmp).astype(f32)
    s = lax.dot_general(qbd, k_cmp, (((1,), (1,)), ((), ())), preferred_element_type=f32) + bc_ref[...]
    p = _softmax_rows(jnp.where(valid > 0.5, s, NEG_INF), valid)
    ocmp_ref[0] = jnp.dot(p.astype(bf16), v_cmp, preferred_element_type=f32)

    R = H // G
    head = lax.broadcasted_iota(jnp.int32, (H, 1), 0)
    imp = jnp.where(head < R, jnp.sum(p[:R], axis=0, keepdims=True), jnp.sum(p[R:], axis=0, keepdims=True))
    mw = mw_ref[...]
    hi_p = imp.astype(bf16)
    r1 = imp - hi_p.astype(f32)
    mid_p = r1.astype(bf16)
    lo_p = (r1 - mid_p.astype(f32)).astype(bf16)
    p_slc = (jnp.dot(hi_p, mw, preferred_element_type=f32) + jnp.dot(mid_p, mw, preferred_element_type=f32)
             + jnp.dot(lo_p, mw, preferred_element_type=f32))
    nsp = mw_ref.shape[1]
    tb = n_blocks - 1
    jb = lax.broadcasted_iota(jnp.int32, (1, nsp), 1)
    forced = (jb == 0) | (jb == tb) | (jb == tb - 1)
    score = jnp.where(jb <= tb, p_slc + FORCE_SCORE * forced.astype(f32), -1.0)
    j_col = lax.broadcasted_iota(jnp.int32, (nsp, 1), 0)
    sels = []
    for g in range(G):
        row = score[g * R:g * R + 1, :]
        col = jnp.broadcast_to(row, (nsp, nsp)).T
        beats = (col > row) | ((col == row) & (j_col < jb))
        rank = jnp.sum(beats.astype(f32), axis=0, keepdims=True)
        sels.append(jnp.broadcast_to((rank < n_select).astype(f32), (R, nsp)))
    sel_ref[0] = jnp.concatenate(sels, axis=0)


def _nsa_decode_attn_kernel(pt_ref, qbd_ref, sel_ref, ocmp_ref, gate_ref, new_ref, e_ref, bs_ref, bw_ref, b0_ref,
                            kw_ref, vw_ref, kpool_ref, vpool_ref, o_ref, kbuf, vbuf, s_ref, ksem, vsem):
    f32, bf16 = jnp.float32, jnp.bfloat16
    slot = _paged_pipeline(pt_ref, (kpool_ref, vpool_ref), (kbuf, vbuf), (ksem, vsem))
    n_pages = pt_ref.shape[1]
    qbd = qbd_ref[0]
    qf = qbd.astype(f32)
    new = new_ref[0]
    b0 = b0_ref[...]

    def new_score(k_row):
        return jnp.sum(qf * k_row.astype(bf16).astype(f32), axis=-1, keepdims=True) + b0

    for p in range(n_pages):
        s_ref[:, p * PAGE_SIZE:(p + 1) * PAGE_SIZE] = jnp.dot(qbd, kbuf[slot, p].astype(bf16),
                                                              preferred_element_type=f32)
    n_blk_past = e_ref.shape[0]
    sel_tok = jnp.dot(sel_ref[0][:, :n_blk_past].astype(bf16), e_ref[...], preferred_element_type=f32)
    s_past = jnp.where(sel_tok > 0.5, s_ref[...] + bs_ref[...], NEG_INF)
    s_new = new_score(new[0:1])
    m = jnp.maximum(jnp.max(s_past, axis=-1, keepdims=True), s_new)
    e_new = jnp.exp(s_new - m)
    s_ref[...] = jnp.exp(s_past - m)
    denom = jnp.sum(s_ref[...], axis=-1, keepdims=True) + e_new
    acc = e_new.astype(bf16).astype(f32) * new[1:2].astype(bf16).astype(f32)
    for p in range(n_pages):
        pe = s_ref[:, p * PAGE_SIZE:(p + 1) * PAGE_SIZE].astype(bf16)
        acc = acc + lax.dot_general(pe, vbuf[slot, p].astype(bf16), (((1,), (1,)), ((), ())),
                                    preferred_element_type=f32)
    o_slc = acc / denom

    s_w = jnp.dot(qbd, kw_ref[0].astype(bf16), preferred_element_type=f32) + bw_ref[...]
    s_wn = new_score(new[2:3])
    m = jnp.maximum(jnp.max(s_w, axis=-1, keepdims=True), s_wn)
    e_w, e_wn = jnp.exp(s_w - m), jnp.exp(s_wn - m)
    denom = jnp.sum(e_w, axis=-1, keepdims=True) + e_wn
    acc = (lax.dot_general(e_w.astype(bf16), vw_ref[0].astype(bf16), (((1,), (1,)), ((), ())),
                           preferred_element_type=f32)
           + e_wn.astype(bf16).astype(f32) * new[3:4].astype(bf16).astype(f32))
    o_win = acc / denom

    gates = jax.nn.sigmoid(gate_ref[0])
    o_ref[0] = gates[:, 0:1] * ocmp_ref[0] + gates[:, 1:2] * o_slc + gates[:, 2:3] * o_win


def _block_diag_heads(x):
    B, H, DH = x.shape
    g_of_h = jnp.arange(H) // (H // NSA_KV)
    onehot = (g_of_h[:, None] == jnp.arange(NSA_KV)[None, :]).astype(x.dtype)
    return (x[:, :, None, :] * onehot[None, :, :, None]).reshape(B, H, NSA_KV * DH)


def nsa_decode(qn, ks, vs, kw, vw, g_pre, gate_b, rel_bias, page_table, cmp_k_pool, cmp_v_pool,
               slc_k_pool, slc_v_pool, win_k, win_v, pe_k, w1_k, w2_k, pe_v, w1_v, w2_v, gk_cmp):
    f32, bf16 = jnp.float32, jnp.bfloat16
    B, n_pages = page_table.shape
    G, R, DH, H = NSA_KV, NSA_R, NSA_DH, NSA_H
    GD = G * DH
    past = n_pages * PAGE_SIZE
    n_chunk = past // CMP_STRIDE
    n_cmp = (past + 1 - CMP_LEN) // CMP_STRIDE + 1
    n_blocks = -(-(past + 1) // SLC_BLOCK)
    n_blk_past = past // SLC_BLOCK
    nsp = -(-n_blocks // V7X_LANES) * V7X_LANES
    n_win = win_k.shape[1]
    assert GD == V7X_LANES and PAGE_SIZE == V7X_LANES and n_blocks >= N_SELECT and n_win == WINDOW

    def pool_view(pool):
        return jnp.transpose(pool, (0, 2, 3, 1)).reshape(pool.shape[0], GD, PAGE_SIZE)
    kc_pool, vc_pool, ks_pool, vs_pool = (pool_view(a) for a in (cmp_k_pool, cmp_v_pool, slc_k_pool, slc_v_pool))
    kw_t = jnp.transpose(win_k, (0, 2, 3, 1)).reshape(B, GD, n_win)
    vw_t = jnp.transpose(win_v, (0, 2, 3, 1)).reshape(B, GD, n_win)

    qbd = _block_diag_heads((qn[:, 0] * NSA_SCALE)).astype(bf16)
    eye = jnp.eye(G, dtype=f32)

    def chunk_weights(w1):
        def bd(w):
            return jnp.einsum('jdh,gk->jgdkh', w, eye).reshape(CMP_STRIDE, GD, G * CMP_HID)
        w = jnp.concatenate([bd(w1[:CMP_STRIDE]), bd(w1[CMP_STRIDE:])], axis=-1)
        return w.reshape(CMP_STRIDE // 2, 2 * GD, 2 * G * CMP_HID).astype(bf16)

    def pe_term(pe, w1):
        return jnp.tile(jnp.einsum('jd,jdh->h', pe, w1), G).reshape(1, G * CMP_HID)

    def w2_bd(w2):
        return jnp.einsum('hd,gk->ghkd', w2, eye).reshape(G * CMP_HID, GD).astype(bf16)

    c_i = jnp.arange(n_chunk)
    bias_c = _bias_lookup(rel_bias, past - (c_i * CMP_STRIDE + CMP_LEN - 1)).T
    j_i = jnp.arange(nsp)[None, :]
    mw = sum(w * ((c_i[:, None] == SLC_RATIO * j_i + k - 1) & (j_i < n_blocks))
             for k, w in enumerate(SLC_OVERLAP_W)).astype(bf16)
    tok = jnp.arange(past)
    expand = (tok[None, :] // SLC_BLOCK == jnp.arange(n_blk_past)[:, None]).astype(bf16)
    bias_s = _bias_lookup(rel_bias, past - tok).T
    bias_w = _bias_lookup(rel_bias, n_win - jnp.arange(n_win)).T
    bias_0 = _bias_lookup(rel_bias, jnp.zeros((1,), jnp.int32)).T
    new_rows = jnp.stack([a.reshape(B, GD) for a in (ks, vs, kw, vw)], axis=1)
    gate_in = g_pre.reshape(B, H, 3) + gate_b

    const = lambda shape: pl.BlockSpec(shape, lambda b, pt: (0,) * len(shape))
    per_seq = lambda shape: pl.BlockSpec((1,) + shape, lambda b, pt: (b,) + (0,) * len(shape))
    any_spec = pl.BlockSpec(memory_space=pl.ANY)
    page_buf = pltpu.VMEM((2, n_pages, GD, PAGE_SIZE), f32)
    params = pltpu.CompilerParams(dimension_semantics=("arbitrary",), vmem_limit_bytes=V7X_VMEM_LIMIT_BYTES)

    o_cmp, sel = pl.pallas_call(
        functools.partial(_nsa_decode_cmp_kernel, n_cmp=n_cmp, n_blocks=n_blocks, n_select=min(N_SELECT, n_blocks)),
        grid_spec=pltpu.PrefetchScalarGridSpec(
            num_scalar_prefetch=1, grid=(B,),
            in_specs=[per_seq((H, GD)),
                      const((CMP_STRIDE // 2, 2 * GD, 2 * G * CMP_HID)),
                      const((CMP_STRIDE // 2, 2 * GD, 2 * G * CMP_HID)),
                      const((1, G * CMP_HID)), const((1, G * CMP_HID)),
                      const((G * CMP_HID, GD)), const((G * CMP_HID, GD)), const((1, GD)),
                      const((H, n_chunk)), const((n_chunk, nsp)), any_spec, any_spec],
            out_specs=[per_seq((H, GD)), per_seq((H, nsp))],
            scratch_shapes=[page_buf, page_buf, pltpu.VMEM((past, GD), f32),
                            pltpu.SemaphoreType.DMA((2,)), pltpu.SemaphoreType.DMA((2,))]),
        out_shape=[jax.ShapeDtypeStruct((B, H, GD), f32), jax.ShapeDtypeStruct((B, H, nsp), f32)],
        compiler_params=params, name="nsa_decode_cmp",
    )(page_table, qbd, chunk_weights(w1_k), chunk_weights(w1_v), pe_term(pe_k, w1_k), pe_term(pe_v, w1_v),
      w2_bd(w2_k), w2_bd(w2_v), jnp.tile(gk_cmp, G).reshape(1, GD), bias_c, mw, kc_pool, vc_pool)

    out = pl.pallas_call(
        _nsa_decode_attn_kernel,
        grid_spec=pltpu.PrefetchScalarGridSpec(
            num_scalar_prefetch=1, grid=(B,),
            in_specs=[per_seq((H, GD)), per_seq((H, nsp)), per_seq((H, GD)), per_seq((H, 3)), per_seq((4, GD)),
                      const((n_blk_past, past)), const((H, past)), const((H, n_win)), const((H, 1)),
                      per_seq((GD, n_win)), per_seq((GD, n_win)), any_spec, any_spec],
            out_specs=per_seq((H, GD)),
            scratch_shapes=[page_buf, page_buf, pltpu.VMEM((H, past), f32),
                            pltpu.SemaphoreType.DMA((2,)), pltpu.SemaphoreType.DMA((2,))]),
        out_shape=jax.ShapeDtypeStruct((B, H, GD), f32),
        compiler_params=params, name="nsa_decode_attn",
    )(page_table, qbd, sel, o_cmp, gate_in, new_rows, expand, bias_s, bias_w, bias_0, kw_t, vw_t, ks_pool, vs_pool)

    out = out.reshape(B, G, R, G, DH)
    h_b = jnp.stack([out[:, g, :, g, :] for g in range(G)], axis=1)
    return h_b.reshape(B, 1, NSA_W)


def _mla_decode_kernel(pt_ref, ql_ref, qr_ref, new_ref, newr_ref, cpool_ref, rpool_ref, o_ref,
                       cbuf, rbuf, cb16, s_ref, csem, rsem):
    f32, bf16 = jnp.float32, jnp.bfloat16
    slot = _paged_pipeline(pt_ref, (cpool_ref, rpool_ref), (cbuf, rbuf), (csem, rsem))
    n_pages = pt_ref.shape[1]
    ql, qr = ql_ref[0], qr_ref[0]
    for p in range(n_pages):
        c16 = cbuf[slot, p].astype(bf16)
        cb16[p] = c16
        s_ref[:, p * PAGE_SIZE:(p + 1) * PAGE_SIZE] = (
            lax.dot_general(ql, c16, (((1,), (1,)), ((), ())), preferred_element_type=f32)
            + jnp.dot(qr, rbuf[slot, p].astype(bf16), preferred_element_type=f32)) * MLA_SCALE
    c_new = new_ref[0].astype(bf16).astype(f32)
    r_new = newr_ref[0].astype(bf16).astype(f32)
    s_new = (jnp.sum(ql.astype(f32) * c_new, axis=-1, keepdims=True)
             + jnp.sum(qr.astype(f32) * r_new, axis=-1, keepdims=True)) * MLA_SCALE
    s_past = s_ref[...]
    m = jnp.maximum(jnp.max(s_past, axis=-1, keepdims=True), s_new)
    e_new = jnp.exp(s_new - m)
    s_ref[...] = jnp.exp(s_past - m)
    denom = jnp.sum(s_ref[...], axis=-1, keepdims=True) + e_new
    inv = 1.0 / denom
    acc = (e_new * inv).astype(bf16).astype(f32) * c_new
    for p in range(n_pages):
        pe = (s_ref[:, p * PAGE_SIZE:(p + 1) * PAGE_SIZE] * inv).astype(bf16)
        acc = acc + jnp.dot(pe, cb16[p], preferred_element_type=f32)
    o_ref[0] = acc


def mla_decode_attention(q_lat, q_rope, ckv_new, kr_new, page_table, ckv_pool, krope_pool):
    f32, bf16 = jnp.float32, jnp.bfloat16
    B, n_pages = page_table.shape
    H = q_lat.shape[1]
    past = n_pages * PAGE_SIZE
    rpool_t = jnp.transpose(krope_pool, (0, 2, 1))
    per_seq = lambda shape: pl.BlockSpec((1,) + shape, lambda b, pt: (b,) + (0,) * len(shape))
    any_spec = pl.BlockSpec(memory_space=pl.ANY)
    return pl.pallas_call(
        _mla_decode_kernel,
        grid_spec=pltpu.PrefetchScalarGridSpec(
            num_scalar_prefetch=1, grid=(B,),
            in_specs=[per_seq((H, KV_LORA)), per_seq((H, MLA_ROPE)), per_seq((1, KV_LORA)), per_seq((1, MLA_ROPE)),
                      any_spec, any_spec],
            out_specs=per_seq((H, KV_LORA)),
            scratch_shapes=[pltpu.VMEM((2, n_pages, PAGE_SIZE, KV_LORA), f32),
                            pltpu.VMEM((2, n_pages, MLA_ROPE, PAGE_SIZE), f32),
                            pltpu.VMEM((n_pages, PAGE_SIZE, KV_LORA), bf16),
                            pltpu.VMEM((H, past), f32),
                            pltpu.SemaphoreType.DMA((2,)), pltpu.SemaphoreType.DMA((2,))]),
        out_shape=jax.ShapeDtypeStruct((B, H, KV_LORA), f32),
        compiler_params=pltpu.CompilerParams(dimension_semantics=("arbitrary",),
                                             vmem_limit_bytes=V7X_VMEM_LIMIT_BYTES),
        name="mla_decode",
    )(page_table, q_lat.astype(bf16), q_rope.astype(bf16), ckv_new.reshape(B, 1, KV_LORA),
      kr_new.reshape(B, 1, MLA_ROPE), ckv_pool, rpool_t)


def split_cols(a, sizes):
    idx = [int(s) for s in np.cumsum(sizes)[:-1]]
    return jnp.split(a, idx, axis=-1)


def rms_norm(x, g):
    xf = x.astype(jnp.float32)
    y = xf * lax.rsqrt(jnp.mean(xf * xf, axis=-1, keepdims=True) + EPS)
    return (y * g.astype(jnp.float32)).astype(x.dtype)


def t5_bucket(dist):
    n = jnp.maximum(dist, 0)
    exact = REL_BUCKETS // 2
    nf = jnp.maximum(n, exact).astype(jnp.float32)
    large = exact + (jnp.log(nf / exact) / math.log(REL_MAX_DIST / exact) * (REL_BUCKETS - exact)).astype(jnp.int32)
    return jnp.where(n < exact, n, jnp.minimum(large, REL_BUCKETS - 1))


def apply_rope(x, pos):
    half = x.shape[-1] // 2
    inv = ROPE_THETA ** (-jnp.arange(half, dtype=jnp.float32) / half)
    ang = pos.astype(jnp.float32)[:, None] * inv[None, :]
    ang = ang.reshape(ang.shape[:1] + (1,) * (x.ndim - 3) + (half,))
    cos, sin = jnp.cos(ang).astype(x.dtype), jnp.sin(ang).astype(x.dtype)
    x1, x2 = x[..., :half], x[..., half:]
    return jnp.concatenate([x1 * cos - x2 * sin, x1 * sin + x2 * cos], axis=-1)


def causal_conv(u, buf, w, b):
    S = u.shape[1]
    full = jnp.concatenate([buf.astype(u.dtype), u], axis=1)
    out = b + sum(full[:, j:j + S] * w[j] for j in range(CONV_K))
    return out, full[:, S:]


def mlstm_chunkwise(q, k, v, i_pre, logf, C0, n0, m0):
    f32 = jnp.float32
    q, k, v, i_pre, logf = (a.astype(f32) for a in (q, k, v, i_pre, logf))
    B, H, S, D = q.shape
    L = MLSTM_CHUNK if S % MLSTM_CHUNK == 0 else S
    NC = S // L

    def chunks(a):
        return jnp.moveaxis(a.reshape((B, H, NC, L) + a.shape[3:]), 2, 0)

    causal = jnp.tril(jnp.ones((L, L), dtype=bool))

    def step(carry, inp):
        C, n, m = carry
        qc, kc, vc, ic, fc = inp
        b = jnp.cumsum(fc, axis=-1)
        g = b + m[..., None]
        dmat = jnp.where(causal, b[..., :, None] - b[..., None, :] + ic[..., None, :], -jnp.inf)
        mt = jnp.maximum(g, jnp.max(dmat, axis=-1))
        inter = jnp.exp(g - mt)
        sqk = jnp.einsum('bhtd,bhsd->bhts', qc, kc) * jnp.exp(dmat - mt[..., None])
        num = inter[..., None] * jnp.einsum('bhvd,bhtd->bhtv', C, qc) + jnp.einsum('bhts,bhsv->bhtv', sqk, vc)
        den = inter * jnp.einsum('bhd,bhtd->bht', n, qc) + jnp.sum(sqk, axis=-1)
        h = num / jnp.maximum(jnp.abs(den), jnp.exp(-mt))[..., None]
        b_end = b[..., -1]
        w_log = b_end[..., None] - b + ic
        m_new = jnp.maximum(b_end + m, jnp.max(w_log, axis=-1))
        decay = jnp.exp(b_end + m - m_new)
        w_in = jnp.exp(w_log - m_new[..., None])
        C_new = decay[..., None, None] * C + jnp.einsum('bhs,bhsv,bhsd->bhvd', w_in, vc, kc)
        n_new = decay[..., None] * n + jnp.einsum('bhs,bhsd->bhd', w_in, kc)
        return (C_new, n_new, m_new), h

    (C1, n1, m1), hs = lax.scan(step, (C0.astype(f32), n0.astype(f32), m0.astype(f32)),
                                tuple(chunks(a) for a in (q, k, v, i_pre, logf)))
    return jnp.moveaxis(hs, 0, 2).reshape(B, H, S, D), C1, n1, m1


def to_chunks(a):
    B, T = a.shape[:2]
    pad = (-T) % CMP_STRIDE
    a = jnp.pad(a, ((0, 0), (0, pad), (0, 0), (0, 0)))
    return a.reshape((B, (T + pad) // CMP_STRIDE, CMP_STRIDE) + a.shape[2:])


def cmp_summaries(chunk_list, T, pe, w1, w2):
    lo = jnp.concatenate([jnp.einsum('bcjgd,jdh->bcgh', r, w1[:CMP_STRIDE]) for r in chunk_list], axis=1)
    hi = jnp.concatenate([jnp.einsum('bcjgd,jdh->bcgh', r, w1[CMP_STRIDE:]) for r in chunk_list], axis=1)
    n_cmp = (T - CMP_LEN) // CMP_STRIDE + 1
    hid = jax.nn.silu(lo[:, :n_cmp] + hi[:, 1:n_cmp + 1] + jnp.einsum('jd,jdh->h', pe, w1))
    return hid @ w2


def even_mixer(xn, pos0, past, w_in, w_out, conv_w, conv_b, ml_wq, ml_wk, ml_b_i, ml_b_f, ml_out_g,
               nsa_gq, nsa_gk_cmp, nsa_gk_slc, nsa_gk_win, pe_k, w1_k, w2_k, pe_v, w1_v, w2_v, gate_b, rel_bias):
    B, S, _ = xn.shape
    dt = xn.dtype
    (u, v_m, o_pre, i_pre, f_pre, q, kc, vc, ks, vs, kw, vw, g_pre) = split_cols(xn @ w_in, EV_SPLITS)

    if past is None:
        assert S >= CONV_K - 1
        h_a, C1, n1, m1 = mlstm_prompt(u, v_m, o_pre, conv_w, conv_b, ml_wq, ml_wk, ml_out_g,
                                       i_pre + ml_b_i, jax.nn.log_sigmoid(f_pre + ml_b_f))
        conv_new = u[:, S - (CONV_K - 1):]
    else:
        conv_buf, C0, n0, m0 = past['conv'], past['C'], past['n'], past['m']
        c, conv_new = causal_conv(u, conv_buf, conv_w, conv_b)
        ch = jax.nn.silu(c).reshape(B, S, ML_H, ML_DH)
        qm = jnp.einsum('bshd,hde->bhse', ch, ml_wq)
        km = jnp.einsum('bshd,hde->bhse', ch, ml_wk) * (ML_DH ** -0.5)
        vm = jnp.transpose(v_m.reshape(B, S, ML_H, ML_DH), (0, 2, 1, 3))
        ig = jnp.transpose(i_pre + ml_b_i, (0, 2, 1))
        lf = jax.nn.log_sigmoid(jnp.transpose(f_pre + ml_b_f, (0, 2, 1)).astype(jnp.float32))
        hm, C1, n1, m1 = mlstm_chunkwise(qm, km, vm, ig, lf, C0, n0, m0)
        hm = rms_norm(jnp.transpose(hm, (0, 2, 1, 3)).astype(dt), ml_out_g)
        h_a = (hm * jax.nn.sigmoid(o_pre).reshape(B, S, ML_H, ML_DH)).reshape(B, S, ML_W)

    q = rms_norm(q.reshape(B, S, NSA_H, NSA_DH), nsa_gq)
    kv_shape = (B, S, NSA_KV, NSA_DH)
    kc, vc, vs, vw = (a.reshape(kv_shape) for a in (kc, vc, vs, vw))
    ks = rms_norm(ks.reshape(kv_shape), nsa_gk_slc)
    kw = rms_norm(kw.reshape(kv_shape), nsa_gk_win)

    if past is None:
        k_cmp = rms_norm(cmp_summaries([to_chunks(kc)], S, pe_k, w1_k, w2_k), nsa_gk_cmp)
        v_cmp = cmp_summaries([to_chunks(vc)], S, pe_v, w1_v, w2_v)
        h_b = nsa_prompt(q, k_cmp, v_cmp, ks, vs, kw, vw, g_pre, gate_b, rel_bias)
        nb = min(WINDOW, S)
        win_k_new, win_v_new = kw[:, S - nb:], vw[:, S - nb:]
    else:
        assert S == 1
        e = past['e']
        h_b = nsa_decode(q, ks, vs, kw, vw, g_pre, gate_b, rel_bias, past['page_table'],
                         past['cmp_k'][e], past['cmp_v'][e], past['slc_k'][e], past['slc_v'][e],
                         past['win_k'], past['win_v'], pe_k, w1_k, w2_k, pe_v, w1_v, w2_v, nsa_gk_cmp)
        win_k_new = jnp.concatenate([past['win_k'][:, S:], kw], axis=1)
        win_v_new = jnp.concatenate([past['win_v'][:, S:], vw], axis=1)
    out = jnp.concatenate([h_a, h_b], axis=-1) @ w_out
    new = dict(C=C1.astype(dt), n=n1.astype(dt), m=m1.astype(dt), conv=conv_new,
               cmp_k=kc, cmp_v=vc, slc_k=ks, slc_v=vs, win_k=win_k_new, win_v=win_v_new)
    return out, new


def odd_mixer(xn, pos0, past, w_in, g_cq, w_uq, g_q, g_ckv, g_kr, w_uk, w_uv, w_out):
    B, S, _ = xn.shape
    cq, ckv, kr = split_cols(xn @ w_in, OD_SPLITS)
    tq = pos0 + jnp.arange(S)
    ckv = rms_norm(ckv, g_ckv)
    kr = apply_rope(rms_norm(kr, g_kr), tq)
    if past is None:
        bf16 = jnp.bfloat16
        o = mla_prompt_attention(rms_norm(cq, g_cq).astype(bf16), w_uq, g_q, tq, ckv.astype(bf16), kr.astype(bf16),
                                 jnp.transpose(w_uk, (1, 0, 2)).astype(bf16),
                                 jnp.transpose(w_uv, (1, 0, 2)).astype(bf16))
    else:
        assert S == 1
        e = past['e']
        q = rms_norm((rms_norm(cq, g_cq) @ w_uq).reshape(B, S, MLA_H, MLA_NOPE + MLA_ROPE), g_q)
        q_nope = q[..., :MLA_NOPE]
        q_rope = apply_rope(q[..., MLA_NOPE:], tq)
        q_lat = jnp.einsum('bqhn,chn->bqhc', q_nope, w_uk)
        o_lat = mla_decode_attention(q_lat[:, 0], q_rope[:, 0], ckv[:, 0], kr[:, 0], past['page_table'],
                                     past['ckv'][e], past['krope'][e])[:, None]
        o = jnp.einsum('bqhc,chv->bqhv', o_lat, w_uv).reshape(B, S, MLA_H * MLA_V)
    return o @ w_out, dict(ckv=ckv, krope=kr)


def mem_kv(mem, g_mem, wk, wv, gk):
    B, M, _ = mem.shape
    m = rms_norm(mem, g_mem)
    k = rms_norm((m @ wk).reshape(B, M, XM_H, XM_DH), gk)
    v = (m @ wv).reshape(B, M, XM_H, XM_DH)
    return k, v


def mem_attend(xn, k, v, wq, gq, wo):
    B, S, _ = xn.shape
    q = rms_norm((xn @ wq).reshape(B, S, XM_H, XM_DH), gq)
    s = jnp.einsum('bshd,bmhd->bhsm', q, k.astype(q.dtype)).astype(jnp.float32) * (XM_DH ** -0.5)
    p = jax.nn.softmax(s, axis=-1)
    return jnp.einsum('bhsm,bmhd->bshd', p.astype(xn.dtype), v.astype(xn.dtype)).reshape(B, S, XM_W) @ wo


def stack_key(lst, name):
    return jnp.stack([d[name] for d in lst])


def kernel(x_prompt, x_sample, mem_prompt,
           state_ml_C, state_ml_n, state_ml_m, state_ml_conv,
           cache_cmp_k, cache_cmp_v, cache_slc_k, cache_slc_v, cache_win_k, cache_win_v,
           cache_mla_ckv, cache_mla_krope, cache_mem_k, cache_mem_v, page_table,
           rel_bias, ffn1_norm, ffn1_wg, ffn1_wu, ffn1_wd, mix_norm,
           xm_norm, xm_mem_norm, xm_wq, xm_wk, xm_wv, xm_wo, xm_gq, xm_gk,
           ffn2_norm, ffn2_wg, ffn2_wu, ffn2_wd,
           ev_w_in, ev_w_out, ml_conv_w, ml_conv_b, ml_wq, ml_wk, ml_b_i, ml_b_f, ml_out_g,
           nsa_gq, nsa_gk_cmp, nsa_gk_slc, nsa_gk_win, cmp_pe_k, cmp_w1_k, cmp_w2_k,
           cmp_pe_v, cmp_w1_v, cmp_w2_v, nsa_gate_b,
           od_w_in, mla_g_cq, mla_w_uq, mla_g_q, mla_g_ckv, mla_g_kr, mla_w_uk, mla_w_uv, od_w_out):
    past_len = page_table.shape[1] * PAGE_SIZE
    bf = jnp.bfloat16
    ffn_w = [[(n[layer], wg[layer].astype(bf), wu[layer].astype(bf), wd[layer].astype(bf))
              for n, wg, wu, wd in ((ffn1_norm, ffn1_wg, ffn1_wu, ffn1_wd), (ffn2_norm, ffn2_wg, ffn2_wu, ffn2_wd))]
             for layer in range(DEPTH)]

    def run_group(y, prompt):
        ev, od, memk, memv = [], [], [], []
        for layer in range(DEPTH):
            y = swiglu_half(y, *ffn_w[layer][0])
            xn = rms_norm(y, mix_norm[layer])
            if layer % 2 == 0:
                e = layer // 2
                ew = dict(w_in=ev_w_in[e], w_out=ev_w_out[e], conv_w=ml_conv_w[e], conv_b=ml_conv_b[e],
                          ml_wq=ml_wq[e], ml_wk=ml_wk[e], ml_b_i=ml_b_i[e], ml_b_f=ml_b_f[e], ml_out_g=ml_out_g[e],
                          nsa_gq=nsa_gq[e], nsa_gk_cmp=nsa_gk_cmp[e], nsa_gk_slc=nsa_gk_slc[e],
                          nsa_gk_win=nsa_gk_win[e], pe_k=cmp_pe_k[e], w1_k=cmp_w1_k[e], w2_k=cmp_w2_k[e],
                          pe_v=cmp_pe_v[e], w1_v=cmp_w1_v[e], w2_v=cmp_w2_v[e],
                          gate_b=nsa_gate_b[e], rel_bias=rel_bias)
                past = None if prompt else dict(
                    e=e, page_table=page_table, C=state_ml_C[e], n=state_ml_n[e], m=state_ml_m[e],
                    conv=state_ml_conv[e], cmp_k=cache_cmp_k, cmp_v=cache_cmp_v,
                    slc_k=cache_slc_k, slc_v=cache_slc_v, win_k=cache_win_k[e], win_v=cache_win_v[e])
                h, st = even_mixer(xn, 0 if prompt else past_len, past, **ew)
                ev.append(st)
            else:
                o = layer // 2
                ow = dict(w_in=od_w_in[o], g_cq=mla_g_cq[o], w_uq=mla_w_uq[o], g_q=mla_g_q[o], g_ckv=mla_g_ckv[o],
                          g_kr=mla_g_kr[o], w_uk=mla_w_uk[o], w_uv=mla_w_uv[o], w_out=od_w_out[o])
                past = None if prompt else dict(e=o, page_table=page_table, ckv=cache_mla_ckv, krope=cache_mla_krope)
                h, st = odd_mixer(xn, 0 if prompt else past_len, past, **ow)
                od.append(st)
            y = y + h
            if prompt:
                mk, mv = mem_kv(mem_prompt, xm_mem_norm[layer], xm_wk[layer], xm_wv[layer], xm_gk[layer])
                memk.append(mk)
                memv.append(mv)
                y = mem_attention(y, xm_norm[layer], xm_wq[layer], xm_gq[layer], mk, mv, xm_wo[layer])
            else:
                y = y + mem_attend(rms_norm(y, xm_norm[layer]), cache_mem_k[layer], cache_mem_v[layer],
                                   xm_wq[layer], xm_gq[layer], xm_wo[layer])
            y = swiglu_half(y, *ffn_w[layer][1])
        return y, ev, od, memk, memv

    ys, ev_s, od_s, _, _ = run_group(x_sample, False)
    yp, ev_p, od_p, memk_p, memv_p = run_group(x_prompt, True)
    return (yp, ys,
            stack_key(ev_p, 'C'), stack_key(ev_p, 'n'), stack_key(ev_p, 'm'), stack_key(ev_p, 'conv'),
            stack_key(ev_p, 'cmp_k'), stack_key(ev_p, 'cmp_v'), stack_key(ev_p, 'slc_k'), stack_key(ev_p, 'slc_v'),
            stack_key(ev_p, 'win_k'), stack_key(ev_p, 'win_v'),
            stack_key(od_p, 'ckv'), stack_key(od_p, 'krope'),
            jnp.stack(memk_p), jnp.stack(memv_p),
            stack_key(ev_s, 'C'), stack_key(ev_s, 'n'), stack_key(ev_s, 'm'), stack_key(ev_s, 'conv'),
            stack_key(ev_s, 'cmp_k'), stack_key(ev_s, 'cmp_v'), stack_key(ev_s, 'slc_k'), stack_key(ev_s, 'slc_v'),
            stack_key(ev_s, 'win_k'), stack_key(ev_s, 'win_v'),
            stack_key(od_s, 'ckv'), stack_key(od_s, 'krope'))
```

```python
import functools
import math

import jax
import jax.numpy as jnp
import numpy as np
from jax import lax
from jax.experimental import pallas as pl
from jax.experimental.pallas import tpu as pltpu

D_MODEL = 1024
DEPTH = 2
PAGE_SIZE = 128
EPS = 1e-6
NEG_INF = -1e30
D_FF = 2816
ML_H = 4
ML_DH = 128
ML_W = ML_H * ML_DH
CONV_K = 4
MLSTM_CHUNK = 64
NSA_H = 8
NSA_KV = 2
NSA_R = NSA_H // NSA_KV
NSA_DH = 64
NSA_W = NSA_H * NSA_DH
NSA_KVW = NSA_KV * NSA_DH
NSA_SCALE = NSA_DH ** -0.5
CMP_STRIDE = 16
CMP_LEN = 2 * CMP_STRIDE
CMP_HID = 2 * NSA_DH
SLC_BLOCK = 64
SLC_RATIO = SLC_BLOCK // CMP_STRIDE
SLC_OVERLAP_W = (1.0, 2.0, 2.0, 2.0, 1.0)
N_SELECT = 16
WINDOW = 512
SLC_QBLOCK = 32
WIN_QBLOCK = 128
FORCE_SCORE = 1e6
REL_BUCKETS = 32
REL_MAX_DIST = 128
MLA_H = 16
MLA_NOPE = 64
MLA_ROPE = 32
MLA_V = 64
Q_LORA = 384
KV_LORA = 256
MLA_SCALE = (MLA_NOPE + MLA_ROPE) ** -0.5
MLA_QBLOCK = 128
ROPE_THETA = 10000.0
MEM_LEN = 256
XM_H = 4
XM_DH = 128
XM_W = XM_H * XM_DH
EV_SPLITS = (ML_W, ML_W, ML_W, ML_H, ML_H, NSA_W) + (NSA_KVW,) * 6 + (NSA_H * 3,)
OD_SPLITS = (Q_LORA, KV_LORA, MLA_ROPE)

V7X_LANES = 128
V7X_VMEM_LIMIT_BYTES = 56 * 1024 * 1024
FFN_CHUNK = 256
FFN_ROWS = 512


def _ffn_kernel(x_ref, g_ref, wg_ref, wu_ref, wd_ref, o_ref, act_ref):
    x = x_ref[...]
    h = x * lax.rsqrt(jnp.mean(x * x, axis=-1, keepdims=True) + EPS) * g_ref[...]
    hb = h.astype(jnp.bfloat16)
    for c in range(D_FF // FFN_CHUNK):
        cols = slice(c * FFN_CHUNK, (c + 1) * FFN_CHUNK)
        gate = jnp.dot(hb, wg_ref[:, cols], preferred_element_type=jnp.float32)
        up = jnp.dot(hb, wu_ref[:, cols], preferred_element_type=jnp.float32)
        act_ref[:, cols] = (gate * jax.nn.sigmoid(gate) * up).astype(jnp.bfloat16)
    o_ref[...] = x + 0.5 * jnp.dot(act_ref[...], wd_ref[...], preferred_element_type=jnp.float32)


def ffn_half(x2d, g, wg, wu, wd):
    m = x2d.shape[0]
    tm = min(FFN_ROWS, m)
    assert m % tm == 0
    resident = functools.partial(pl.BlockSpec, pipeline_mode=pl.Buffered(1))
    return pl.pallas_call(
        _ffn_kernel,
        grid=(m // tm,),
        in_specs=[
            pl.BlockSpec((tm, D_MODEL), lambda i: (i, 0)),
            resident((1, D_MODEL), lambda i: (0, 0)),
            resident((D_MODEL, D_FF), lambda i: (0, 0)),
            resident((D_MODEL, D_FF), lambda i: (0, 0)),
            resident((D_FF, D_MODEL), lambda i: (0, 0)),
        ],
        out_specs=pl.BlockSpec((tm, D_MODEL), lambda i: (i, 0)),
        out_shape=jax.ShapeDtypeStruct((m, D_MODEL), jnp.float32),
        scratch_shapes=[pltpu.VMEM((tm, D_FF), jnp.bfloat16)],
        compiler_params=pltpu.CompilerParams(
            dimension_semantics=("arbitrary",), vmem_limit_bytes=V7X_VMEM_LIMIT_BYTES),
        name="ffn_half",
    )(x2d, g.reshape(1, D_MODEL), wg, wu, wd)


def swiglu_half(x, g, wg, wu, wd):
    shp = x.shape
    return ffn_half(x.reshape(-1, D_MODEL), g, wg, wu, wd).reshape(shp)


EV_PROJ_ROWS = 512
EV_PROJ_GROUPS = (("u", ML_W), ("v", ML_W), ("o", ML_W), ("q", NSA_W), ("kc", NSA_KVW), ("vc", NSA_KVW),
                  ("ks", NSA_KVW), ("vs", NSA_KVW), ("kw", NSA_KVW), ("vw", NSA_KVW), ("small", V7X_LANES))


def _even_proj_kernel(x_ref, g_ref, w_ref, gq_ref, gks_ref, gkw_ref,
                      u_ref, v_ref, o_ref, small_ref, kc_ref, vc_ref, ks_ref, vs_ref, kw_ref, vw_ref,
                      q5_ref, kst_ref, vst_ref, kwt_ref, vwt_ref):
    f32, bf16 = jnp.float32, jnp.bfloat16
    x = x_ref[0]
    xn = (x * lax.rsqrt(jnp.mean(x * x, axis=-1, keepdims=True) + EPS) * g_ref[...]).astype(bf16)
    y = jnp.dot(xn, w_ref[...], preferred_element_type=f32)
    off, cols = 0, {}
    for name, width in EV_PROJ_GROUPS:
        cols[name] = y[:, off:off + width]
        off += width
    lane = lax.broadcasted_iota(jnp.int32, (1, V7X_LANES), 1)
    DH = NSA_DH

    def pair_norm(p, gain):
        sq = p * p
        s_all = jnp.sum(sq, axis=-1, keepdims=True)
        s_lo = jnp.sum(jnp.where(lane < DH, sq, 0.0), axis=-1, keepdims=True)
        ms = jnp.where(lane < DH, s_lo, s_all - s_lo) * (1.0 / DH)
        return p * lax.rsqrt(ms + EPS) * gain

    def put_pair(ref, first, p):
        p = p.astype(bf16)
        ref[0, first] = p[:, :DH]
        ref[0, first + 1] = p[:, DH:]

    u_ref[0], v_ref[0], o_ref[0], small_ref[0] = cols["u"], cols["v"], cols["o"], cols["small"]
    for pr in range(NSA_H // 2):
        qn = pair_norm(cols["q"][:, pr * V7X_LANES:(pr + 1) * V7X_LANES], gq_ref[...])
        put_pair(q5_ref, 2 * pr, qn * NSA_SCALE)
    kc_ref[0], vc_ref[0], vs_ref[0], vw_ref[0] = cols["kc"], cols["vc"], cols["vs"], cols["vw"]
    ks = pair_norm(cols["ks"], gks_ref[...])
    kw = pair_norm(cols["kw"], gkw_ref[...])
    ks_ref[0], kw_ref[0] = ks, kw
    put_pair(kst_ref, 0, ks)
    put_pair(vst_ref, 0, cols["vs"])
    put_pair(kwt_ref, 0, kw)
    put_pair(vwt_ref, 0, cols["vw"])


def even_in_proj_prompt(y, g_mix, w_in, nsa_gq, nsa_gk_slc, nsa_gk_win):
    f32, bf16 = jnp.float32, jnp.bfloat16
    B, S, D = y.shape
    tm = min(EV_PROJ_ROWS, S)
    assert S % tm == 0 and NSA_KVW == V7X_LANES and 2 * ML_H + 3 * NSA_H <= V7X_LANES
    (w_u, w_v, w_o, w_i, w_f, w_q, w_kc, w_vc, w_ks, w_vs, w_kw, w_vw, w_g) = split_cols(w_in, EV_SPLITS)
    w_small = jnp.pad(jnp.concatenate([w_i, w_f, w_g], axis=1), ((0, 0), (0, V7X_LANES - 2 * ML_H - 3 * NSA_H)))
    w = jnp.concatenate([w_u, w_v, w_o, w_q, w_kc, w_vc, w_ks, w_vs, w_kw, w_vw, w_small], axis=1).astype(bf16)
    n_cols = w.shape[1]
    assert n_cols == sum(width for _, width in EV_PROJ_GROUPS)
    const = lambda shape: pl.BlockSpec(shape, lambda b, i: (0,) * len(shape), pipeline_mode=pl.Buffered(1))
    rows = lambda width: pl.BlockSpec((1, tm, width), lambda b, i: (b, i, 0))
    heads = lambda n: pl.BlockSpec((1, n, tm, NSA_DH), lambda b, i: (b, 0, i, 0))
    wide = jax.ShapeDtypeStruct((B, S, ML_W), f32)
    kv = jax.ShapeDtypeStruct((B, S, NSA_KVW), f32)
    kvt = jax.ShapeDtypeStruct((B, NSA_KV, S, NSA_DH), bf16)
    pair_gain = lambda g: jnp.tile(g, 2).reshape(1, V7X_LANES)
    (u, v, o, small, kc, vc, ks, vs, kw, vw, q5, kst, vst, kwt, vwt) = pl.pallas_call(
        _even_proj_kernel,
        grid=(B, S // tm),
        in_specs=[rows(D), const((1, D)), const((D, n_cols)), const((1, V7X_LANES)), const((1, V7X_LANES)),
                  const((1, V7X_LANES))],
        out_specs=[rows(ML_W), rows(ML_W), rows(ML_W), rows(V7X_LANES)] + [rows(NSA_KVW)] * 6
                  + [heads(NSA_H)] + [heads(NSA_KV)] * 4,
        out_shape=[wide, wide, wide, jax.ShapeDtypeStruct((B, S, V7X_LANES), f32)] + [kv] * 6
                  + [jax.ShapeDtypeStruct((B, NSA_H, S, NSA_DH), bf16)] + [kvt] * 4,
        compiler_params=pltpu.CompilerParams(
            dimension_semantics=("arbitrary", "arbitrary"), vmem_limit_bytes=V7X_VMEM_LIMIT_BYTES),
        name="even_in_proj",
    )(y, g_mix.reshape(1, D), w, pair_gain(nsa_gq), pair_gain(nsa_gk_slc), pair_gain(nsa_gk_win))
    kv4 = lambda a: a.reshape(B, S, NSA_KV, NSA_DH)
    return dict(u=u, v=v, o=o, i_pre=small[..., :ML_H], f_pre=small[..., ML_H:2 * ML_H],
                g_pre=small[..., 2 * ML_H:2 * ML_H + 3 * NSA_H], kc=kv4(kc), vc=kv4(vc), ks=kv4(ks), vs=kv4(vs),
                kw=kv4(kw), vw=kv4(vw), q5=q5, kst=kst, vst=vst, kwt=kwt, vwt=vwt)


NSA_TQ = 256
NSA_NEAR = 128


def _nsa_prompt_kernel(q_ref, kc_ref, vc_ref, ks_ref, vs_ref, kw_ref, vw_ref, bc_ref, bn_ref,
                       e_ref, mw_ref, gp_ref, gb_ref, o_ref, *, n_cmp, n_select):
    f32, bf16 = jnp.float32, jnp.bfloat16
    qi = pl.program_id(2)
    R, TQ, DH = q_ref.shape[2:]
    S = ks_ref.shape[2]
    t0 = qi * TQ
    q2 = q_ref[0, 0].reshape(R * TQ, DH)
    t_col = t0 + lax.broadcasted_iota(jnp.int32, (TQ, 1), 0)

    def scores(k):
        s = lax.dot_general(q2, k, (((1,), (1,)), ((), ())), preferred_element_type=f32)
        return s.reshape(R, TQ, k.shape[0])

    ncp = kc_ref.shape[2]
    c_row = lax.broadcasted_iota(jnp.int32, (1, ncp), 1)
    mask_c = (t_col >= c_row * CMP_STRIDE + (CMP_LEN - 1)) & (c_row < n_cmp)
    s = jnp.where(mask_c[None], scores(kc_ref[0, 0]) + bc_ref[0], NEG_INF)
    e = jnp.exp(s - jnp.max(s, axis=-1, keepdims=True))
    p = e / jnp.sum(e, axis=-1, keepdims=True) * mask_c.astype(f32)[None]
    o_cmp = jnp.dot(p.reshape(R * TQ, ncp).astype(bf16), vc_ref[0, 0],
                    preferred_element_type=f32).reshape(R, TQ, DH)
    imp = jnp.sum(p, axis=0)

    mw = mw_ref[...]
    hi = imp.astype(bf16)
    r1 = imp - hi.astype(f32)
    mid = r1.astype(bf16)
    lo = (r1 - mid.astype(f32)).astype(bf16)
    p_slc = (jnp.dot(hi, mw, preferred_element_type=f32) + jnp.dot(mid, mw, preferred_element_type=f32)
             + jnp.dot(lo, mw, preferred_element_type=f32))
    ns = mw_ref.shape[1]
    jb = lax.broadcasted_iota(jnp.int32, (1, ns), 1)
    tb = jnp.right_shift(t_col, int(math.log2(SLC_BLOCK)))
    forced = (jb == 0) | (jb == tb) | (jb == tb - 1)
    score = jnp.where(jb <= tb, p_slc + FORCE_SCORE * forced.astype(f32), -1.0)
    rank = jnp.zeros((TQ, ns), f32)
    for j in range(ns):
        col = score[:, j:j + 1]
        rank = rank + ((col > score) | ((col == score) & (jb > j))).astype(f32)
    sel = (rank < n_select).astype(bf16)
    gates = jax.nn.sigmoid(gp_ref[0, 0] + gb_ref[0])
    q = q_ref[0, 0]

    def attend(r, k, v, add_mask, first_key, t_start):
        s = lax.dot_general(q[r], k, (((1,), (1,)), ((), ())), preferred_element_type=f32) + add_mask
        near_lo = max(t_start - NSA_NEAR, first_key)
        band = bn_ref[0, r][:, near_lo - (t_start - NSA_NEAR):]
        cut = near_lo - first_key
        near = s[:, cut:] + band
        s = near if cut == 0 else jnp.concatenate([s[:, :cut], near], axis=1)
        e = jnp.exp(s - jnp.max(s, axis=-1, keepdims=True))
        return jnp.dot(e.astype(bf16), v, preferred_element_type=f32) / jnp.sum(e, axis=-1, keepdims=True)

    for c in range(S // TQ):
        @pl.when(qi == c)
        def _(c=c):
            t_start, n_keys = c * TQ, (c + 1) * TQ
            row = t_start + lax.broadcasted_iota(jnp.int32, (TQ, 1), 0)
            col = lax.broadcasted_iota(jnp.int32, (1, n_keys), 1)
            sel_tok = jnp.dot(sel, e_ref[:, :n_keys], preferred_element_type=f32)
            mask_s = jnp.where((sel_tok > 0.5) & (col <= row), 0.0, NEG_INF)
            w_lo = max(t_start - WINDOW, 0)
            col_w = w_lo + lax.broadcasted_iota(jnp.int32, (1, n_keys - w_lo), 1)
            mask_w = jnp.where((col_w <= row) & (row - col_w <= WINDOW), 0.0, NEG_INF)
            ks, vs = ks_ref[0, 0, :n_keys, :], vs_ref[0, 0, :n_keys, :]
            kw, vw = kw_ref[0, 0, w_lo:n_keys, :], vw_ref[0, 0, w_lo:n_keys, :]
            outs = []
            for r in range(R):
                o_slc = attend(r, ks, vs, mask_s, 0, t_start)
                o_win = attend(r, kw, vw, mask_w, w_lo, t_start)
                outs.append(gates[:, 3 * r:3 * r + 1] * o_cmp[r] + gates[:, 3 * r + 1:3 * r + 2] * o_slc
                            + gates[:, 3 * r + 2:3 * r + 3] * o_win)
            o_ref[0] = jnp.concatenate(outs, axis=-1)


def _bias_lookup(rel_bias, dist):
    bucket = t5_bucket(dist)[..., None]
    out = jnp.zeros(dist.shape + rel_bias.shape[1:], rel_bias.dtype)
    for b in range(REL_BUCKETS):
        out = jnp.where(bucket == b, rel_bias[b], out)
    return out


def nsa_prompt(q_heads, k_cmp, v_cmp, ks_t, vs_t, kw_t, vw_t, g_pre, gate_b, rel_bias):
    f32, bf16 = jnp.float32, jnp.bfloat16
    B, _, S, _ = q_heads.shape
    G, R, DH, TQ = NSA_KV, NSA_R, NSA_DH, NSA_TQ
    assert S % TQ == 0 and TQ % V7X_LANES == 0 and WINDOW % V7X_LANES == 0 and NSA_NEAR % V7X_LANES == 0
    n_cmp = k_cmp.shape[1]
    ncp = S // CMP_STRIDE
    ns = S // SLC_BLOCK
    assert n_cmp == ncp - 1
    n_select = min(N_SELECT, ns)

    def kv_layout(a, n):
        a = jnp.pad(a, ((0, 0), (0, n - a.shape[1]), (0, 0), (0, 0)))
        return jnp.transpose(a, (0, 2, 1, 3)).astype(bf16)

    q5 = q_heads.reshape(B, G, R, S, DH)
    kc, vc = kv_layout(k_cmp, ncp), kv_layout(v_cmp, ncp)

    half = REL_BUCKETS // 2
    assert half + int(math.log(NSA_NEAR / half) / math.log(REL_MAX_DIST / half) * half) >= REL_BUCKETS - 1
    t = jnp.arange(S)
    dist_c = t[:, None] - (jnp.arange(ncp) * CMP_STRIDE + CMP_LEN - 1)[None, :]
    bias_c = jnp.transpose(_bias_lookup(rel_bias, dist_c).reshape(S, ncp, G, R), (2, 3, 0, 1))
    d_near = jnp.arange(TQ)[:, None] + NSA_NEAR - jnp.arange(TQ + NSA_NEAR)[None, :]
    bias_n = jnp.transpose((_bias_lookup(rel_bias, d_near) - rel_bias[REL_BUCKETS - 1]).reshape(
        TQ, TQ + NSA_NEAR, G, R), (2, 3, 0, 1))
    expand = (jnp.arange(S)[None, :] // SLC_BLOCK == jnp.arange(ns)[:, None]).astype(bf16)
    c_i = jnp.arange(ncp)[:, None]
    j_i = jnp.arange(ns)[None, :]
    mw = sum(w * (c_i == SLC_RATIO * j_i + k - 1) for k, w in enumerate(SLC_OVERLAP_W)).astype(bf16)
    gp = jnp.transpose(g_pre.reshape(B, S, G, 3 * R), (0, 2, 1, 3))
    gb = gate_b.reshape(G, 1, 3 * R)

    kv_spec = lambda n: pl.BlockSpec((1, 1, n, DH), lambda b, g, i: (b, g, 0, 0))
    return pl.pallas_call(
        functools.partial(_nsa_prompt_kernel, n_cmp=n_cmp, n_select=n_select),
        grid=(B, G, S // TQ),
        in_specs=[
            pl.BlockSpec((1, 1, R, TQ, DH), lambda b, g, i: (b, g, 0, i, 0)),
            kv_spec(ncp), kv_spec(ncp), kv_spec(S), kv_spec(S), kv_spec(S), kv_spec(S),
            pl.BlockSpec((1, R, TQ, ncp), lambda b, g, i: (g, 0, i, 0)),
            pl.BlockSpec((1, R, TQ, TQ + NSA_NEAR), lambda b, g, i: (g, 0, 0, 0)),
            pl.BlockSpec((ns, S), lambda b, g, i: (0, 0)),
            pl.BlockSpec((ncp, ns), lambda b, g, i: (0, 0)),
            pl.BlockSpec((1, 1, TQ, 3 * R), lambda b, g, i: (b, g, i, 0)),
            pl.BlockSpec((1, 1, 3 * R), lambda b, g, i: (g, 0, 0)),
        ],
        out_specs=pl.BlockSpec((1, TQ, R * DH), lambda b, g, i: (b, i, g)),
        out_shape=jax.ShapeDtypeStruct((B, S, NSA_W), f32),
        compiler_params=pltpu.CompilerParams(
            dimension_semantics=("arbitrary", "arbitrary", "arbitrary"), vmem_limit_bytes=V7X_VMEM_LIMIT_BYTES),
        name="nsa_prompt",
    )(q5, kc, vc, ks_t, vs_t, kw_t, vw_t, bias_c, bias_n, expand, mw, gp, gb)


MLA_TQ = 512
MLA_HEADS_PER_STEP = 2


def _mla_prompt_kernel(cq_ref, wuq_ref, ra_ref, rm_ref, rp_ref, ckv_ref, kr_ref, wuk_ref, wuv_ref,
                       o_ref, q_ref, k_ref, v_ref, s_ref):
    f32, bf16 = jnp.float32, jnp.bfloat16
    qi = pl.program_id(2)
    HP, S, DP = k_ref.shape
    TQ = cq_ref.shape[1]
    causal = lax.broadcasted_iota(jnp.int32, (TQ, 1), 0) >= lax.broadcasted_iota(jnp.int32, (1, TQ), 1)

    @pl.when(qi == 0)
    def _():
        ckv = ckv_ref[0]
        pad = jnp.zeros((S, DP - MLA_NOPE - MLA_ROPE), bf16)
        for h in range(HP):
            k_nope = jnp.dot(ckv, wuk_ref[h], preferred_element_type=f32).astype(bf16)
            k_ref[h] = jnp.concatenate([k_nope, kr_ref[0], pad], axis=-1)
            v_ref[h] = jnp.dot(ckv, wuv_ref[h], preferred_element_type=f32).astype(bf16)

    for h in range(HP):
        q3 = jnp.dot(cq_ref[0], wuq_ref[h], preferred_element_type=f32)
        q = q3[:, :DP]
        ms = jnp.sum(q * q, axis=-1, keepdims=True) * (1.0 / (MLA_NOPE + MLA_ROPE))
        q = (q * ra_ref[...] + q3[:, DP:2 * DP] * rm_ref[...] + q3[:, 2 * DP:] * rp_ref[...]) * lax.rsqrt(ms + EPS)
        q_ref[h] = q.astype(bf16)

    for c in range(S // TQ):
        @pl.when(qi == c)
        def _(c=c):
            n_keys = (c + 1) * TQ
            outs = []
            for h in range(HP):
                s_ref[h, :, :n_keys] = lax.dot_general(
                    q_ref[h], k_ref[h, :n_keys, :], (((1,), (1,)), ((), ())),
                    preferred_element_type=f32) * MLA_SCALE
                s_ref[h, :, n_keys - TQ:n_keys] = jnp.where(causal, s_ref[h, :, n_keys - TQ:n_keys], NEG_INF)
                s = s_ref[h, :, :n_keys]
                e = jnp.exp(s - jnp.max(s, axis=-1, keepdims=True))
                pv = jnp.dot(e.astype(bf16), v_ref[h, :n_keys, :], preferred_element_type=f32)
                outs.append(pv / jnp.sum(e, axis=-1, keepdims=True))
            o_ref[0] = jnp.concatenate(outs, axis=-1)


def mla_prompt_attention(cq, w_uq, g_q, pos, ckv, kr, w_uk, w_uv):
    f32, bf16 = jnp.float32, jnp.bfloat16
    B, S, _ = cq.shape
    H, DQ, DP = MLA_H, MLA_NOPE + MLA_ROPE, V7X_LANES
    HP, TQ = MLA_HEADS_PER_STEP, min(MLA_TQ, S)
    half = MLA_ROPE // 2
    assert S % TQ == 0 and H % HP == 0 and HP * MLA_V == V7X_LANES and DQ <= DP
    w = jnp.transpose(w_uq.reshape(Q_LORA, H, DQ), (1, 0, 2))
    zw = lambda n: jnp.zeros((H, Q_LORA, n), f32)
    w_left = jnp.concatenate([zw(MLA_NOPE), w[..., MLA_NOPE + half:], zw(half)], axis=-1)
    w_right = jnp.concatenate([zw(MLA_NOPE + half), w[..., MLA_NOPE:MLA_NOPE + half]], axis=-1)
    pad_w = lambda a: jnp.pad(a, ((0, 0), (0, 0), (0, DP - DQ)))
    wuq_p = jnp.concatenate([pad_w(w), pad_w(w_left), pad_w(w_right)], axis=-1).astype(bf16)
    inv = ROPE_THETA ** (-jnp.arange(half, dtype=f32) / half)
    ang = pos.astype(f32)[:, None] * inv[None, :]
    cos, sin, zero = jnp.cos(ang), jnp.sin(ang), jnp.zeros((S, half), f32)
    g_nope, g_lo, g_hi = g_q[:MLA_NOPE], g_q[MLA_NOPE:MLA_NOPE + half], g_q[MLA_NOPE + half:]
    tail = jnp.zeros((S, DP - DQ), f32)
    lead0 = jnp.zeros((S, MLA_NOPE), f32)
    rope_a = jnp.concatenate([jnp.broadcast_to(g_nope, (S, MLA_NOPE)), cos * g_lo, cos * g_hi, tail], axis=1)
    rope_m = jnp.concatenate([lead0, -sin * g_hi, zero, tail], axis=1)
    rope_p = jnp.concatenate([lead0, zero, sin * g_lo, tail], axis=1)
    row_tab = pl.BlockSpec((TQ, DP), lambda b, h, i: (i, 0))
    return pl.pallas_call(
        _mla_prompt_kernel,
        grid=(B, H // HP, S // TQ),
        in_specs=[
            pl.BlockSpec((1, TQ, Q_LORA), lambda b, h, i: (b, i, 0)),
            pl.BlockSpec((HP, Q_LORA, 3 * DP), lambda b, h, i: (h, 0, 0)),
            row_tab, row_tab, row_tab,
            pl.BlockSpec((1, S, KV_LORA), lambda b, h, i: (b, 0, 0)),
            pl.BlockSpec((1, S, MLA_ROPE), lambda b, h, i: (b, 0, 0)),
            pl.BlockSpec((HP, KV_LORA, MLA_NOPE), lambda b, h, i: (h, 0, 0)),
            pl.BlockSpec((HP, KV_LORA, MLA_V), lambda b, h, i: (h, 0, 0)),
        ],
        out_specs=pl.BlockSpec((1, TQ, HP * MLA_V), lambda b, h, i: (b, i, h)),
        out_shape=jax.ShapeDtypeStruct((B, S, H * MLA_V), jnp.float32),
        scratch_shapes=[pltpu.VMEM((HP, TQ, DP), bf16), pltpu.VMEM((HP, S, DP), bf16),
                        pltpu.VMEM((HP, S, MLA_V), bf16), pltpu.VMEM((HP, TQ, S), f32)],
        compiler_params=pltpu.CompilerParams(
            dimension_semantics=("arbitrary", "arbitrary", "arbitrary"), vmem_limit_bytes=V7X_VMEM_LIMIT_BYTES),
        name="mla_prompt",
    )(cq, wuq_p, rope_a, rope_m, rope_p, ckv, kr, w_uk, w_uv)


def _split3(x):
    bf16, f32 = jnp.bfloat16, jnp.float32
    hi = x.astype(bf16)
    r = x - hi.astype(f32)
    mid = r.astype(bf16)
    return hi, mid, (r - mid.astype(f32)).astype(bf16)


def _mlstm_kernel(u_ref, v_ref, og_ref, cw_ref, cb_ref, wq_ref, wk_ref, g_ref, ir_ref, fr_ref, it_ref, ft_ref,
                  tri_ref, h_ref, c_ref, n_ref, m_ref):
    f32, bf16 = jnp.float32, jnp.bfloat16
    H, NC, L = ir_ref.shape[1:]
    DH = ML_DH
    PRE = 8
    tri = tri_ref[...]
    causal = (tri > 0.5)[None]

    def bmm(a, b, ca, cb):
        return lax.dot_general(a, b, (((ca,), (cb,)), ((0,), (0,))), preferred_element_type=f32)

    def split_heads(x):
        return jnp.stack([x[:, h * DH:(h + 1) * DH] for h in range(H)], axis=0)

    def conv_act(c):
        cur = u_ref[0, c * L:(c + 1) * L, :]
        prev = jnp.zeros((PRE, cur.shape[1]), f32) if c == 0 else u_ref[0, c * L - PRE:c * L, :]
        full = jnp.concatenate([prev, cur], axis=0)
        acc = cb_ref[...]
        for j in range(CONV_K):
            lo = PRE - (CONV_K - 1) + j
            acc = acc + full[lo:lo + L, :] * cw_ref[j:j + 1, :]
        return acc * jax.nn.sigmoid(acc)

    b_rows = jnp.stack([sum(lax.dot_general(p, tri, (((1,), (1,)), ((), ())), preferred_element_type=f32)
                            for p in _split3(fr_ref[0, h])) for h in range(H)], axis=0)
    b_cols = jnp.stack([sum(jnp.dot(tri, p, preferred_element_type=f32) for p in _split3(ft_ref[0, h]))
                        for h in range(H)], axis=0)
    i_rows, i_cols = ir_ref[0], it_ref[0]
    C = jnp.zeros((H, DH, DH), f32)
    n = jnp.zeros((H, 1, DH), f32)
    m = jnp.zeros((H, 1, 1), f32)
    for c in range(NC):
        rows = slice(c * L, (c + 1) * L)
        ch = split_heads(conv_act(c)).astype(bf16)
        q = bmm(ch, wq_ref[...], 2, 1).astype(bf16)
        k = (bmm(ch, wk_ref[...], 2, 1) * (DH ** -0.5)).astype(bf16)
        v = split_heads(v_ref[0, rows, :])
        b_row, i_row = b_rows[:, c:c + 1, :], i_rows[:, c:c + 1, :]
        b_col, i_col = b_cols[:, :, c:c + 1], i_cols[:, :, c:c + 1]
        g = b_col + m
        dmat = jnp.where(causal, b_col - b_row + i_row, -jnp.inf)
        mt = jnp.maximum(g, jnp.max(dmat, axis=-1, keepdims=True))
        inter = jnp.exp(g - mt)
        sqk = bmm(q, k, 2, 2) * jnp.exp(dmat - mt)
        num = inter * bmm(q, C.astype(bf16), 2, 2) + bmm(sqk.astype(bf16), v.astype(bf16), 2, 1)
        nq = jnp.sum(q.astype(f32) * n.astype(bf16).astype(f32), axis=-1, keepdims=True)
        den = inter * nq + jnp.sum(sqk, axis=-1, keepdims=True)
        hs = num / jnp.maximum(jnp.abs(den), jnp.exp(-mt))
        hs = hs * lax.rsqrt(jnp.mean(hs * hs, axis=-1, keepdims=True) + EPS)
        gate = jax.nn.sigmoid(og_ref[0, rows, :])
        for h in range(H):
            lanes = slice(h * DH, (h + 1) * DH)
            h_ref[0, rows, lanes] = hs[h] * g_ref[:, lanes] * gate[:, lanes]
        b_end = b_row[:, :, L - 1:L]
        w_row = b_end - b_row + i_row
        m_new = jnp.maximum(b_end + m, jnp.max(w_row, axis=-1, keepdims=True))
        decay = jnp.exp(b_end + m - m_new)
        w_col = jnp.exp(b_end - b_col + i_col - m_new)
        C = decay * C + bmm((w_col * v).astype(bf16), k, 1, 1)
        n = decay * n + bmm(jnp.exp(w_row - m_new).astype(bf16), k, 2, 1)
        m = m_new
    c_ref[0] = C
    n_ref[0] = n
    m_ref[0] = jnp.broadcast_to(m, (H, 1, V7X_LANES))


def mlstm_prompt(u, v, o_pre, conv_w, conv_b, wq, wk, out_g, i_pre, logf):
    f32, bf16 = jnp.float32, jnp.bfloat16
    B, S, W = u.shape
    H, L = ML_H, MLSTM_CHUNK
    assert S % L == 0 and W == H * ML_DH and L % 8 == 0 and CONV_K - 1 <= 8
    const = lambda shape: pl.BlockSpec(shape, lambda b: (0,) * len(shape))
    NC = S // L
    rows = lambda a: jnp.transpose(a, (0, 2, 1)).reshape(B, H, NC, L)
    cols = lambda a: jnp.transpose(rows(a), (0, 1, 3, 2))
    tri = jnp.tril(jnp.ones((L, L), jnp.bfloat16))
    seq = pl.BlockSpec((1, S, W), lambda b: (b, 0, 0))
    gate_r = pl.BlockSpec((1, H, NC, L), lambda b: (b, 0, 0, 0))
    gate_c = pl.BlockSpec((1, H, L, NC), lambda b: (b, 0, 0, 0))
    hs, C, n, m = pl.pallas_call(
        _mlstm_kernel,
        grid=(B,),
        in_specs=[seq, seq, seq, const((CONV_K, W)), const((1, W)), const((H, ML_DH, ML_DH)),
                  const((H, ML_DH, ML_DH)), const((1, W)), gate_r, gate_r, gate_c, gate_c, const((L, L))],
        out_specs=[seq, pl.BlockSpec((1, H, ML_DH, ML_DH), lambda b: (b, 0, 0, 0)),
                   pl.BlockSpec((1, H, 1, ML_DH), lambda b: (b, 0, 0, 0)),
                   pl.BlockSpec((1, H, 1, V7X_LANES), lambda b: (b, 0, 0, 0))],
        out_shape=[jax.ShapeDtypeStruct((B, S, W), f32), jax.ShapeDtypeStruct((B, H, ML_DH, ML_DH), f32),
                   jax.ShapeDtypeStruct((B, H, 1, ML_DH), f32), jax.ShapeDtypeStruct((B, H, 1, V7X_LANES), f32)],
        compiler_params=pltpu.CompilerParams(dimension_semantics=("arbitrary",),
                                             vmem_limit_bytes=V7X_VMEM_LIMIT_BYTES),
        name="mlstm_prompt",
    )(u, v, o_pre, conv_w, conv_b.reshape(1, W), wq.astype(bf16), wk.astype(bf16), out_g.reshape(1, W),
      rows(i_pre), rows(logf), cols(i_pre), cols(logf), tri)
    return hs, C, n[:, :, 0], m[:, :, 0, 0]


def _odd_proj_kernel(x_ref, g_ref, w_ref, gcq_ref, gckv_ref, ra_ref, rm_ref, rp_ref,
                     cq_ref, ckv_ref, ckv16_ref, kr_ref, kr16_ref):
    f32, bf16 = jnp.float32, jnp.bfloat16
    x = x_ref[0]
    xn = (x * lax.rsqrt(jnp.mean(x * x, axis=-1, keepdims=True) + EPS) * g_ref[...]).astype(bf16)
    y = jnp.dot(xn, w_ref[...], preferred_element_type=f32)
    norm = lambda a: a * lax.rsqrt(jnp.mean(a * a, axis=-1, keepdims=True) + EPS)
    cq_ref[0] = (norm(y[:, :Q_LORA]) * gcq_ref[...]).astype(bf16)
    ckv = norm(y[:, Q_LORA:Q_LORA + KV_LORA]) * gckv_ref[...]
    ckv_ref[0] = ckv
    ckv16_ref[0] = ckv.astype(bf16)
    base = Q_LORA + KV_LORA
    kr, kr_l, kr_r = (y[:, base + j * V7X_LANES:base + (j + 1) * V7X_LANES] for j in range(3))
    ms = jnp.sum(kr * kr, axis=-1, keepdims=True) * (1.0 / MLA_ROPE)
    roped = (kr * ra_ref[...] + kr_l * rm_ref[...] + kr_r * rp_ref[...]) * lax.rsqrt(ms + EPS)
    kr_ref[0] = roped[:, :MLA_ROPE]
    kr16_ref[0] = roped[:, :MLA_ROPE].astype(bf16)


def odd_in_proj_prompt(y, g_mix, w_in, g_cq, g_ckv, g_kr, pos):
    f32, bf16 = jnp.float32, jnp.bfloat16
    B, S, D = y.shape
    tm = min(EV_PROJ_ROWS, S)
    half, lanes = MLA_ROPE // 2, V7X_LANES
    assert S % tm == 0 and Q_LORA % lanes == 0 and KV_LORA % lanes == 0 and MLA_ROPE <= lanes
    w_cq, w_ckv, w_kr = split_cols(w_in, OD_SPLITS)
    zw = lambda n: jnp.zeros((D, n), f32)
    blocks = [w_kr, jnp.concatenate([w_kr[:, half:], zw(half)], axis=1),
              jnp.concatenate([zw(half), w_kr[:, :half]], axis=1)]
    w = jnp.concatenate([w_cq, w_ckv] + [jnp.pad(b, ((0, 0), (0, lanes - MLA_ROPE))) for b in blocks],
                        axis=1).astype(bf16)
    n_cols = w.shape[1]
    inv = ROPE_THETA ** (-jnp.arange(half, dtype=f32) / half)
    ang = pos.astype(f32)[:, None] * inv[None, :]
    cos, sin, zero = jnp.cos(ang), jnp.sin(ang), jnp.zeros((S, half), f32)
    tail = jnp.zeros((S, lanes - MLA_ROPE), f32)
    g_lo, g_hi = g_kr[:half], g_kr[half:]
    rope_a = jnp.concatenate([cos * g_lo, cos * g_hi, tail], axis=1)
    rope_m = jnp.concatenate([-sin * g_hi, zero, tail], axis=1)
    rope_p = jnp.concatenate([zero, sin * g_lo, tail], axis=1)
    const = lambda shape: pl.BlockSpec(shape, lambda b, i: (0,) * len(shape), pipeline_mode=pl.Buffered(1))
    rows = lambda width: pl.BlockSpec((1, tm, width), lambda b, i: (b, i, 0))
    tab = pl.BlockSpec((tm, lanes), lambda b, i: (i, 0))
    shape = lambda width, dt: jax.ShapeDtypeStruct((B, S, width), dt)
    return pl.pallas_call(
        _odd_proj_kernel,
        grid=(B, S // tm),
        in_specs=[rows(D), const((1, D)), const((D, n_cols)), const((1, Q_LORA)), const((1, KV_LORA)), tab, tab, tab],
        out_specs=[rows(Q_LORA), rows(KV_LORA), rows(KV_LORA), rows(MLA_ROPE), rows(MLA_ROPE)],
        out_shape=[shape(Q_LORA, bf16), shape(KV_LORA, f32), shape(KV_LORA, bf16), shape(MLA_ROPE, f32),
                   shape(MLA_ROPE, bf16)],
        compiler_params=pltpu.CompilerParams(
            dimension_semantics=("arbitrary", "arbitrary"), vmem_limit_bytes=V7X_VMEM_LIMIT_BYTES),
        name="odd_in_proj",
    )(y, g_mix.reshape(1, D), w, g_cq.reshape(1, Q_LORA), g_ckv.reshape(1, KV_LORA), rope_a, rope_m, rope_p)


MEM_ROWS = 512


def _mem_attn_kernel(x_ref, g_ref, wq_ref, gq_ref, k_ref, v_ref, wo_ref, o_ref):
    f32, bf16 = jnp.float32, jnp.bfloat16
    x = x_ref[0]
    xn = (x * lax.rsqrt(jnp.mean(x * x, axis=-1, keepdims=True) + EPS) * g_ref[...]).astype(bf16)
    q = jnp.dot(xn, wq_ref[...], preferred_element_type=f32)
    k, v = k_ref[0], v_ref[0]
    outs = []
    for h in range(XM_H):
        cols = slice(h * XM_DH, (h + 1) * XM_DH)
        qh = q[:, cols]
        qh = (qh * lax.rsqrt(jnp.mean(qh * qh, axis=-1, keepdims=True) + EPS) * gq_ref[...]).astype(bf16)
        s = lax.dot_general(qh, k[:, cols], (((1,), (1,)), ((), ())), preferred_element_type=f32) * (XM_DH ** -0.5)
        e = jnp.exp(s - jnp.max(s, axis=-1, keepdims=True))
        p = (e / jnp.sum(e, axis=-1, keepdims=True)).astype(bf16)
        outs.append(jnp.dot(p, v[:, cols], preferred_element_type=f32))
    o = jnp.concatenate(outs, axis=-1).astype(bf16)
    o_ref[0] = x + jnp.dot(o, wo_ref[...], preferred_element_type=f32)


def mem_attention(y, g, wq, gq, k, v, wo):
    bf16 = jnp.bfloat16
    B, S, D = y.shape
    M = k.shape[1]
    ts = min(MEM_ROWS, S)
    assert S % ts == 0
    const = lambda shape: pl.BlockSpec(shape, lambda b, i: (0,) * len(shape), pipeline_mode=pl.Buffered(1))
    kv_spec = pl.BlockSpec((1, M, XM_W), lambda b, i: (b, 0, 0))
    return pl.pallas_call(
        _mem_attn_kernel,
        grid=(B, S // ts),
        in_specs=[pl.BlockSpec((1, ts, D), lambda b, i: (b, i, 0)), const((1, D)), const((D, XM_W)),
                  const((1, XM_DH)), kv_spec, kv_spec, const((XM_W, D))],
        out_specs=pl.BlockSpec((1, ts, D), lambda b, i: (b, i, 0)),
        out_shape=jax.ShapeDtypeStruct((B, S, D), jnp.float32),
        compiler_params=pltpu.CompilerParams(
            dimension_semantics=("arbitrary", "arbitrary"), vmem_limit_bytes=V7X_VMEM_LIMIT_BYTES),
        name="mem_attn",
    )(y, g.reshape(1, D), wq.astype(bf16), gq.reshape(1, XM_DH), k.reshape(B, M, XM_W).astype(bf16),
      v.reshape(B, M, XM_W).astype(bf16), wo.astype(bf16))


DECODE_XPOSE_UNROLL = 8


def _paged_copy(pool_ref, page, buf_ref, slot, p, sem_ref):
    return pltpu.make_async_copy(pool_ref.at[page], buf_ref.at[slot, p], sem_ref.at[slot])


def _paged_pipeline(pt_ref, pools, bufs, sems):
    b = pl.program_id(0)
    n_pages = pt_ref.shape[1]

    def start(seq, slot):
        def body(p, carry):
            for pool, buf, sem in zip(pools, bufs, sems):
                _paged_copy(pool, pt_ref[seq, p], buf, slot, p, sem).start()
            return carry
        lax.fori_loop(0, n_pages, body, 0)

    @pl.when(b == 0)
    def _():
        start(0, 0)

    @pl.when(b + 1 < pl.num_programs(0))
    def _():
        start(b + 1, (b + 1) % 2)

    slot = b % 2

    def wait_body(p, carry):
        for pool, buf, sem in zip(pools, bufs, sems):
            _paged_copy(pool, 0, buf, slot, p, sem).wait()
        return carry
    lax.fori_loop(0, n_pages, wait_body, 0)
    return slot


def _softmax_rows(s, valid):
    e = jnp.exp(s - jnp.max(s, axis=-1, keepdims=True))
    return e / jnp.sum(e, axis=-1, keepdims=True) * valid


def _nsa_decode_cmp_kernel(pt_ref, qbd_ref, wk_ref, wv_ref, pek_ref, pev_ref, w2k_ref, w2v_ref, gk_ref,
                           bc_ref, mw_ref, kpool_ref, vpool_ref, ocmp_ref, sel_ref,
                           kbuf, vbuf, xrm_ref, ksem, vsem, *, n_cmp, n_blocks, n_select):
    f32, bf16 = jnp.float32, jnp.bfloat16
    slot = _paged_pipeline(pt_ref, (kpool_ref, vpool_ref), (kbuf, vbuf), (ksem, vsem))
    n_pages = pt_ref.shape[1]
    n_chunk = n_pages * (PAGE_SIZE // CMP_STRIDE)
    G, DH, HID = NSA_KV, NSA_DH, CMP_HID

    def summaries(buf, w_ref, pe_ref, w2_ref):
        def xpose(i, carry):
            for u in range(DECODE_XPOSE_UNROLL):
                p = i * DECODE_XPOSE_UNROLL + u
                xrm_ref[pl.ds(pl.multiple_of(p * PAGE_SIZE, PAGE_SIZE), PAGE_SIZE), :] = buf[slot, p].T
            return carry
        lax.fori_loop(0, n_pages // DECODE_XPOSE_UNROLL, xpose, 0)
        acc = jnp.zeros((n_chunk, 2 * G * HID), f32)
        for j in range(0, CMP_STRIDE, 2):
            rows = jnp.concatenate([xrm_ref[pl.ds(j + u, n_chunk, stride=CMP_STRIDE), :].astype(bf16)
                                    for u in range(2)], axis=1)
            acc = acc + jnp.dot(rows, w_ref[j // 2], preferred_element_type=f32)
        lo, hi = acc[:, :G * HID], acc[:, G * HID:]
        pre = lo + pltpu.roll(hi, n_chunk - 1, 0) + pe_ref[...]
        hid = pre * jax.nn.sigmoid(pre)
        return jnp.dot(hid.astype(bf16), w2_ref[...], preferred_element_type=f32)

    k_sum = summaries(kbuf, wk_ref, pek_ref, w2k_ref)
    lane = lax.broadcasted_iota(jnp.int32, (1, G * DH), 1)
    sq = k_sum * k_sum
    s_all = jnp.sum(sq, axis=-1, keepdims=True)
    s_g0 = jnp.sum(jnp.where(lane < DH, sq, 0.0), axis=-1, keepdims=True)
    ms = jnp.where(lane < DH, s_g0, s_all - s_g0) * (1.0 / DH)
    k_cmp = (k_sum * lax.rsqrt(ms + EPS) * gk_ref[...]).astype(bf16)
    v_cmp = summaries(vbuf, wv_ref, pev_ref, w2v_ref).astype(bf16)

    qbd = qbd_ref[0]
    H = qbd.shape[0]
    c_row = lax.broadcasted_iota(jnp.int32, (1, n_chunk), 1)
    valid = (c_row < n_cmp).astype(f32)
    s = lax.dot_general(qbd, k_cmp, (((1,), (1,)), ((), ())), preferred_element_type=f32) + bc_ref[...]
    p = _softmax_rows(jnp.where(valid > 0.5, s, NEG_INF), valid)
    ocmp_ref[0] = jnp.dot(p.astype(bf16), v_cmp, preferred_element_type=f32)

    R = H // G
    head = lax.broadcasted_iota(jnp.int32, (H, 1), 0)
    imp = jnp.where(head < R, jnp.sum(p[:R], axis=0, keepdims=True), jnp.sum(p[R:], axis=0, keepdims=True))
    mw = mw_ref[...]
    hi_p = imp.astype(bf16)
    r1 = imp - hi_p.astype(f32)
    mid_p = r1.astype(bf16)
    lo_p = (r1 - mid_p.astype(f32)).astype(bf16)
    p_slc = (jnp.dot(hi_p, mw, preferred_element_type=f32) + jnp.dot(mid_p, mw, preferred_element_type=f32)
             + jnp.dot(lo_p, mw, preferred_element_type=f32))
    nsp = mw_ref.shape[1]
    tb = n_blocks - 1
    jb = lax.broadcasted_iota(jnp.int32, (1, nsp), 1)
    forced = (jb == 0) | (jb == tb) | (jb == tb - 1)
    score = jnp.where(jb <= tb, p_slc + FORCE_SCORE * forced.astype(f32), -1.0)
    j_col = lax.broadcasted_iota(jnp.int32, (nsp, 1), 0)
    sels = []
    for g in range(G):
        row = score[g * R:g * R + 1, :]
        col = jnp.broadcast_to(row, (nsp, nsp)).T
        beats = (col > row) | ((col == row) & (j_col < jb))
        rank = jnp.sum(beats.astype(f32), axis=0, keepdims=True)
        sels.append(jnp.broadcast_to((rank < n_select).astype(f32), (R, nsp)))
    sel_ref[0] = jnp.concatenate(sels, axis=0)


def _nsa_decode_attn_kernel(pt_ref, qbd_ref, sel_ref, ocmp_ref, gate_ref, new_ref, e_ref, bs_ref, bw_ref, b0_ref,
                            kw_ref, vw_ref, kpool_ref, vpool_ref, o_ref, kbuf, vbuf, s_ref, ksem, vsem):
    f32, bf16 = jnp.float32, jnp.bfloat16
    slot = _paged_pipeline(pt_ref, (kpool_ref, vpool_ref), (kbuf, vbuf), (ksem, vsem))
    n_pages = pt_ref.shape[1]
    qbd = qbd_ref[0]
    qf = qbd.astype(f32)
    new = new_ref[0]
    b0 = b0_ref[...]

    def new_score(k_row):
        return jnp.sum(qf * k_row.astype(bf16).astype(f32), axis=-1, keepdims=True) + b0

    for p in range(n_pages):
        s_ref[:, p * PAGE_SIZE:(p + 1) * PAGE_SIZE] = jnp.dot(qbd, kbuf[slot, p].astype(bf16),
                                                              preferred_element_type=f32)
    n_blk_past = e_ref.shape[0]
    sel_tok = jnp.dot(sel_ref[0][:, :n_blk_past].astype(bf16), e_ref[...], preferred_element_type=f32)
    s_past = jnp.where(sel_tok > 0.5, s_ref[...] + bs_ref[...], NEG_INF)
    s_new = new_score(new[0:1])
    m = jnp.maximum(jnp.max(s_past, axis=-1, keepdims=True), s_new)
    e_new = jnp.exp(s_new - m)
    s_ref[...] = jnp.exp(s_past - m)
    denom = jnp.sum(s_ref[...], axis=-1, keepdims=True) + e_new
    acc = e_new.astype(bf16).astype(f32) * new[1:2].astype(bf16).astype(f32)
    for p in range(n_pages):
        pe = s_ref[:, p * PAGE_SIZE:(p + 1) * PAGE_SIZE].astype(bf16)
        acc = acc + lax.dot_general(pe, vbuf[slot, p].astype(bf16), (((1,), (1,)), ((), ())),
                                    preferred_element_type=f32)
    o_slc = acc / denom

    s_w = jnp.dot(qbd, kw_ref[0].astype(bf16), preferred_element_type=f32) + bw_ref[...]
    s_wn = new_score(new[2:3])
    m = jnp.maximum(jnp.max(s_w, axis=-1, keepdims=True), s_wn)
    e_w, e_wn = jnp.exp(s_w - m), jnp.exp(s_wn - m)
    denom = jnp.sum(e_w, axis=-1, keepdims=True) + e_wn
    acc = (lax.dot_general(e_w.astype(bf16), vw_ref[0].astype(bf16), (((1,), (1,)), ((), ())),
                           preferred_element_type=f32)
           + e_wn.astype(bf16).astype(f32) * new[3:4].astype(bf16).astype(f32))
    o_win = acc / denom

    gates = jax.nn.sigmoid(gate_ref[0])
    o_ref[0] = gates[:, 0:1] * ocmp_ref[0] + gates[:, 1:2] * o_slc + gates[:, 2:3] * o_win


def _block_diag_heads(x):
    B, H, DH = x.shape
    g_of_h = jnp.arange(H) // (H // NSA_KV)
    onehot = (g_of_h[:, None] == jnp.arange(NSA_KV)[None, :]).astype(x.dtype)
    return (x[:, :, None, :] * onehot[None, :, :, None]).reshape(B, H, NSA_KV * DH)


def nsa_decode(qn, ks, vs, kw, vw, g_pre, gate_b, rel_bias, page_table, cmp_k_pool, cmp_v_pool,
               slc_k_pool, slc_v_pool, win_k, win_v, pe_k, w1_k, w2_k, pe_v, w1_v, w2_v, gk_cmp):
    f32, bf16 = jnp.float32, jnp.bfloat16
    B, n_pages = page_table.shape
    G, R, DH, H = NSA_KV, NSA_R, NSA_DH, NSA_H
    GD = G * DH
    past = n_pages * PAGE_SIZE
    n_chunk = past // CMP_STRIDE
    n_cmp = (past + 1 - CMP_LEN) // CMP_STRIDE + 1
    n_blocks = -(-(past + 1) // SLC_BLOCK)
    n_blk_past = past // SLC_BLOCK
    nsp = -(-n_blocks // V7X_LANES) * V7X_LANES
    n_win = win_k.shape[1]
    assert GD == V7X_LANES and PAGE_SIZE == V7X_LANES and n_blocks >= N_SELECT and n_win == WINDOW

    def pool_view(pool):
        return jnp.transpose(pool, (0, 2, 3, 1)).reshape(pool.shape[0], GD, PAGE_SIZE)
    kc_pool, vc_pool, ks_pool, vs_pool = (pool_view(a) for a in (cmp_k_pool, cmp_v_pool, slc_k_pool, slc_v_pool))
    kw_t = jnp.transpose(win_k, (0, 2, 3, 1)).reshape(B, GD, n_win)
    vw_t = jnp.transpose(win_v, (0, 2, 3, 1)).reshape(B, GD, n_win)

    qbd = _block_diag_heads((qn[:, 0] * NSA_SCALE)).astype(bf16)
    eye = jnp.eye(G, dtype=f32)

    def chunk_weights(w1):
        def bd(w):
            return jnp.einsum('jdh,gk->jgdkh', w, eye).reshape(CMP_STRIDE, GD, G * CMP_HID)
        w = jnp.concatenate([bd(w1[:CMP_STRIDE]), bd(w1[CMP_STRIDE:])], axis=-1)
        return w.reshape(CMP_STRIDE // 2, 2 * GD, 2 * G * CMP_HID).astype(bf16)

    def pe_term(pe, w1):
        return jnp.tile(jnp.einsum('jd,jdh->h', pe, w1), G).reshape(1, G * CMP_HID)

    def w2_bd(w2):
        return jnp.einsum('hd,gk->ghkd', w2, eye).reshape(G * CMP_HID, GD).astype(bf16)

    c_i = jnp.arange(n_chunk)
    bias_c = _bias_lookup(rel_bias, past - (c_i * CMP_STRIDE + CMP_LEN - 1)).T
    j_i = jnp.arange(nsp)[None, :]
    mw = sum(w * ((c_i[:, None] == SLC_RATIO * j_i + k - 1) & (j_i < n_blocks))
             for k, w in enumerate(SLC_OVERLAP_W)).astype(bf16)
    tok = jnp.arange(past)
    expand = (tok[None, :] // SLC_BLOCK == jnp.arange(n_blk_past)[:, None]).astype(bf16)
    bias_s = _bias_lookup(rel_bias, past - tok).T
    bias_w = _bias_lookup(rel_bias, n_win - jnp.arange(n_win)).T
    bias_0 = _bias_lookup(rel_bias, jnp.zeros((1,), jnp.int32)).T
    new_rows = jnp.stack([a.reshape(B, GD) for a in (ks, vs, kw, vw)], axis=1)
    gate_in = g_pre.reshape(B, H, 3) + gate_b

    const = lambda shape: pl.BlockSpec(shape, lambda b, pt: (0,) * len(shape))
    per_seq = lambda shape: pl.BlockSpec((1,) + shape, lambda b, pt: (b,) + (0,) * len(shape))
    any_spec = pl.BlockSpec(memory_space=pl.ANY)
    page_buf = pltpu.VMEM((2, n_pages, GD, PAGE_SIZE), f32)
    params = pltpu.CompilerParams(dimension_semantics=("arbitrary",), vmem_limit_bytes=V7X_VMEM_LIMIT_BYTES)

    o_cmp, sel = pl.pallas_call(
        functools.partial(_nsa_decode_cmp_kernel, n_cmp=n_cmp, n_blocks=n_blocks, n_select=min(N_SELECT, n_blocks)),
        grid_spec=pltpu.PrefetchScalarGridSpec(
            num_scalar_prefetch=1, grid=(B,),
            in_specs=[per_seq((H, GD)),
                      const((CMP_STRIDE // 2, 2 * GD, 2 * G * CMP_HID)),
                      const((CMP_STRIDE // 2, 2 * GD, 2 * G * CMP_HID)),
                      const((1, G * CMP_HID)), const((1, G * CMP_HID)),
                      const((G * CMP_HID, GD)), const((G * CMP_HID, GD)), const((1, GD)),
                      const((H, n_chunk)), const((n_chunk, nsp)), any_spec, any_spec],
            out_specs=[per_seq((H, GD)), per_seq((H, nsp))],
            scratch_shapes=[page_buf, page_buf, pltpu.VMEM((past, GD), f32),
                            pltpu.SemaphoreType.DMA((2,)), pltpu.SemaphoreType.DMA((2,))]),
        out_shape=[jax.ShapeDtypeStruct((B, H, GD), f32), jax.ShapeDtypeStruct((B, H, nsp), f32)],
        compiler_params=params, name="nsa_decode_cmp",
    )(page_table, qbd, chunk_weights(w1_k), chunk_weights(w1_v), pe_term(pe_k, w1_k), pe_term(pe_v, w1_v),
      w2_bd(w2_k), w2_bd(w2_v), jnp.tile(gk_cmp, G).reshape(1, GD), bias_c, mw, kc_pool, vc_pool)

    out = pl.pallas_call(
        _nsa_decode_attn_kernel,
        grid_spec=pltpu.PrefetchScalarGridSpec(
            num_scalar_prefetch=1, grid=(B,),
            in_specs=[per_seq((H, GD)), per_seq((H, nsp)), per_seq((H, GD)), per_seq((H, 3)), per_seq((4, GD)),
                      const((n_blk_past, past)), const((H, past)), const((H, n_win)), const((H, 1)),
                      per_seq((GD, n_win)), per_seq((GD, n_win)), any_spec, any_spec],
            out_specs=per_seq((H, GD)),
            scratch_shapes=[page_buf, page_buf, pltpu.VMEM((H, past), f32),
                            pltpu.SemaphoreType.DMA((2,)), pltpu.SemaphoreType.DMA((2,))]),
        out_shape=jax.ShapeDtypeStruct((B, H, GD), f32),
        compiler_params=params, name="nsa_decode_attn",
    )(page_table, qbd, sel, o_cmp, gate_in, new_rows, expand, bias_s, bias_w, bias_0, kw_t, vw_t, ks_pool, vs_pool)

    out = out.reshape(B, G, R, G, DH)
    h_b = jnp.stack([out[:, g, :, g, :] for g in range(G)], axis=1)
    return h_b.reshape(B, 1, NSA_W)


def _mla_decode_kernel(pt_ref, ql_ref, qr_ref, new_ref, newr_ref, cpool_ref, rpool_ref, o_ref,
                       cbuf, rbuf, cb16, s_ref, csem, rsem):
    f32, bf16 = jnp.float32, jnp.bfloat16
    slot = _paged_pipeline(pt_ref, (cpool_ref, rpool_ref), (cbuf, rbuf), (csem, rsem))
    n_pages = pt_ref.shape[1]
    ql, qr = ql_ref[0], qr_ref[0]
    for p in range(n_pages):
        c16 = cbuf[slot, p].astype(bf16)
        cb16[p] = c16
        s_ref[:, p * PAGE_SIZE:(p + 1) * PAGE_SIZE] = (
            lax.dot_general(ql, c16, (((1,), (1,)), ((), ())), preferred_element_type=f32)
            + jnp.dot(qr, rbuf[slot, p].astype(bf16), preferred_element_type=f32)) * MLA_SCALE
    c_new = new_ref[0].astype(bf16).astype(f32)
    r_new = newr_ref[0].astype(bf16).astype(f32)
    s_new = (jnp.sum(ql.astype(f32) * c_new, axis=-1, keepdims=True)
             + jnp.sum(qr.astype(f32) * r_new, axis=-1, keepdims=True)) * MLA_SCALE
    s_past = s_ref[...]
    m = jnp.maximum(jnp.max(s_past, axis=-1, keepdims=True), s_new)
    e_new = jnp.exp(s_new - m)
    s_ref[...] = jnp.exp(s_past - m)
    denom = jnp.sum(s_ref[...], axis=-1, keepdims=True) + e_new
    inv = 1.0 / denom
    acc = (e_new * inv).astype(bf16).astype(f32) * c_new
    for p in range(n_pages):
        pe = (s_ref[:, p * PAGE_SIZE:(p + 1) * PAGE_SIZE] * inv).astype(bf16)
        acc = acc + jnp.dot(pe, cb16[p], preferred_element_type=f32)
    o_ref[0] = acc


def mla_decode_attention(q_lat, q_rope, ckv_new, kr_new, page_table, ckv_pool, krope_pool):
    f32, bf16 = jnp.float32, jnp.bfloat16
    B, n_pages = page_table.shape
    H = q_lat.shape[1]
    past = n_pages * PAGE_SIZE
    rpool_t = jnp.transpose(krope_pool, (0, 2, 1))
    per_seq = lambda shape: pl.BlockSpec((1,) + shape, lambda b, pt: (b,) + (0,) * len(shape))
    any_spec = pl.BlockSpec(memory_space=pl.ANY)
    return pl.pallas_call(
        _mla_decode_kernel,
        grid_spec=pltpu.PrefetchScalarGridSpec(
            num_scalar_prefetch=1, grid=(B,),
            in_specs=[per_seq((H, KV_LORA)), per_seq((H, MLA_ROPE)), per_seq((1, KV_LORA)), per_seq((1, MLA_ROPE)),
                      any_spec, any_spec],
            out_specs=per_seq((H, KV_LORA)),
            scratch_shapes=[pltpu.VMEM((2, n_pages, PAGE_SIZE, KV_LORA), f32),
                            pltpu.VMEM((2, n_pages, MLA_ROPE, PAGE_SIZE), f32),
                            pltpu.VMEM((n_pages, PAGE_SIZE, KV_LORA), bf16),
                            pltpu.VMEM((H, past), f32),
                            pltpu.SemaphoreType.DMA((2,)), pltpu.SemaphoreType.DMA((2,))]),
        out_shape=jax.ShapeDtypeStruct((B, H, KV_LORA), f32),
        compiler_params=pltpu.CompilerParams(dimension_semantics=("arbitrary",),
                                             vmem_limit_bytes=V7X_VMEM_LIMIT_BYTES),
        name="mla_decode",
    )(page_table, q_lat.astype(bf16), q_rope.astype(bf16), ckv_new.reshape(B, 1, KV_LORA),
      kr_new.reshape(B, 1, MLA_ROPE), ckv_pool, rpool_t)


def split_cols(a, sizes):
    idx = [int(s) for s in np.cumsum(sizes)[:-1]]
    return jnp.split(a, idx, axis=-1)


def rms_norm(x, g):
    xf = x.astype(jnp.float32)
    y = xf * lax.rsqrt(jnp.mean(xf * xf, axis=-1, keepdims=True) + EPS)
    return (y * g.astype(jnp.float32)).astype(x.dtype)


def t5_bucket(dist):
    n = jnp.maximum(dist, 0)
    exact = REL_BUCKETS // 2
    nf = jnp.maximum(n, exact).astype(jnp.float32)
    large = exact + (jnp.log(nf / exact) / math.log(REL_MAX_DIST / exact) * (REL_BUCKETS - exact)).astype(jnp.int32)
    return jnp.where(n < exact, n, jnp.minimum(large, REL_BUCKETS - 1))


def apply_rope(x, pos):
    half = x.shape[-1] // 2
    inv = ROPE_THETA ** (-jnp.arange(half, dtype=jnp.float32) / half)
    ang = pos.astype(jnp.float32)[:, None] * inv[None, :]
    ang = ang.reshape(ang.shape[:1] + (1,) * (x.ndim - 3) + (half,))
    cos, sin = jnp.cos(ang).astype(x.dtype), jnp.sin(ang).astype(x.dtype)
    x1, x2 = x[..., :half], x[..., half:]
    return jnp.concatenate([x1 * cos - x2 * sin, x1 * sin + x2 * cos], axis=-1)


def causal_conv(u, buf, w, b):
    S = u.shape[1]
    full = jnp.concatenate([buf.astype(u.dtype), u], axis=1)
    out = b + sum(full[:, j:j + S] * w[j] for j in range(CONV_K))
    return out, full[:, S:]


def mlstm_chunkwise(q, k, v, i_pre, logf, C0, n0, m0):
    f32 = jnp.float32
    q, k, v, i_pre, logf = (a.astype(f32) for a in (q, k, v, i_pre, logf))
    B, H, S, D = q.shape
    L = MLSTM_CHUNK if S % MLSTM_CHUNK == 0 else S
    NC = S // L

    def chunks(a):
        return jnp.moveaxis(a.reshape((B, H, NC, L) + a.shape[3:]), 2, 0)

    causal = jnp.tril(jnp.ones((L, L), dtype=bool))

    def step(carry, inp):
        C, n, m = carry
        qc, kc, vc, ic, fc = inp
        b = jnp.cumsum(fc, axis=-1)
        g = b + m[..., None]
        dmat = jnp.where(causal, b[..., :, None] - b[..., None, :] + ic[..., None, :], -jnp.inf)
        mt = jnp.maximum(g, jnp.max(dmat, axis=-1))
        inter = jnp.exp(g - mt)
        sqk = jnp.einsum('bhtd,bhsd->bhts', qc, kc) * jnp.exp(dmat - mt[..., None])
        num = inter[..., None] * jnp.einsum('bhvd,bhtd->bhtv', C, qc) + jnp.einsum('bhts,bhsv->bhtv', sqk, vc)
        den = inter * jnp.einsum('bhd,bhtd->bht', n, qc) + jnp.sum(sqk, axis=-1)
        h = num / jnp.maximum(jnp.abs(den), jnp.exp(-mt))[..., None]
        b_end = b[..., -1]
        w_log = b_end[..., None] - b + ic
        m_new = jnp.maximum(b_end + m, jnp.max(w_log, axis=-1))
        decay = jnp.exp(b_end + m - m_new)
        w_in = jnp.exp(w_log - m_new[..., None])
        C_new = decay[..., None, None] * C + jnp.einsum('bhs,bhsv,bhsd->bhvd', w_in, vc, kc)
        n_new = decay[..., None] * n + jnp.einsum('bhs,bhsd->bhd', w_in, kc)
        return (C_new, n_new, m_new), h

    (C1, n1, m1), hs = lax.scan(step, (C0.astype(f32), n0.astype(f32), m0.astype(f32)),
                                tuple(chunks(a) for a in (q, k, v, i_pre, logf)))
    return jnp.moveaxis(hs, 0, 2).reshape(B, H, S, D), C1, n1, m1


def to_chunks(a):
    B, T = a.shape[:2]
    pad = (-T) % CMP_STRIDE
    a = jnp.pad(a, ((0, 0), (0, pad), (0, 0), (0, 0)))
    return a.reshape((B, (T + pad) // CMP_STRIDE, CMP_STRIDE) + a.shape[2:])


def cmp_summaries(chunk_list, T, pe, w1, w2):
    lo = jnp.concatenate([jnp.einsum('bcjgd,jdh->bcgh', r, w1[:CMP_STRIDE]) for r in chunk_list], axis=1)
    hi = jnp.concatenate([jnp.einsum('bcjgd,jdh->bcgh', r, w1[CMP_STRIDE:]) for r in chunk_list], axis=1)
    n_cmp = (T - CMP_LEN) // CMP_STRIDE + 1
    hid = jax.nn.silu(lo[:, :n_cmp] + hi[:, 1:n_cmp + 1] + jnp.einsum('jd,jdh->h', pe, w1))
    return hid @ w2


def even_mixer_prompt(y, g_mix, w_in, w_out, conv_w, conv_b, ml_wq, ml_wk, ml_b_i, ml_b_f, ml_out_g,
                      nsa_gq, nsa_gk_cmp, nsa_gk_slc, nsa_gk_win, pe_k, w1_k, w2_k, pe_v, w1_v, w2_v, gate_b, rel_bias):
    B, S, _ = y.shape
    assert S >= CONV_K - 1
    p = even_in_proj_prompt(y, g_mix, w_in, nsa_gq, nsa_gk_slc, nsa_gk_win)
    h_a, C1, n1, m1 = mlstm_prompt(p['u'], p['v'], p['o'], conv_w, conv_b, ml_wq, ml_wk, ml_out_g,
                                   p['i_pre'] + ml_b_i, jax.nn.log_sigmoid(p['f_pre'] + ml_b_f))
    k_cmp = rms_norm(cmp_summaries([to_chunks(p['kc'])], S, pe_k, w1_k, w2_k), nsa_gk_cmp)
    v_cmp = cmp_summaries([to_chunks(p['vc'])], S, pe_v, w1_v, w2_v)
    h_b = nsa_prompt(p['q5'], k_cmp, v_cmp, p['kst'], p['vst'], p['kwt'], p['vwt'], p['g_pre'], gate_b, rel_bias)
    nb = min(WINDOW, S)
    out = jnp.concatenate([h_a, h_b], axis=-1) @ w_out
    new = dict(C=C1, n=n1, m=m1, conv=p['u'][:, S - (CONV_K - 1):], cmp_k=p['kc'], cmp_v=p['vc'],
               slc_k=p['ks'], slc_v=p['vs'], win_k=p['kw'][:, S - nb:], win_v=p['vw'][:, S - nb:])
    return out, new


def even_mixer(xn, pos0, past, w_in, w_out, conv_w, conv_b, ml_wq, ml_wk, ml_b_i, ml_b_f, ml_out_g,
               nsa_gq, nsa_gk_cmp, nsa_gk_slc, nsa_gk_win, pe_k, w1_k, w2_k, pe_v, w1_v, w2_v, gate_b, rel_bias):
    B, S, _ = xn.shape
    dt = xn.dtype
    (u, v_m, o_pre, i_pre, f_pre, q, kc, vc, ks, vs, kw, vw, g_pre) = split_cols(xn @ w_in, EV_SPLITS)
    assert S == 1
    conv_buf, C0, n0, m0 = past['conv'], past['C'], past['n'], past['m']
    c, conv_new = causal_conv(u, conv_buf, conv_w, conv_b)
    ch = jax.nn.silu(c).reshape(B, S, ML_H, ML_DH)
    qm = jnp.einsum('bshd,hde->bhse', ch, ml_wq)
    km = jnp.einsum('bshd,hde->bhse', ch, ml_wk) * (ML_DH ** -0.5)
    vm = jnp.transpose(v_m.reshape(B, S, ML_H, ML_DH), (0, 2, 1, 3))
    ig = jnp.transpose(i_pre + ml_b_i, (0, 2, 1))
    lf = jax.nn.log_sigmoid(jnp.transpose(f_pre + ml_b_f, (0, 2, 1)).astype(jnp.float32))
    hm, C1, n1, m1 = mlstm_chunkwise(qm, km, vm, ig, lf, C0, n0, m0)
    hm = rms_norm(jnp.transpose(hm, (0, 2, 1, 3)).astype(dt), ml_out_g)
    h_a = (hm * jax.nn.sigmoid(o_pre).reshape(B, S, ML_H, ML_DH)).reshape(B, S, ML_W)

    q = rms_norm(q.reshape(B, S, NSA_H, NSA_DH), nsa_gq)
    kv_shape = (B, S, NSA_KV, NSA_DH)
    kc, vc, vs, vw = (a.reshape(kv_shape) for a in (kc, vc, vs, vw))
    ks = rms_norm(ks.reshape(kv_shape), nsa_gk_slc)
    kw = rms_norm(kw.reshape(kv_shape), nsa_gk_win)
    e = past['e']
    h_b = nsa_decode(q, ks, vs, kw, vw, g_pre, gate_b, rel_bias, past['page_table'],
                     past['cmp_k'][e], past['cmp_v'][e], past['slc_k'][e], past['slc_v'][e],
                     past['win_k'], past['win_v'], pe_k, w1_k, w2_k, pe_v, w1_v, w2_v, nsa_gk_cmp)
    win_k_new = jnp.concatenate([past['win_k'][:, S:], kw], axis=1)
    win_v_new = jnp.concatenate([past['win_v'][:, S:], vw], axis=1)
    out = jnp.concatenate([h_a, h_b], axis=-1) @ w_out
    new = dict(C=C1.astype(dt), n=n1.astype(dt), m=m1.astype(dt), conv=conv_new,
               cmp_k=kc, cmp_v=vc, slc_k=ks, slc_v=vs, win_k=win_k_new, win_v=win_v_new)
    return out, new


def odd_mixer_prompt(y, g_mix, w_in, g_cq, w_uq, g_q, g_ckv, g_kr, w_uk, w_uv, w_out):
    bf16 = jnp.bfloat16
    tq = jnp.arange(y.shape[1])
    cq16, ckv, ckv16, kr, kr16 = odd_in_proj_prompt(y, g_mix, w_in, g_cq, g_ckv, g_kr, tq)
    o = mla_prompt_attention(cq16, w_uq, g_q, tq, ckv16, kr16, jnp.transpose(w_uk, (1, 0, 2)).astype(bf16),
                             jnp.transpose(w_uv, (1, 0, 2)).astype(bf16))
    return o @ w_out, dict(ckv=ckv, krope=kr)


def odd_mixer(xn, pos0, past, w_in, g_cq, w_uq, g_q, g_ckv, g_kr, w_uk, w_uv, w_out):
    B, S, _ = xn.shape
    assert S == 1
    cq, ckv, kr = split_cols(xn @ w_in, OD_SPLITS)
    tq = pos0 + jnp.arange(S)
    ckv = rms_norm(ckv, g_ckv)
    kr = apply_rope(rms_norm(kr, g_kr), tq)
    e = past['e']
    q = rms_norm((rms_norm(cq, g_cq) @ w_uq).reshape(B, S, MLA_H, MLA_NOPE + MLA_ROPE), g_q)
    q_nope = q[..., :MLA_NOPE]
    q_rope = apply_rope(q[..., MLA_NOPE:], tq)
    q_lat = jnp.einsum('bqhn,chn->bqhc', q_nope, w_uk)
    o_lat = mla_decode_attention(q_lat[:, 0], q_rope[:, 0], ckv[:, 0], kr[:, 0], past['page_table'],
                                 past['ckv'][e], past['krope'][e])[:, None]
    o = jnp.einsum('bqhc,chv->bqhv', o_lat, w_uv).reshape(B, S, MLA_H * MLA_V)
    return o @ w_out, dict(ckv=ckv, krope=kr)


def mem_kv(mem, g_mem, wk, wv, gk):
    B, M, _ = mem.shape
    m = rms_norm(mem, g_mem)
    k = rms_norm((m @ wk).reshape(B, M, XM_H, XM_DH), gk)
    v = (m @ wv).reshape(B, M, XM_H, XM_DH)
    return k, v


def mem_attend(xn, k, v, wq, gq, wo):
    B, S, _ = xn.shape
    q = rms_norm((xn @ wq).reshape(B, S, XM_H, XM_DH), gq)
    s = jnp.einsum('bshd,bmhd->bhsm', q, k.astype(q.dtype)).astype(jnp.float32) * (XM_DH ** -0.5)
    p = jax.nn.softmax(s, axis=-1)
    return jnp.einsum('bhsm,bmhd->bshd', p.astype(xn.dtype), v.astype(xn.dtype)).reshape(B, S, XM_W) @ wo


def stack_key(lst, name):
    return jnp.stack([d[name] for d in lst])


def kernel(x_prompt, x_sample, mem_prompt,
           state_ml_C, state_ml_n, state_ml_m, state_ml_conv,
           cache_cmp_k, cache_cmp_v, cache_slc_k, cache_slc_v, cache_win_k, cache_win_v,
           cache_mla_ckv, cache_mla_krope, cache_mem_k, cache_mem_v, page_table,
           rel_bias, ffn1_norm, ffn1_wg, ffn1_wu, ffn1_wd, mix_norm,
           xm_norm, xm_mem_norm, xm_wq, xm_wk, xm_wv, xm_wo, xm_gq, xm_gk,
           ffn2_norm, ffn2_wg, ffn2_wu, ffn2_wd,
           ev_w_in, ev_w_out, ml_conv_w, ml_conv_b, ml_wq, ml_wk, ml_b_i, ml_b_f, ml_out_g,
           nsa_gq, nsa_gk_cmp, nsa_gk_slc, nsa_gk_win, cmp_pe_k, cmp_w1_k, cmp_w2_k,
           cmp_pe_v, cmp_w1_v, cmp_w2_v, nsa_gate_b,
           od_w_in, mla_g_cq, mla_w_uq, mla_g_q, mla_g_ckv, mla_g_kr, mla_w_uk, mla_w_uv, od_w_out):
    past_len = page_table.shape[1] * PAGE_SIZE
    bf = jnp.bfloat16
    ffn_w = [[(n[layer], wg[layer].astype(bf), wu[layer].astype(bf), wd[layer].astype(bf))
              for n, wg, wu, wd in ((ffn1_norm, ffn1_wg, ffn1_wu, ffn1_wd), (ffn2_norm, ffn2_wg, ffn2_wu, ffn2_wd))]
             for layer in range(DEPTH)]

    def run_group(y, prompt):
        ev, od, memk, memv = [], [], [], []
        for layer in range(DEPTH):
            y = swiglu_half(y, *ffn_w[layer][0])
            xn = None if prompt else rms_norm(y, mix_norm[layer])
            if layer % 2 == 0:
                e = layer // 2
                ew = dict(w_in=ev_w_in[e], w_out=ev_w_out[e], conv_w=ml_conv_w[e], conv_b=ml_conv_b[e],
                          ml_wq=ml_wq[e], ml_wk=ml_wk[e], ml_b_i=ml_b_i[e], ml_b_f=ml_b_f[e], ml_out_g=ml_out_g[e],
                          nsa_gq=nsa_gq[e], nsa_gk_cmp=nsa_gk_cmp[e], nsa_gk_slc=nsa_gk_slc[e],
                          nsa_gk_win=nsa_gk_win[e], pe_k=cmp_pe_k[e], w1_k=cmp_w1_k[e], w2_k=cmp_w2_k[e],
                          pe_v=cmp_pe_v[e], w1_v=cmp_w1_v[e], w2_v=cmp_w2_v[e],
                          gate_b=nsa_gate_b[e], rel_bias=rel_bias)
                past = None if prompt else dict(
                    e=e, page_table=page_table, C=state_ml_C[e], n=state_ml_n[e], m=state_ml_m[e],
                    conv=state_ml_conv[e], cmp_k=cache_cmp_k, cmp_v=cache_cmp_v,
                    slc_k=cache_slc_k, slc_v=cache_slc_v, win_k=cache_win_k[e], win_v=cache_win_v[e])
                if prompt:
                    h, st = even_mixer_prompt(y, mix_norm[layer], **ew)
                else:
                    h, st = even_mixer(xn, past_len, past, **ew)
                ev.append(st)
            else:
                o = layer // 2
                ow = dict(w_in=od_w_in[o], g_cq=mla_g_cq[o], w_uq=mla_w_uq[o], g_q=mla_g_q[o], g_ckv=mla_g_ckv[o],
                          g_kr=mla_g_kr[o], w_uk=mla_w_uk[o], w_uv=mla_w_uv[o], w_out=od_w_out[o])
                past = None if prompt else dict(e=o, page_table=page_table, ckv=cache_mla_ckv, krope=cache_mla_krope)
                if prompt:
                    h, st = odd_mixer_prompt(y, mix_norm[layer], **ow)
                else:
                    h, st = odd_mixer(xn, past_len, past, **ow)
                od.append(st)
            y = y + h
            if prompt:
                mk, mv = mem_kv(mem_prompt, xm_mem_norm[layer], xm_wk[layer], xm_wv[layer], xm_gk[layer])
                memk.append(mk)
                memv.append(mv)
                y = mem_attention(y, xm_norm[layer], xm_wq[layer], xm_gq[layer], mk, mv, xm_wo[layer])
            else:
                y = y + mem_attend(rms_norm(y, xm_norm[layer]), cache_mem_k[layer], cache_mem_v[layer],
                                   xm_wq[layer], xm_gq[layer], xm_wo[layer])
            y = swiglu_half(y, *ffn_w[layer][1])
        return y, ev, od, memk, memv

    ys, ev_s, od_s, _, _ = run_group(x_sample, False)
    yp, ev_p, od_p, memk_p, memv_p = run_group(x_prompt, True)
    return (yp, ys,
            stack_key(ev_p, 'C'), stack_key(ev_p, 'n'), stack_key(ev_p, 'm'), stack_key(ev_p, 'conv'),
            stack_key(ev_p, 'cmp_k'), stack_key(ev_p, 'cmp_v'), stack_key(ev_p, 'slc_k'), stack_key(ev_p, 'slc_v'),
            stack_key(ev_p, 'win_k'), stack_key(ev_p, 'win_v'),
            stack_key(od_p, 'ckv'), stack_key(od_p, 'krope'),
            jnp.stack(memk_p), jnp.stack(memv_p),
            stack_key(ev_s, 'C'), stack_key(ev_s, 'n'), stack_key(ev_s, 'm'), stack_key(ev_s, 'conv'),
            stack_key(ev_s, 'cmp_k'), stack_key(ev_s, 'cmp_v'), stack_key(ev_s, 'slc_k'), stack_key(ev_s, 'slc_v'),
            stack_key(ev_s, 'win_k'), stack_key(ev_s, 'win_v'),
            stack_key(od_s, 'ckv'), stack_key(od_s, 'krope'))
```

```python
import functools
import math

import jax
import jax.numpy as jnp
import numpy as np
from jax import lax
from jax.experimental import pallas as pl
from jax.experimental.pallas import tpu as pltpu

D_MODEL = 1024
DEPTH = 2
PAGE_SIZE = 128
EPS = 1e-6
NEG_INF = -1e30
D_FF = 2816
ML_H = 4
ML_DH = 128
ML_W = ML_H * ML_DH
CONV_K = 4
MLSTM_CHUNK = 64
NSA_H = 8
NSA_KV = 2
NSA_R = NSA_H // NSA_KV
NSA_DH = 64
NSA_W = NSA_H * NSA_DH
NSA_KVW = NSA_KV * NSA_DH
NSA_SCALE = NSA_DH ** -0.5
CMP_STRIDE = 16
CMP_LEN = 2 * CMP_STRIDE
CMP_HID = 2 * NSA_DH
SLC_BLOCK = 64
SLC_RATIO = SLC_BLOCK // CMP_STRIDE
SLC_OVERLAP_W = (1.0, 2.0, 2.0, 2.0, 1.0)
N_SELECT = 16
WINDOW = 512
SLC_QBLOCK = 32
WIN_QBLOCK = 128
FORCE_SCORE = 1e6
REL_BUCKETS = 32
REL_MAX_DIST = 128
MLA_H = 16
MLA_NOPE = 64
MLA_ROPE = 32
MLA_V = 64
Q_LORA = 384
KV_LORA = 256
MLA_SCALE = (MLA_NOPE + MLA_ROPE) ** -0.5
MLA_QBLOCK = 128
ROPE_THETA = 10000.0
MEM_LEN = 256
XM_H = 4
XM_DH = 128
XM_W = XM_H * XM_DH
EV_SPLITS = (ML_W, ML_W, ML_W, ML_H, ML_H, NSA_W) + (NSA_KVW,) * 6 + (NSA_H * 3,)
OD_SPLITS = (Q_LORA, KV_LORA, MLA_ROPE)

V7X_LANES = 128
V7X_VMEM_LIMIT_BYTES = 56 * 1024 * 1024
FFN_CHUNK = 256
FFN_ROWS = 512


def _ffn_kernel(x_ref, g_ref, wg_ref, wu_ref, wd_ref, o_ref, act_ref):
    x = x_ref[...]
    h = x * lax.rsqrt(jnp.mean(x * x, axis=-1, keepdims=True) + EPS) * g_ref[...]
    hb = h.astype(jnp.bfloat16)
    for c in range(D_FF // FFN_CHUNK):
        cols = slice(c * FFN_CHUNK, (c + 1) * FFN_CHUNK)
        gate = jnp.dot(hb, wg_ref[:, cols], preferred_element_type=jnp.float32)
        up = jnp.dot(hb, wu_ref[:, cols], preferred_element_type=jnp.float32)
        act_ref[:, cols] = (gate * jax.nn.sigmoid(gate) * up).astype(jnp.bfloat16)
    o_ref[...] = x + 0.5 * jnp.dot(act_ref[...], wd_ref[...], preferred_element_type=jnp.float32)


def ffn_half(x2d, g, wg, wu, wd):
    m = x2d.shape[0]
    tm = min(FFN_ROWS, m)
    assert m % tm == 0
    resident = functools.partial(pl.BlockSpec, pipeline_mode=pl.Buffered(1))
    return pl.pallas_call(
        _ffn_kernel,
        grid=(m // tm,),
        in_specs=[
            pl.BlockSpec((tm, D_MODEL), lambda i: (i, 0)),
            resident((1, D_MODEL), lambda i: (0, 0)),
            resident((D_MODEL, D_FF), lambda i: (0, 0)),
            resident((D_MODEL, D_FF), lambda i: (0, 0)),
            resident((D_FF, D_MODEL), lambda i: (0, 0)),
        ],
        out_specs=pl.BlockSpec((tm, D_MODEL), lambda i: (i, 0)),
        out_shape=jax.ShapeDtypeStruct((m, D_MODEL), jnp.float32),
        scratch_shapes=[pltpu.VMEM((tm, D_FF), jnp.bfloat16)],
        compiler_params=pltpu.CompilerParams(
            dimension_semantics=("arbitrary",), vmem_limit_bytes=V7X_VMEM_LIMIT_BYTES),
        name="ffn_half",
    )(x2d, g.reshape(1, D_MODEL), wg, wu, wd)


def swiglu_half(x, g, wg, wu, wd):
    shp = x.shape
    return ffn_half(x.reshape(-1, D_MODEL), g, wg, wu, wd).reshape(shp)


EV_PROJ_ROWS = 512
EV_PROJ_GROUPS = (("u", ML_W), ("v", ML_W), ("o", ML_W), ("q", NSA_W), ("kc", NSA_KVW), ("vc", NSA_KVW),
                  ("ks", NSA_KVW), ("vs", NSA_KVW), ("kw", NSA_KVW), ("vw", NSA_KVW), ("small", V7X_LANES))


def _even_proj_kernel(x_ref, g_ref, w_ref, gq_ref, gks_ref, gkw_ref,
                      u_ref, v_ref, o_ref, small_ref, kc_ref, vc_ref, ks_ref, vs_ref, kw_ref, vw_ref,
                      q5_ref, kst_ref, vst_ref, kwt_ref, vwt_ref):
    f32, bf16 = jnp.float32, jnp.bfloat16
    x = x_ref[0]
    xn = (x * lax.rsqrt(jnp.mean(x * x, axis=-1, keepdims=True) + EPS) * g_ref[...]).astype(bf16)
    y = jnp.dot(xn, w_ref[...], preferred_element_type=f32)
    off, cols = 0, {}
    for name, width in EV_PROJ_GROUPS:
        cols[name] = y[:, off:off + width]
        off += width
    lane = lax.broadcasted_iota(jnp.int32, (1, V7X_LANES), 1)
    DH = NSA_DH

    def pair_norm(p, gain):
        sq = p * p
        s_all = jnp.sum(sq, axis=-1, keepdims=True)
        s_lo = jnp.sum(jnp.where(lane < DH, sq, 0.0), axis=-1, keepdims=True)
        ms = jnp.where(lane < DH, s_lo, s_all - s_lo) * (1.0 / DH)
        return p * lax.rsqrt(ms + EPS) * gain

    def put_pair(ref, first, p):
        p = p.astype(bf16)
        ref[0, first] = p[:, :DH]
        ref[0, first + 1] = p[:, DH:]

    u_ref[0], v_ref[0], o_ref[0], small_ref[0] = cols["u"], cols["v"], cols["o"], cols["small"]
    for pr in range(NSA_H // 2):
        qn = pair_norm(cols["q"][:, pr * V7X_LANES:(pr + 1) * V7X_LANES], gq_ref[...])
        put_pair(q5_ref, 2 * pr, qn * NSA_SCALE)
    kc_ref[0], vc_ref[0], vs_ref[0], vw_ref[0] = cols["kc"], cols["vc"], cols["vs"], cols["vw"]
    ks = pair_norm(cols["ks"], gks_ref[...])
    kw = pair_norm(cols["kw"], gkw_ref[...])
    ks_ref[0], kw_ref[0] = ks, kw
    put_pair(kst_ref, 0, ks)
    put_pair(vst_ref, 0, cols["vs"])
    put_pair(kwt_ref, 0, kw)
    put_pair(vwt_ref, 0, cols["vw"])


def even_in_proj_prompt(y, g_mix, w_in, nsa_gq, nsa_gk_slc, nsa_gk_win):
    f32, bf16 = jnp.float32, jnp.bfloat16
    B, S, D = y.shape
    tm = min(EV_PROJ_ROWS, S)
    assert S % tm == 0 and NSA_KVW == V7X_LANES and 2 * ML_H + 3 * NSA_H <= V7X_LANES
    (w_u, w_v, w_o, w_i, w_f, w_q, w_kc, w_vc, w_ks, w_vs, w_kw, w_vw, w_g) = split_cols(w_in, EV_SPLITS)
    w_small = jnp.pad(jnp.concatenate([w_i, w_f, w_g], axis=1), ((0, 0), (0, V7X_LANES - 2 * ML_H - 3 * NSA_H)))
    w = jnp.concatenate([w_u, w_v, w_o, w_q, w_kc, w_vc, w_ks, w_vs, w_kw, w_vw, w_small], axis=1).astype(bf16)
    n_cols = w.shape[1]
    assert n_cols == sum(width for _, width in EV_PROJ_GROUPS)
    const = lambda shape: pl.BlockSpec(shape, lambda b, i: (0,) * len(shape), pipeline_mode=pl.Buffered(1))
    rows = lambda width: pl.BlockSpec((1, tm, width), lambda b, i: (b, i, 0))
    heads = lambda n: pl.BlockSpec((1, n, tm, NSA_DH), lambda b, i: (b, 0, i, 0))
    wide = jax.ShapeDtypeStruct((B, S, ML_W), f32)
    kv = jax.ShapeDtypeStruct((B, S, NSA_KVW), f32)
    kvt = jax.ShapeDtypeStruct((B, NSA_KV, S, NSA_DH), bf16)
    pair_gain = lambda g: jnp.tile(g, 2).reshape(1, V7X_LANES)
    (u, v, o, small, kc, vc, ks, vs, kw, vw, q5, kst, vst, kwt, vwt) = pl.pallas_call(
        _even_proj_kernel,
        grid=(B, S // tm),
        in_specs=[rows(D), const((1, D)), const((D, n_cols)), const((1, V7X_LANES)), const((1, V7X_LANES)),
                  const((1, V7X_LANES))],
        out_specs=[rows(ML_W), rows(ML_W), rows(ML_W), rows(V7X_LANES)] + [rows(NSA_KVW)] * 6
                  + [heads(NSA_H)] + [heads(NSA_KV)] * 4,
        out_shape=[wide, wide, wide, jax.ShapeDtypeStruct((B, S, V7X_LANES), f32)] + [kv] * 6
                  + [jax.ShapeDtypeStruct((B, NSA_H, S, NSA_DH), bf16)] + [kvt] * 4,
        compiler_params=pltpu.CompilerParams(
            dimension_semantics=("arbitrary", "arbitrary"), vmem_limit_bytes=V7X_VMEM_LIMIT_BYTES),
        name="even_in_proj",
    )(y, g_mix.reshape(1, D), w, pair_gain(nsa_gq), pair_gain(nsa_gk_slc), pair_gain(nsa_gk_win))
    kv4 = lambda a: a.reshape(B, S, NSA_KV, NSA_DH)
    return dict(u=u, v=v, o=o, i_pre=small[..., :ML_H], f_pre=small[..., ML_H:2 * ML_H],
                g_pre=small[..., 2 * ML_H:2 * ML_H + 3 * NSA_H], kc=kv4(kc), vc=kv4(vc), ks=kv4(ks), vs=kv4(vs),
                kw=kv4(kw), vw=kv4(vw), q5=q5, kst=kst, vst=vst, kwt=kwt, vwt=vwt)


NSA_TQ = 256
NSA_NEAR = 128


def _nsa_prompt_kernel(q_ref, kc_ref, vc_ref, ks_ref, vs_ref, kw_ref, vw_ref, bc_ref, bn_ref,
                       e_ref, mw_ref, gp_ref, gb_ref, o_ref, *, n_cmp, n_select):
    f32, bf16 = jnp.float32, jnp.bfloat16
    qi = pl.program_id(2)
    R, TQ, DH = q_ref.shape[2:]
    S = ks_ref.shape[2]
    t0 = qi * TQ
    q2 = q_ref[0, 0].reshape(R * TQ, DH)
    t_col = t0 + lax.broadcasted_iota(jnp.int32, (TQ, 1), 0)

    def scores(k):
        s = lax.dot_general(q2, k, (((1,), (1,)), ((), ())), preferred_element_type=f32)
        return s.reshape(R, TQ, k.shape[0])

    ncp = kc_ref.shape[2]
    c_row = lax.broadcasted_iota(jnp.int32, (1, ncp), 1)
    mask_c = (t_col >= c_row * CMP_STRIDE + (CMP_LEN - 1)) & (c_row < n_cmp)
    s = jnp.where(mask_c[None], scores(kc_ref[0, 0]) + bc_ref[0], NEG_INF)
    e = jnp.exp(s - jnp.max(s, axis=-1, keepdims=True))
    p = e / jnp.sum(e, axis=-1, keepdims=True) * mask_c.astype(f32)[None]
    o_cmp = jnp.dot(p.reshape(R * TQ, ncp).astype(bf16), vc_ref[0, 0],
                    preferred_element_type=f32).reshape(R, TQ, DH)
    imp = jnp.sum(p, axis=0)

    mw = mw_ref[...]
    hi = imp.astype(bf16)
    r1 = imp - hi.astype(f32)
    mid = r1.astype(bf16)
    lo = (r1 - mid.astype(f32)).astype(bf16)
    p_slc = (jnp.dot(hi, mw, preferred_element_type=f32) + jnp.dot(mid, mw, preferred_element_type=f32)
             + jnp.dot(lo, mw, preferred_element_type=f32))
    ns = mw_ref.shape[1]
    jb = lax.broadcasted_iota(jnp.int32, (1, ns), 1)
    tb = jnp.right_shift(t_col, int(math.log2(SLC_BLOCK)))
    forced = (jb == 0) | (jb == tb) | (jb == tb - 1)
    score = jnp.where(jb <= tb, p_slc + FORCE_SCORE * forced.astype(f32), -1.0)
    rank = jnp.zeros((TQ, ns), f32)
    for j in range(ns):
        col = score[:, j:j + 1]
        rank = rank + ((col > score) | ((col == score) & (jb > j))).astype(f32)
    sel = (rank < n_select).astype(bf16)
    gates = jax.nn.sigmoid(gp_ref[0, 0] + gb_ref[0])
    q = q_ref[0, 0]

    def attend(r, k, v, add_mask, first_key, t_start):
        s = lax.dot_general(q[r], k, (((1,), (1,)), ((), ())), preferred_element_type=f32) + add_mask
        near_lo = max(t_start - NSA_NEAR, first_key)
        band = bn_ref[0, r][:, near_lo - (t_start - NSA_NEAR):]
        cut = near_lo - first_key
        near = s[:, cut:] + band
        s = near if cut == 0 else jnp.concatenate([s[:, :cut], near], axis=1)
        e = jnp.exp(s - jnp.max(s, axis=-1, keepdims=True))
        return jnp.dot(e.astype(bf16), v, preferred_element_type=f32) / jnp.sum(e, axis=-1, keepdims=True)

    for c in range(S // TQ):
        @pl.when(qi == c)
        def _(c=c):
            t_start, n_keys = c * TQ, (c + 1) * TQ
            row = t_start + lax.broadcasted_iota(jnp.int32, (TQ, 1), 0)
            col = lax.broadcasted_iota(jnp.int32, (1, n_keys), 1)
            sel_tok = jnp.dot(sel, e_ref[:, :n_keys], preferred_element_type=f32)
            mask_s = jnp.where((sel_tok > 0.5) & (col <= row), 0.0, NEG_INF)
            w_lo = max(t_start - WINDOW, 0)
            col_w = w_lo + lax.broadcasted_iota(jnp.int32, (1, n_keys - w_lo), 1)
            mask_w = jnp.where((col_w <= row) & (row - col_w <= WINDOW), 0.0, NEG_INF)
            ks, vs = ks_ref[0, 0, :n_keys, :], vs_ref[0, 0, :n_keys, :]
            kw, vw = kw_ref[0, 0, w_lo:n_keys, :], vw_ref[0, 0, w_lo:n_keys, :]
            outs = []
            for r in range(R):
                o_slc = attend(r, ks, vs, mask_s, 0, t_start)
                o_win = attend(r, kw, vw, mask_w, w_lo, t_start)
                outs.append(gates[:, 3 * r:3 * r + 1] * o_cmp[r] + gates[:, 3 * r + 1:3 * r + 2] * o_slc
                            + gates[:, 3 * r + 2:3 * r + 3] * o_win)
            o_ref[0] = jnp.concatenate(outs, axis=-1)


def _bias_lookup(rel_bias, dist):
    bucket = t5_bucket(dist)[..., None]
    out = jnp.zeros(dist.shape + rel_bias.shape[1:], rel_bias.dtype)
    for b in range(REL_BUCKETS):
        out = jnp.where(bucket == b, rel_bias[b], out)
    return out


def nsa_prompt(q_heads, k_cmp, v_cmp, ks_t, vs_t, kw_t, vw_t, g_pre, gate_b, rel_bias):
    f32, bf16 = jnp.float32, jnp.bfloat16
    B, _, S, _ = q_heads.shape
    G, R, DH, TQ = NSA_KV, NSA_R, NSA_DH, NSA_TQ
    assert S % TQ == 0 and TQ % V7X_LANES == 0 and WINDOW % V7X_LANES == 0 and NSA_NEAR % V7X_LANES == 0
    n_cmp = k_cmp.shape[1]
    ncp = S // CMP_STRIDE
    ns = S // SLC_BLOCK
    assert n_cmp == ncp - 1
    n_select = min(N_SELECT, ns)

    def kv_layout(a, n):
        a = jnp.pad(a, ((0, 0), (0, n - a.shape[1]), (0, 0), (0, 0)))
        return jnp.transpose(a, (0, 2, 1, 3)).astype(bf16)

    q5 = q_heads.reshape(B, G, R, S, DH)
    kc, vc = kv_layout(k_cmp, ncp), kv_layout(v_cmp, ncp)

    half = REL_BUCKETS // 2
    assert half + int(math.log(NSA_NEAR / half) / math.log(REL_MAX_DIST / half) * half) >= REL_BUCKETS - 1
    t = jnp.arange(S)
    dist_c = t[:, None] - (jnp.arange(ncp) * CMP_STRIDE + CMP_LEN - 1)[None, :]
    bias_c = jnp.transpose(_bias_lookup(rel_bias, dist_c).reshape(S, ncp, G, R), (2, 3, 0, 1))
    d_near = jnp.arange(TQ)[:, None] + NSA_NEAR - jnp.arange(TQ + NSA_NEAR)[None, :]
    bias_n = jnp.transpose((_bias_lookup(rel_bias, d_near) - rel_bias[REL_BUCKETS - 1]).reshape(
        TQ, TQ + NSA_NEAR, G, R), (2, 3, 0, 1))
    expand = (jnp.arange(S)[None, :] // SLC_BLOCK == jnp.arange(ns)[:, None]).astype(bf16)
    c_i = jnp.arange(ncp)[:, None]
    j_i = jnp.arange(ns)[None, :]
    mw = sum(w * (c_i == SLC_RATIO * j_i + k - 1) for k, w in enumerate(SLC_OVERLAP_W)).astype(bf16)
    gp = jnp.transpose(g_pre.reshape(B, S, G, 3 * R), (0, 2, 1, 3))
    gb = gate_b.reshape(G, 1, 3 * R)

    kv_spec = lambda n: pl.BlockSpec((1, 1, n, DH), lambda b, g, i: (b, g, 0, 0))
    return pl.pallas_call(
        functools.partial(_nsa_prompt_kernel, n_cmp=n_cmp, n_select=n_select),
        grid=(B, G, S // TQ),
        in_specs=[
            pl.BlockSpec((1, 1, R, TQ, DH), lambda b, g, i: (b, g, 0, i, 0)),
            kv_spec(ncp), kv_spec(ncp), kv_spec(S), kv_spec(S), kv_spec(S), kv_spec(S),
            pl.BlockSpec((1, R, TQ, ncp), lambda b, g, i: (g, 0, i, 0)),
            pl.BlockSpec((1, R, TQ, TQ + NSA_NEAR), lambda b, g, i: (g, 0, 0, 0)),
            pl.BlockSpec((ns, S), lambda b, g, i: (0, 0)),
            pl.BlockSpec((ncp, ns), lambda b, g, i: (0, 0)),
            pl.BlockSpec((1, 1, TQ, 3 * R), lambda b, g, i: (b, g, i, 0)),
            pl.BlockSpec((1, 1, 3 * R), lambda b, g, i: (g, 0, 0)),
        ],
        out_specs=pl.BlockSpec((1, TQ, R * DH), lambda b, g, i: (b, i, g)),
        out_shape=jax.ShapeDtypeStruct((B, S, NSA_W), f32),
        compiler_params=pltpu.CompilerParams(
            dimension_semantics=("arbitrary", "arbitrary", "arbitrary"), vmem_limit_bytes=V7X_VMEM_LIMIT_BYTES),
        name="nsa_prompt",
    )(q5, kc, vc, ks_t, vs_t, kw_t, vw_t, bias_c, bias_n, expand, mw, gp, gb)


MLA_TQ = 512
MLA_HEADS_PER_STEP = 2


def _mla_prompt_kernel(cq_ref, wuq_ref, ra_ref, rm_ref, rp_ref, ckv_ref, kr_ref, wuk_ref, wuv_ref,
                       o_ref, q_ref, k_ref, v_ref):
    f32, bf16 = jnp.float32, jnp.bfloat16
    qi = pl.program_id(2)
    HP, S, DP = k_ref.shape
    TQ = cq_ref.shape[1]
    causal = lax.broadcasted_iota(jnp.int32, (TQ, 1), 0) >= lax.broadcasted_iota(jnp.int32, (1, TQ), 1)

    @pl.when(qi == 0)
    def _():
        ckv = ckv_ref[0]
        pad = jnp.zeros((S, DP - MLA_NOPE - MLA_ROPE), bf16)
        for h in range(HP):
            k_nope = jnp.dot(ckv, wuk_ref[h], preferred_element_type=f32).astype(bf16)
            k_ref[h] = jnp.concatenate([k_nope, kr_ref[0], pad], axis=-1)
            v_ref[h] = jnp.dot(ckv, wuv_ref[h], preferred_element_type=f32).astype(bf16)

    for h in range(HP):
        q3 = jnp.dot(cq_ref[0], wuq_ref[h], preferred_element_type=f32)
        q = q3[:, :DP]
        ms = jnp.sum(q * q, axis=-1, keepdims=True) * (1.0 / (MLA_NOPE + MLA_ROPE))
        q = q * ra_ref[...] + q3[:, DP:2 * DP] * rm_ref[...] + q3[:, 2 * DP:] * rp_ref[...]
        q_ref[h] = (q * (lax.rsqrt(ms + EPS) * MLA_SCALE)).astype(bf16)

    for c in range(S // TQ):
        @pl.when(qi == c)
        def _(c=c):
            n_vis = c * TQ
            nt = (((1,), (1,)), ((), ()))
            outs = []
            for h in range(HP):
                qh = q_ref[h]
                s_diag = jnp.where(causal, lax.dot_general(qh, k_ref[h, n_vis:n_vis + TQ, :], nt,
                                                           preferred_element_type=f32), NEG_INF)
                m = jnp.max(s_diag, axis=-1, keepdims=True)
                if n_vis > 0:
                    s_vis = lax.dot_general(qh, k_ref[h, :n_vis, :], nt, preferred_element_type=f32)
                    m = jnp.maximum(m, jnp.max(s_vis, axis=-1, keepdims=True))
                e_diag = jnp.exp(s_diag - m)
                total = jnp.sum(e_diag, axis=-1, keepdims=True)
                pv = jnp.dot(e_diag.astype(bf16), v_ref[h, n_vis:n_vis + TQ, :], preferred_element_type=f32)
                if n_vis > 0:
                    e_vis = jnp.exp(s_vis - m)
                    total = total + jnp.sum(e_vis, axis=-1, keepdims=True)
                    pv = pv + jnp.dot(e_vis.astype(bf16), v_ref[h, :n_vis, :], preferred_element_type=f32)
                outs.append(pv / total)
            o_ref[0] = jnp.concatenate(outs, axis=-1)


def mla_prompt_attention(cq, w_uq, g_q, pos, ckv, kr, w_uk, w_uv):
    f32, bf16 = jnp.float32, jnp.bfloat16
    B, S, _ = cq.shape
    H, DQ, DP = MLA_H, MLA_NOPE + MLA_ROPE, V7X_LANES
    HP, TQ = MLA_HEADS_PER_STEP, min(MLA_TQ, S)
    half = MLA_ROPE // 2
    assert S % TQ == 0 and H % HP == 0 and HP * MLA_V == V7X_LANES and DQ <= DP
    w = jnp.transpose(w_uq.reshape(Q_LORA, H, DQ), (1, 0, 2))
    zw = lambda n: jnp.zeros((H, Q_LORA, n), f32)
    w_left = jnp.concatenate([zw(MLA_NOPE), w[..., MLA_NOPE + half:], zw(half)], axis=-1)
    w_right = jnp.concatenate([zw(MLA_NOPE + half), w[..., MLA_NOPE:MLA_NOPE + half]], axis=-1)
    pad_w = lambda a: jnp.pad(a, ((0, 0), (0, 0), (0, DP - DQ)))
    wuq_p = jnp.concatenate([pad_w(w), pad_w(w_left), pad_w(w_right)], axis=-1).astype(bf16)
    inv = ROPE_THETA ** (-jnp.arange(half, dtype=f32) / half)
    ang = pos.astype(f32)[:, None] * inv[None, :]
    cos, sin, zero = jnp.cos(ang), jnp.sin(ang), jnp.zeros((S, half), f32)
    g_nope, g_lo, g_hi = g_q[:MLA_NOPE], g_q[MLA_NOPE:MLA_NOPE + half], g_q[MLA_NOPE + half:]
    tail = jnp.zeros((S, DP - DQ), f32)
    lead0 = jnp.zeros((S, MLA_NOPE), f32)
    rope_a = jnp.concatenate([jnp.broadcast_to(g_nope, (S, MLA_NOPE)), cos * g_lo, cos * g_hi, tail], axis=1)
    rope_m = jnp.concatenate([lead0, -sin * g_hi, zero, tail], axis=1)
    rope_p = jnp.concatenate([lead0, zero, sin * g_lo, tail], axis=1)
    row_tab = pl.BlockSpec((TQ, DP), lambda b, h, i: (i, 0))
    return pl.pallas_call(
        _mla_prompt_kernel,
        grid=(B, H // HP, S // TQ),
        in_specs=[
            pl.BlockSpec((1, TQ, Q_LORA), lambda b, h, i: (b, i, 0)),
            pl.BlockSpec((HP, Q_LORA, 3 * DP), lambda b, h, i: (h, 0, 0)),
            row_tab, row_tab, row_tab,
            pl.BlockSpec((1, S, KV_LORA), lambda b, h, i: (b, 0, 0)),
            pl.BlockSpec((1, S, MLA_ROPE), lambda b, h, i: (b, 0, 0)),
            pl.BlockSpec((HP, KV_LORA, MLA_NOPE), lambda b, h, i: (h, 0, 0)),
            pl.BlockSpec((HP, KV_LORA, MLA_V), lambda b, h, i: (h, 0, 0)),
        ],
        out_specs=pl.BlockSpec((1, TQ, HP * MLA_V), lambda b, h, i: (b, i, h)),
        out_shape=jax.ShapeDtypeStruct((B, S, H * MLA_V), jnp.float32),
        scratch_shapes=[pltpu.VMEM((HP, TQ, DP), bf16), pltpu.VMEM((HP, S, DP), bf16),
                        pltpu.VMEM((HP, S, MLA_V), bf16)],
        compiler_params=pltpu.CompilerParams(
            dimension_semantics=("arbitrary", "arbitrary", "arbitrary"), vmem_limit_bytes=V7X_VMEM_LIMIT_BYTES),
        name="mla_prompt",
    )(cq, wuq_p, rope_a, rope_m, rope_p, ckv, kr, w_uk, w_uv)


def _split3(x):
    bf16, f32 = jnp.bfloat16, jnp.float32
    hi = x.astype(bf16)
    r = x - hi.astype(f32)
    mid = r.astype(bf16)
    return hi, mid, (r - mid.astype(f32)).astype(bf16)


def _mlstm_kernel(u_ref, v_ref, og_ref, cw_ref, cb_ref, wq_ref, wk_ref, g_ref, ir_ref, fr_ref, it_ref, ft_ref,
                  tri_ref, h_ref, c_ref, n_ref, m_ref):
    f32, bf16 = jnp.float32, jnp.bfloat16
    H, NC, L = ir_ref.shape[1:]
    DH = ML_DH
    PRE = 8
    tri = tri_ref[...]
    causal = (tri > 0.5)[None]

    def bmm(a, b, ca, cb):
        return lax.dot_general(a, b, (((ca,), (cb,)), ((0,), (0,))), preferred_element_type=f32)

    def split_heads(x):
        return jnp.stack([x[:, h * DH:(h + 1) * DH] for h in range(H)], axis=0)

    def conv_act(c):
        cur = u_ref[0, c * L:(c + 1) * L, :]
        prev = jnp.zeros((PRE, cur.shape[1]), f32) if c == 0 else u_ref[0, c * L - PRE:c * L, :]
        full = jnp.concatenate([prev, cur], axis=0)
        acc = cb_ref[...]
        for j in range(CONV_K):
            lo = PRE - (CONV_K - 1) + j
            acc = acc + full[lo:lo + L, :] * cw_ref[j:j + 1, :]
        return acc * jax.nn.sigmoid(acc)

    b_rows = jnp.stack([sum(lax.dot_general(p, tri, (((1,), (1,)), ((), ())), preferred_element_type=f32)
                            for p in _split3(fr_ref[0, h])) for h in range(H)], axis=0)
    b_cols = jnp.stack([sum(jnp.dot(tri, p, preferred_element_type=f32) for p in _split3(ft_ref[0, h]))
                        for h in range(H)], axis=0)
    i_rows, i_cols = ir_ref[0], it_ref[0]
    C = jnp.zeros((H, DH, DH), f32)
    n = jnp.zeros((H, 1, DH), f32)
    m = jnp.zeros((H, 1, 1), f32)
    for c in range(NC):
        rows = slice(c * L, (c + 1) * L)
        ch = split_heads(conv_act(c)).astype(bf16)
        q = bmm(ch, wq_ref[...], 2, 1).astype(bf16)
        k = (bmm(ch, wk_ref[...], 2, 1) * (DH ** -0.5)).astype(bf16)
        v = split_heads(v_ref[0, rows, :])
        b_row, i_row = b_rows[:, c:c + 1, :], i_rows[:, c:c + 1, :]
        b_col, i_col = b_cols[:, :, c:c + 1], i_cols[:, :, c:c + 1]
        g = b_col + m
        dmat = jnp.where(causal, b_col - b_row + i_row, -jnp.inf)
        mt = jnp.maximum(g, jnp.max(dmat, axis=-1, keepdims=True))
        inter = jnp.exp(g - mt)
        sqk = bmm(q, k, 2, 2) * jnp.exp(dmat - mt)
        num = inter * bmm(q, C.astype(bf16), 2, 2) + bmm(sqk.astype(bf16), v.astype(bf16), 2, 1)
        nq = jnp.sum(q.astype(f32) * n.astype(bf16).astype(f32), axis=-1, keepdims=True)
        den = inter * nq + jnp.sum(sqk, axis=-1, keepdims=True)
        hs = num / jnp.maximum(jnp.abs(den), jnp.exp(-mt))
        hs = hs * lax.rsqrt(jnp.mean(hs * hs, axis=-1, keepdims=True) + EPS)
        gate = jax.nn.sigmoid(og_ref[0, rows, :])
        for h in range(H):
            lanes = slice(h * DH, (h + 1) * DH)
            h_ref[0, rows, lanes] = hs[h] * g_ref[:, lanes] * gate[:, lanes]
        b_end = b_row[:, :, L - 1:L]
        w_row = b_end - b_row + i_row
        m_new = jnp.maximum(b_end + m, jnp.max(w_row, axis=-1, keepdims=True))
        decay = jnp.exp(b_end + m - m_new)
        w_col = jnp.exp(b_end - b_col + i_col - m_new)
        C = decay * C + bmm((w_col * v).astype(bf16), k, 1, 1)
        n = decay * n + bmm(jnp.exp(w_row - m_new).astype(bf16), k, 2, 1)
        m = m_new
    c_ref[0] = C
    n_ref[0] = n
    m_ref[0] = jnp.broadcast_to(m, (H, 1, V7X_LANES))


def mlstm_prompt(u, v, o_pre, conv_w, conv_b, wq, wk, out_g, i_pre, logf):
    f32, bf16 = jnp.float32, jnp.bfloat16
    B, S, W = u.shape
    H, L = ML_H, MLSTM_CHUNK
    assert S % L == 0 and W == H * ML_DH and L % 8 == 0 and CONV_K - 1 <= 8
    const = lambda shape: pl.BlockSpec(shape, lambda b: (0,) * len(shape))
    NC = S // L
    rows = lambda a: jnp.transpose(a, (0, 2, 1)).reshape(B, H, NC, L)
    cols = lambda a: jnp.transpose(rows(a), (0, 1, 3, 2))
    tri = jnp.tril(jnp.ones((L, L), jnp.bfloat16))
    seq = pl.BlockSpec((1, S, W), lambda b: (b, 0, 0))
    gate_r = pl.BlockSpec((1, H, NC, L), lambda b: (b, 0, 0, 0))
    gate_c = pl.BlockSpec((1, H, L, NC), lambda b: (b, 0, 0, 0))
    hs, C, n, m = pl.pallas_call(
        _mlstm_kernel,
        grid=(B,),
        in_specs=[seq, seq, seq, const((CONV_K, W)), const((1, W)), const((H, ML_DH, ML_DH)),
                  const((H, ML_DH, ML_DH)), const((1, W)), gate_r, gate_r, gate_c, gate_c, const((L, L))],
        out_specs=[seq, pl.BlockSpec((1, H, ML_DH, ML_DH), lambda b: (b, 0, 0, 0)),
                   pl.BlockSpec((1, H, 1, ML_DH), lambda b: (b, 0, 0, 0)),
                   pl.BlockSpec((1, H, 1, V7X_LANES), lambda b: (b, 0, 0, 0))],
        out_shape=[jax.ShapeDtypeStruct((B, S, W), f32), jax.ShapeDtypeStruct((B, H, ML_DH, ML_DH), f32),
                   jax.ShapeDtypeStruct((B, H, 1, ML_DH), f32), jax.ShapeDtypeStruct((B, H, 1, V7X_LANES), f32)],
        compiler_params=pltpu.CompilerParams(dimension_semantics=("arbitrary",),
                                             vmem_limit_bytes=V7X_VMEM_LIMIT_BYTES),
        name="mlstm_prompt",
    )(u, v, o_pre, conv_w, conv_b.reshape(1, W), wq.astype(bf16), wk.astype(bf16), out_g.reshape(1, W),
      rows(i_pre), rows(logf), cols(i_pre), cols(logf), tri)
    return hs, C, n[:, :, 0], m[:, :, 0, 0]


def _odd_proj_kernel(x_ref, g_ref, w_ref, gcq_ref, gckv_ref, ra_ref, rm_ref, rp_ref,
                     cq_ref, ckv_ref, ckv16_ref, kr_ref, kr16_ref):
    f32, bf16 = jnp.float32, jnp.bfloat16
    x = x_ref[0]
    xn = (x * lax.rsqrt(jnp.mean(x * x, axis=-1, keepdims=True) + EPS) * g_ref[...]).astype(bf16)
    y = jnp.dot(xn, w_ref[...], preferred_element_type=f32)
    norm = lambda a: a * lax.rsqrt(jnp.mean(a * a, axis=-1, keepdims=True) + EPS)
    cq_ref[0] = (norm(y[:, :Q_LORA]) * gcq_ref[...]).astype(bf16)
    ckv = norm(y[:, Q_LORA:Q_LORA + KV_LORA]) * gckv_ref[...]
    ckv_ref[0] = ckv
    ckv16_ref[0] = ckv.astype(bf16)
    base = Q_LORA + KV_LORA
    kr, kr_l, kr_r = (y[:, base + j * V7X_LANES:base + (j + 1) * V7X_LANES] for j in range(3))
    ms = jnp.sum(kr * kr, axis=-1, keepdims=True) * (1.0 / MLA_ROPE)
    roped = (kr * ra_ref[...] + kr_l * rm_ref[...] + kr_r * rp_ref[...]) * lax.rsqrt(ms + EPS)
    kr_ref[0] = roped[:, :MLA_ROPE]
    kr16_ref[0] = roped[:, :MLA_ROPE].astype(bf16)


def odd_in_proj_prompt(y, g_mix, w_in, g_cq, g_ckv, g_kr, pos):
    f32, bf16 = jnp.float32, jnp.bfloat16
    B, S, D = y.shape
    tm = min(EV_PROJ_ROWS, S)
    half, lanes = MLA_ROPE // 2, V7X_LANES
    assert S % tm == 0 and Q_LORA % lanes == 0 and KV_LORA % lanes == 0 and MLA_ROPE <= lanes
    w_cq, w_ckv, w_kr = split_cols(w_in, OD_SPLITS)
    zw = lambda n: jnp.zeros((D, n), f32)
    blocks = [w_kr, jnp.concatenate([w_kr[:, half:], zw(half)], axis=1),
              jnp.concatenate([zw(half), w_kr[:, :half]], axis=1)]
    w = jnp.concatenate([w_cq, w_ckv] + [jnp.pad(b, ((0, 0), (0, lanes - MLA_ROPE))) for b in blocks],
                        axis=1).astype(bf16)
    n_cols = w.shape[1]
    inv = ROPE_THETA ** (-jnp.arange(half, dtype=f32) / half)
    ang = pos.astype(f32)[:, None] * inv[None, :]
    cos, sin, zero = jnp.cos(ang), jnp.sin(ang), jnp.zeros((S, half), f32)
    tail = jnp.zeros((S, lanes - MLA_ROPE), f32)
    g_lo, g_hi = g_kr[:half], g_kr[half:]
    rope_a = jnp.concatenate([cos * g_lo, cos * g_hi, tail], axis=1)
    rope_m = jnp.concatenate([-sin * g_hi, zero, tail], axis=1)
    rope_p = jnp.concatenate([zero, sin * g_lo, tail], axis=1)
    const = lambda shape: pl.BlockSpec(shape, lambda b, i: (0,) * len(shape), pipeline_mode=pl.Buffered(1))
    rows = lambda width: pl.BlockSpec((1, tm, width), lambda b, i: (b, i, 0))
    tab = pl.BlockSpec((tm, lanes), lambda b, i: (i, 0))
    shape = lambda width, dt: jax.ShapeDtypeStruct((B, S, width), dt)
    return pl.pallas_call(
        _odd_proj_kernel,
        grid=(B, S // tm),
        in_specs=[rows(D), const((1, D)), const((D, n_cols)), const((1, Q_LORA)), const((1, KV_LORA)), tab, tab, tab],
        out_specs=[rows(Q_LORA), rows(KV_LORA), rows(KV_LORA), rows(MLA_ROPE), rows(MLA_ROPE)],
        out_shape=[shape(Q_LORA, bf16), shape(KV_LORA, f32), shape(KV_LORA, bf16), shape(MLA_ROPE, f32),
                   shape(MLA_ROPE, bf16)],
        compiler_params=pltpu.CompilerParams(
            dimension_semantics=("arbitrary", "arbitrary"), vmem_limit_bytes=V7X_VMEM_LIMIT_BYTES),
        name="odd_in_proj",
    )(y, g_mix.reshape(1, D), w, g_cq.reshape(1, Q_LORA), g_ckv.reshape(1, KV_LORA), rope_a, rope_m, rope_p)


def _out_proj_kernel(*refs):
    n = (len(refs) - 2) // 2
    y_ref, part_refs, w_refs, o_ref = refs[0], refs[1:1 + n], refs[1 + n:1 + 2 * n], refs[-1]
    acc = y_ref[0]
    for p_ref, w_ref in zip(part_refs, w_refs):
        acc = acc + jnp.dot(p_ref[0].astype(jnp.bfloat16), w_ref[...], preferred_element_type=jnp.float32)
    o_ref[0] = acc


def out_proj_residual(y, parts, w):
    B, S, D = y.shape
    tm = min(EV_PROJ_ROWS, S)
    assert S % tm == 0 and sum(p.shape[-1] for p in parts) == w.shape[0]
    offs = np.cumsum([0] + [p.shape[-1] for p in parts])
    ws = [w[offs[i]:offs[i + 1]].astype(jnp.bfloat16) for i in range(len(parts))]
    rows = lambda width: pl.BlockSpec((1, tm, width), lambda b, i: (b, i, 0))
    const = lambda shape: pl.BlockSpec(shape, lambda b, i: (0,) * len(shape), pipeline_mode=pl.Buffered(1))
    return pl.pallas_call(
        _out_proj_kernel,
        grid=(B, S // tm),
        in_specs=[rows(D)] + [rows(p.shape[-1]) for p in parts] + [const(wi.shape) for wi in ws],
        out_specs=rows(D),
        out_shape=jax.ShapeDtypeStruct((B, S, D), jnp.float32),
        compiler_params=pltpu.CompilerParams(
            dimension_semantics=("arbitrary", "arbitrary"), vmem_limit_bytes=V7X_VMEM_LIMIT_BYTES),
        name="out_proj",
    )(y, *parts, *ws)


MEM_ROWS = 512


def _mem_attn_kernel(x_ref, g_ref, wq_ref, gq_ref, k_ref, v_ref, wo_ref, o_ref):
    f32, bf16 = jnp.float32, jnp.bfloat16
    x = x_ref[0]
    xn = (x * lax.rsqrt(jnp.mean(x * x, axis=-1, keepdims=True) + EPS) * g_ref[...]).astype(bf16)
    q = jnp.dot(xn, wq_ref[...], preferred_element_type=f32)
    k, v = k_ref[0], v_ref[0]
    outs = []
    for h in range(XM_H):
        cols = slice(h * XM_DH, (h + 1) * XM_DH)
        qh = q[:, cols]
        qh = (qh * lax.rsqrt(jnp.mean(qh * qh, axis=-1, keepdims=True) + EPS) * gq_ref[...]).astype(bf16)
        s = lax.dot_general(qh, k[:, cols], (((1,), (1,)), ((), ())), preferred_element_type=f32) * (XM_DH ** -0.5)
        e = jnp.exp(s - jnp.max(s, axis=-1, keepdims=True))
        p = (e / jnp.sum(e, axis=-1, keepdims=True)).astype(bf16)
        outs.append(jnp.dot(p, v[:, cols], preferred_element_type=f32))
    o = jnp.concatenate(outs, axis=-1).astype(bf16)
    o_ref[0] = x + jnp.dot(o, wo_ref[...], preferred_element_type=f32)


def mem_attention(y, g, wq, gq, k, v, wo):
    bf16 = jnp.bfloat16
    B, S, D = y.shape
    M = k.shape[1]
    ts = min(MEM_ROWS, S)
    assert S % ts == 0
    const = lambda shape: pl.BlockSpec(shape, lambda b, i: (0,) * len(shape), pipeline_mode=pl.Buffered(1))
    kv_spec = pl.BlockSpec((1, M, XM_W), lambda b, i: (b, 0, 0))
    return pl.pallas_call(
        _mem_attn_kernel,
        grid=(B, S // ts),
        in_specs=[pl.BlockSpec((1, ts, D), lambda b, i: (b, i, 0)), const((1, D)), const((D, XM_W)),
                  const((1, XM_DH)), kv_spec, kv_spec, const((XM_W, D))],
        out_specs=pl.BlockSpec((1, ts, D), lambda b, i: (b, i, 0)),
        out_shape=jax.ShapeDtypeStruct((B, S, D), jnp.float32),
        compiler_params=pltpu.CompilerParams(
            dimension_semantics=("arbitrary", "arbitrary"), vmem_limit_bytes=V7X_VMEM_LIMIT_BYTES),
        name="mem_attn",
    )(y, g.reshape(1, D), wq.astype(bf16), gq.reshape(1, XM_DH), k.reshape(B, M, XM_W).astype(bf16),
      v.reshape(B, M, XM_W).astype(bf16), wo.astype(bf16))


DECODE_XPOSE_UNROLL = 8


def _paged_copy(pool_ref, page, buf_ref, slot, p, sem_ref):
    return pltpu.make_async_copy(pool_ref.at[page], buf_ref.at[slot, p], sem_ref.at[slot])


def _paged_pipeline(pt_ref, pools, bufs, sems):
    b = pl.program_id(0)
    n_pages = pt_ref.shape[1]

    def start(seq, slot):
        def body(p, carry):
            for pool, buf, sem in zip(pools, bufs, sems):
                _paged_copy(pool, pt_ref[seq, p], buf, slot, p, sem).start()
            return carry
        lax.fori_loop(0, n_pages, body, 0)

    @pl.when(b == 0)
    def _():
        start(0, 0)

    @pl.when(b + 1 < pl.num_programs(0))
    def _():
        start(b + 1, (b + 1) % 2)

    slot = b % 2

    def wait_body(p, carry):
        for pool, buf, sem in zip(pools, bufs, sems):
            _paged_copy(pool, 0, buf, slot, p, sem).wait()
        return carry
    lax.fori_loop(0, n_pages, wait_body, 0)
    return slot


def _softmax_rows(s, valid):
    e = jnp.exp(s - jnp.max(s, axis=-1, keepdims=True))
    return e / jnp.sum(e, axis=-1, keepdims=True) * valid


def _nsa_decode_cmp_kernel(pt_ref, qbd_ref, wk_ref, wv_ref, pek_ref, pev_ref, w2k_ref, w2v_ref, gk_ref,
                           bc_ref, mw_ref, kpool_ref, vpool_ref, ocmp_ref, sel_ref,
                           kbuf, vbuf, xrm_ref, ksem, vsem, *, n_cmp, n_blocks, n_select):
    f32, bf16 = jnp.float32, jnp.bfloat16
    slot = _paged_pipeline(pt_ref, (kpool_ref, vpool_ref), (kbuf, vbuf), (ksem, vsem))
    n_pages = pt_ref.shape[1]
    n_chunk = n_pages * (PAGE_SIZE // CMP_STRIDE)
    G, DH, HID = NSA_KV, NSA_DH, CMP_HID

    def summaries(buf, w_ref, pe_ref, w2_ref):
        def xpose(i, carry):
            for u in range(DECODE_XPOSE_UNROLL):
                p = i * DECODE_XPOSE_UNROLL + u
                xrm_ref[pl.ds(pl.multiple_of(p * PAGE_SIZE, PAGE_SIZE), PAGE_SIZE), :] = buf[slot, p].T
            return carry
        lax.fori_loop(0, n_pages // DECODE_XPOSE_UNROLL, xpose, 0)
        acc = jnp.zeros((n_chunk, 2 * G * HID), f32)
        for j in range(0, CMP_STRIDE, 2):
            rows = jnp.concatenate([xrm_ref[pl.ds(j + u, n_chunk, stride=CMP_STRIDE), :].astype(bf16)
                                    for u in range(2)], axis=1)
            acc = acc + jnp.dot(rows, w_ref[j // 2], preferred_element_type=f32)
        lo, hi = acc[:, :G * HID], acc[:, G * HID:]
        pre = lo + pltpu.roll(hi, n_chunk - 1, 0) + pe_ref[...]
        hid = pre * jax.nn.sigmoid(pre)
        return jnp.dot(hid.astype(bf16), w2_ref[...], preferred_element_type=f32)

    k_sum = summaries(kbuf, wk_ref, pek_ref, w2k_ref)
    lane = lax.broadcasted_iota(jnp.int32, (1, G * DH), 1)
    sq = k_sum * k_sum
    s_all = jnp.sum(sq, axis=-1, keepdims=True)
    s_g0 = jnp.sum(jnp.where(lane < DH, sq, 0.0), axis=-1, keepdims=True)
    ms = jnp.where(lane < DH, s_g0, s_all - s_g0) * (1.0 / DH)
    k_cmp = (k_sum * lax.rsqrt(ms + EPS) * gk_ref[...]).astype(bf16)
    v_cmp = summaries(vbuf, wv_ref, pev_ref, w2v_ref).astype(bf16)

    qbd = qbd_ref[0]
    H = qbd.shape[0]
    c_row = lax.broadcasted_iota(jnp.int32, (1, n_chunk), 1)
    valid = (c_row < n_cmp).astype(f32)
    s = lax.dot_general(qbd, k_cmp, (((1,), (1,)), ((), ())), preferred_element_type=f32) + bc_ref[...]
    p = _softmax_rows(jnp.where(valid > 0.5, s, NEG_INF), valid)
    ocmp_ref[0] = jnp.dot(p.astype(bf16), v_cmp, preferred_element_type=f32)

    R = H // G
    head = lax.broadcasted_iota(jnp.int32, (H, 1), 0)
    imp = jnp.where(head < R, jnp.sum(p[:R], axis=0, keepdims=True), jnp.sum(p[R:], axis=0, keepdims=True))
    mw = mw_ref[...]
    hi_p = imp.astype(bf16)
    r1 = imp - hi_p.astype(f32)
    mid_p = r1.astype(bf16)
    lo_p = (r1 - mid_p.astype(f32)).astype(bf16)
    p_slc = (jnp.dot(hi_p, mw, preferred_element_type=f32) + jnp.dot(mid_p, mw, preferred_element_type=f32)
             + jnp.dot(lo_p, mw, preferred_element_type=f32))
    nsp = mw_ref.shape[1]
    tb = n_blocks - 1
    jb = lax.broadcasted_iota(jnp.int32, (1, nsp), 1)
    forced = (jb == 0) | (jb == tb) | (jb == tb - 1)
    score = jnp.where(jb <= tb, p_slc + FORCE_SCORE * forced.astype(f32), -1.0)
    j_col = lax.broadcasted_iota(jnp.int32, (nsp, 1), 0)
    sels = []
    for g in range(G):
        row = score[g * R:g * R + 1, :]
        col = jnp.broadcast_to(row, (nsp, nsp)).T
        beats = (col > row) | ((col == row) & (j_col < jb))
        rank = jnp.sum(beats.astype(f32), axis=0, keepdims=True)
        sels.append(jnp.broadcast_to((rank < n_select).astype(f32), (R, nsp)))
    sel_ref[0] = jnp.concatenate(sels, axis=0)


def _nsa_decode_attn_kernel(pt_ref, qbd_ref, sel_ref, ocmp_ref, gate_ref, new_ref, e_ref, bs_ref, bw_ref, b0_ref,
                            kw_ref, vw_ref, kpool_ref, vpool_ref, o_ref, kbuf, vbuf, s_ref, ksem, vsem):
    f32, bf16 = jnp.float32, jnp.bfloat16
    slot = _paged_pipeline(pt_ref, (kpool_ref, vpool_ref), (kbuf, vbuf), (ksem, vsem))
    n_pages = pt_ref.shape[1]
    qbd = qbd_ref[0]
    qf = qbd.astype(f32)
    new = new_ref[0]
    b0 = b0_ref[...]

    def new_score(k_row):
        return jnp.sum(qf * k_row.astype(bf16).astype(f32), axis=-1, keepdims=True) + b0

    for p in range(n_pages):
        s_ref[:, p * PAGE_SIZE:(p + 1) * PAGE_SIZE] = jnp.dot(qbd, kbuf[slot, p].astype(bf16),
                                                              preferred_element_type=f32)
    n_blk_past = e_ref.shape[0]
    sel_tok = jnp.dot(sel_ref[0][:, :n_blk_past].astype(bf16), e_ref[...], preferred_element_type=f32)
    s_past = jnp.where(sel_tok > 0.5, s_ref[...] + bs_ref[...], NEG_INF)
    s_new = new_score(new[0:1])
    m = jnp.maximum(jnp.max(s_past, axis=-1, keepdims=True), s_new)
    e_new = jnp.exp(s_new - m)
    s_ref[...] = jnp.exp(s_past - m)
    denom = jnp.sum(s_ref[...], axis=-1, keepdims=True) + e_new
    acc = e_new.astype(bf16).astype(f32) * new[1:2].astype(bf16).astype(f32)
    for p in range(n_pages):
        pe = s_ref[:, p * PAGE_SIZE:(p + 1) * PAGE_SIZE].astype(bf16)
        acc = acc + lax.dot_general(pe, vbuf[slot, p].astype(bf16), (((1,), (1,)), ((), ())),
                                    preferred_element_type=f32)
    o_slc = acc / denom

    s_w = jnp.dot(qbd, kw_ref[0].astype(bf16), preferred_element_type=f32) + bw_ref[...]
    s_wn = new_score(new[2:3])
    m = jnp.maximum(jnp.max(s_w, axis=-1, keepdims=True), s_wn)
    e_w, e_wn = jnp.exp(s_w - m), jnp.exp(s_wn - m)
    denom = jnp.sum(e_w, axis=-1, keepdims=True) + e_wn
    acc = (lax.dot_general(e_w.astype(bf16), vw_ref[0].astype(bf16), (((1,), (1,)), ((), ())),
                           preferred_element_type=f32)
           + e_wn.astype(bf16).astype(f32) * new[3:4].astype(bf16).astype(f32))
    o_win = acc / denom

    gates = jax.nn.sigmoid(gate_ref[0])
    o_ref[0] = gates[:, 0:1] * ocmp_ref[0] + gates[:, 1:2] * o_slc + gates[:, 2:3] * o_win


def _block_diag_heads(x):
    B, H, DH = x.shape
    g_of_h = jnp.arange(H) // (H // NSA_KV)
    onehot = (g_of_h[:, None] == jnp.arange(NSA_KV)[None, :]).astype(x.dtype)
    return (x[:, :, None, :] * onehot[None, :, :, None]).reshape(B, H, NSA_KV * DH)


def nsa_decode(qn, ks, vs, kw, vw, g_pre, gate_b, rel_bias, page_table, cmp_k_pool, cmp_v_pool,
               slc_k_pool, slc_v_pool, win_k, win_v, pe_k, w1_k, w2_k, pe_v, w1_v, w2_v, gk_cmp):
    f32, bf16 = jnp.float32, jnp.bfloat16
    B, n_pages = page_table.shape
    G, R, DH, H = NSA_KV, NSA_R, NSA_DH, NSA_H
    GD = G * DH
    past = n_pages * PAGE_SIZE
    n_chunk = past // CMP_STRIDE
    n_cmp = (past + 1 - CMP_LEN) // CMP_STRIDE + 1
    n_blocks = -(-(past + 1) // SLC_BLOCK)
    n_blk_past = past // SLC_BLOCK
    nsp = -(-n_blocks // V7X_LANES) * V7X_LANES
    n_win = win_k.shape[1]
    assert GD == V7X_LANES and PAGE_SIZE == V7X_LANES and n_blocks >= N_SELECT and n_win == WINDOW

    def pool_view(pool):
        return jnp.transpose(pool, (0, 2, 3, 1)).reshape(pool.shape[0], GD, PAGE_SIZE)
    kc_pool, vc_pool, ks_pool, vs_pool = (pool_view(a) for a in (cmp_k_pool, cmp_v_pool, slc_k_pool, slc_v_pool))
    kw_t = jnp.transpose(win_k, (0, 2, 3, 1)).reshape(B, GD, n_win)
    vw_t = jnp.transpose(win_v, (0, 2, 3, 1)).reshape(B, GD, n_win)

    qbd = _block_diag_heads((qn[:, 0] * NSA_SCALE)).astype(bf16)
    eye = jnp.eye(G, dtype=f32)

    def chunk_weights(w1):
        def bd(w):
            return jnp.einsum('jdh,gk->jgdkh', w, eye).reshape(CMP_STRIDE, GD, G * CMP_HID)
        w = jnp.concatenate([bd(w1[:CMP_STRIDE]), bd(w1[CMP_STRIDE:])], axis=-1)
        return w.reshape(CMP_STRIDE // 2, 2 * GD, 2 * G * CMP_HID).astype(bf16)

    def pe_term(pe, w1):
        return jnp.tile(jnp.einsum('jd,jdh->h', pe, w1), G).reshape(1, G * CMP_HID)

    def w2_bd(w2):
        return jnp.einsum('hd,gk->ghkd', w2, eye).reshape(G * CMP_HID, GD).astype(bf16)

    c_i = jnp.arange(n_chunk)
    bias_c = _bias_lookup(rel_bias, past - (c_i * CMP_STRIDE + CMP_LEN - 1)).T
    j_i = jnp.arange(nsp)[None, :]
    mw = sum(w * ((c_i[:, None] == SLC_RATIO * j_i + k - 1) & (j_i < n_blocks))
             for k, w in enumerate(SLC_OVERLAP_W)).astype(bf16)
    tok = jnp.arange(past)
    expand = (tok[None, :] // SLC_BLOCK == jnp.arange(n_blk_past)[:, None]).astype(bf16)
    bias_s = _bias_lookup(rel_bias, past - tok).T
    bias_w = _bias_lookup(rel_bias, n_win - jnp.arange(n_win)).T
    bias_0 = _bias_lookup(rel_bias, jnp.zeros((1,), jnp.int32)).T
    new_rows = jnp.stack([a.reshape(B, GD) for a in (ks, vs, kw, vw)], axis=1)
    gate_in = g_pre.reshape(B, H, 3) + gate_b

    const = lambda shape: pl.BlockSpec(shape, lambda b, pt: (0,) * len(shape))
    per_seq = lambda shape: pl.BlockSpec((1,) + shape, lambda b, pt: (b,) + (0,) * len(shape))
    any_spec = pl.BlockSpec(memory_space=pl.ANY)
    page_buf = pltpu.VMEM((2, n_pages, GD, PAGE_SIZE), f32)
    params = pltpu.CompilerParams(dimension_semantics=("arbitrary",), vmem_limit_bytes=V7X_VMEM_LIMIT_BYTES)

    o_cmp, sel = pl.pallas_call(
        functools.partial(_nsa_decode_cmp_kernel, n_cmp=n_cmp, n_blocks=n_blocks, n_select=min(N_SELECT, n_blocks)),
        grid_spec=pltpu.PrefetchScalarGridSpec(
            num_scalar_prefetch=1, grid=(B,),
            in_specs=[per_seq((H, GD)),
                      const((CMP_STRIDE // 2, 2 * GD, 2 * G * CMP_HID)),
                      const((CMP_STRIDE // 2, 2 * GD, 2 * G * CMP_HID)),
                      const((1, G * CMP_HID)), const((1, G * CMP_HID)),
                      const((G * CMP_HID, GD)), const((G * CMP_HID, GD)), const((1, GD)),
                      const((H, n_chunk)), const((n_chunk, nsp)), any_spec, any_spec],
            out_specs=[per_seq((H, GD)), per_seq((H, nsp))],
            scratch_shapes=[page_buf, page_buf, pltpu.VMEM((past, GD), f32),
                            pltpu.SemaphoreType.DMA((2,)), pltpu.SemaphoreType.DMA((2,))]),
        out_shape=[jax.ShapeDtypeStruct((B, H, GD), f32), jax.ShapeDtypeStruct((B, H, nsp), f32)],
        compiler_params=params, name="nsa_decode_cmp",
    )(page_table, qbd, chunk_weights(w1_k), chunk_weights(w1_v), pe_term(pe_k, w1_k), pe_term(pe_v, w1_v),
      w2_bd(w2_k), w2_bd(w2_v), jnp.tile(gk_cmp, G).reshape(1, GD), bias_c, mw, kc_pool, vc_pool)

    out = pl.pallas_call(
        _nsa_decode_attn_kernel,
        grid_spec=pltpu.PrefetchScalarGridSpec(
            num_scalar_prefetch=1, grid=(B,),
            in_specs=[per_seq((H, GD)), per_seq((H, nsp)), per_seq((H, GD)), per_seq((H, 3)), per_seq((4, GD)),
                      const((n_blk_past, past)), const((H, past)), const((H, n_win)), const((H, 1)),
                      per_seq((GD, n_win)), per_seq((GD, n_win)), any_spec, any_spec],
            out_specs=per_seq((H, GD)),
            scratch_shapes=[page_buf, page_buf, pltpu.VMEM((H, past), f32),
                            pltpu.SemaphoreType.DMA((2,)), pltpu.SemaphoreType.DMA((2,))]),
        out_shape=jax.ShapeDtypeStruct((B, H, GD), f32),
        compiler_params=params, name="nsa_decode_attn",
    )(page_table, qbd, sel, o_cmp, gate_in, new_rows, expand, bias_s, bias_w, bias_0, kw_t, vw_t, ks_pool, vs_pool)

    out = out.reshape(B, G, R, G, DH)
    h_b = jnp.stack([out[:, g, :, g, :] for g in range(G)], axis=1)
    return h_b.reshape(B, 1, NSA_W)


def _mla_decode_kernel(pt_ref, ql_ref, qr_ref, new_ref, newr_ref, cpool_ref, rpool_ref, o_ref,
                       cbuf, rbuf, cb16, s_ref, csem, rsem):
    f32, bf16 = jnp.float32, jnp.bfloat16
    slot = _paged_pipeline(pt_ref, (cpool_ref, rpool_ref), (cbuf, rbuf), (csem, rsem))
    n_pages = pt_ref.shape[1]
    ql, qr = ql_ref[0], qr_ref[0]
    for p in range(n_pages):
        c16 = cbuf[slot, p].astype(bf16)
        cb16[p] = c16
        s_ref[:, p * PAGE_SIZE:(p + 1) * PAGE_SIZE] = (
            lax.dot_general(ql, c16, (((1,), (1,)), ((), ())), preferred_element_type=f32)
            + jnp.dot(qr, rbuf[slot, p].astype(bf16), preferred_element_type=f32)) * MLA_SCALE
    c_new = new_ref[0].astype(bf16).astype(f32)
    r_new = newr_ref[0].astype(bf16).astype(f32)
    s_new = (jnp.sum(ql.astype(f32) * c_new, axis=-1, keepdims=True)
             + jnp.sum(qr.astype(f32) * r_new, axis=-1, keepdims=True)) * MLA_SCALE
    s_past = s_ref[...]
    m = jnp.maximum(jnp.max(s_past, axis=-1, keepdims=True), s_new)
    e_new = jnp.exp(s_new - m)
    s_ref[...] = jnp.exp(s_past - m)
    denom = jnp.sum(s_ref[...], axis=-1, keepdims=True) + e_new
    inv = 1.0 / denom
    acc = (e_new * inv).astype(bf16).astype(f32) * c_new
    for p in range(n_pages):
        pe = (s_ref[:, p * PAGE_SIZE:(p + 1) * PAGE_SIZE] * inv).astype(bf16)
        acc = acc + jnp.dot(pe, cb16[p], preferred_element_type=f32)
    o_ref[0] = acc


def mla_decode_attention(q_lat, q_rope, ckv_new, kr_new, page_table, ckv_pool, krope_pool):
    f32, bf16 = jnp.float32, jnp.bfloat16
    B, n_pages = page_table.shape
    H = q_lat.shape[1]
    past = n_pages * PAGE_SIZE
    rpool_t = jnp.transpose(krope_pool, (0, 2, 1))
    per_seq = lambda shape: pl.BlockSpec((1,) + shape, lambda b, pt: (b,) + (0,) * len(shape))
    any_spec = pl.BlockSpec(memory_space=pl.ANY)
    return pl.pallas_call(
        _mla_decode_kernel,
        grid_spec=pltpu.PrefetchScalarGridSpec(
            num_scalar_prefetch=1, grid=(B,),
            in_specs=[per_seq((H, KV_LORA)), per_seq((H, MLA_ROPE)), per_seq((1, KV_LORA)), per_seq((1, MLA_ROPE)),
                      any_spec, any_spec],
            out_specs=per_seq((H, KV_LORA)),
            scratch_shapes=[pltpu.VMEM((2, n_pages, PAGE_SIZE, KV_LORA), f32),
                            pltpu.VMEM((2, n_pages, MLA_ROPE, PAGE_SIZE), f32),
                            pltpu.VMEM((n_pages, PAGE_SIZE, KV_LORA), bf16),
                            pltpu.VMEM((H, past), f32),
                            pltpu.SemaphoreType.DMA((2,)), pltpu.SemaphoreType.DMA((2,))]),
        out_shape=jax.ShapeDtypeStruct((B, H, KV_LORA), f32),
        compiler_params=pltpu.CompilerParams(dimension_semantics=("arbitrary",),
                                             vmem_limit_bytes=V7X_VMEM_LIMIT_BYTES),
        name="mla_decode",
    )(page_table, q_lat.astype(bf16), q_rope.astype(bf16), ckv_new.reshape(B, 1, KV_LORA),
      kr_new.reshape(B, 1, MLA_ROPE), ckv_pool, rpool_t)


def split_cols(a, sizes):
    idx = [int(s) for s in np.cumsum(sizes)[:-1]]
    return jnp.split(a, idx, axis=-1)


def rms_norm(x, g):
    xf = x.astype(jnp.float32)
    y = xf * lax.rsqrt(jnp.mean(xf * xf, axis=-1, keepdims=True) + EPS)
    return (y * g.astype(jnp.float32)).astype(x.dtype)


def t5_bucket(dist):
    n = jnp.maximum(dist, 0)
    exact = REL_BUCKETS // 2
    nf = jnp.maximum(n, exact).astype(jnp.float32)
    large = exact + (jnp.log(nf / exact) / math.log(REL_MAX_DIST / exact) * (REL_BUCKETS - exact)).astype(jnp.int32)
    return jnp.where(n < exact, n, jnp.minimum(large, REL_BUCKETS - 1))


def apply_rope(x, pos):
    half = x.shape[-1] // 2
    inv = ROPE_THETA ** (-jnp.arange(half, dtype=jnp.float32) / half)
    ang = pos.astype(jnp.float32)[:, None] * inv[None, :]
    ang = ang.reshape(ang.shape[:1] + (1,) * (x.ndim - 3) + (half,))
    cos, sin = jnp.cos(ang).astype(x.dtype), jnp.sin(ang).astype(x.dtype)
    x1, x2 = x[..., :half], x[..., half:]
    return jnp.concatenate([x1 * cos - x2 * sin, x1 * sin + x2 * cos], axis=-1)


def causal_conv(u, buf, w, b):
    S = u.shape[1]
    full = jnp.concatenate([buf.astype(u.dtype), u], axis=1)
    out = b + sum(full[:, j:j + S] * w[j] for j in range(CONV_K))
    return out, full[:, S:]


def mlstm_chunkwise(q, k, v, i_pre, logf, C0, n0, m0):
    f32 = jnp.float32
    q, k, v, i_pre, logf = (a.astype(f32) for a in (q, k, v, i_pre, logf))
    B, H, S, D = q.shape
    L = MLSTM_CHUNK if S % MLSTM_CHUNK == 0 else S
    NC = S // L

    def chunks(a):
        return jnp.moveaxis(a.reshape((B, H, NC, L) + a.shape[3:]), 2, 0)

    causal = jnp.tril(jnp.ones((L, L), dtype=bool))

    def step(carry, inp):
        C, n, m = carry
        qc, kc, vc, ic, fc = inp
        b = jnp.cumsum(fc, axis=-1)
        g = b + m[..., None]
        dmat = jnp.where(causal, b[..., :, None] - b[..., None, :] + ic[..., None, :], -jnp.inf)
        mt = jnp.maximum(g, jnp.max(dmat, axis=-1))
        inter = jnp.exp(g - mt)
        sqk = jnp.einsum('bhtd,bhsd->bhts', qc, kc) * jnp.exp(dmat - mt[..., None])
        num = inter[..., None] * jnp.einsum('bhvd,bhtd->bhtv', C, qc) + jnp.einsum('bhts,bhsv->bhtv', sqk, vc)
        den = inter * jnp.einsum('bhd,bhtd->bht', n, qc) + jnp.sum(sqk, axis=-1)
        h = num / jnp.maximum(jnp.abs(den), jnp.exp(-mt))[..., None]
        b_end = b[..., -1]
        w_log = b_end[..., None] - b + ic
        m_new = jnp.maximum(b_end + m, jnp.max(w_log, axis=-1))
        decay = jnp.exp(b_end + m - m_new)
        w_in = jnp.exp(w_log - m_new[..., None])
        C_new = decay[..., None, None] * C + jnp.einsum('bhs,bhsv,bhsd->bhvd', w_in, vc, kc)
        n_new = decay[..., None] * n + jnp.einsum('bhs,bhsd->bhd', w_in, kc)
        return (C_new, n_new, m_new), h

    (C1, n1, m1), hs = lax.scan(step, (C0.astype(f32), n0.astype(f32), m0.astype(f32)),
                                tuple(chunks(a) for a in (q, k, v, i_pre, logf)))
    return jnp.moveaxis(hs, 0, 2).reshape(B, H, S, D), C1, n1, m1


def to_chunks(a):
    B, T = a.shape[:2]
    pad = (-T) % CMP_STRIDE
    a = jnp.pad(a, ((0, 0), (0, pad), (0, 0), (0, 0)))
    return a.reshape((B, (T + pad) // CMP_STRIDE, CMP_STRIDE) + a.shape[2:])


def cmp_summaries(chunk_list, T, pe, w1, w2):
    lo = jnp.concatenate([jnp.einsum('bcjgd,jdh->bcgh', r, w1[:CMP_STRIDE]) for r in chunk_list], axis=1)
    hi = jnp.concatenate([jnp.einsum('bcjgd,jdh->bcgh', r, w1[CMP_STRIDE:]) for r in chunk_list], axis=1)
    n_cmp = (T - CMP_LEN) // CMP_STRIDE + 1
    hid = jax.nn.silu(lo[:, :n_cmp] + hi[:, 1:n_cmp + 1] + jnp.einsum('jd,jdh->h', pe, w1))
    return hid @ w2


def even_mixer_prompt(y, g_mix, w_in, w_out, conv_w, conv_b, ml_wq, ml_wk, ml_b_i, ml_b_f, ml_out_g,
                      nsa_gq, nsa_gk_cmp, nsa_gk_slc, nsa_gk_win, pe_k, w1_k, w2_k, pe_v, w1_v, w2_v, gate_b, rel_bias):
    B, S, _ = y.shape
    assert S >= CONV_K - 1
    p = even_in_proj_prompt(y, g_mix, w_in, nsa_gq, nsa_gk_slc, nsa_gk_win)
    h_a, C1, n1, m1 = mlstm_prompt(p['u'], p['v'], p['o'], conv_w, conv_b, ml_wq, ml_wk, ml_out_g,
                                   p['i_pre'] + ml_b_i, jax.nn.log_sigmoid(p['f_pre'] + ml_b_f))
    k_cmp = rms_norm(cmp_summaries([to_chunks(p['kc'])], S, pe_k, w1_k, w2_k), nsa_gk_cmp)
    v_cmp = cmp_summaries([to_chunks(p['vc'])], S, pe_v, w1_v, w2_v)
    h_b = nsa_prompt(p['q5'], k_cmp, v_cmp, p['kst'], p['vst'], p['kwt'], p['vwt'], p['g_pre'], gate_b, rel_bias)
    nb = min(WINDOW, S)
    out = out_proj_residual(y, [h_a, h_b], w_out)
    new = dict(C=C1, n=n1, m=m1, conv=p['u'][:, S - (CONV_K - 1):], cmp_k=p['kc'], cmp_v=p['vc'],
               slc_k=p['ks'], slc_v=p['vs'], win_k=p['kw'][:, S - nb:], win_v=p['vw'][:, S - nb:])
    return out, new


def even_mixer(xn, pos0, past, w_in, w_out, conv_w, conv_b, ml_wq, ml_wk, ml_b_i, ml_b_f, ml_out_g,
               nsa_gq, nsa_gk_cmp, nsa_gk_slc, nsa_gk_win, pe_k, w1_k, w2_k, pe_v, w1_v, w2_v, gate_b, rel_bias):
    B, S, _ = xn.shape
    dt = xn.dtype
    (u, v_m, o_pre, i_pre, f_pre, q, kc, vc, ks, vs, kw, vw, g_pre) = split_cols(xn @ w_in, EV_SPLITS)
    assert S == 1
    conv_buf, C0, n0, m0 = past['conv'], past['C'], past['n'], past['m']
    c, conv_new = causal_conv(u, conv_buf, conv_w, conv_b)
    ch = jax.nn.silu(c).reshape(B, S, ML_H, ML_DH)
    qm = jnp.einsum('bshd,hde->bhse', ch, ml_wq)
    km = jnp.einsum('bshd,hde->bhse', ch, ml_wk) * (ML_DH ** -0.5)
    vm = jnp.transpose(v_m.reshape(B, S, ML_H, ML_DH), (0, 2, 1, 3))
    ig = jnp.transpose(i_pre + ml_b_i, (0, 2, 1))
    lf = jax.nn.log_sigmoid(jnp.transpose(f_pre + ml_b_f, (0, 2, 1)).astype(jnp.float32))
    hm, C1, n1, m1 = mlstm_chunkwise(qm, km, vm, ig, lf, C0, n0, m0)
    hm = rms_norm(jnp.transpose(hm, (0, 2, 1, 3)).astype(dt), ml_out_g)
    h_a = (hm * jax.nn.sigmoid(o_pre).reshape(B, S, ML_H, ML_DH)).reshape(B, S, ML_W)

    q = rms_norm(q.reshape(B, S, NSA_H, NSA_DH), nsa_gq)
    kv_shape = (B, S, NSA_KV, NSA_DH)
    kc, vc, vs, vw = (a.reshape(kv_shape) for a in (kc, vc, vs, vw))
    ks = rms_norm(ks.reshape(kv_shape), nsa_gk_slc)
    kw = rms_norm(kw.reshape(kv_shape), nsa_gk_win)
    e = past['e']
    h_b = nsa_decode(q, ks, vs, kw, vw, g_pre, gate_b, rel_bias, past['page_table'],
                     past['cmp_k'][e], past['cmp_v'][e], past['slc_k'][e], past['slc_v'][e],
                     past['win_k'], past['win_v'], pe_k, w1_k, w2_k, pe_v, w1_v, w2_v, nsa_gk_cmp)
    win_k_new = jnp.concatenate([past['win_k'][:, S:], kw], axis=1)
    win_v_new = jnp.concatenate([past['win_v'][:, S:], vw], axis=1)
    out = jnp.concatenate([h_a, h_b], axis=-1) @ w_out
    new = dict(C=C1.astype(dt), n=n1.astype(dt), m=m1.astype(dt), conv=conv_new,
               cmp_k=kc, cmp_v=vc, slc_k=ks, slc_v=vs, win_k=win_k_new, win_v=win_v_new)
    return out, new


def odd_mixer_prompt(y, g_mix, w_in, g_cq, w_uq, g_q, g_ckv, g_kr, w_uk, w_uv, w_out):
    bf16 = jnp.bfloat16
    tq = jnp.arange(y.shape[1])
    cq16, ckv, ckv16, kr, kr16 = odd_in_proj_prompt(y, g_mix, w_in, g_cq, g_ckv, g_kr, tq)
    o = mla_prompt_attention(cq16, w_uq, g_q, tq, ckv16, kr16, jnp.transpose(w_uk, (1, 0, 2)).astype(bf16),
                             jnp.transpose(w_uv, (1, 0, 2)).astype(bf16))
    return out_proj_residual(y, [o], w_out), dict(ckv=ckv, krope=kr)


def odd_mixer(xn, pos0, past, w_in, g_cq, w_uq, g_q, g_ckv, g_kr, w_uk, w_uv, w_out):
    B, S, _ = xn.shape
    assert S == 1
    cq, ckv, kr = split_cols(xn @ w_in, OD_SPLITS)
    tq = pos0 + jnp.arange(S)
    ckv = rms_norm(ckv, g_ckv)
    kr = apply_rope(rms_norm(kr, g_kr), tq)
    e = past['e']
    q = rms_norm((rms_norm(cq, g_cq) @ w_uq).reshape(B, S, MLA_H, MLA_NOPE + MLA_ROPE), g_q)
    q_nope = q[..., :MLA_NOPE]
    q_rope = apply_rope(q[..., MLA_NOPE:], tq)
    q_lat = jnp.einsum('bqhn,chn->bqhc', q_nope, w_uk)
    o_lat = mla_decode_attention(q_lat[:, 0], q_rope[:, 0], ckv[:, 0], kr[:, 0], past['page_table'],
                                 past['ckv'][e], past['krope'][e])[:, None]
    o = jnp.einsum('bqhc,chv->bqhv', o_lat, w_uv).reshape(B, S, MLA_H * MLA_V)
    return o @ w_out, dict(ckv=ckv, krope=kr)


def mem_kv(mem, g_mem, wk, wv, gk):
    B, M, _ = mem.shape
    m = rms_norm(mem, g_mem)
    k = rms_norm((m @ wk).reshape(B, M, XM_H, XM_DH), gk)
    v = (m @ wv).reshape(B, M, XM_H, XM_DH)
    return k, v


def mem_attend(xn, k, v, wq, gq, wo):
    B, S, _ = xn.shape
    q = rms_norm((xn @ wq).reshape(B, S, XM_H, XM_DH), gq)
    s = jnp.einsum('bshd,bmhd->bhsm', q, k.astype(q.dtype)).astype(jnp.float32) * (XM_DH ** -0.5)
    p = jax.nn.softmax(s, axis=-1)
    return jnp.einsum('bhsm,bmhd->bshd', p.astype(xn.dtype), v.astype(xn.dtype)).reshape(B, S, XM_W) @ wo


def stack_key(lst, name):
    return jnp.stack([d[name] for d in lst])


def kernel(x_prompt, x_sample, mem_prompt,
           state_ml_C, state_ml_n, state_ml_m, state_ml_conv,
           cache_cmp_k, cache_cmp_v, cache_slc_k, cache_slc_v, cache_win_k, cache_win_v,
           cache_mla_ckv, cache_mla_krope, cache_mem_k, cache_mem_v, page_table,
           rel_bias, ffn1_norm, ffn1_wg, ffn1_wu, ffn1_wd, mix_norm,
           xm_norm, xm_mem_norm, xm_wq, xm_wk, xm_wv, xm_wo, xm_gq, xm_gk,
           ffn2_norm, ffn2_wg, ffn2_wu, ffn2_wd,
           ev_w_in, ev_w_out, ml_conv_w, ml_conv_b, ml_wq, ml_wk, ml_b_i, ml_b_f, ml_out_g,
           nsa_gq, nsa_gk_cmp, nsa_gk_slc, nsa_gk_win, cmp_pe_k, cmp_w1_k, cmp_w2_k,
           cmp_pe_v, cmp_w1_v, cmp_w2_v, nsa_gate_b,
           od_w_in, mla_g_cq, mla_w_uq, mla_g_q, mla_g_ckv, mla_g_kr, mla_w_uk, mla_w_uv, od_w_out):
    past_len = page_table.shape[1] * PAGE_SIZE
    bf = jnp.bfloat16
    ffn_w = [[(n[layer], wg[layer].astype(bf), wu[layer].astype(bf), wd[layer].astype(bf))
              for n, wg, wu, wd in ((ffn1_norm, ffn1_wg, ffn1_wu, ffn1_wd), (ffn2_norm, ffn2_wg, ffn2_wu, ffn2_wd))]
             for layer in range(DEPTH)]

    def run_group(y, prompt):
        ev, od, memk, memv = [], [], [], []
        for layer in range(DEPTH):
            y = swiglu_half(y, *ffn_w[layer][0])
            xn = None if prompt else rms_norm(y, mix_norm[layer])
            if layer % 2 == 0:
                e = layer // 2
                ew = dict(w_in=ev_w_in[e], w_out=ev_w_out[e], conv_w=ml_conv_w[e], conv_b=ml_conv_b[e],
                          ml_wq=ml_wq[e], ml_wk=ml_wk[e], ml_b_i=ml_b_i[e], ml_b_f=ml_b_f[e], ml_out_g=ml_out_g[e],
                          nsa_gq=nsa_gq[e], nsa_gk_cmp=nsa_gk_cmp[e], nsa_gk_slc=nsa_gk_slc[e],
                          nsa_gk_win=nsa_gk_win[e], pe_k=cmp_pe_k[e], w1_k=cmp_w1_k[e], w2_k=cmp_w2_k[e],
                          pe_v=cmp_pe_v[e], w1_v=cmp_w1_v[e], w2_v=cmp_w2_v[e],
                          gate_b=nsa_gate_b[e], rel_bias=rel_bias)
                past = None if prompt else dict(
                    e=e, page_table=page_table, C=state_ml_C[e], n=state_ml_n[e], m=state_ml_m[e],
                    conv=state_ml_conv[e], cmp_k=cache_cmp_k, cmp_v=cache_cmp_v,
                    slc_k=cache_slc_k, slc_v=cache_slc_v, win_k=cache_win_k[e], win_v=cache_win_v[e])
                if prompt:
                    y, st = even_mixer_prompt(y, mix_norm[layer], **ew)
                else:
                    h, st = even_mixer(xn, past_len, past, **ew)
                    y = y + h
                ev.append(st)
            else:
                o = layer // 2
                ow = dict(w_in=od_w_in[o], g_cq=mla_g_cq[o], w_uq=mla_w_uq[o], g_q=mla_g_q[o], g_ckv=mla_g_ckv[o],
                          g_kr=mla_g_kr[o], w_uk=mla_w_uk[o], w_uv=mla_w_uv[o], w_out=od_w_out[o])
                past = None if prompt else dict(e=o, page_table=page_table, ckv=cache_mla_ckv, krope=cache_mla_krope)
                if prompt:
                    y, st = odd_mixer_prompt(y, mix_norm[layer], **ow)
                else:
                    h, st = odd_mixer(xn, past_len, past, **ow)
                    y = y + h
                od.append(st)
            if prompt:
                mk, mv = mem_kv(mem_prompt, xm_mem_norm[layer], xm_wk[layer], xm_wv[layer], xm_gk[layer])
                memk.append(mk)
                memv.append(mv)
                y = mem_attention(y, xm_norm[layer], xm_wq[layer], xm_gq[layer], mk, mv, xm_wo[layer])
            else:
                y = y + mem_attend(rms_norm(y, xm_norm[layer]), cache_mem_k[layer], cache_mem_v[layer],
                                   xm_wq[layer], xm_gq[layer], xm_wo[layer])
            y = swiglu_half(y, *ffn_w[layer][1])
        return y, ev, od, memk, memv

    ys, ev_s, od_s, _, _ = run_group(x_sample, False)
    yp, ev_p, od_p, memk_p, memv_p = run_group(x_prompt, True)
    return (yp, ys,
            stack_key(ev_p, 'C'), stack_key(ev_p, 'n'), stack_key(ev_p, 'm'), stack_key(ev_p, 'conv'),
            stack_key(ev_p, 'cmp_k'), stack_key(ev_p, 'cmp_v'), stack_key(ev_p, 'slc_k'), stack_key(ev_p, 'slc_v'),
            stack_key(ev_p, 'win_k'), stack_key(ev_p, 'win_v'),
            stack_key(od_p, 'ckv'), stack_key(od_p, 'krope'),
            jnp.stack(memk_p), jnp.stack(memv_p),
            stack_key(ev_s, 'C'), stack_key(ev_s, 'n'), stack_key(ev_s, 'm'), stack_key(ev_s, 'conv'),
            stack_key(ev_s, 'cmp_k'), stack_key(ev_s, 'cmp_v'), stack_key(ev_s, 'slc_k'), stack_key(ev_s, 'slc_v'),
            stack_key(ev_s, 'win_k'), stack_key(ev_s, 'win_v'),
            stack_key(od_s, 'ckv'), stack_key(od_s, 'krope'))
```

```python
import functools
import math

import jax
import jax.numpy as jnp
import numpy as np
from jax import lax
from jax.experimental import pallas as pl
from jax.experimental.pallas import tpu as pltpu

D_MODEL = 1024
DEPTH = 2
PAGE_SIZE = 128
EPS = 1e-6
NEG_INF = -1e30
D_FF = 2816
ML_H = 4
ML_DH = 128
ML_W = ML_H * ML_DH
CONV_K = 4
MLSTM_CHUNK = 64
NSA_H = 8
NSA_KV = 2
NSA_R = NSA_H // NSA_KV
NSA_DH = 64
NSA_W = NSA_H * NSA_DH
NSA_KVW = NSA_KV * NSA_DH
NSA_SCALE = NSA_DH ** -0.5
CMP_STRIDE = 16
CMP_LEN = 2 * CMP_STRIDE
CMP_HID = 2 * NSA_DH
SLC_BLOCK = 64
SLC_RATIO = SLC_BLOCK // CMP_STRIDE
SLC_OVERLAP_W = (1.0, 2.0, 2.0, 2.0, 1.0)
N_SELECT = 16
WINDOW = 512
SLC_QBLOCK = 32
WIN_QBLOCK = 128
FORCE_SCORE = 1e6
REL_BUCKETS = 32
REL_MAX_DIST = 128
MLA_H = 16
MLA_NOPE = 64
MLA_ROPE = 32
MLA_V = 64
Q_LORA = 384
KV_LORA = 256
MLA_SCALE = (MLA_NOPE + MLA_ROPE) ** -0.5
MLA_QBLOCK = 128
ROPE_THETA = 10000.0
MEM_LEN = 256
XM_H = 4
XM_DH = 128
XM_W = XM_H * XM_DH
EV_SPLITS = (ML_W, ML_W, ML_W, ML_H, ML_H, NSA_W) + (NSA_KVW,) * 6 + (NSA_H * 3,)
OD_SPLITS = (Q_LORA, KV_LORA, MLA_ROPE)

V7X_LANES = 128
V7X_VMEM_LIMIT_BYTES = 56 * 1024 * 1024
FFN_CHUNK = 256
FFN_ROWS = 512


def _ffn_kernel(x_ref, g_ref, wg_ref, wu_ref, wd_ref, o_ref, act_ref):
    x = x_ref[...]
    h = x * lax.rsqrt(jnp.mean(x * x, axis=-1, keepdims=True) + EPS) * g_ref[...]
    hb = h.astype(jnp.bfloat16)
    for c in range(D_FF // FFN_CHUNK):
        cols = slice(c * FFN_CHUNK, (c + 1) * FFN_CHUNK)
        gate = jnp.dot(hb, wg_ref[:, cols], preferred_element_type=jnp.float32)
        up = jnp.dot(hb, wu_ref[:, cols], preferred_element_type=jnp.float32)
        act_ref[:, cols] = (gate * jax.nn.sigmoid(gate) * up).astype(jnp.bfloat16)
    o_ref[...] = x + 0.5 * jnp.dot(act_ref[...], wd_ref[...], preferred_element_type=jnp.float32)


def ffn_half(x2d, g, wg, wu, wd):
    m = x2d.shape[0]
    tm = min(FFN_ROWS, m)
    assert m % tm == 0
    resident = functools.partial(pl.BlockSpec, pipeline_mode=pl.Buffered(1))
    return pl.pallas_call(
        _ffn_kernel,
        grid=(m // tm,),
        in_specs=[
            pl.BlockSpec((tm, D_MODEL), lambda i: (i, 0)),
            resident((1, D_MODEL), lambda i: (0, 0)),
            resident((D_MODEL, D_FF), lambda i: (0, 0)),
            resident((D_MODEL, D_FF), lambda i: (0, 0)),
            resident((D_FF, D_MODEL), lambda i: (0, 0)),
        ],
        out_specs=pl.BlockSpec((tm, D_MODEL), lambda i: (i, 0)),
        out_shape=jax.ShapeDtypeStruct((m, D_MODEL), jnp.float32),
        scratch_shapes=[pltpu.VMEM((tm, D_FF), jnp.bfloat16)],
        compiler_params=pltpu.CompilerParams(
            dimension_semantics=("arbitrary",), vmem_limit_bytes=V7X_VMEM_LIMIT_BYTES),
        name="ffn_half",
    )(x2d, g.reshape(1, D_MODEL), wg, wu, wd)


def swiglu_half(x, g, wg, wu, wd):
    shp = x.shape
    return ffn_half(x.reshape(-1, D_MODEL), g, wg, wu, wd).reshape(shp)


EV_PROJ_ROWS = 512
EV_PROJ_GROUPS = (("u", ML_W), ("v", ML_W), ("o", ML_W), ("q", NSA_W), ("kc", NSA_KVW), ("vc", NSA_KVW),
                  ("ks", NSA_KVW), ("vs", NSA_KVW), ("kw", NSA_KVW), ("vw", NSA_KVW), ("small", V7X_LANES))


def _even_proj_kernel(x_ref, g_ref, w_ref, gq_ref, gks_ref, gkw_ref,
                      u_ref, v_ref, o_ref, small_ref, kc_ref, vc_ref, ks_ref, vs_ref, kw_ref, vw_ref,
                      q5_ref, kst_ref, vst_ref, kwt_ref, vwt_ref):
    f32, bf16 = jnp.float32, jnp.bfloat16
    x = x_ref[0]
    xn = (x * lax.rsqrt(jnp.mean(x * x, axis=-1, keepdims=True) + EPS) * g_ref[...]).astype(bf16)
    y = jnp.dot(xn, w_ref[...], preferred_element_type=f32)
    off, cols = 0, {}
    for name, width in EV_PROJ_GROUPS:
        cols[name] = y[:, off:off + width]
        off += width
    lane = lax.broadcasted_iota(jnp.int32, (1, V7X_LANES), 1)
    DH = NSA_DH

    def pair_norm(p, gain):
        sq = p * p
        s_all = jnp.sum(sq, axis=-1, keepdims=True)
        s_lo = jnp.sum(jnp.where(lane < DH, sq, 0.0), axis=-1, keepdims=True)
        ms = jnp.where(lane < DH, s_lo, s_all - s_lo) * (1.0 / DH)
        return p * lax.rsqrt(ms + EPS) * gain

    def put_pair(ref, first, p):
        p = p.astype(bf16)
        ref[0, first] = p[:, :DH]
        ref[0, first + 1] = p[:, DH:]

    u_ref[0], v_ref[0], o_ref[0], small_ref[0] = cols["u"], cols["v"], cols["o"], cols["small"]
    for pr in range(NSA_H // 2):
        qn = pair_norm(cols["q"][:, pr * V7X_LANES:(pr + 1) * V7X_LANES], gq_ref[...])
        put_pair(q5_ref, 2 * pr, qn * NSA_SCALE)
    kc_ref[0], vc_ref[0], vs_ref[0], vw_ref[0] = cols["kc"], cols["vc"], cols["vs"], cols["vw"]
    ks = pair_norm(cols["ks"], gks_ref[...])
    kw = pair_norm(cols["kw"], gkw_ref[...])
    ks_ref[0], kw_ref[0] = ks, kw
    put_pair(kst_ref, 0, ks)
    put_pair(vst_ref, 0, cols["vs"])
    put_pair(kwt_ref, 0, kw)
    put_pair(vwt_ref, 0, cols["vw"])


def even_in_proj_prompt(y, g_mix, w_in, nsa_gq, nsa_gk_slc, nsa_gk_win):
    f32, bf16 = jnp.float32, jnp.bfloat16
    B, S, D = y.shape
    tm = min(EV_PROJ_ROWS, S)
    assert S % tm == 0 and NSA_KVW == V7X_LANES and 2 * ML_H + 3 * NSA_H <= V7X_LANES
    (w_u, w_v, w_o, w_i, w_f, w_q, w_kc, w_vc, w_ks, w_vs, w_kw, w_vw, w_g) = split_cols(w_in, EV_SPLITS)
    w_small = jnp.pad(jnp.concatenate([w_i, w_f, w_g], axis=1), ((0, 0), (0, V7X_LANES - 2 * ML_H - 3 * NSA_H)))
    w = jnp.concatenate([w_u, w_v, w_o, w_q, w_kc, w_vc, w_ks, w_vs, w_kw, w_vw, w_small], axis=1).astype(bf16)
    n_cols = w.shape[1]
    assert n_cols == sum(width for _, width in EV_PROJ_GROUPS)
    const = lambda shape: pl.BlockSpec(shape, lambda b, i: (0,) * len(shape), pipeline_mode=pl.Buffered(1))
    rows = lambda width: pl.BlockSpec((1, tm, width), lambda b, i: (b, i, 0))
    heads = lambda n: pl.BlockSpec((1, n, tm, NSA_DH), lambda b, i: (b, 0, i, 0))
    wide = jax.ShapeDtypeStruct((B, S, ML_W), f32)
    kv = jax.ShapeDtypeStruct((B, S, NSA_KVW), f32)
    kvt = jax.ShapeDtypeStruct((B, NSA_KV, S, NSA_DH), bf16)
    pair_gain = lambda g: jnp.tile(g, 2).reshape(1, V7X_LANES)
    (u, v, o, small, kc, vc, ks, vs, kw, vw, q5, kst, vst, kwt, vwt) = pl.pallas_call(
        _even_proj_kernel,
        grid=(B, S // tm),
        in_specs=[rows(D), const((1, D)), const((D, n_cols)), const((1, V7X_LANES)), const((1, V7X_LANES)),
                  const((1, V7X_LANES))],
        out_specs=[rows(ML_W), rows(ML_W), rows(ML_W), rows(V7X_LANES)] + [rows(NSA_KVW)] * 6
                  + [heads(NSA_H)] + [heads(NSA_KV)] * 4,
        out_shape=[wide, wide, wide, jax.ShapeDtypeStruct((B, S, V7X_LANES), f32)] + [kv] * 6
                  + [jax.ShapeDtypeStruct((B, NSA_H, S, NSA_DH), bf16)] + [kvt] * 4,
        compiler_params=pltpu.CompilerParams(
            dimension_semantics=("arbitrary", "arbitrary"), vmem_limit_bytes=V7X_VMEM_LIMIT_BYTES),
        name="even_in_proj",
    )(y, g_mix.reshape(1, D), w, pair_gain(nsa_gq), pair_gain(nsa_gk_slc), pair_gain(nsa_gk_win))
    kv4 = lambda a: a.reshape(B, S, NSA_KV, NSA_DH)
    return dict(u=u, v=v, o=o, i_pre=small[..., :ML_H], f_pre=small[..., ML_H:2 * ML_H],
                g_pre=small[..., 2 * ML_H:2 * ML_H + 3 * NSA_H], kc=kv4(kc), vc=kv4(vc), ks=kv4(ks), vs=kv4(vs),
                kw=kv4(kw), vw=kv4(vw), q5=q5, kst=kst, vst=vst, kwt=kwt, vwt=vwt)


NSA_TQ = 256
NSA_NEAR = 128


def _nsa_prompt_kernel(q_ref, kc_ref, vc_ref, ks_ref, vs_ref, kw_ref, vw_ref, bc_ref, bn_ref,
                       e_ref, mw_ref, gp_ref, gb_ref, o_ref, *, n_cmp, n_select):
    f32, bf16 = jnp.float32, jnp.bfloat16
    qi = pl.program_id(2)
    R, TQ, DH = q_ref.shape[2:]
    S = ks_ref.shape[2]
    t0 = qi * TQ
    q2 = q_ref[0, 0].reshape(R * TQ, DH)
    t_col = t0 + lax.broadcasted_iota(jnp.int32, (TQ, 1), 0)

    def scores(k):
        s = lax.dot_general(q2, k, (((1,), (1,)), ((), ())), preferred_element_type=f32)
        return s.reshape(R, TQ, k.shape[0])

    ncp = kc_ref.shape[2]
    c_row = lax.broadcasted_iota(jnp.int32, (1, ncp), 1)
    mask_c = (t_col >= c_row * CMP_STRIDE + (CMP_LEN - 1)) & (c_row < n_cmp)
    s = jnp.where(mask_c[None], scores(kc_ref[0, 0]) + bc_ref[0], NEG_INF)
    e = jnp.exp(s - jnp.max(s, axis=-1, keepdims=True))
    p = e / jnp.sum(e, axis=-1, keepdims=True) * mask_c.astype(f32)[None]
    o_cmp = jnp.dot(p.reshape(R * TQ, ncp).astype(bf16), vc_ref[0, 0],
                    preferred_element_type=f32).reshape(R, TQ, DH)
    imp = jnp.sum(p, axis=0)

    mw = mw_ref[...]
    hi = imp.astype(bf16)
    r1 = imp - hi.astype(f32)
    mid = r1.astype(bf16)
    lo = (r1 - mid.astype(f32)).astype(bf16)
    p_slc = (jnp.dot(hi, mw, preferred_element_type=f32) + jnp.dot(mid, mw, preferred_element_type=f32)
             + jnp.dot(lo, mw, preferred_element_type=f32))
    ns = mw_ref.shape[1]
    jb = lax.broadcasted_iota(jnp.int32, (1, ns), 1)
    tb = jnp.right_shift(t_col, int(math.log2(SLC_BLOCK)))
    forced = (jb == 0) | (jb == tb) | (jb == tb - 1)
    score = jnp.where(jb <= tb, p_slc + FORCE_SCORE * forced.astype(f32), -1.0)
    rank = jnp.zeros((TQ, ns), f32)
    for j in range(ns):
        col = score[:, j:j + 1]
        rank = rank + ((col > score) | ((col == score) & (jb > j))).astype(f32)
    sel = (rank < n_select).astype(bf16)
    gates = jax.nn.sigmoid(gp_ref[0, 0] + gb_ref[0])
    q = q_ref[0, 0]

    def attend(r, k, v, add_mask, first_key, t_start):
        s = lax.dot_general(q[r], k, (((1,), (1,)), ((), ())), preferred_element_type=f32) + add_mask
        near_lo = max(t_start - NSA_NEAR, first_key)
        band = bn_ref[0, r][:, near_lo - (t_start - NSA_NEAR):]
        cut = near_lo - first_key
        near = s[:, cut:] + band
        s = near if cut == 0 else jnp.concatenate([s[:, :cut], near], axis=1)
        e = jnp.exp(s - jnp.max(s, axis=-1, keepdims=True))
        return jnp.dot(e.astype(bf16), v, preferred_element_type=f32) / jnp.sum(e, axis=-1, keepdims=True)

    for c in range(S // TQ):
        @pl.when(qi == c)
        def _(c=c):
            t_start, n_keys = c * TQ, (c + 1) * TQ
            row = t_start + lax.broadcasted_iota(jnp.int32, (TQ, 1), 0)
            col = lax.broadcasted_iota(jnp.int32, (1, n_keys), 1)
            sel_tok = jnp.dot(sel, e_ref[:, :n_keys], preferred_element_type=f32)
            mask_s = jnp.where((sel_tok > 0.5) & (col <= row), 0.0, NEG_INF)
            w_lo = max(t_start - WINDOW, 0)
            col_w = w_lo + lax.broadcasted_iota(jnp.int32, (1, n_keys - w_lo), 1)
            mask_w = jnp.where((col_w <= row) & (row - col_w <= WINDOW), 0.0, NEG_INF)
            ks, vs = ks_ref[0, 0, :n_keys, :], vs_ref[0, 0, :n_keys, :]
            kw, vw = kw_ref[0, 0, w_lo:n_keys, :], vw_ref[0, 0, w_lo:n_keys, :]
            outs = []
            for r in range(R):
                o_slc = attend(r, ks, vs, mask_s, 0, t_start)
                o_win = attend(r, kw, vw, mask_w, w_lo, t_start)
                outs.append(gates[:, 3 * r:3 * r + 1] * o_cmp[r] + gates[:, 3 * r + 1:3 * r + 2] * o_slc
                            + gates[:, 3 * r + 2:3 * r + 3] * o_win)
            o_ref[0] = jnp.concatenate(outs, axis=-1)


def _bias_lookup(rel_bias, dist):
    bucket = t5_bucket(dist)[..., None]
    out = jnp.zeros(dist.shape + rel_bias.shape[1:], rel_bias.dtype)
    for b in range(REL_BUCKETS):
        out = jnp.where(bucket == b, rel_bias[b], out)
    return out


def nsa_prompt(q_heads, k_cmp, v_cmp, ks_t, vs_t, kw_t, vw_t, g_pre, gate_b, rel_bias):
    f32, bf16 = jnp.float32, jnp.bfloat16
    B, _, S, _ = q_heads.shape
    G, R, DH, TQ = NSA_KV, NSA_R, NSA_DH, NSA_TQ
    assert S % TQ == 0 and TQ % V7X_LANES == 0 and WINDOW % V7X_LANES == 0 and NSA_NEAR % V7X_LANES == 0
    n_cmp = k_cmp.shape[1]
    ncp = S // CMP_STRIDE
    ns = S // SLC_BLOCK
    assert n_cmp == ncp - 1
    n_select = min(N_SELECT, ns)

    def kv_layout(a, n):
        a = jnp.pad(a, ((0, 0), (0, n - a.shape[1]), (0, 0), (0, 0)))
        return jnp.transpose(a, (0, 2, 1, 3)).astype(bf16)

    q5 = q_heads.reshape(B, G, R, S, DH)
    kc, vc = kv_layout(k_cmp, ncp), kv_layout(v_cmp, ncp)

    half = REL_BUCKETS // 2
    assert half + int(math.log(NSA_NEAR / half) / math.log(REL_MAX_DIST / half) * half) >= REL_BUCKETS - 1
    t = jnp.arange(S)
    dist_c = t[:, None] - (jnp.arange(ncp) * CMP_STRIDE + CMP_LEN - 1)[None, :]
    bias_c = jnp.transpose(_bias_lookup(rel_bias, dist_c).reshape(S, ncp, G, R), (2, 3, 0, 1))
    d_near = jnp.arange(TQ)[:, None] + NSA_NEAR - jnp.arange(TQ + NSA_NEAR)[None, :]
    bias_n = jnp.transpose((_bias_lookup(rel_bias, d_near) - rel_bias[REL_BUCKETS - 1]).reshape(
        TQ, TQ + NSA_NEAR, G, R), (2, 3, 0, 1))
    expand = (jnp.arange(S)[None, :] // SLC_BLOCK == jnp.arange(ns)[:, None]).astype(bf16)
    c_i = jnp.arange(ncp)[:, None]
    j_i = jnp.arange(ns)[None, :]
    mw = sum(w * (c_i == SLC_RATIO * j_i + k - 1) for k, w in enumerate(SLC_OVERLAP_W)).astype(bf16)
    gp = jnp.transpose(g_pre.reshape(B, S, G, 3 * R), (0, 2, 1, 3))
    gb = gate_b.reshape(G, 1, 3 * R)

    kv_spec = lambda n: pl.BlockSpec((1, 1, n, DH), lambda b, g, i: (b, g, 0, 0))
    return pl.pallas_call(
        functools.partial(_nsa_prompt_kernel, n_cmp=n_cmp, n_select=n_select),
        grid=(B, G, S // TQ),
        in_specs=[
            pl.BlockSpec((1, 1, R, TQ, DH), lambda b, g, i: (b, g, 0, i, 0)),
            kv_spec(ncp), kv_spec(ncp), kv_spec(S), kv_spec(S), kv_spec(S), kv_spec(S),
            pl.BlockSpec((1, R, TQ, ncp), lambda b, g, i: (g, 0, i, 0)),
            pl.BlockSpec((1, R, TQ, TQ + NSA_NEAR), lambda b, g, i: (g, 0, 0, 0)),
            pl.BlockSpec((ns, S), lambda b, g, i: (0, 0)),
            pl.BlockSpec((ncp, ns), lambda b, g, i: (0, 0)),
            pl.BlockSpec((1, 1, TQ, 3 * R), lambda b, g, i: (b, g, i, 0)),
            pl.BlockSpec((1, 1, 3 * R), lambda b, g, i: (g, 0, 0)),
        ],
        out_specs=pl.BlockSpec((1, TQ, R * DH), lambda b, g, i: (b, i, g)),
        out_shape=jax.ShapeDtypeStruct((B, S, NSA_W), f32),
        compiler_params=pltpu.CompilerParams(
            dimension_semantics=("arbitrary", "arbitrary", "arbitrary"), vmem_limit_bytes=V7X_VMEM_LIMIT_BYTES),
        name="nsa_prompt",
    )(q5, kc, vc, ks_t, vs_t, kw_t, vw_t, bias_c, bias_n, expand, mw, gp, gb)


MLA_TQ = 512
MLA_HEADS_PER_STEP = 2


def _mla_prompt_kernel(cq_ref, wuq_ref, ra_ref, rm_ref, rp_ref, ckv_ref, kr_ref, wuk_ref, wuv_ref,
                       o_ref, q_ref, k_ref, v_ref):
    f32, bf16 = jnp.float32, jnp.bfloat16
    qi = pl.program_id(2)
    HP, S, DP = k_ref.shape
    TQ = cq_ref.shape[1]
    causal = lax.broadcasted_iota(jnp.int32, (TQ, 1), 0) >= lax.broadcasted_iota(jnp.int32, (1, TQ), 1)

    @pl.when(qi == 0)
    def _():
        ckv = ckv_ref[0]
        pad = jnp.zeros((S, DP - MLA_NOPE - MLA_ROPE), bf16)
        for h in range(HP):
            k_nope = jnp.dot(ckv, wuk_ref[h], preferred_element_type=f32).astype(bf16)
            k_ref[h] = jnp.concatenate([k_nope, kr_ref[0], pad], axis=-1)
            v_ref[h] = jnp.dot(ckv, wuv_ref[h], preferred_element_type=f32).astype(bf16)

    for h in range(HP):
        q3 = jnp.dot(cq_ref[0], wuq_ref[h], preferred_element_type=f32)
        q = q3[:, :DP]
        ms = jnp.sum(q * q, axis=-1, keepdims=True) * (1.0 / (MLA_NOPE + MLA_ROPE))
        q = q * ra_ref[...] + q3[:, DP:2 * DP] * rm_ref[...] + q3[:, 2 * DP:] * rp_ref[...]
        q_ref[h] = (q * (lax.rsqrt(ms + EPS) * MLA_SCALE)).astype(bf16)

    for c in range(S // TQ):
        @pl.when(qi == c)
        def _(c=c):
            n_vis = c * TQ
            nt = (((1,), (1,)), ((), ()))
            outs = []
            for h in range(HP):
                qh = q_ref[h]
                s_diag = jnp.where(causal, lax.dot_general(qh, k_ref[h, n_vis:n_vis + TQ, :], nt,
                                                           preferred_element_type=f32), NEG_INF)
                m = jnp.max(s_diag, axis=-1, keepdims=True)
                if n_vis > 0:
                    s_vis = lax.dot_general(qh, k_ref[h, :n_vis, :], nt, preferred_element_type=f32)
                    m = jnp.maximum(m, jnp.max(s_vis, axis=-1, keepdims=True))
                e_diag = jnp.exp(s_diag - m)
                total = jnp.sum(e_diag, axis=-1, keepdims=True)
                pv = jnp.dot(e_diag.astype(bf16), v_ref[h, n_vis:n_vis + TQ, :], preferred_element_type=f32)
                if n_vis > 0:
                    e_vis = jnp.exp(s_vis - m)
                    total = total + jnp.sum(e_vis, axis=-1, keepdims=True)
                    pv = pv + jnp.dot(e_vis.astype(bf16), v_ref[h, :n_vis, :], preferred_element_type=f32)
                outs.append(pv / total)
            o_ref[0] = jnp.concatenate(outs, axis=-1)


def mla_prompt_attention(cq, w_uq, g_q, pos, ckv, kr, w_uk, w_uv):
    f32, bf16 = jnp.float32, jnp.bfloat16
    B, S, _ = cq.shape
    H, DQ, DP = MLA_H, MLA_NOPE + MLA_ROPE, V7X_LANES
    HP, TQ = MLA_HEADS_PER_STEP, min(MLA_TQ, S)
    half = MLA_ROPE // 2
    assert S % TQ == 0 and H % HP == 0 and HP * MLA_V == V7X_LANES and DQ <= DP
    w = jnp.transpose(w_uq.reshape(Q_LORA, H, DQ), (1, 0, 2))
    zw = lambda n: jnp.zeros((H, Q_LORA, n), f32)
    w_left = jnp.concatenate([zw(MLA_NOPE), w[..., MLA_NOPE + half:], zw(half)], axis=-1)
    w_right = jnp.concatenate([zw(MLA_NOPE + half), w[..., MLA_NOPE:MLA_NOPE + half]], axis=-1)
    pad_w = lambda a: jnp.pad(a, ((0, 0), (0, 0), (0, DP - DQ)))
    wuq_p = jnp.concatenate([pad_w(w), pad_w(w_left), pad_w(w_right)], axis=-1).astype(bf16)
    inv = ROPE_THETA ** (-jnp.arange(half, dtype=f32) / half)
    ang = pos.astype(f32)[:, None] * inv[None, :]
    cos, sin, zero = jnp.cos(ang), jnp.sin(ang), jnp.zeros((S, half), f32)
    g_nope, g_lo, g_hi = g_q[:MLA_NOPE], g_q[MLA_NOPE:MLA_NOPE + half], g_q[MLA_NOPE + half:]
    tail = jnp.zeros((S, DP - DQ), f32)
    lead0 = jnp.zeros((S, MLA_NOPE), f32)
    rope_a = jnp.concatenate([jnp.broadcast_to(g_nope, (S, MLA_NOPE)), cos * g_lo, cos * g_hi, tail], axis=1)
    rope_m = jnp.concatenate([lead0, -sin * g_hi, zero, tail], axis=1)
    rope_p = jnp.concatenate([lead0, zero, sin * g_lo, tail], axis=1)
    row_tab = pl.BlockSpec((TQ, DP), lambda b, h, i: (i, 0))
    return pl.pallas_call(
        _mla_prompt_kernel,
        grid=(B, H // HP, S // TQ),
        in_specs=[
            pl.BlockSpec((1, TQ, Q_LORA), lambda b, h, i: (b, i, 0)),
            pl.BlockSpec((HP, Q_LORA, 3 * DP), lambda b, h, i: (h, 0, 0)),
            row_tab, row_tab, row_tab,
            pl.BlockSpec((1, S, KV_LORA), lambda b, h, i: (b, 0, 0)),
            pl.BlockSpec((1, S, MLA_ROPE), lambda b, h, i: (b, 0, 0)),
            pl.BlockSpec((HP, KV_LORA, MLA_NOPE), lambda b, h, i: (h, 0, 0)),
            pl.BlockSpec((HP, KV_LORA, MLA_V), lambda b, h, i: (h, 0, 0)),
        ],
        out_specs=pl.BlockSpec((1, TQ, HP * MLA_V), lambda b, h, i: (b, i, h)),
        out_shape=jax.ShapeDtypeStruct((B, S, H * MLA_V), jnp.float32),
        scratch_shapes=[pltpu.VMEM((HP, TQ, DP), bf16), pltpu.VMEM((HP, S, DP), bf16),
                        pltpu.VMEM((HP, S, MLA_V), bf16)],
        compiler_params=pltpu.CompilerParams(
            dimension_semantics=("arbitrary", "arbitrary", "arbitrary"), vmem_limit_bytes=V7X_VMEM_LIMIT_BYTES),
        name="mla_prompt",
    )(cq, wuq_p, rope_a, rope_m, rope_p, ckv, kr, w_uk, w_uv)


def _split3(x):
    bf16, f32 = jnp.bfloat16, jnp.float32
    hi = x.astype(bf16)
    r = x - hi.astype(f32)
    mid = r.astype(bf16)
    return hi, mid, (r - mid.astype(f32)).astype(bf16)


def _mlstm_kernel(u_ref, v_ref, og_ref, cw_ref, cb_ref, wq_ref, wk_ref, g_ref, ir_ref, fr_ref, it_ref, ft_ref,
                  tri_ref, h_ref, c_ref, n_ref, m_ref):
    f32, bf16 = jnp.float32, jnp.bfloat16
    H, NC, L = ir_ref.shape[1:]
    DH = ML_DH
    PRE = 8
    tri = tri_ref[...]
    causal = (tri > 0.5)[None]

    def bmm(a, b, ca, cb):
        return lax.dot_general(a, b, (((ca,), (cb,)), ((0,), (0,))), preferred_element_type=f32)

    def split_heads(x):
        return jnp.stack([x[:, h * DH:(h + 1) * DH] for h in range(H)], axis=0)

    def conv_act(c):
        cur = u_ref[0, c * L:(c + 1) * L, :]
        prev = jnp.zeros((PRE, cur.shape[1]), f32) if c == 0 else u_ref[0, c * L - PRE:c * L, :]
        full = jnp.concatenate([prev, cur], axis=0)
        acc = cb_ref[...]
        for j in range(CONV_K):
            lo = PRE - (CONV_K - 1) + j
            acc = acc + full[lo:lo + L, :] * cw_ref[j:j + 1, :]
        return acc * jax.nn.sigmoid(acc)

    b_rows = jnp.stack([sum(lax.dot_general(p, tri, (((1,), (1,)), ((), ())), preferred_element_type=f32)
                            for p in _split3(fr_ref[0, h])) for h in range(H)], axis=0)
    b_cols = jnp.stack([sum(jnp.dot(tri, p, preferred_element_type=f32) for p in _split3(ft_ref[0, h]))
                        for h in range(H)], axis=0)
    i_rows, i_cols = ir_ref[0], it_ref[0]
    C = jnp.zeros((H, DH, DH), f32)
    n = jnp.zeros((H, 1, DH), f32)
    m = jnp.zeros((H, 1, 1), f32)
    for c in range(NC):
        rows = slice(c * L, (c + 1) * L)
        ch = split_heads(conv_act(c)).astype(bf16)
        q = bmm(ch, wq_ref[...], 2, 1).astype(bf16)
        k = (bmm(ch, wk_ref[...], 2, 1) * (DH ** -0.5)).astype(bf16)
        v = split_heads(v_ref[0, rows, :])
        b_row, i_row = b_rows[:, c:c + 1, :], i_rows[:, c:c + 1, :]
        b_col, i_col = b_cols[:, :, c:c + 1], i_cols[:, :, c:c + 1]
        g = b_col + m
        dmat = jnp.where(causal, b_col - b_row + i_row, -jnp.inf)
        mt = jnp.maximum(g, jnp.max(dmat, axis=-1, keepdims=True))
        inter = jnp.exp(g - mt)
        sqk = bmm(q, k, 2, 2) * jnp.exp(dmat - mt)
        num = inter * bmm(q, C.astype(bf16), 2, 2) + bmm(sqk.astype(bf16), v.astype(bf16), 2, 1)
        nq = jnp.sum(q.astype(f32) * n.astype(bf16).astype(f32), axis=-1, keepdims=True)
        den = inter * nq + jnp.sum(sqk, axis=-1, keepdims=True)
        hs = num / jnp.maximum(jnp.abs(den), jnp.exp(-mt))
        hs = hs * lax.rsqrt(jnp.mean(hs * hs, axis=-1, keepdims=True) + EPS)
        gate = jax.nn.sigmoid(og_ref[0, rows, :])
        for h in range(H):
            lanes = slice(h * DH, (h + 1) * DH)
            h_ref[0, rows, lanes] = hs[h] * g_ref[:, lanes] * gate[:, lanes]
        b_end = b_row[:, :, L - 1:L]
        w_row = b_end - b_row + i_row
        m_new = jnp.maximum(b_end + m, jnp.max(w_row, axis=-1, keepdims=True))
        decay = jnp.exp(b_end + m - m_new)
        w_col = jnp.exp(b_end - b_col + i_col - m_new)
        C = decay * C + bmm((w_col * v).astype(bf16), k, 1, 1)
        n = decay * n + bmm(jnp.exp(w_row - m_new).astype(bf16), k, 2, 1)
        m = m_new
    c_ref[0] = C
    n_ref[0] = n
    m_ref[0] = jnp.broadcast_to(m, (H, 1, V7X_LANES))


def mlstm_prompt(u, v, o_pre, conv_w, conv_b, wq, wk, out_g, i_pre, logf):
    f32, bf16 = jnp.float32, jnp.bfloat16
    B, S, W = u.shape
    H, L = ML_H, MLSTM_CHUNK
    assert S % L == 0 and W == H * ML_DH and L % 8 == 0 and CONV_K - 1 <= 8
    const = lambda shape: pl.BlockSpec(shape, lambda b: (0,) * len(shape))
    NC = S // L
    rows = lambda a: jnp.transpose(a, (0, 2, 1)).reshape(B, H, NC, L)
    cols = lambda a: jnp.transpose(rows(a), (0, 1, 3, 2))
    tri = jnp.tril(jnp.ones((L, L), jnp.bfloat16))
    seq = pl.BlockSpec((1, S, W), lambda b: (b, 0, 0))
    gate_r = pl.BlockSpec((1, H, NC, L), lambda b: (b, 0, 0, 0))
    gate_c = pl.BlockSpec((1, H, L, NC), lambda b: (b, 0, 0, 0))
    hs, C, n, m = pl.pallas_call(
        _mlstm_kernel,
        grid=(B,),
        in_specs=[seq, seq, seq, const((CONV_K, W)), const((1, W)), const((H, ML_DH, ML_DH)),
                  const((H, ML_DH, ML_DH)), const((1, W)), gate_r, gate_r, gate_c, gate_c, const((L, L))],
        out_specs=[seq, pl.BlockSpec((1, H, ML_DH, ML_DH), lambda b: (b, 0, 0, 0)),
                   pl.BlockSpec((1, H, 1, ML_DH), lambda b: (b, 0, 0, 0)),
                   pl.BlockSpec((1, H, 1, V7X_LANES), lambda b: (b, 0, 0, 0))],
        out_shape=[jax.ShapeDtypeStruct((B, S, W), f32), jax.ShapeDtypeStruct((B, H, ML_DH, ML_DH), f32),
                   jax.ShapeDtypeStruct((B, H, 1, ML_DH), f32), jax.ShapeDtypeStruct((B, H, 1, V7X_LANES), f32)],
        compiler_params=pltpu.CompilerParams(dimension_semantics=("arbitrary",),
                                             vmem_limit_bytes=V7X_VMEM_LIMIT_BYTES),
        name="mlstm_prompt",
    )(u, v, o_pre, conv_w, conv_b.reshape(1, W), wq.astype(bf16), wk.astype(bf16), out_g.reshape(1, W),
      rows(i_pre), rows(logf), cols(i_pre), cols(logf), tri)
    return hs, C, n[:, :, 0], m[:, :, 0, 0]


def _odd_proj_kernel(x_ref, g_ref, w_ref, gcq_ref, gckv_ref, ra_ref, rm_ref, rp_ref,
                     cq_ref, ckv_ref, ckv16_ref, kr_ref, kr16_ref):
    f32, bf16 = jnp.float32, jnp.bfloat16
    x = x_ref[0]
    xn = (x * lax.rsqrt(jnp.mean(x * x, axis=-1, keepdims=True) + EPS) * g_ref[...]).astype(bf16)
    y = jnp.dot(xn, w_ref[...], preferred_element_type=f32)
    norm = lambda a: a * lax.rsqrt(jnp.mean(a * a, axis=-1, keepdims=True) + EPS)
    cq_ref[0] = (norm(y[:, :Q_LORA]) * gcq_ref[...]).astype(bf16)
    ckv = norm(y[:, Q_LORA:Q_LORA + KV_LORA]) * gckv_ref[...]
    ckv_ref[0] = ckv
    ckv16_ref[0] = ckv.astype(bf16)
    base = Q_LORA + KV_LORA
    kr, kr_l, kr_r = (y[:, base + j * V7X_LANES:base + (j + 1) * V7X_LANES] for j in range(3))
    ms = jnp.sum(kr * kr, axis=-1, keepdims=True) * (1.0 / MLA_ROPE)
    roped = (kr * ra_ref[...] + kr_l * rm_ref[...] + kr_r * rp_ref[...]) * lax.rsqrt(ms + EPS)
    kr_ref[0] = roped[:, :MLA_ROPE]
    kr16_ref[0] = roped[:, :MLA_ROPE].astype(bf16)


def odd_in_proj_prompt(y, g_mix, w_in, g_cq, g_ckv, g_kr, pos):
    f32, bf16 = jnp.float32, jnp.bfloat16
    B, S, D = y.shape
    tm = min(EV_PROJ_ROWS, S)
    half, lanes = MLA_ROPE // 2, V7X_LANES
    assert S % tm == 0 and Q_LORA % lanes == 0 and KV_LORA % lanes == 0 and MLA_ROPE <= lanes
    w_cq, w_ckv, w_kr = split_cols(w_in, OD_SPLITS)
    zw = lambda n: jnp.zeros((D, n), f32)
    blocks = [w_kr, jnp.concatenate([w_kr[:, half:], zw(half)], axis=1),
              jnp.concatenate([zw(half), w_kr[:, :half]], axis=1)]
    w = jnp.concatenate([w_cq, w_ckv] + [jnp.pad(b, ((0, 0), (0, lanes - MLA_ROPE))) for b in blocks],
                        axis=1).astype(bf16)
    n_cols = w.shape[1]
    inv = ROPE_THETA ** (-jnp.arange(half, dtype=f32) / half)
    ang = pos.astype(f32)[:, None] * inv[None, :]
    cos, sin, zero = jnp.cos(ang), jnp.sin(ang), jnp.zeros((S, half), f32)
    tail = jnp.zeros((S, lanes - MLA_ROPE), f32)
    g_lo, g_hi = g_kr[:half], g_kr[half:]
    rope_a = jnp.concatenate([cos * g_lo, cos * g_hi, tail], axis=1)
    rope_m = jnp.concatenate([-sin * g_hi, zero, tail], axis=1)
    rope_p = jnp.concatenate([zero, sin * g_lo, tail], axis=1)
    const = lambda shape: pl.BlockSpec(shape, lambda b, i: (0,) * len(shape), pipeline_mode=pl.Buffered(1))
    rows = lambda width: pl.BlockSpec((1, tm, width), lambda b, i: (b, i, 0))
    tab = pl.BlockSpec((tm, lanes), lambda b, i: (i, 0))
    shape = lambda width, dt: jax.ShapeDtypeStruct((B, S, width), dt)
    return pl.pallas_call(
        _odd_proj_kernel,
        grid=(B, S // tm),
        in_specs=[rows(D), const((1, D)), const((D, n_cols)), const((1, Q_LORA)), const((1, KV_LORA)), tab, tab, tab],
        out_specs=[rows(Q_LORA), rows(KV_LORA), rows(KV_LORA), rows(MLA_ROPE), rows(MLA_ROPE)],
        out_shape=[shape(Q_LORA, bf16), shape(KV_LORA, f32), shape(KV_LORA, bf16), shape(MLA_ROPE, f32),
                   shape(MLA_ROPE, bf16)],
        compiler_params=pltpu.CompilerParams(
            dimension_semantics=("arbitrary", "arbitrary"), vmem_limit_bytes=V7X_VMEM_LIMIT_BYTES),
        name="odd_in_proj",
    )(y, g_mix.reshape(1, D), w, g_cq.reshape(1, Q_LORA), g_ckv.reshape(1, KV_LORA), rope_a, rope_m, rope_p)


MEM_ROWS = 512


def _mem_attn_kernel(*refs):
    f32, bf16 = jnp.float32, jnp.bfloat16
    n = (len(refs) - 8) // 2
    x_ref, part_refs, w_refs = refs[0], refs[1:1 + n], refs[1 + n:1 + 2 * n]
    g_ref, wq_ref, gq_ref, k_ref, v_ref, wo_ref, o_ref = refs[1 + 2 * n:]
    x = x_ref[0]
    for p_ref, w_ref in zip(part_refs, w_refs):
        x = x + jnp.dot(p_ref[0].astype(bf16), w_ref[...], preferred_element_type=f32)
    xn = (x * lax.rsqrt(jnp.mean(x * x, axis=-1, keepdims=True) + EPS) * g_ref[...]).astype(bf16)
    q = jnp.dot(xn, wq_ref[...], preferred_element_type=f32)
    k, v = k_ref[0], v_ref[0]
    outs = []
    for h in range(XM_H):
        cols = slice(h * XM_DH, (h + 1) * XM_DH)
        qh = q[:, cols]
        qh = (qh * lax.rsqrt(jnp.mean(qh * qh, axis=-1, keepdims=True) + EPS) * gq_ref[...]).astype(bf16)
        s = lax.dot_general(qh, k[:, cols], (((1,), (1,)), ((), ())), preferred_element_type=f32) * (XM_DH ** -0.5)
        e = jnp.exp(s - jnp.max(s, axis=-1, keepdims=True))
        p = (e / jnp.sum(e, axis=-1, keepdims=True)).astype(bf16)
        outs.append(jnp.dot(p, v[:, cols], preferred_element_type=f32))
    o = jnp.concatenate(outs, axis=-1).astype(bf16)
    o_ref[0] = x + jnp.dot(o, wo_ref[...], preferred_element_type=f32)


def mem_attention(y, g, wq, gq, k, v, wo, parts=(), w_mix=None):
    bf16 = jnp.bfloat16
    B, S, D = y.shape
    M = k.shape[1]
    ts = min(MEM_ROWS, S)
    assert S % ts == 0
    offs = np.cumsum([0] + [p.shape[-1] for p in parts])
    ws = [w_mix[offs[i]:offs[i + 1]].astype(bf16) for i in range(len(parts))]
    const = lambda shape: pl.BlockSpec(shape, lambda b, i: (0,) * len(shape), pipeline_mode=pl.Buffered(1))
    rows = lambda width: pl.BlockSpec((1, ts, width), lambda b, i: (b, i, 0))
    kv_spec = pl.BlockSpec((1, M, XM_W), lambda b, i: (b, 0, 0))
    return pl.pallas_call(
        _mem_attn_kernel,
        grid=(B, S // ts),
        in_specs=[rows(D)] + [rows(p.shape[-1]) for p in parts] + [const(wi.shape) for wi in ws]
                 + [const((1, D)), const((D, XM_W)), const((1, XM_DH)), kv_spec, kv_spec, const((XM_W, D))],
        out_specs=pl.BlockSpec((1, ts, D), lambda b, i: (b, i, 0)),
        out_shape=jax.ShapeDtypeStruct((B, S, D), jnp.float32),
        compiler_params=pltpu.CompilerParams(
            dimension_semantics=("arbitrary", "arbitrary"), vmem_limit_bytes=V7X_VMEM_LIMIT_BYTES),
        name="mem_attn",
    )(y, *parts, *ws, g.reshape(1, D), wq.astype(bf16), gq.reshape(1, XM_DH), k.reshape(B, M, XM_W).astype(bf16),
      v.reshape(B, M, XM_W).astype(bf16), wo.astype(bf16))


DECODE_XPOSE_UNROLL = 8


def _paged_copy(pool_ref, page, buf_ref, slot, p, sem_ref):
    return pltpu.make_async_copy(pool_ref.at[page], buf_ref.at[slot, p], sem_ref.at[slot])


def _paged_pipeline(pt_ref, pools, bufs, sems):
    b = pl.program_id(0)
    n_pages = pt_ref.shape[1]

    def start(seq, slot):
        def body(p, carry):
            for pool, buf, sem in zip(pools, bufs, sems):
                _paged_copy(pool, pt_ref[seq, p], buf, slot, p, sem).start()
            return carry
        lax.fori_loop(0, n_pages, body, 0)

    @pl.when(b == 0)
    def _():
        start(0, 0)

    @pl.when(b + 1 < pl.num_programs(0))
    def _():
        start(b + 1, (b + 1) % 2)

    slot = b % 2

    def wait_body(p, carry):
        for pool, buf, sem in zip(pools, bufs, sems):
            _paged_copy(pool, 0, buf, slot, p, sem).wait()
        return carry
    lax.fori_loop(0, n_pages, wait_body, 0)
    return slot


def _softmax_rows(s, valid):
    e = jnp.exp(s - jnp.max(s, axis=-1, keepdims=True))
    return e / jnp.sum(e, axis=-1, keepdims=True) * valid


def _nsa_decode_cmp_kernel(pt_ref, qbd_ref, wk_ref, wv_ref, pek_ref, pev_ref, w2k_ref, w2v_ref, gk_ref,
                           bc_ref, mw_ref, kpool_ref, vpool_ref, ocmp_ref, sel_ref,
                           kbuf, vbuf, xrm_ref, ksem, vsem, *, n_cmp, n_blocks, n_select):
    f32, bf16 = jnp.float32, jnp.bfloat16
    slot = _paged_pipeline(pt_ref, (kpool_ref, vpool_ref), (kbuf, vbuf), (ksem, vsem))
    n_pages = pt_ref.shape[1]
    n_chunk = n_pages * (PAGE_SIZE // CMP_STRIDE)
    G, DH, HID = NSA_KV, NSA_DH, CMP_HID

    def summaries(buf, w_ref, pe_ref, w2_ref):
        def xpose(i, carry):
            for u in range(DECODE_XPOSE_UNROLL):
                p = i * DECODE_XPOSE_UNROLL + u
                xrm_ref[pl.ds(pl.multiple_of(p * PAGE_SIZE, PAGE_SIZE), PAGE_SIZE), :] = buf[slot, p].T
            return carry
        lax.fori_loop(0, n_pages // DECODE_XPOSE_UNROLL, xpose, 0)
        acc = jnp.zeros((n_chunk, 2 * G * HID), f32)
        for j in range(0, CMP_STRIDE, 2):
            rows = jnp.concatenate([xrm_ref[pl.ds(j + u, n_chunk, stride=CMP_STRIDE), :].astype(bf16)
                                    for u in range(2)], axis=1)
            acc = acc + jnp.dot(rows, w_ref[j // 2], preferred_element_type=f32)
        lo, hi = acc[:, :G * HID], acc[:, G * HID:]
        pre = lo + pltpu.roll(hi, n_chunk - 1, 0) + pe_ref[...]
        hid = pre * jax.nn.sigmoid(pre)
        return jnp.dot(hid.astype(bf16), w2_ref[...], preferred_element_type=f32)

    k_sum = summaries(kbuf, wk_ref, pek_ref, w2k_ref)
    lane = lax.broadcasted_iota(jnp.int32, (1, G * DH), 1)
    sq = k_sum * k_sum
    s_all = jnp.sum(sq, axis=-1, keepdims=True)
    s_g0 = jnp.sum(jnp.where(lane < DH, sq, 0.0), axis=-1, keepdims=True)
    ms = jnp.where(lane < DH, s_g0, s_all - s_g0) * (1.0 / DH)
    k_cmp = (k_sum * lax.rsqrt(ms + EPS) * gk_ref[...]).astype(bf16)
    v_cmp = summaries(vbuf, wv_ref, pev_ref, w2v_ref).astype(bf16)

    qbd = qbd_ref[0]
    H = qbd.shape[0]
    c_row = lax.broadcasted_iota(jnp.int32, (1, n_chunk), 1)
    valid = (c_row < n_cmp).astype(f32)
    s = lax.dot_general(qbd, k_cmp, (((1,), (1,)), ((), ())), preferred_element_type=f32) + bc_ref[...]
    p = _softmax_rows(jnp.where(valid > 0.5, s, NEG_INF), valid)
    ocmp_ref[0] = jnp.dot(p.astype(bf16), v_cmp, preferred_element_type=f32)

    R = H // G
    head = lax.broadcasted_iota(jnp.int32, (H, 1), 0)
    imp = jnp.where(head < R, jnp.sum(p[:R], axis=0, keepdims=True), jnp.sum(p[R:], axis=0, keepdims=True))
    mw = mw_ref[...]
    hi_p = imp.astype(bf16)
    r1 = imp - hi_p.astype(f32)
    mid_p = r1.astype(bf16)
    lo_p = (r1 - mid_p.astype(f32)).astype(bf16)
    p_slc = (jnp.dot(hi_p, mw, preferred_element_type=f32) + jnp.dot(mid_p, mw, preferred_element_type=f32)
             + jnp.dot(lo_p, mw, preferred_element_type=f32))
    nsp = mw_ref.shape[1]
    tb = n_blocks - 1
    jb = lax.broadcasted_iota(jnp.int32, (1, nsp), 1)
    forced = (jb == 0) | (jb == tb) | (jb == tb - 1)
    score = jnp.where(jb <= tb, p_slc + FORCE_SCORE * forced.astype(f32), -1.0)
    j_col = lax.broadcasted_iota(jnp.int32, (nsp, 1), 0)
    sels = []
    for g in range(G):
        row = score[g * R:g * R + 1, :]
        col = jnp.broadcast_to(row, (nsp, nsp)).T
        beats = (col > row) | ((col == row) & (j_col < jb))
        rank = jnp.sum(beats.astype(f32), axis=0, keepdims=True)
        sels.append(jnp.broadcast_to((rank < n_select).astype(f32), (R, nsp)))
    sel_ref[0] = jnp.concatenate(sels, axis=0)


def _nsa_decode_attn_kernel(pt_ref, qbd_ref, sel_ref, ocmp_ref, gate_ref, new_ref, e_ref, bs_ref, bw_ref, b0_ref,
                            kw_ref, vw_ref, kpool_ref, vpool_ref, o_ref, kbuf, vbuf, s_ref, ksem, vsem):
    f32, bf16 = jnp.float32, jnp.bfloat16
    slot = _paged_pipeline(pt_ref, (kpool_ref, vpool_ref), (kbuf, vbuf), (ksem, vsem))
    n_pages = pt_ref.shape[1]
    qbd = qbd_ref[0]
    qf = qbd.astype(f32)
    new = new_ref[0]
    b0 = b0_ref[...]

    def new_score(k_row):
        return jnp.sum(qf * k_row.astype(bf16).astype(f32), axis=-1, keepdims=True) + b0

    for p in range(n_pages):
        s_ref[:, p * PAGE_SIZE:(p + 1) * PAGE_SIZE] = jnp.dot(qbd, kbuf[slot, p].astype(bf16),
                                                              preferred_element_type=f32)
    n_blk_past = e_ref.shape[0]
    sel_tok = jnp.dot(sel_ref[0][:, :n_blk_past].astype(bf16), e_ref[...], preferred_element_type=f32)
    s_past = jnp.where(sel_tok > 0.5, s_ref[...] + bs_ref[...], NEG_INF)
    s_new = new_score(new[0:1])
    m = jnp.maximum(jnp.max(s_past, axis=-1, keepdims=True), s_new)
    e_new = jnp.exp(s_new - m)
    s_ref[...] = jnp.exp(s_past - m)
    denom = jnp.sum(s_ref[...], axis=-1, keepdims=True) + e_new
    acc = e_new.astype(bf16).astype(f32) * new[1:2].astype(bf16).astype(f32)
    for p in range(n_pages):
        pe = s_ref[:, p * PAGE_SIZE:(p + 1) * PAGE_SIZE].astype(bf16)
        acc = acc + lax.dot_general(pe, vbuf[slot, p].astype(bf16), (((1,), (1,)), ((), ())),
                                    preferred_element_type=f32)
    o_slc = acc / denom

    s_w = jnp.dot(qbd, kw_ref[0].astype(bf16), preferred_element_type=f32) + bw_ref[...]
    s_wn = new_score(new[2:3])
    m = jnp.maximum(jnp.max(s_w, axis=-1, keepdims=True), s_wn)
    e_w, e_wn = jnp.exp(s_w - m), jnp.exp(s_wn - m)
    denom = jnp.sum(e_w, axis=-1, keepdims=True) + e_wn
    acc = (lax.dot_general(e_w.astype(bf16), vw_ref[0].astype(bf16), (((1,), (1,)), ((), ())),
                           preferred_element_type=f32)
           + e_wn.astype(bf16).astype(f32) * new[3:4].astype(bf16).astype(f32))
    o_win = acc / denom

    gates = jax.nn.sigmoid(gate_ref[0])
    o_ref[0] = gates[:, 0:1] * ocmp_ref[0] + gates[:, 1:2] * o_slc + gates[:, 2:3] * o_win


def _block_diag_heads(x):
    B, H, DH = x.shape
    g_of_h = jnp.arange(H) // (H // NSA_KV)
    onehot = (g_of_h[:, None] == jnp.arange(NSA_KV)[None, :]).astype(x.dtype)
    return (x[:, :, None, :] * onehot[None, :, :, None]).reshape(B, H, NSA_KV * DH)


def nsa_decode(qn, ks, vs, kw, vw, g_pre, gate_b, rel_bias, page_table, cmp_k_pool, cmp_v_pool,
               slc_k_pool, slc_v_pool, win_k, win_v, pe_k, w1_k, w2_k, pe_v, w1_v, w2_v, gk_cmp):
    f32, bf16 = jnp.float32, jnp.bfloat16
    B, n_pages = page_table.shape
    G, R, DH, H = NSA_KV, NSA_R, NSA_DH, NSA_H
    GD = G * DH
    past = n_pages * PAGE_SIZE
    n_chunk = past // CMP_STRIDE
    n_cmp = (past + 1 - CMP_LEN) // CMP_STRIDE + 1
    n_blocks = -(-(past + 1) // SLC_BLOCK)
    n_blk_past = past // SLC_BLOCK
    nsp = -(-n_blocks // V7X_LANES) * V7X_LANES
    n_win = win_k.shape[1]
    assert GD == V7X_LANES and PAGE_SIZE == V7X_LANES and n_blocks >= N_SELECT and n_win == WINDOW

    def pool_view(pool):
        return jnp.transpose(pool, (0, 2, 3, 1)).reshape(pool.shape[0], GD, PAGE_SIZE)
    kc_pool, vc_pool, ks_pool, vs_pool = (pool_view(a) for a in (cmp_k_pool, cmp_v_pool, slc_k_pool, slc_v_pool))
    kw_t = jnp.transpose(win_k, (0, 2, 3, 1)).reshape(B, GD, n_win)
    vw_t = jnp.transpose(win_v, (0, 2, 3, 1)).reshape(B, GD, n_win)

    qbd = _block_diag_heads((qn[:, 0] * NSA_SCALE)).astype(bf16)
    eye = jnp.eye(G, dtype=f32)

    def chunk_weights(w1):
        def bd(w):
            return jnp.einsum('jdh,gk->jgdkh', w, eye).reshape(CMP_STRIDE, GD, G * CMP_HID)
        w = jnp.concatenate([bd(w1[:CMP_STRIDE]), bd(w1[CMP_STRIDE:])], axis=-1)
        return w.reshape(CMP_STRIDE // 2, 2 * GD, 2 * G * CMP_HID).astype(bf16)

    def pe_term(pe, w1):
        return jnp.tile(jnp.einsum('jd,jdh->h', pe, w1), G).reshape(1, G * CMP_HID)

    def w2_bd(w2):
        return jnp.einsum('hd,gk->ghkd', w2, eye).reshape(G * CMP_HID, GD).astype(bf16)

    c_i = jnp.arange(n_chunk)
    bias_c = _bias_lookup(rel_bias, past - (c_i * CMP_STRIDE + CMP_LEN - 1)).T
    j_i = jnp.arange(nsp)[None, :]
    mw = sum(w * ((c_i[:, None] == SLC_RATIO * j_i + k - 1) & (j_i < n_blocks))
             for k, w in enumerate(SLC_OVERLAP_W)).astype(bf16)
    tok = jnp.arange(past)
    expand = (tok[None, :] // SLC_BLOCK == jnp.arange(n_blk_past)[:, None]).astype(bf16)
    bias_s = _bias_lookup(rel_bias, past - tok).T
    bias_w = _bias_lookup(rel_bias, n_win - jnp.arange(n_win)).T
    bias_0 = _bias_lookup(rel_bias, jnp.zeros((1,), jnp.int32)).T
    new_rows = jnp.stack([a.reshape(B, GD) for a in (ks, vs, kw, vw)], axis=1)
    gate_in = g_pre.reshape(B, H, 3) + gate_b

    const = lambda shape: pl.BlockSpec(shape, lambda b, pt: (0,) * len(shape))
    per_seq = lambda shape: pl.BlockSpec((1,) + shape, lambda b, pt: (b,) + (0,) * len(shape))
    any_spec = pl.BlockSpec(memory_space=pl.ANY)
    page_buf = pltpu.VMEM((2, n_pages, GD, PAGE_SIZE), f32)
    params = pltpu.CompilerParams(dimension_semantics=("arbitrary",), vmem_limit_bytes=V7X_VMEM_LIMIT_BYTES)

    o_cmp, sel = pl.pallas_call(
        functools.partial(_nsa_decode_cmp_kernel, n_cmp=n_cmp, n_blocks=n_blocks, n_select=min(N_SELECT, n_blocks)),
        grid_spec=pltpu.PrefetchScalarGridSpec(
            num_scalar_prefetch=1, grid=(B,),
            in_specs=[per_seq((H, GD)),
                      const((CMP_STRIDE // 2, 2 * GD, 2 * G * CMP_HID)),
                      const((CMP_STRIDE // 2, 2 * GD, 2 * G * CMP_HID)),
                      const((1, G * CMP_HID)), const((1, G * CMP_HID)),
                      const((G * CMP_HID, GD)), const((G * CMP_HID, GD)), const((1, GD)),
                      const((H, n_chunk)), const((n_chunk, nsp)), any_spec, any_spec],
            out_specs=[per_seq((H, GD)), per_seq((H, nsp))],
            scratch_shapes=[page_buf, page_buf, pltpu.VMEM((past, GD), f32),
                            pltpu.SemaphoreType.DMA((2,)), pltpu.SemaphoreType.DMA((2,))]),
        out_shape=[jax.ShapeDtypeStruct((B, H, GD), f32), jax.ShapeDtypeStruct((B, H, nsp), f32)],
        compiler_params=params, name="nsa_decode_cmp",
    )(page_table, qbd, chunk_weights(w1_k), chunk_weights(w1_v), pe_term(pe_k, w1_k), pe_term(pe_v, w1_v),
      w2_bd(w2_k), w2_bd(w2_v), jnp.tile(gk_cmp, G).reshape(1, GD), bias_c, mw, kc_pool, vc_pool)

    out = pl.pallas_call(
        _nsa_decode_attn_kernel,
        grid_spec=pltpu.PrefetchScalarGridSpec(
            num_scalar_prefetch=1, grid=(B,),
            in_specs=[per_seq((H, GD)), per_seq((H, nsp)), per_seq((H, GD)), per_seq((H, 3)), per_seq((4, GD)),
                      const((n_blk_past, past)), const((H, past)), const((H, n_win)), const((H, 1)),
                      per_seq((GD, n_win)), per_seq((GD, n_win)), any_spec, any_spec],
            out_specs=per_seq((H, GD)),
            scratch_shapes=[page_buf, page_buf, pltpu.VMEM((H, past), f32),
                            pltpu.SemaphoreType.DMA((2,)), pltpu.SemaphoreType.DMA((2,))]),
        out_shape=jax.ShapeDtypeStruct((B, H, GD), f32),
        compiler_params=params, name="nsa_decode_attn",
    )(page_table, qbd, sel, o_cmp, gate_in, new_rows, expand, bias_s, bias_w, bias_0, kw_t, vw_t, ks_pool, vs_pool)

    out = out.reshape(B, G, R, G, DH)
    h_b = jnp.stack([out[:, g, :, g, :] for g in range(G)], axis=1)
    return h_b.reshape(B, 1, NSA_W)


def _mla_decode_kernel(pt_ref, ql_ref, qr_ref, new_ref, newr_ref, cpool_ref, rpool_ref, o_ref,
                       cbuf, rbuf, cb16, s_ref, csem, rsem):
    f32, bf16 = jnp.float32, jnp.bfloat16
    slot = _paged_pipeline(pt_ref, (cpool_ref, rpool_ref), (cbuf, rbuf), (csem, rsem))
    n_pages = pt_ref.shape[1]
    ql, qr = ql_ref[0], qr_ref[0]
    for p in range(n_pages):
        c16 = cbuf[slot, p].astype(bf16)
        cb16[p] = c16
        s_ref[:, p * PAGE_SIZE:(p + 1) * PAGE_SIZE] = (
            lax.dot_general(ql, c16, (((1,), (1,)), ((), ())), preferred_element_type=f32)
            + jnp.dot(qr, rbuf[slot, p].astype(bf16), preferred_element_type=f32)) * MLA_SCALE
    c_new = new_ref[0].astype(bf16).astype(f32)
    r_new = newr_ref[0].astype(bf16).astype(f32)
    s_new = (jnp.sum(ql.astype(f32) * c_new, axis=-1, keepdims=True)
             + jnp.sum(qr.astype(f32) * r_new, axis=-1, keepdims=True)) * MLA_SCALE
    s_past = s_ref[...]
    m = jnp.maximum(jnp.max(s_past, axis=-1, keepdims=True), s_new)
    e_new = jnp.exp(s_new - m)
    s_ref[...] = jnp.exp(s_past - m)
    denom = jnp.sum(s_ref[...], axis=-1, keepdims=True) + e_new
    inv = 1.0 / denom
    acc = (e_new * inv).astype(bf16).astype(f32) * c_new
    for p in range(n_pages):
        pe = (s_ref[:, p * PAGE_SIZE:(p + 1) * PAGE_SIZE] * inv).astype(bf16)
        acc = acc + jnp.dot(pe, cb16[p], preferred_element_type=f32)
    o_ref[0] = acc


def mla_decode_attention(q_lat, q_rope, ckv_new, kr_new, page_table, ckv_pool, krope_pool):
    f32, bf16 = jnp.float32, jnp.bfloat16
    B, n_pages = page_table.shape
    H = q_lat.shape[1]
    past = n_pages * PAGE_SIZE
    rpool_t = jnp.transpose(krope_pool, (0, 2, 1))
    per_seq = lambda shape: pl.BlockSpec((1,) + shape, lambda b, pt: (b,) + (0,) * len(shape))
    any_spec = pl.BlockSpec(memory_space=pl.ANY)
    return pl.pallas_call(
        _mla_decode_kernel,
        grid_spec=pltpu.PrefetchScalarGridSpec(
            num_scalar_prefetch=1, grid=(B,),
            in_specs=[per_seq((H, KV_LORA)), per_seq((H, MLA_ROPE)), per_seq((1, KV_LORA)), per_seq((1, MLA_ROPE)),
                      any_spec, any_spec],
            out_specs=per_seq((H, KV_LORA)),
            scratch_shapes=[pltpu.VMEM((2, n_pages, PAGE_SIZE, KV_LORA), f32),
                            pltpu.VMEM((2, n_pages, MLA_ROPE, PAGE_SIZE), f32),
                            pltpu.VMEM((n_pages, PAGE_SIZE, KV_LORA), bf16),
                            pltpu.VMEM((H, past), f32),
                            pltpu.SemaphoreType.DMA((2,)), pltpu.SemaphoreType.DMA((2,))]),
        out_shape=jax.ShapeDtypeStruct((B, H, KV_LORA), f32),
        compiler_params=pltpu.CompilerParams(dimension_semantics=("arbitrary",),
                                             vmem_limit_bytes=V7X_VMEM_LIMIT_BYTES),
        name="mla_decode",
    )(page_table, q_lat.astype(bf16), q_rope.astype(bf16), ckv_new.reshape(B, 1, KV_LORA),
      kr_new.reshape(B, 1, MLA_ROPE), ckv_pool, rpool_t)


def split_cols(a, sizes):
    idx = [int(s) for s in np.cumsum(sizes)[:-1]]
    return jnp.split(a, idx, axis=-1)


def rms_norm(x, g):
    xf = x.astype(jnp.float32)
    y = xf * lax.rsqrt(jnp.mean(xf * xf, axis=-1, keepdims=True) + EPS)
    return (y * g.astype(jnp.float32)).astype(x.dtype)


def t5_bucket(dist):
    n = jnp.maximum(dist, 0)
    exact = REL_BUCKETS // 2
    nf = jnp.maximum(n, exact).astype(jnp.float32)
    large = exact + (jnp.log(nf / exact) / math.log(REL_MAX_DIST / exact) * (REL_BUCKETS - exact)).astype(jnp.int32)
    return jnp.where(n < exact, n, jnp.minimum(large, REL_BUCKETS - 1))


def apply_rope(x, pos):
    half = x.shape[-1] // 2
    inv = ROPE_THETA ** (-jnp.arange(half, dtype=jnp.float32) / half)
    ang = pos.astype(jnp.float32)[:, None] * inv[None, :]
    ang = ang.reshape(ang.shape[:1] + (1,) * (x.ndim - 3) + (half,))
    cos, sin = jnp.cos(ang).astype(x.dtype), jnp.sin(ang).astype(x.dtype)
    x1, x2 = x[..., :half], x[..., half:]
    return jnp.concatenate([x1 * cos - x2 * sin, x1 * sin + x2 * cos], axis=-1)


def causal_conv(u, buf, w, b):
    S = u.shape[1]
    full = jnp.concatenate([buf.astype(u.dtype), u], axis=1)
    out = b + sum(full[:, j:j + S] * w[j] for j in range(CONV_K))
    return out, full[:, S:]


def mlstm_chunkwise(q, k, v, i_pre, logf, C0, n0, m0):
    f32 = jnp.float32
    q, k, v, i_pre, logf = (a.astype(f32) for a in (q, k, v, i_pre, logf))
    B, H, S, D = q.shape
    L = MLSTM_CHUNK if S % MLSTM_CHUNK == 0 else S
    NC = S // L

    def chunks(a):
        return jnp.moveaxis(a.reshape((B, H, NC, L) + a.shape[3:]), 2, 0)

    causal = jnp.tril(jnp.ones((L, L), dtype=bool))

    def step(carry, inp):
        C, n, m = carry
        qc, kc, vc, ic, fc = inp
        b = jnp.cumsum(fc, axis=-1)
        g = b + m[..., None]
        dmat = jnp.where(causal, b[..., :, None] - b[..., None, :] + ic[..., None, :], -jnp.inf)
        mt = jnp.maximum(g, jnp.max(dmat, axis=-1))
        inter = jnp.exp(g - mt)
        sqk = jnp.einsum('bhtd,bhsd->bhts', qc, kc) * jnp.exp(dmat - mt[..., None])
        num = inter[..., None] * jnp.einsum('bhvd,bhtd->bhtv', C, qc) + jnp.einsum('bhts,bhsv->bhtv', sqk, vc)
        den = inter * jnp.einsum('bhd,bhtd->bht', n, qc) + jnp.sum(sqk, axis=-1)
        h = num / jnp.maximum(jnp.abs(den), jnp.exp(-mt))[..., None]
        b_end = b[..., -1]
        w_log = b_end[..., None] - b + ic
        m_new = jnp.maximum(b_end + m, jnp.max(w_log, axis=-1))
        decay = jnp.exp(b_end + m - m_new)
        w_in = jnp.exp(w_log - m_new[..., None])
        C_new = decay[..., None, None] * C + jnp.einsum('bhs,bhsv,bhsd->bhvd', w_in, vc, kc)
        n_new = decay[..., None] * n + jnp.einsum('bhs,bhsd->bhd', w_in, kc)
        return (C_new, n_new, m_new), h

    (C1, n1, m1), hs = lax.scan(step, (C0.astype(f32), n0.astype(f32), m0.astype(f32)),
                                tuple(chunks(a) for a in (q, k, v, i_pre, logf)))
    return jnp.moveaxis(hs, 0, 2).reshape(B, H, S, D), C1, n1, m1


def to_chunks(a):
    B, T = a.shape[:2]
    pad = (-T) % CMP_STRIDE
    a = jnp.pad(a, ((0, 0), (0, pad), (0, 0), (0, 0)))
    return a.reshape((B, (T + pad) // CMP_STRIDE, CMP_STRIDE) + a.shape[2:])


def cmp_summaries(chunk_list, T, pe, w1, w2):
    lo = jnp.concatenate([jnp.einsum('bcjgd,jdh->bcgh', r, w1[:CMP_STRIDE]) for r in chunk_list], axis=1)
    hi = jnp.concatenate([jnp.einsum('bcjgd,jdh->bcgh', r, w1[CMP_STRIDE:]) for r in chunk_list], axis=1)
    n_cmp = (T - CMP_LEN) // CMP_STRIDE + 1
    hid = jax.nn.silu(lo[:, :n_cmp] + hi[:, 1:n_cmp + 1] + jnp.einsum('jd,jdh->h', pe, w1))
    return hid @ w2


def even_mixer_prompt(y, g_mix, w_in, conv_w, conv_b, ml_wq, ml_wk, ml_b_i, ml_b_f, ml_out_g,
                      nsa_gq, nsa_gk_cmp, nsa_gk_slc, nsa_gk_win, pe_k, w1_k, w2_k, pe_v, w1_v, w2_v, gate_b, rel_bias):
    B, S, _ = y.shape
    assert S >= CONV_K - 1
    p = even_in_proj_prompt(y, g_mix, w_in, nsa_gq, nsa_gk_slc, nsa_gk_win)
    h_a, C1, n1, m1 = mlstm_prompt(p['u'], p['v'], p['o'], conv_w, conv_b, ml_wq, ml_wk, ml_out_g,
                                   p['i_pre'] + ml_b_i, jax.nn.log_sigmoid(p['f_pre'] + ml_b_f))
    k_cmp = rms_norm(cmp_summaries([to_chunks(p['kc'])], S, pe_k, w1_k, w2_k), nsa_gk_cmp)
    v_cmp = cmp_summaries([to_chunks(p['vc'])], S, pe_v, w1_v, w2_v)
    h_b = nsa_prompt(p['q5'], k_cmp, v_cmp, p['kst'], p['vst'], p['kwt'], p['vwt'], p['g_pre'], gate_b, rel_bias)
    nb = min(WINDOW, S)
    new = dict(C=C1, n=n1, m=m1, conv=p['u'][:, S - (CONV_K - 1):], cmp_k=p['kc'], cmp_v=p['vc'],
               slc_k=p['ks'], slc_v=p['vs'], win_k=p['kw'][:, S - nb:], win_v=p['vw'][:, S - nb:])
    return [h_a, h_b], new


def even_mixer(xn, pos0, past, w_in, w_out, conv_w, conv_b, ml_wq, ml_wk, ml_b_i, ml_b_f, ml_out_g,
               nsa_gq, nsa_gk_cmp, nsa_gk_slc, nsa_gk_win, pe_k, w1_k, w2_k, pe_v, w1_v, w2_v, gate_b, rel_bias):
    B, S, _ = xn.shape
    dt = xn.dtype
    (u, v_m, o_pre, i_pre, f_pre, q, kc, vc, ks, vs, kw, vw, g_pre) = split_cols(xn @ w_in, EV_SPLITS)
    assert S == 1
    conv_buf, C0, n0, m0 = past['conv'], past['C'], past['n'], past['m']
    c, conv_new = causal_conv(u, conv_buf, conv_w, conv_b)
    ch = jax.nn.silu(c).reshape(B, S, ML_H, ML_DH)
    qm = jnp.einsum('bshd,hde->bhse', ch, ml_wq)
    km = jnp.einsum('bshd,hde->bhse', ch, ml_wk) * (ML_DH ** -0.5)
    vm = jnp.transpose(v_m.reshape(B, S, ML_H, ML_DH), (0, 2, 1, 3))
    ig = jnp.transpose(i_pre + ml_b_i, (0, 2, 1))
    lf = jax.nn.log_sigmoid(jnp.transpose(f_pre + ml_b_f, (0, 2, 1)).astype(jnp.float32))
    hm, C1, n1, m1 = mlstm_chunkwise(qm, km, vm, ig, lf, C0, n0, m0)
    hm = rms_norm(jnp.transpose(hm, (0, 2, 1, 3)).astype(dt), ml_out_g)
    h_a = (hm * jax.nn.sigmoid(o_pre).reshape(B, S, ML_H, ML_DH)).reshape(B, S, ML_W)

    q = rms_norm(q.reshape(B, S, NSA_H, NSA_DH), nsa_gq)
    kv_shape = (B, S, NSA_KV, NSA_DH)
    kc, vc, vs, vw = (a.reshape(kv_shape) for a in (kc, vc, vs, vw))
    ks = rms_norm(ks.reshape(kv_shape), nsa_gk_slc)
    kw = rms_norm(kw.reshape(kv_shape), nsa_gk_win)
    e = past['e']
    h_b = nsa_decode(q, ks, vs, kw, vw, g_pre, gate_b, rel_bias, past['page_table'],
                     past['cmp_k'][e], past['cmp_v'][e], past['slc_k'][e], past['slc_v'][e],
                     past['win_k'], past['win_v'], pe_k, w1_k, w2_k, pe_v, w1_v, w2_v, nsa_gk_cmp)
    win_k_new = jnp.concatenate([past['win_k'][:, S:], kw], axis=1)
    win_v_new = jnp.concatenate([past['win_v'][:, S:], vw], axis=1)
    out = jnp.concatenate([h_a, h_b], axis=-1) @ w_out
    new = dict(C=C1.astype(dt), n=n1.astype(dt), m=m1.astype(dt), conv=conv_new,
               cmp_k=kc, cmp_v=vc, slc_k=ks, slc_v=vs, win_k=win_k_new, win_v=win_v_new)
    return out, new


def odd_mixer_prompt(y, g_mix, w_in, g_cq, w_uq, g_q, g_ckv, g_kr, w_uk, w_uv):
    bf16 = jnp.bfloat16
    tq = jnp.arange(y.shape[1])
    cq16, ckv, ckv16, kr, kr16 = odd_in_proj_prompt(y, g_mix, w_in, g_cq, g_ckv, g_kr, tq)
    o = mla_prompt_attention(cq16, w_uq, g_q, tq, ckv16, kr16, jnp.transpose(w_uk, (1, 0, 2)).astype(bf16),
                             jnp.transpose(w_uv, (1, 0, 2)).astype(bf16))
    return [o], dict(ckv=ckv, krope=kr)


def odd_mixer(xn, pos0, past, w_in, g_cq, w_uq, g_q, g_ckv, g_kr, w_uk, w_uv, w_out):
    B, S, _ = xn.shape
    assert S == 1
    cq, ckv, kr = split_cols(xn @ w_in, OD_SPLITS)
    tq = pos0 + jnp.arange(S)
    ckv = rms_norm(ckv, g_ckv)
    kr = apply_rope(rms_norm(kr, g_kr), tq)
    e = past['e']
    q = rms_norm((rms_norm(cq, g_cq) @ w_uq).reshape(B, S, MLA_H, MLA_NOPE + MLA_ROPE), g_q)
    q_nope = q[..., :MLA_NOPE]
    q_rope = apply_rope(q[..., MLA_NOPE:], tq)
    q_lat = jnp.einsum('bqhn,chn->bqhc', q_nope, w_uk)
    o_lat = mla_decode_attention(q_lat[:, 0], q_rope[:, 0], ckv[:, 0], kr[:, 0], past['page_table'],
                                 past['ckv'][e], past['krope'][e])[:, None]
    o = jnp.einsum('bqhc,chv->bqhv', o_lat, w_uv).reshape(B, S, MLA_H * MLA_V)
    return o @ w_out, dict(ckv=ckv, krope=kr)


def mem_kv(mem, g_mem, wk, wv, gk):
    B, M, _ = mem.shape
    m = rms_norm(mem, g_mem)
    k = rms_norm((m @ wk).reshape(B, M, XM_H, XM_DH), gk)
    v = (m @ wv).reshape(B, M, XM_H, XM_DH)
    return k, v


def mem_attend(xn, k, v, wq, gq, wo):
    B, S, _ = xn.shape
    q = rms_norm((xn @ wq).reshape(B, S, XM_H, XM_DH), gq)
    s = jnp.einsum('bshd,bmhd->bhsm', q, k.astype(q.dtype)).astype(jnp.float32) * (XM_DH ** -0.5)
    p = jax.nn.softmax(s, axis=-1)
    return jnp.einsum('bhsm,bmhd->bshd', p.astype(xn.dtype), v.astype(xn.dtype)).reshape(B, S, XM_W) @ wo


def stack_key(lst, name):
    return jnp.stack([d[name] for d in lst])


def kernel(x_prompt, x_sample, mem_prompt,
           state_ml_C, state_ml_n, state_ml_m, state_ml_conv,
           cache_cmp_k, cache_cmp_v, cache_slc_k, cache_slc_v, cache_win_k, cache_win_v,
           cache_mla_ckv, cache_mla_krope, cache_mem_k, cache_mem_v, page_table,
           rel_bias, ffn1_norm, ffn1_wg, ffn1_wu, ffn1_wd, mix_norm,
           xm_norm, xm_mem_norm, xm_wq, xm_wk, xm_wv, xm_wo, xm_gq, xm_gk,
           ffn2_norm, ffn2_wg, ffn2_wu, ffn2_wd,
           ev_w_in, ev_w_out, ml_conv_w, ml_conv_b, ml_wq, ml_wk, ml_b_i, ml_b_f, ml_out_g,
           nsa_gq, nsa_gk_cmp, nsa_gk_slc, nsa_gk_win, cmp_pe_k, cmp_w1_k, cmp_w2_k,
           cmp_pe_v, cmp_w1_v, cmp_w2_v, nsa_gate_b,
           od_w_in, mla_g_cq, mla_w_uq, mla_g_q, mla_g_ckv, mla_g_kr, mla_w_uk, mla_w_uv, od_w_out):
    past_len = page_table.shape[1] * PAGE_SIZE
    bf = jnp.bfloat16
    ffn_w = [[(n[layer], wg[layer].astype(bf), wu[layer].astype(bf), wd[layer].astype(bf))
              for n, wg, wu, wd in ((ffn1_norm, ffn1_wg, ffn1_wu, ffn1_wd), (ffn2_norm, ffn2_wg, ffn2_wu, ffn2_wd))]
             for layer in range(DEPTH)]

    def run_group(y, prompt):
        ev, od, memk, memv = [], [], [], []
        for layer in range(DEPTH):
            y = swiglu_half(y, *ffn_w[layer][0])
            xn = None if prompt else rms_norm(y, mix_norm[layer])
            if layer % 2 == 0:
                e = layer // 2
                ew = dict(w_in=ev_w_in[e], w_out=ev_w_out[e], conv_w=ml_conv_w[e], conv_b=ml_conv_b[e],
                          ml_wq=ml_wq[e], ml_wk=ml_wk[e], ml_b_i=ml_b_i[e], ml_b_f=ml_b_f[e], ml_out_g=ml_out_g[e],
                          nsa_gq=nsa_gq[e], nsa_gk_cmp=nsa_gk_cmp[e], nsa_gk_slc=nsa_gk_slc[e],
                          nsa_gk_win=nsa_gk_win[e], pe_k=cmp_pe_k[e], w1_k=cmp_w1_k[e], w2_k=cmp_w2_k[e],
                          pe_v=cmp_pe_v[e], w1_v=cmp_w1_v[e], w2_v=cmp_w2_v[e],
                          gate_b=nsa_gate_b[e], rel_bias=rel_bias)
                past = None if prompt else dict(
                    e=e, page_table=page_table, C=state_ml_C[e], n=state_ml_n[e], m=state_ml_m[e],
                    conv=state_ml_conv[e], cmp_k=cache_cmp_k, cmp_v=cache_cmp_v,
                    slc_k=cache_slc_k, slc_v=cache_slc_v, win_k=cache_win_k[e], win_v=cache_win_v[e])
                if prompt:
                    w_mix = ew.pop('w_out')
                    parts, st = even_mixer_prompt(y, mix_norm[layer], **ew)
                else:
                    h, st = even_mixer(xn, past_len, past, **ew)
                    y = y + h
                ev.append(st)
            else:
                o = layer // 2
                ow = dict(w_in=od_w_in[o], g_cq=mla_g_cq[o], w_uq=mla_w_uq[o], g_q=mla_g_q[o], g_ckv=mla_g_ckv[o],
                          g_kr=mla_g_kr[o], w_uk=mla_w_uk[o], w_uv=mla_w_uv[o], w_out=od_w_out[o])
                past = None if prompt else dict(e=o, page_table=page_table, ckv=cache_mla_ckv, krope=cache_mla_krope)
                if prompt:
                    w_mix = ow.pop('w_out')
                    parts, st = odd_mixer_prompt(y, mix_norm[layer], **ow)
                else:
                    h, st = odd_mixer(xn, past_len, past, **ow)
                    y = y + h
                od.append(st)
            if prompt:
                mk, mv = mem_kv(mem_prompt, xm_mem_norm[layer], xm_wk[layer], xm_wv[layer], xm_gk[layer])
                memk.append(mk)
                memv.append(mv)
                y = mem_attention(y, xm_norm[layer], xm_wq[layer], xm_gq[layer], mk, mv, xm_wo[layer],
                                  parts=parts, w_mix=w_mix)
            else:
                y = y + mem_attend(rms_norm(y, xm_norm[layer]), cache_mem_k[layer], cache_mem_v[layer],
                                   xm_wq[layer], xm_gq[layer], xm_wo[layer])
            y = swiglu_half(y, *ffn_w[layer][1])
        return y, ev, od, memk, memv

    ys, ev_s, od_s, _, _ = run_group(x_sample, False)
    yp, ev_p, od_p, memk_p, memv_p = run_group(x_prompt, True)
    return (yp, ys,
            stack_key(ev_p, 'C'), stack_key(ev_p, 'n'), stack_key(ev_p, 'm'), stack_key(ev_p, 'conv'),
            stack_key(ev_p, 'cmp_k'), stack_key(ev_p, 'cmp_v'), stack_key(ev_p, 'slc_k'), stack_key(ev_p, 'slc_v'),
            stack_key(ev_p, 'win_k'), stack_key(ev_p, 'win_v'),
            stack_key(od_p, 'ckv'), stack_key(od_p, 'krope'),
            jnp.stack(memk_p), jnp.stack(memv_p),
            stack_key(ev_s, 'C'), stack_key(ev_s, 'n'), stack_key(ev_s, 'm'), stack_key(ev_s, 'conv'),
            stack_key(ev_s, 'cmp_k'), stack_key(ev_s, 'cmp_v'), stack_key(ev_s, 'slc_k'), stack_key(ev_s, 'slc_v'),
            stack_key(ev_s, 'win_k'), stack_key(ev_s, 'win_v'),
            stack_key(od_s, 'ckv'), stack_key(od_s, 'krope'))
```

```python
import functools
import math

import jax
import jax.numpy as jnp
import numpy as np
from jax import lax
from jax.experimental import pallas as pl
from jax.experimental.pallas import tpu as pltpu

D_MODEL = 1024
DEPTH = 2
PAGE_SIZE = 128
EPS = 1e-6
NEG_INF = -1e30
D_FF = 2816
ML_H = 4
ML_DH = 128
ML_W = ML_H * ML_DH
CONV_K = 4
MLSTM_CHUNK = 64
NSA_H = 8
NSA_KV = 2
NSA_R = NSA_H // NSA_KV
NSA_DH = 64
NSA_W = NSA_H * NSA_DH
NSA_KVW = NSA_KV * NSA_DH
NSA_SCALE = NSA_DH ** -0.5
CMP_STRIDE = 16
CMP_LEN = 2 * CMP_STRIDE
CMP_HID = 2 * NSA_DH
SLC_BLOCK = 64
SLC_RATIO = SLC_BLOCK // CMP_STRIDE
SLC_OVERLAP_W = (1.0, 2.0, 2.0, 2.0, 1.0)
N_SELECT = 16
WINDOW = 512
SLC_QBLOCK = 32
WIN_QBLOCK = 128
FORCE_SCORE = 1e6
REL_BUCKETS = 32
REL_MAX_DIST = 128
MLA_H = 16
MLA_NOPE = 64
MLA_ROPE = 32
MLA_V = 64
Q_LORA = 384
KV_LORA = 256
MLA_SCALE = (MLA_NOPE + MLA_ROPE) ** -0.5
MLA_QBLOCK = 128
ROPE_THETA = 10000.0
MEM_LEN = 256
XM_H = 4
XM_DH = 128
XM_W = XM_H * XM_DH
EV_SPLITS = (ML_W, ML_W, ML_W, ML_H, ML_H, NSA_W) + (NSA_KVW,) * 6 + (NSA_H * 3,)
OD_SPLITS = (Q_LORA, KV_LORA, MLA_ROPE)

V7X_LANES = 128
V7X_VMEM_LIMIT_BYTES = 56 * 1024 * 1024
FFN_CHUNK = 256
FFN_ROWS = 512


def _ffn_kernel(x_ref, g_ref, wg_ref, wu_ref, wd_ref, o_ref, act_ref):
    x = x_ref[...]
    h = x * lax.rsqrt(jnp.mean(x * x, axis=-1, keepdims=True) + EPS) * g_ref[...]
    hb = h.astype(jnp.bfloat16)
    for c in range(D_FF // FFN_CHUNK):
        cols = slice(c * FFN_CHUNK, (c + 1) * FFN_CHUNK)
        gate = jnp.dot(hb, wg_ref[:, cols], preferred_element_type=jnp.float32)
        up = jnp.dot(hb, wu_ref[:, cols], preferred_element_type=jnp.float32)
        act_ref[:, cols] = (gate * jax.nn.sigmoid(gate) * up).astype(jnp.bfloat16)
    o_ref[...] = x + 0.5 * jnp.dot(act_ref[...], wd_ref[...], preferred_element_type=jnp.float32)


def ffn_half(x2d, g, wg, wu, wd):
    m = x2d.shape[0]
    tm = min(FFN_ROWS, m)
    assert m % tm == 0
    resident = functools.partial(pl.BlockSpec, pipeline_mode=pl.Buffered(1))
    return pl.pallas_call(
        _ffn_kernel,
        grid=(m // tm,),
        in_specs=[
            pl.BlockSpec((tm, D_MODEL), lambda i: (i, 0)),
            resident((1, D_MODEL), lambda i: (0, 0)),
            resident((D_MODEL, D_FF), lambda i: (0, 0)),
            resident((D_MODEL, D_FF), lambda i: (0, 0)),
            resident((D_FF, D_MODEL), lambda i: (0, 0)),
        ],
        out_specs=pl.BlockSpec((tm, D_MODEL), lambda i: (i, 0)),
        out_shape=jax.ShapeDtypeStruct((m, D_MODEL), jnp.float32),
        scratch_shapes=[pltpu.VMEM((tm, D_FF), jnp.bfloat16)],
        compiler_params=pltpu.CompilerParams(
            dimension_semantics=("arbitrary",), vmem_limit_bytes=V7X_VMEM_LIMIT_BYTES),
        name="ffn_half",
    )(x2d, g.reshape(1, D_MODEL), wg, wu, wd)


def swiglu_half(x, g, wg, wu, wd):
    shp = x.shape
    return ffn_half(x.reshape(-1, D_MODEL), g, wg, wu, wd).reshape(shp)


EV_PROJ_ROWS = 512
EV_PROJ_GROUPS = (("u", ML_W), ("v", ML_W), ("o", ML_W), ("q", NSA_W), ("kc", NSA_KVW), ("vc", NSA_KVW),
                  ("ks", NSA_KVW), ("vs", NSA_KVW), ("kw", NSA_KVW), ("vw", NSA_KVW), ("small", V7X_LANES))


def _even_proj_kernel(x_ref, g_ref, w_ref, gq_ref, gks_ref, gkw_ref,
                      u_ref, v_ref, o_ref, small_ref, kc_ref, vc_ref, ks_ref, vs_ref, kw_ref, vw_ref,
                      q5_ref, kst_ref, vst_ref, kwt_ref, vwt_ref):
    f32, bf16 = jnp.float32, jnp.bfloat16
    x = x_ref[0]
    xn = (x * lax.rsqrt(jnp.mean(x * x, axis=-1, keepdims=True) + EPS) * g_ref[...]).astype(bf16)
    y = jnp.dot(xn, w_ref[...], preferred_element_type=f32)
    off, cols = 0, {}
    for name, width in EV_PROJ_GROUPS:
        cols[name] = y[:, off:off + width]
        off += width
    lane = lax.broadcasted_iota(jnp.int32, (1, V7X_LANES), 1)
    DH = NSA_DH

    def pair_norm(p, gain):
        sq = p * p
        s_all = jnp.sum(sq, axis=-1, keepdims=True)
        s_lo = jnp.sum(jnp.where(lane < DH, sq, 0.0), axis=-1, keepdims=True)
        ms = jnp.where(lane < DH, s_lo, s_all - s_lo) * (1.0 / DH)
        return p * lax.rsqrt(ms + EPS) * gain

    def put_pair(ref, first, p):
        p = p.astype(bf16)
        ref[0, first] = p[:, :DH]
        ref[0, first + 1] = p[:, DH:]

    u_ref[0], v_ref[0], o_ref[0], small_ref[0] = cols["u"], cols["v"], cols["o"], cols["small"]
    for pr in range(NSA_H // 2):
        qn = pair_norm(cols["q"][:, pr * V7X_LANES:(pr + 1) * V7X_LANES], gq_ref[...])
        put_pair(q5_ref, 2 * pr, qn * NSA_SCALE)
    kc_ref[0], vc_ref[0], vs_ref[0], vw_ref[0] = cols["kc"], cols["vc"], cols["vs"], cols["vw"]
    ks = pair_norm(cols["ks"], gks_ref[...])
    kw = pair_norm(cols["kw"], gkw_ref[...])
    ks_ref[0], kw_ref[0] = ks, kw
    put_pair(kst_ref, 0, ks)
    put_pair(vst_ref, 0, cols["vs"])
    put_pair(kwt_ref, 0, kw)
    put_pair(vwt_ref, 0, cols["vw"])


def even_in_proj_prompt(y, g_mix, w_in, nsa_gq, nsa_gk_slc, nsa_gk_win):
    f32, bf16 = jnp.float32, jnp.bfloat16
    B, S, D = y.shape
    tm = min(EV_PROJ_ROWS, S)
    assert S % tm == 0 and NSA_KVW == V7X_LANES and 2 * ML_H + 3 * NSA_H <= V7X_LANES
    (w_u, w_v, w_o, w_i, w_f, w_q, w_kc, w_vc, w_ks, w_vs, w_kw, w_vw, w_g) = split_cols(w_in, EV_SPLITS)
    w_small = jnp.pad(jnp.concatenate([w_i, w_f, w_g], axis=1), ((0, 0), (0, V7X_LANES - 2 * ML_H - 3 * NSA_H)))
    w = jnp.concatenate([w_u, w_v, w_o, w_q, w_kc, w_vc, w_ks, w_vs, w_kw, w_vw, w_small], axis=1).astype(bf16)
    n_cols = w.shape[1]
    assert n_cols == sum(width for _, width in EV_PROJ_GROUPS)
    const = lambda shape: pl.BlockSpec(shape, lambda b, i: (0,) * len(shape), pipeline_mode=pl.Buffered(1))
    rows = lambda width: pl.BlockSpec((1, tm, width), lambda b, i: (b, i, 0))
    heads = lambda n: pl.BlockSpec((1, n, tm, NSA_DH), lambda b, i: (b, 0, i, 0))
    wide = jax.ShapeDtypeStruct((B, S, ML_W), f32)
    kv = jax.ShapeDtypeStruct((B, S, NSA_KVW), f32)
    kvt = jax.ShapeDtypeStruct((B, NSA_KV, S, NSA_DH), bf16)
    pair_gain = lambda g: jnp.tile(g, 2).reshape(1, V7X_LANES)
    (u, v, o, small, kc, vc, ks, vs, kw, vw, q5, kst, vst, kwt, vwt) = pl.pallas_call(
        _even_proj_kernel,
        grid=(B, S // tm),
        in_specs=[rows(D), const((1, D)), const((D, n_cols)), const((1, V7X_LANES)), const((1, V7X_LANES)),
                  const((1, V7X_LANES))],
        out_specs=[rows(ML_W), rows(ML_W), rows(ML_W), rows(V7X_LANES)] + [rows(NSA_KVW)] * 6
                  + [heads(NSA_H)] + [heads(NSA_KV)] * 4,
        out_shape=[wide, wide, wide, jax.ShapeDtypeStruct((B, S, V7X_LANES), f32)] + [kv] * 6
                  + [jax.ShapeDtypeStruct((B, NSA_H, S, NSA_DH), bf16)] + [kvt] * 4,
        compiler_params=pltpu.CompilerParams(
            dimension_semantics=("arbitrary", "arbitrary"), vmem_limit_bytes=V7X_VMEM_LIMIT_BYTES),
        name="even_in_proj",
    )(y, g_mix.reshape(1, D), w, pair_gain(nsa_gq), pair_gain(nsa_gk_slc), pair_gain(nsa_gk_win))
    kv4 = lambda a: a.reshape(B, S, NSA_KV, NSA_DH)
    return dict(u=u, v=v, o=o, i_pre=small[..., :ML_H], f_pre=small[..., ML_H:2 * ML_H],
                g_pre=small[..., 2 * ML_H:2 * ML_H + 3 * NSA_H], kc=kv4(kc), vc=kv4(vc), ks=kv4(ks), vs=kv4(vs),
                kw=kv4(kw), vw=kv4(vw), q5=q5, kst=kst, vst=vst, kwt=kwt, vwt=vwt)


NSA_TQ = 256
NSA_NEAR = 128


def _nsa_prompt_kernel(q_ref, kc_ref, vc_ref, ks_ref, vs_ref, kw_ref, vw_ref, bc_ref, bn_ref,
                       e_ref, mw_ref, gp_ref, gb_ref, o_ref, *, n_cmp, n_select):
    f32, bf16 = jnp.float32, jnp.bfloat16
    qi = pl.program_id(2)
    R, TQ, DH = q_ref.shape[2:]
    S = ks_ref.shape[2]
    t0 = qi * TQ
    q2 = q_ref[0, 0].reshape(R * TQ, DH)
    t_col = t0 + lax.broadcasted_iota(jnp.int32, (TQ, 1), 0)

    def scores(k):
        s = lax.dot_general(q2, k, (((1,), (1,)), ((), ())), preferred_element_type=f32)
        return s.reshape(R, TQ, k.shape[0])

    ncp = kc_ref.shape[2]
    c_row = lax.broadcasted_iota(jnp.int32, (1, ncp), 1)
    mask_c = (t_col >= c_row * CMP_STRIDE + (CMP_LEN - 1)) & (c_row < n_cmp)
    s = jnp.where(mask_c[None], scores(kc_ref[0, 0]) + bc_ref[0], NEG_INF)
    e = jnp.exp(s - jnp.max(s, axis=-1, keepdims=True))
    p = e / jnp.sum(e, axis=-1, keepdims=True) * mask_c.astype(f32)[None]
    o_cmp = jnp.dot(p.reshape(R * TQ, ncp).astype(bf16), vc_ref[0, 0],
                    preferred_element_type=f32).reshape(R, TQ, DH)
    imp = jnp.sum(p, axis=0)

    mw = mw_ref[...]
    hi = imp.astype(bf16)
    r1 = imp - hi.astype(f32)
    mid = r1.astype(bf16)
    lo = (r1 - mid.astype(f32)).astype(bf16)
    p_slc = (jnp.dot(hi, mw, preferred_element_type=f32) + jnp.dot(mid, mw, preferred_element_type=f32)
             + jnp.dot(lo, mw, preferred_element_type=f32))
    ns = mw_ref.shape[1]
    jb = lax.broadcasted_iota(jnp.int32, (1, ns), 1)
    tb = jnp.right_shift(t_col, int(math.log2(SLC_BLOCK)))
    forced = (jb == 0) | (jb == tb) | (jb == tb - 1)
    score = jnp.where(jb <= tb, p_slc + FORCE_SCORE * forced.astype(f32), -1.0)
    rank = jnp.zeros((TQ, ns), f32)
    for j in range(ns):
        col = score[:, j:j + 1]
        rank = rank + ((col > score) | ((col == score) & (jb > j))).astype(f32)
    sel = (rank < n_select).astype(bf16)
    gates = jax.nn.sigmoid(gp_ref[0, 0] + gb_ref[0])
    q = q_ref[0, 0]

    def attend(r, k, v, add_mask, first_key, t_start):
        s = lax.dot_general(q[r], k, (((1,), (1,)), ((), ())), preferred_element_type=f32) + add_mask
        near_lo = max(t_start - NSA_NEAR, first_key)
        band = bn_ref[0, r][:, near_lo - (t_start - NSA_NEAR):]
        cut = near_lo - first_key
        near = s[:, cut:] + band
        s = near if cut == 0 else jnp.concatenate([s[:, :cut], near], axis=1)
        e = jnp.exp(s - jnp.max(s, axis=-1, keepdims=True))
        return jnp.dot(e.astype(bf16), v, preferred_element_type=f32) / jnp.sum(e, axis=-1, keepdims=True)

    for c in range(S // TQ):
        @pl.when(qi == c)
        def _(c=c):
            t_start, n_keys = c * TQ, (c + 1) * TQ
            row = t_start + lax.broadcasted_iota(jnp.int32, (TQ, 1), 0)
            col = lax.broadcasted_iota(jnp.int32, (1, n_keys), 1)
            sel_tok = jnp.dot(sel, e_ref[:, :n_keys], preferred_element_type=f32)
            mask_s = jnp.where((sel_tok > 0.5) & (col <= row), 0.0, NEG_INF)
            w_lo = max(t_start - WINDOW, 0)
            col_w = w_lo + lax.broadcasted_iota(jnp.int32, (1, n_keys - w_lo), 1)
            mask_w = jnp.where((col_w <= row) & (row - col_w <= WINDOW), 0.0, NEG_INF)
            ks, vs = ks_ref[0, 0, :n_keys, :], vs_ref[0, 0, :n_keys, :]
            kw, vw = kw_ref[0, 0, w_lo:n_keys, :], vw_ref[0, 0, w_lo:n_keys, :]
            outs = []
            for r in range(R):
                o_slc = attend(r, ks, vs, mask_s, 0, t_start)
                o_win = attend(r, kw, vw, mask_w, w_lo, t_start)
                outs.append(gates[:, 3 * r:3 * r + 1] * o_cmp[r] + gates[:, 3 * r + 1:3 * r + 2] * o_slc
                            + gates[:, 3 * r + 2:3 * r + 3] * o_win)
            o_ref[0] = jnp.concatenate(outs, axis=-1)


def _bias_lookup(rel_bias, dist):
    bucket = t5_bucket(dist)[..., None]
    out = jnp.zeros(dist.shape + rel_bias.shape[1:], rel_bias.dtype)
    for b in range(REL_BUCKETS):
        out = jnp.where(bucket == b, rel_bias[b], out)
    return out


def nsa_prompt(q_heads, k_cmp, v_cmp, ks_t, vs_t, kw_t, vw_t, g_pre, gate_b, rel_bias):
    f32, bf16 = jnp.float32, jnp.bfloat16
    B, _, S, _ = q_heads.shape
    G, R, DH, TQ = NSA_KV, NSA_R, NSA_DH, NSA_TQ
    assert S % TQ == 0 and TQ % V7X_LANES == 0 and WINDOW % V7X_LANES == 0 and NSA_NEAR % V7X_LANES == 0
    n_cmp = k_cmp.shape[1]
    ncp = S // CMP_STRIDE
    ns = S // SLC_BLOCK
    assert n_cmp == ncp - 1
    n_select = min(N_SELECT, ns)

    def kv_layout(a, n):
        a = jnp.pad(a, ((0, 0), (0, n - a.shape[1]), (0, 0), (0, 0)))
        return jnp.transpose(a, (0, 2, 1, 3)).astype(bf16)

    q5 = q_heads.reshape(B, G, R, S, DH)
    kc, vc = kv_layout(k_cmp, ncp), kv_layout(v_cmp, ncp)

    half = REL_BUCKETS // 2
    assert half + int(math.log(NSA_NEAR / half) / math.log(REL_MAX_DIST / half) * half) >= REL_BUCKETS - 1
    t = jnp.arange(S)
    dist_c = t[:, None] - (jnp.arange(ncp) * CMP_STRIDE + CMP_LEN - 1)[None, :]
    bias_c = jnp.transpose(_bias_lookup(rel_bias, dist_c).reshape(S, ncp, G, R), (2, 3, 0, 1))
    d_near = jnp.arange(TQ)[:, None] + NSA_NEAR - jnp.arange(TQ + NSA_NEAR)[None, :]
    bias_n = jnp.transpose((_bias_lookup(rel_bias, d_near) - rel_bias[REL_BUCKETS - 1]).reshape(
        TQ, TQ + NSA_NEAR, G, R), (2, 3, 0, 1))
    expand = (jnp.arange(S)[None, :] // SLC_BLOCK == jnp.arange(ns)[:, None]).astype(bf16)
    c_i = jnp.arange(ncp)[:, None]
    j_i = jnp.arange(ns)[None, :]
    mw = sum(w * (c_i == SLC_RATIO * j_i + k - 1) for k, w in enumerate(SLC_OVERLAP_W)).astype(bf16)
    gp = jnp.transpose(g_pre.reshape(B, S, G, 3 * R), (0, 2, 1, 3))
    gb = gate_b.reshape(G, 1, 3 * R)

    kv_spec = lambda n: pl.BlockSpec((1, 1, n, DH), lambda b, g, i: (b, g, 0, 0))
    return pl.pallas_call(
        functools.partial(_nsa_prompt_kernel, n_cmp=n_cmp, n_select=n_select),
        grid=(B, G, S // TQ),
        in_specs=[
            pl.BlockSpec((1, 1, R, TQ, DH), lambda b, g, i: (b, g, 0, i, 0)),
            kv_spec(ncp), kv_spec(ncp), kv_spec(S), kv_spec(S), kv_spec(S), kv_spec(S),
            pl.BlockSpec((1, R, TQ, ncp), lambda b, g, i: (g, 0, i, 0)),
            pl.BlockSpec((1, R, TQ, TQ + NSA_NEAR), lambda b, g, i: (g, 0, 0, 0)),
            pl.BlockSpec((ns, S), lambda b, g, i: (0, 0)),
            pl.BlockSpec((ncp, ns), lambda b, g, i: (0, 0)),
            pl.BlockSpec((1, 1, TQ, 3 * R), lambda b, g, i: (b, g, i, 0)),
            pl.BlockSpec((1, 1, 3 * R), lambda b, g, i: (g, 0, 0)),
        ],
        out_specs=pl.BlockSpec((1, TQ, R * DH), lambda b, g, i: (b, i, g)),
        out_shape=jax.ShapeDtypeStruct((B, S, NSA_W), f32),
        compiler_params=pltpu.CompilerParams(
            dimension_semantics=("arbitrary", "arbitrary", "arbitrary"), vmem_limit_bytes=V7X_VMEM_LIMIT_BYTES),
        name="nsa_prompt",
    )(q5, kc, vc, ks_t, vs_t, kw_t, vw_t, bias_c, bias_n, expand, mw, gp, gb)


MLA_TQ = 512
MLA_HEADS_PER_STEP = 4


def _mla_prompt_kernel(cq_ref, wuq_ref, ra_ref, rm_ref, rp_ref, ckv_ref, kr_ref, wuk_ref, wuv_ref,
                       o_ref, q_ref, k_ref, v_ref):
    f32, bf16 = jnp.float32, jnp.bfloat16
    qi = pl.program_id(2)
    HP, S, DP = k_ref.shape
    TQ = cq_ref.shape[1]
    causal = lax.broadcasted_iota(jnp.int32, (TQ, 1), 0) >= lax.broadcasted_iota(jnp.int32, (1, TQ), 1)

    @pl.when(qi == 0)
    def _():
        ckv = ckv_ref[0]
        pad = jnp.zeros((S, DP - MLA_NOPE - MLA_ROPE), bf16)
        for h in range(HP):
            k_nope = jnp.dot(ckv, wuk_ref[h], preferred_element_type=f32).astype(bf16)
            k_ref[h] = jnp.concatenate([k_nope, kr_ref[0], pad], axis=-1)
            v_ref[h] = jnp.dot(ckv, wuv_ref[h], preferred_element_type=f32).astype(bf16)

    for h in range(HP):
        q3 = jnp.dot(cq_ref[0], wuq_ref[h], preferred_element_type=f32)
        q = q3[:, :DP]
        ms = jnp.sum(q * q, axis=-1, keepdims=True) * (1.0 / (MLA_NOPE + MLA_ROPE))
        q = q * ra_ref[...] + q3[:, DP:2 * DP] * rm_ref[...] + q3[:, 2 * DP:] * rp_ref[...]
        q_ref[h] = (q * (lax.rsqrt(ms + EPS) * MLA_SCALE)).astype(bf16)

    for c in range(S // TQ):
        @pl.when(qi == c)
        def _(c=c):
            n_vis = c * TQ
            nt = (((1,), (1,)), ((), ()))
            outs = []
            for h in range(HP):
                qh = q_ref[h]
                s_diag = jnp.where(causal, lax.dot_general(qh, k_ref[h, n_vis:n_vis + TQ, :], nt,
                                                           preferred_element_type=f32), NEG_INF)
                m = jnp.max(s_diag, axis=-1, keepdims=True)
                if n_vis > 0:
                    s_vis = lax.dot_general(qh, k_ref[h, :n_vis, :], nt, preferred_element_type=f32)
                    m = jnp.maximum(m, jnp.max(s_vis, axis=-1, keepdims=True))
                e_diag = jnp.exp(s_diag - m)
                total = jnp.sum(e_diag, axis=-1, keepdims=True)
                pv = jnp.dot(e_diag.astype(bf16), v_ref[h, n_vis:n_vis + TQ, :], preferred_element_type=f32)
                if n_vis > 0:
                    e_vis = jnp.exp(s_vis - m)
                    total = total + jnp.sum(e_vis, axis=-1, keepdims=True)
                    pv = pv + jnp.dot(e_vis.astype(bf16), v_ref[h, :n_vis, :], preferred_element_type=f32)
                outs.append(pv / total)
            o_ref[0] = jnp.concatenate(outs, axis=-1)


def mla_prompt_attention(cq, w_uq, g_q, pos, ckv, kr, w_uk, w_uv):
    f32, bf16 = jnp.float32, jnp.bfloat16
    B, S, _ = cq.shape
    H, DQ, DP = MLA_H, MLA_NOPE + MLA_ROPE, V7X_LANES
    HP, TQ = MLA_HEADS_PER_STEP, min(MLA_TQ, S)
    half = MLA_ROPE // 2
    assert S % TQ == 0 and H % HP == 0 and (HP * MLA_V) % V7X_LANES == 0 and DQ <= DP
    w = jnp.transpose(w_uq.reshape(Q_LORA, H, DQ), (1, 0, 2))
    zw = lambda n: jnp.zeros((H, Q_LORA, n), f32)
    w_left = jnp.concatenate([zw(MLA_NOPE), w[..., MLA_NOPE + half:], zw(half)], axis=-1)
    w_right = jnp.concatenate([zw(MLA_NOPE + half), w[..., MLA_NOPE:MLA_NOPE + half]], axis=-1)
    pad_w = lambda a: jnp.pad(a, ((0, 0), (0, 0), (0, DP - DQ)))
    wuq_p = jnp.concatenate([pad_w(w), pad_w(w_left), pad_w(w_right)], axis=-1).astype(bf16)
    inv = ROPE_THETA ** (-jnp.arange(half, dtype=f32) / half)
    ang = pos.astype(f32)[:, None] * inv[None, :]
    cos, sin, zero = jnp.cos(ang), jnp.sin(ang), jnp.zeros((S, half), f32)
    g_nope, g_lo, g_hi = g_q[:MLA_NOPE], g_q[MLA_NOPE:MLA_NOPE + half], g_q[MLA_NOPE + half:]
    tail = jnp.zeros((S, DP - DQ), f32)
    lead0 = jnp.zeros((S, MLA_NOPE), f32)
    rope_a = jnp.concatenate([jnp.broadcast_to(g_nope, (S, MLA_NOPE)), cos * g_lo, cos * g_hi, tail], axis=1)
    rope_m = jnp.concatenate([lead0, -sin * g_hi, zero, tail], axis=1)
    rope_p = jnp.concatenate([lead0, zero, sin * g_lo, tail], axis=1)
    row_tab = pl.BlockSpec((TQ, DP), lambda b, h, i: (i, 0))
    return pl.pallas_call(
        _mla_prompt_kernel,
        grid=(B, H // HP, S // TQ),
        in_specs=[
            pl.BlockSpec((1, TQ, Q_LORA), lambda b, h, i: (b, i, 0)),
            pl.BlockSpec((HP, Q_LORA, 3 * DP), lambda b, h, i: (h, 0, 0)),
            row_tab, row_tab, row_tab,
            pl.BlockSpec((1, S, KV_LORA), lambda b, h, i: (b, 0, 0)),
            pl.BlockSpec((1, S, MLA_ROPE), lambda b, h, i: (b, 0, 0)),
            pl.BlockSpec((HP, KV_LORA, MLA_NOPE), lambda b, h, i: (h, 0, 0)),
            pl.BlockSpec((HP, KV_LORA, MLA_V), lambda b, h, i: (h, 0, 0)),
        ],
        out_specs=pl.BlockSpec((1, TQ, HP * MLA_V), lambda b, h, i: (b, i, h)),
        out_shape=jax.ShapeDtypeStruct((B, S, H * MLA_V), jnp.float32),
        scratch_shapes=[pltpu.VMEM((HP, TQ, DP), bf16), pltpu.VMEM((HP, S, DP), bf16),
                        pltpu.VMEM((HP, S, MLA_V), bf16)],
        compiler_params=pltpu.CompilerParams(
            dimension_semantics=("arbitrary", "arbitrary", "arbitrary"), vmem_limit_bytes=V7X_VMEM_LIMIT_BYTES),
        name="mla_prompt",
    )(cq, wuq_p, rope_a, rope_m, rope_p, ckv, kr, w_uk, w_uv)


def _split3(x):
    bf16, f32 = jnp.bfloat16, jnp.float32
    hi = x.astype(bf16)
    r = x - hi.astype(f32)
    mid = r.astype(bf16)
    return hi, mid, (r - mid.astype(f32)).astype(bf16)


def _mlstm_kernel(u_ref, v_ref, og_ref, cw_ref, cb_ref, wq_ref, wk_ref, g_ref, ir_ref, fr_ref, it_ref, ft_ref,
                  tri_ref, h_ref, c_ref, n_ref, m_ref):
    f32, bf16 = jnp.float32, jnp.bfloat16
    H, NC, L = ir_ref.shape[1:]
    DH = ML_DH
    PRE = 8
    tri = tri_ref[...]
    causal = (tri > 0.5)[None]

    def bmm(a, b, ca, cb):
        return lax.dot_general(a, b, (((ca,), (cb,)), ((0,), (0,))), preferred_element_type=f32)

    def split_heads(x):
        return jnp.stack([x[:, h * DH:(h + 1) * DH] for h in range(H)], axis=0)

    def conv_act(c):
        cur = u_ref[0, c * L:(c + 1) * L, :]
        prev = jnp.zeros((PRE, cur.shape[1]), f32) if c == 0 else u_ref[0, c * L - PRE:c * L, :]
        full = jnp.concatenate([prev, cur], axis=0)
        acc = cb_ref[...]
        for j in range(CONV_K):
            lo = PRE - (CONV_K - 1) + j
            acc = acc + full[lo:lo + L, :] * cw_ref[j:j + 1, :]
        return acc * jax.nn.sigmoid(acc)

    b_rows = jnp.stack([sum(lax.dot_general(p, tri, (((1,), (1,)), ((), ())), preferred_element_type=f32)
                            for p in _split3(fr_ref[0, h])) for h in range(H)], axis=0)
    b_cols = jnp.stack([sum(jnp.dot(tri, p, preferred_element_type=f32) for p in _split3(ft_ref[0, h]))
                        for h in range(H)], axis=0)
    i_rows, i_cols = ir_ref[0], it_ref[0]
    C = jnp.zeros((H, DH, DH), f32)
    n = jnp.zeros((H, 1, DH), f32)
    m = jnp.zeros((H, 1, 1), f32)
    for c in range(NC):
        rows = slice(c * L, (c + 1) * L)
        ch = split_heads(conv_act(c)).astype(bf16)
        q = bmm(ch, wq_ref[...], 2, 1).astype(bf16)
        k = (bmm(ch, wk_ref[...], 2, 1) * (DH ** -0.5)).astype(bf16)
        v = split_heads(v_ref[0, rows, :])
        b_row, i_row = b_rows[:, c:c + 1, :], i_rows[:, c:c + 1, :]
        b_col, i_col = b_cols[:, :, c:c + 1], i_cols[:, :, c:c + 1]
        g = b_col + m
        dmat = jnp.where(causal, b_col - b_row + i_row, -jnp.inf)
        mt = jnp.maximum(g, jnp.max(dmat, axis=-1, keepdims=True))
        inter = jnp.exp(g - mt)
        sqk = bmm(q, k, 2, 2) * jnp.exp(dmat - mt)
        num = inter * bmm(q, C.astype(bf16), 2, 2) + bmm(sqk.astype(bf16), v.astype(bf16), 2, 1)
        nq = jnp.sum(q.astype(f32) * n.astype(bf16).astype(f32), axis=-1, keepdims=True)
        den = inter * nq + jnp.sum(sqk, axis=-1, keepdims=True)
        hs = num / jnp.maximum(jnp.abs(den), jnp.exp(-mt))
        hs = hs * lax.rsqrt(jnp.mean(hs * hs, axis=-1, keepdims=True) + EPS)
        gate = jax.nn.sigmoid(og_ref[0, rows, :])
        for h in range(H):
            lanes = slice(h * DH, (h + 1) * DH)
            h_ref[0, rows, lanes] = hs[h] * g_ref[:, lanes] * gate[:, lanes]
        b_end = b_row[:, :, L - 1:L]
        w_row = b_end - b_row + i_row
        m_new = jnp.maximum(b_end + m, jnp.max(w_row, axis=-1, keepdims=True))
        decay = jnp.exp(b_end + m - m_new)
        w_col = jnp.exp(b_end - b_col + i_col - m_new)
        C = decay * C + bmm((w_col * v).astype(bf16), k, 1, 1)
        n = decay * n + bmm(jnp.exp(w_row - m_new).astype(bf16), k, 2, 1)
        m = m_new
    c_ref[0] = C
    n_ref[0] = n
    m_ref[0] = jnp.broadcast_to(m, (H, 1, V7X_LANES))


def mlstm_prompt(u, v, o_pre, conv_w, conv_b, wq, wk, out_g, i_pre, logf):
    f32, bf16 = jnp.float32, jnp.bfloat16
    B, S, W = u.shape
    H, L = ML_H, MLSTM_CHUNK
    assert S % L == 0 and W == H * ML_DH and L % 8 == 0 and CONV_K - 1 <= 8
    const = lambda shape: pl.BlockSpec(shape, lambda b: (0,) * len(shape))
    NC = S // L
    rows = lambda a: jnp.transpose(a, (0, 2, 1)).reshape(B, H, NC, L)
    cols = lambda a: jnp.transpose(rows(a), (0, 1, 3, 2))
    tri = jnp.tril(jnp.ones((L, L), jnp.bfloat16))
    seq = pl.BlockSpec((1, S, W), lambda b: (b, 0, 0))
    gate_r = pl.BlockSpec((1, H, NC, L), lambda b: (b, 0, 0, 0))
    gate_c = pl.BlockSpec((1, H, L, NC), lambda b: (b, 0, 0, 0))
    hs, C, n, m = pl.pallas_call(
        _mlstm_kernel,
        grid=(B,),
        in_specs=[seq, seq, seq, const((CONV_K, W)), const((1, W)), const((H, ML_DH, ML_DH)),
                  const((H, ML_DH, ML_DH)), const((1, W)), gate_r, gate_r, gate_c, gate_c, const((L, L))],
        out_specs=[seq, pl.BlockSpec((1, H, ML_DH, ML_DH), lambda b: (b, 0, 0, 0)),
                   pl.BlockSpec((1, H, 1, ML_DH), lambda b: (b, 0, 0, 0)),
                   pl.BlockSpec((1, H, 1, V7X_LANES), lambda b: (b, 0, 0, 0))],
        out_shape=[jax.ShapeDtypeStruct((B, S, W), f32), jax.ShapeDtypeStruct((B, H, ML_DH, ML_DH), f32),
                   jax.ShapeDtypeStruct((B, H, 1, ML_DH), f32), jax.ShapeDtypeStruct((B, H, 1, V7X_LANES), f32)],
        compiler_params=pltpu.CompilerParams(dimension_semantics=("arbitrary",),
                                             vmem_limit_bytes=V7X_VMEM_LIMIT_BYTES),
        name="mlstm_prompt",
    )(u, v, o_pre, conv_w, conv_b.reshape(1, W), wq.astype(bf16), wk.astype(bf16), out_g.reshape(1, W),
      rows(i_pre), rows(logf), cols(i_pre), cols(logf), tri)
    return hs, C, n[:, :, 0], m[:, :, 0, 0]


def _odd_proj_kernel(x_ref, g_ref, w_ref, gcq_ref, gckv_ref, ra_ref, rm_ref, rp_ref,
                     cq_ref, ckv_ref, ckv16_ref, kr_ref, kr16_ref):
    f32, bf16 = jnp.float32, jnp.bfloat16
    x = x_ref[0]
    xn = (x * lax.rsqrt(jnp.mean(x * x, axis=-1, keepdims=True) + EPS) * g_ref[...]).astype(bf16)
    y = jnp.dot(xn, w_ref[...], preferred_element_type=f32)
    norm = lambda a: a * lax.rsqrt(jnp.mean(a * a, axis=-1, keepdims=True) + EPS)
    cq_ref[0] = (norm(y[:, :Q_LORA]) * gcq_ref[...]).astype(bf16)
    ckv = norm(y[:, Q_LORA:Q_LORA + KV_LORA]) * gckv_ref[...]
    ckv_ref[0] = ckv
    ckv16_ref[0] = ckv.astype(bf16)
    base = Q_LORA + KV_LORA
    kr, kr_l, kr_r = (y[:, base + j * V7X_LANES:base + (j + 1) * V7X_LANES] for j in range(3))
    ms = jnp.sum(kr * kr, axis=-1, keepdims=True) * (1.0 / MLA_ROPE)
    roped = (kr * ra_ref[...] + kr_l * rm_ref[...] + kr_r * rp_ref[...]) * lax.rsqrt(ms + EPS)
    kr_ref[0] = roped[:, :MLA_ROPE]
    kr16_ref[0] = roped[:, :MLA_ROPE].astype(bf16)


def odd_in_proj_prompt(y, g_mix, w_in, g_cq, g_ckv, g_kr, pos):
    f32, bf16 = jnp.float32, jnp.bfloat16
    B, S, D = y.shape
    tm = min(EV_PROJ_ROWS, S)
    half, lanes = MLA_ROPE // 2, V7X_LANES
    assert S % tm == 0 and Q_LORA % lanes == 0 and KV_LORA % lanes == 0 and MLA_ROPE <= lanes
    w_cq, w_ckv, w_kr = split_cols(w_in, OD_SPLITS)
    zw = lambda n: jnp.zeros((D, n), f32)
    blocks = [w_kr, jnp.concatenate([w_kr[:, half:], zw(half)], axis=1),
              jnp.concatenate([zw(half), w_kr[:, :half]], axis=1)]
    w = jnp.concatenate([w_cq, w_ckv] + [jnp.pad(b, ((0, 0), (0, lanes - MLA_ROPE))) for b in blocks],
                        axis=1).astype(bf16)
    n_cols = w.shape[1]
    inv = ROPE_THETA ** (-jnp.arange(half, dtype=f32) / half)
    ang = pos.astype(f32)[:, None] * inv[None, :]
    cos, sin, zero = jnp.cos(ang), jnp.sin(ang), jnp.zeros((S, half), f32)
    tail = jnp.zeros((S, lanes - MLA_ROPE), f32)
    g_lo, g_hi = g_kr[:half], g_kr[half:]
    rope_a = jnp.concatenate([cos * g_lo, cos * g_hi, tail], axis=1)
    rope_m = jnp.concatenate([-sin * g_hi, zero, tail], axis=1)
    rope_p = jnp.concatenate([zero, sin * g_lo, tail], axis=1)
    const = lambda shape: pl.BlockSpec(shape, lambda b, i: (0,) * len(shape), pipeline_mode=pl.Buffered(1))
    rows = lambda width: pl.BlockSpec((1, tm, width), lambda b, i: (b, i, 0))
    tab = pl.BlockSpec((tm, lanes), lambda b, i: (i, 0))
    shape = lambda width, dt: jax.ShapeDtypeStruct((B, S, width), dt)
    return pl.pallas_call(
        _odd_proj_kernel,
        grid=(B, S // tm),
        in_specs=[rows(D), const((1, D)), const((D, n_cols)), const((1, Q_LORA)), const((1, KV_LORA)), tab, tab, tab],
        out_specs=[rows(Q_LORA), rows(KV_LORA), rows(KV_LORA), rows(MLA_ROPE), rows(MLA_ROPE)],
        out_shape=[shape(Q_LORA, bf16), shape(KV_LORA, f32), shape(KV_LORA, bf16), shape(MLA_ROPE, f32),
                   shape(MLA_ROPE, bf16)],
        compiler_params=pltpu.CompilerParams(
            dimension_semantics=("arbitrary", "arbitrary"), vmem_limit_bytes=V7X_VMEM_LIMIT_BYTES),
        name="odd_in_proj",
    )(y, g_mix.reshape(1, D), w, g_cq.reshape(1, Q_LORA), g_ckv.reshape(1, KV_LORA), rope_a, rope_m, rope_p)


MEM_ROWS = 512


def _mem_attn_kernel(*refs):
    f32, bf16 = jnp.float32, jnp.bfloat16
    n = (len(refs) - 8) // 2
    x_ref, part_refs, w_refs = refs[0], refs[1:1 + n], refs[1 + n:1 + 2 * n]
    g_ref, wq_ref, gq_ref, k_ref, v_ref, wo_ref, o_ref = refs[1 + 2 * n:]
    x = x_ref[0]
    for p_ref, w_ref in zip(part_refs, w_refs):
        x = x + jnp.dot(p_ref[0].astype(bf16), w_ref[...], preferred_element_type=f32)
    xn = (x * lax.rsqrt(jnp.mean(x * x, axis=-1, keepdims=True) + EPS) * g_ref[...]).astype(bf16)
    q = jnp.dot(xn, wq_ref[...], preferred_element_type=f32)
    k, v = k_ref[0], v_ref[0]
    outs = []
    for h in range(XM_H):
        cols = slice(h * XM_DH, (h + 1) * XM_DH)
        qh = q[:, cols]
        qh = (qh * lax.rsqrt(jnp.mean(qh * qh, axis=-1, keepdims=True) + EPS) * gq_ref[...]).astype(bf16)
        s = lax.dot_general(qh, k[:, cols], (((1,), (1,)), ((), ())), preferred_element_type=f32) * (XM_DH ** -0.5)
        e = jnp.exp(s - jnp.max(s, axis=-1, keepdims=True))
        p = (e / jnp.sum(e, axis=-1, keepdims=True)).astype(bf16)
        outs.append(jnp.dot(p, v[:, cols], preferred_element_type=f32))
    o = jnp.concatenate(outs, axis=-1).astype(bf16)
    o_ref[0] = x + jnp.dot(o, wo_ref[...], preferred_element_type=f32)


def mem_attention(y, g, wq, gq, k, v, wo, parts=(), w_mix=None):
    bf16 = jnp.bfloat16
    B, S, D = y.shape
    M = k.shape[1]
    ts = min(MEM_ROWS, S)
    assert S % ts == 0
    offs = np.cumsum([0] + [p.shape[-1] for p in parts])
    ws = [w_mix[offs[i]:offs[i + 1]].astype(bf16) for i in range(len(parts))]
    const = lambda shape: pl.BlockSpec(shape, lambda b, i: (0,) * len(shape), pipeline_mode=pl.Buffered(1))
    rows = lambda width: pl.BlockSpec((1, ts, width), lambda b, i: (b, i, 0))
    kv_spec = pl.BlockSpec((1, M, XM_W), lambda b, i: (b, 0, 0))
    return pl.pallas_call(
        _mem_attn_kernel,
        grid=(B, S // ts),
        in_specs=[rows(D)] + [rows(p.shape[-1]) for p in parts] + [const(wi.shape) for wi in ws]
                 + [const((1, D)), const((D, XM_W)), const((1, XM_DH)), kv_spec, kv_spec, const((XM_W, D))],
        out_specs=pl.BlockSpec((1, ts, D), lambda b, i: (b, i, 0)),
        out_shape=jax.ShapeDtypeStruct((B, S, D), jnp.float32),
        compiler_params=pltpu.CompilerParams(
            dimension_semantics=("arbitrary", "arbitrary"), vmem_limit_bytes=V7X_VMEM_LIMIT_BYTES),
        name="mem_attn",
    )(y, *parts, *ws, g.reshape(1, D), wq.astype(bf16), gq.reshape(1, XM_DH), k.reshape(B, M, XM_W).astype(bf16),
      v.reshape(B, M, XM_W).astype(bf16), wo.astype(bf16))


DECODE_XPOSE_UNROLL = 8


def _paged_copy(pool_ref, page, buf_ref, slot, p, sem_ref):
    return pltpu.make_async_copy(pool_ref.at[page], buf_ref.at[slot, p], sem_ref.at[slot])


def _paged_pipeline(pt_ref, pools, bufs, sems):
    b = pl.program_id(0)
    n_pages = pt_ref.shape[1]

    def start(seq, slot):
        def body(p, carry):
            for pool, buf, sem in zip(pools, bufs, sems):
                _paged_copy(pool, pt_ref[seq, p], buf, slot, p, sem).start()
            return carry
        lax.fori_loop(0, n_pages, body, 0)

    @pl.when(b == 0)
    def _():
        start(0, 0)

    @pl.when(b + 1 < pl.num_programs(0))
    def _():
        start(b + 1, (b + 1) % 2)

    slot = b % 2

    def wait_body(p, carry):
        for pool, buf, sem in zip(pools, bufs, sems):
            _paged_copy(pool, 0, buf, slot, p, sem).wait()
        return carry
    lax.fori_loop(0, n_pages, wait_body, 0)
    return slot


def _softmax_rows(s, valid):
    e = jnp.exp(s - jnp.max(s, axis=-1, keepdims=True))
    return e / jnp.sum(e, axis=-1, keepdims=True) * valid


def _nsa_decode_cmp_kernel(pt_ref, qbd_ref, wk_ref, wv_ref, pek_ref, pev_ref, w2k_ref, w2v_ref, gk_ref,
                           bc_ref, mw_ref, kpool_ref, vpool_ref, ocmp_ref, sel_ref,
                           kbuf, vbuf, xrm_ref, ksem, vsem, *, n_cmp, n_blocks, n_select):
    f32, bf16 = jnp.float32, jnp.bfloat16
    slot = _paged_pipeline(pt_ref, (kpool_ref, vpool_ref), (kbuf, vbuf), (ksem, vsem))
    n_pages = pt_ref.shape[1]
    n_chunk = n_pages * (PAGE_SIZE // CMP_STRIDE)
    G, DH, HID = NSA_KV, NSA_DH, CMP_HID

    def summaries(buf, w_ref, pe_ref, w2_ref):
        def xpose(i, carry):
            for u in range(DECODE_XPOSE_UNROLL):
                p = i * DECODE_XPOSE_UNROLL + u
                xrm_ref[pl.ds(pl.multiple_of(p * PAGE_SIZE, PAGE_SIZE), PAGE_SIZE), :] = buf[slot, p].T
            return carry
        lax.fori_loop(0, n_pages // DECODE_XPOSE_UNROLL, xpose, 0)
        acc = jnp.zeros((n_chunk, 2 * G * HID), f32)
        for j in range(0, CMP_STRIDE, 2):
            rows = jnp.concatenate([xrm_ref[pl.ds(j + u, n_chunk, stride=CMP_STRIDE), :].astype(bf16)
                                    for u in range(2)], axis=1)
            acc = acc + jnp.dot(rows, w_ref[j // 2], preferred_element_type=f32)
        lo, hi = acc[:, :G * HID], acc[:, G * HID:]
        pre = lo + pltpu.roll(hi, n_chunk - 1, 0) + pe_ref[...]
        hid = pre * jax.nn.sigmoid(pre)
        return jnp.dot(hid.astype(bf16), w2_ref[...], preferred_element_type=f32)

    k_sum = summaries(kbuf, wk_ref, pek_ref, w2k_ref)
    lane = lax.broadcasted_iota(jnp.int32, (1, G * DH), 1)
    sq = k_sum * k_sum
    s_all = jnp.sum(sq, axis=-1, keepdims=True)
    s_g0 = jnp.sum(jnp.where(lane < DH, sq, 0.0), axis=-1, keepdims=True)
    ms = jnp.where(lane < DH, s_g0, s_all - s_g0) * (1.0 / DH)
    k_cmp = (k_sum * lax.rsqrt(ms + EPS) * gk_ref[...]).astype(bf16)
    v_cmp = summaries(vbuf, wv_ref, pev_ref, w2v_ref).astype(bf16)

    qbd = qbd_ref[0]
    H = qbd.shape[0]
    c_row = lax.broadcasted_iota(jnp.int32, (1, n_chunk), 1)
    valid = (c_row < n_cmp).astype(f32)
    s = lax.dot_general(qbd, k_cmp, (((1,), (1,)), ((), ())), preferred_element_type=f32) + bc_ref[...]
    p = _softmax_rows(jnp.where(valid > 0.5, s, NEG_INF), valid)
    ocmp_ref[0] = jnp.dot(p.astype(bf16), v_cmp, preferred_element_type=f32)

    R = H // G
    head = lax.broadcasted_iota(jnp.int32, (H, 1), 0)
    imp = jnp.where(head < R, jnp.sum(p[:R], axis=0, keepdims=True), jnp.sum(p[R:], axis=0, keepdims=True))
    mw = mw_ref[...]
    hi_p = imp.astype(bf16)
    r1 = imp - hi_p.astype(f32)
    mid_p = r1.astype(bf16)
    lo_p = (r1 - mid_p.astype(f32)).astype(bf16)
    p_slc = (jnp.dot(hi_p, mw, preferred_element_type=f32) + jnp.dot(mid_p, mw, preferred_element_type=f32)
             + jnp.dot(lo_p, mw, preferred_element_type=f32))
    nsp = mw_ref.shape[1]
    tb = n_blocks - 1
    jb = lax.broadcasted_iota(jnp.int32, (1, nsp), 1)
    forced = (jb == 0) | (jb == tb) | (jb == tb - 1)
    score = jnp.where(jb <= tb, p_slc + FORCE_SCORE * forced.astype(f32), -1.0)
    j_col = lax.broadcasted_iota(jnp.int32, (nsp, 1), 0)
    sels = []
    for g in range(G):
        row = score[g * R:g * R + 1, :]
        col = jnp.broadcast_to(row, (nsp, nsp)).T
        beats = (col > row) | ((col == row) & (j_col < jb))
        rank = jnp.sum(beats.astype(f32), axis=0, keepdims=True)
        sels.append(jnp.broadcast_to((rank < n_select).astype(f32), (R, nsp)))
    sel_ref[0] = jnp.concatenate(sels, axis=0)


def _nsa_decode_attn_kernel(pt_ref, qbd_ref, sel_ref, ocmp_ref, gate_ref, new_ref, e_ref, bs_ref, bw_ref, b0_ref,
                            kw_ref, vw_ref, kpool_ref, vpool_ref, o_ref, kbuf, vbuf, s_ref, ksem, vsem):
    f32, bf16 = jnp.float32, jnp.bfloat16
    slot = _paged_pipeline(pt_ref, (kpool_ref, vpool_ref), (kbuf, vbuf), (ksem, vsem))
    n_pages = pt_ref.shape[1]
    qbd = qbd_ref[0]
    qf = qbd.astype(f32)
    new = new_ref[0]
    b0 = b0_ref[...]

    def new_score(k_row):
        return jnp.sum(qf * k_row.astype(bf16).astype(f32), axis=-1, keepdims=True) + b0

    for p in range(n_pages):
        s_ref[:, p * PAGE_SIZE:(p + 1) * PAGE_SIZE] = jnp.dot(qbd, kbuf[slot, p].astype(bf16),
                                                              preferred_element_type=f32)
    n_blk_past = e_ref.shape[0]
    sel_tok = jnp.dot(sel_ref[0][:, :n_blk_past].astype(bf16), e_ref[...], preferred_element_type=f32)
    s_past = jnp.where(sel_tok > 0.5, s_ref[...] + bs_ref[...], NEG_INF)
    s_new = new_score(new[0:1])
    m = jnp.maximum(jnp.max(s_past, axis=-1, keepdims=True), s_new)
    e_new = jnp.exp(s_new - m)
    s_ref[...] = jnp.exp(s_past - m)
    denom = jnp.sum(s_ref[...], axis=-1, keepdims=True) + e_new
    acc = e_new.astype(bf16).astype(f32) * new[1:2].astype(bf16).astype(f32)
    for p in range(n_pages):
        pe = s_ref[:, p * PAGE_SIZE:(p + 1) * PAGE_SIZE].astype(bf16)
        acc = acc + lax.dot_general(pe, vbuf[slot, p].astype(bf16), (((1,), (1,)), ((), ())),
                                    preferred_element_type=f32)
    o_slc = acc / denom

    s_w = jnp.dot(qbd, kw_ref[0].astype(bf16), preferred_element_type=f32) + bw_ref[...]
    s_wn = new_score(new[2:3])
    m = jnp.maximum(jnp.max(s_w, axis=-1, keepdims=True), s_wn)
    e_w, e_wn = jnp.exp(s_w - m), jnp.exp(s_wn - m)
    denom = jnp.sum(e_w, axis=-1, keepdims=True) + e_wn
    acc = (lax.dot_general(e_w.astype(bf16), vw_ref[0].astype(bf16), (((1,), (1,)), ((), ())),
                           preferred_element_type=f32)
           + e_wn.astype(bf16).astype(f32) * new[3:4].astype(bf16).astype(f32))
    o_win = acc / denom

    gates = jax.nn.sigmoid(gate_ref[0])
    o_ref[0] = gates[:, 0:1] * ocmp_ref[0] + gates[:, 1:2] * o_slc + gates[:, 2:3] * o_win


def _block_diag_heads(x):
    B, H, DH = x.shape
    g_of_h = jnp.arange(H) // (H // NSA_KV)
    onehot = (g_of_h[:, None] == jnp.arange(NSA_KV)[None, :]).astype(x.dtype)
    return (x[:, :, None, :] * onehot[None, :, :, None]).reshape(B, H, NSA_KV * DH)


def nsa_decode(qn, ks, vs, kw, vw, g_pre, gate_b, rel_bias, page_table, cmp_k_pool, cmp_v_pool,
               slc_k_pool, slc_v_pool, win_k, win_v, pe_k, w1_k, w2_k, pe_v, w1_v, w2_v, gk_cmp):
    f32, bf16 = jnp.float32, jnp.bfloat16
    B, n_pages = page_table.shape
    G, R, DH, H = NSA_KV, NSA_R, NSA_DH, NSA_H
    GD = G * DH
    past = n_pages * PAGE_SIZE
    n_chunk = past // CMP_STRIDE
    n_cmp = (past + 1 - CMP_LEN) // CMP_STRIDE + 1
    n_blocks = -(-(past + 1) // SLC_BLOCK)
    n_blk_past = past // SLC_BLOCK
    nsp = -(-n_blocks // V7X_LANES) * V7X_LANES
    n_win = win_k.shape[1]
    assert GD == V7X_LANES and PAGE_SIZE == V7X_LANES and n_blocks >= N_SELECT and n_win == WINDOW

    def pool_view(pool):
        return jnp.transpose(pool, (0, 2, 3, 1)).reshape(pool.shape[0], GD, PAGE_SIZE)
    kc_pool, vc_pool, ks_pool, vs_pool = (pool_view(a) for a in (cmp_k_pool, cmp_v_pool, slc_k_pool, slc_v_pool))
    kw_t = jnp.transpose(win_k, (0, 2, 3, 1)).reshape(B, GD, n_win)
    vw_t = jnp.transpose(win_v, (0, 2, 3, 1)).reshape(B, GD, n_win)

    qbd = _block_diag_heads((qn[:, 0] * NSA_SCALE)).astype(bf16)
    eye = jnp.eye(G, dtype=f32)

    def chunk_weights(w1):
        def bd(w):
            return jnp.einsum('jdh,gk->jgdkh', w, eye).reshape(CMP_STRIDE, GD, G * CMP_HID)
        w = jnp.concatenate([bd(w1[:CMP_STRIDE]), bd(w1[CMP_STRIDE:])], axis=-1)
        return w.reshape(CMP_STRIDE // 2, 2 * GD, 2 * G * CMP_HID).astype(bf16)

    def pe_term(pe, w1):
        return jnp.tile(jnp.einsum('jd,jdh->h', pe, w1), G).reshape(1, G * CMP_HID)

    def w2_bd(w2):
        return jnp.einsum('hd,gk->ghkd', w2, eye).reshape(G * CMP_HID, GD).astype(bf16)

    c_i = jnp.arange(n_chunk)
    bias_c = _bias_lookup(rel_bias, past - (c_i * CMP_STRIDE + CMP_LEN - 1)).T
    j_i = jnp.arange(nsp)[None, :]
    mw = sum(w * ((c_i[:, None] == SLC_RATIO * j_i + k - 1) & (j_i < n_blocks))
             for k, w in enumerate(SLC_OVERLAP_W)).astype(bf16)
    tok = jnp.arange(past)
    expand = (tok[None, :] // SLC_BLOCK == jnp.arange(n_blk_past)[:, None]).astype(bf16)
    bias_s = _bias_lookup(rel_bias, past - tok).T
    bias_w = _bias_lookup(rel_bias, n_win - jnp.arange(n_win)).T
    bias_0 = _bias_lookup(rel_bias, jnp.zeros((1,), jnp.int32)).T
    new_rows = jnp.stack([a.reshape(B, GD) for a in (ks, vs, kw, vw)], axis=1)
    gate_in = g_pre.reshape(B, H, 3) + gate_b

    const = lambda shape: pl.BlockSpec(shape, lambda b, pt: (0,) * len(shape))
    per_seq = lambda shape: pl.BlockSpec((1,) + shape, lambda b, pt: (b,) + (0,) * len(shape))
    any_spec = pl.BlockSpec(memory_space=pl.ANY)
    page_buf = pltpu.VMEM((2, n_pages, GD, PAGE_SIZE), f32)
    params = pltpu.CompilerParams(dimension_semantics=("arbitrary",), vmem_limit_bytes=V7X_VMEM_LIMIT_BYTES)

    o_cmp, sel = pl.pallas_call(
        functools.partial(_nsa_decode_cmp_kernel, n_cmp=n_cmp, n_blocks=n_blocks, n_select=min(N_SELECT, n_blocks)),
        grid_spec=pltpu.PrefetchScalarGridSpec(
            num_scalar_prefetch=1, grid=(B,),
            in_specs=[per_seq((H, GD)),
                      const((CMP_STRIDE // 2, 2 * GD, 2 * G * CMP_HID)),
                      const((CMP_STRIDE // 2, 2 * GD, 2 * G * CMP_HID)),
                      const((1, G * CMP_HID)), const((1, G * CMP_HID)),
                      const((G * CMP_HID, GD)), const((G * CMP_HID, GD)), const((1, GD)),
                      const((H, n_chunk)), const((n_chunk, nsp)), any_spec, any_spec],
            out_specs=[per_seq((H, GD)), per_seq((H, nsp))],
            scratch_shapes=[page_buf, page_buf, pltpu.VMEM((past, GD), f32),
                            pltpu.SemaphoreType.DMA((2,)), pltpu.SemaphoreType.DMA((2,))]),
        out_shape=[jax.ShapeDtypeStruct((B, H, GD), f32), jax.ShapeDtypeStruct((B, H, nsp), f32)],
        compiler_params=params, name="nsa_decode_cmp",
    )(page_table, qbd, chunk_weights(w1_k), chunk_weights(w1_v), pe_term(pe_k, w1_k), pe_term(pe_v, w1_v),
      w2_bd(w2_k), w2_bd(w2_v), jnp.tile(gk_cmp, G).reshape(1, GD), bias_c, mw, kc_pool, vc_pool)

    out = pl.pallas_call(
        _nsa_decode_attn_kernel,
        grid_spec=pltpu.PrefetchScalarGridSpec(
            num_scalar_prefetch=1, grid=(B,),
            in_specs=[per_seq((H, GD)), per_seq((H, nsp)), per_seq((H, GD)), per_seq((H, 3)), per_seq((4, GD)),
                      const((n_blk_past, past)), const((H, past)), const((H, n_win)), const((H, 1)),
                      per_seq((GD, n_win)), per_seq((GD, n_win)), any_spec, any_spec],
            out_specs=per_seq((H, GD)),
            scratch_shapes=[page_buf, page_buf, pltpu.VMEM((H, past), f32),
                            pltpu.SemaphoreType.DMA((2,)), pltpu.SemaphoreType.DMA((2,))]),
        out_shape=jax.ShapeDtypeStruct((B, H, GD), f32),
        compiler_params=params, name="nsa_decode_attn",
    )(page_table, qbd, sel, o_cmp, gate_in, new_rows, expand, bias_s, bias_w, bias_0, kw_t, vw_t, ks_pool, vs_pool)

    out = out.reshape(B, G, R, G, DH)
    h_b = jnp.stack([out[:, g, :, g, :] for g in range(G)], axis=1)
    return h_b.reshape(B, 1, NSA_W)


def _mla_decode_kernel(pt_ref, ql_ref, qr_ref, new_ref, newr_ref, cpool_ref, rpool_ref, o_ref,
                       cbuf, rbuf, cb16, s_ref, csem, rsem):
    f32, bf16 = jnp.float32, jnp.bfloat16
    slot = _paged_pipeline(pt_ref, (cpool_ref, rpool_ref), (cbuf, rbuf), (csem, rsem))
    n_pages = pt_ref.shape[1]
    ql, qr = ql_ref[0], qr_ref[0]
    for p in range(n_pages):
        c16 = cbuf[slot, p].astype(bf16)
        cb16[p] = c16
        s_ref[:, p * PAGE_SIZE:(p + 1) * PAGE_SIZE] = (
            lax.dot_general(ql, c16, (((1,), (1,)), ((), ())), preferred_element_type=f32)
            + jnp.dot(qr, rbuf[slot, p].astype(bf16), preferred_element_type=f32)) * MLA_SCALE
    c_new = new_ref[0].astype(bf16).astype(f32)
    r_new = newr_ref[0].astype(bf16).astype(f32)
    s_new = (jnp.sum(ql.astype(f32) * c_new, axis=-1, keepdims=True)
             + jnp.sum(qr.astype(f32) * r_new, axis=-1, keepdims=True)) * MLA_SCALE
    s_past = s_ref[...]
    m = jnp.maximum(jnp.max(s_past, axis=-1, keepdims=True), s_new)
    e_new = jnp.exp(s_new - m)
    s_ref[...] = jnp.exp(s_past - m)
    denom = jnp.sum(s_ref[...], axis=-1, keepdims=True) + e_new
    inv = 1.0 / denom
    acc = (e_new * inv).astype(bf16).astype(f32) * c_new
    for p in range(n_pages):
        pe = (s_ref[:, p * PAGE_SIZE:(p + 1) * PAGE_SIZE] * inv).astype(bf16)
        acc = acc + jnp.dot(pe, cb16[p], preferred_element_type=f32)
    o_ref[0] = acc


def mla_decode_attention(q_lat, q_rope, ckv_new, kr_new, page_table, ckv_pool, krope_pool):
    f32, bf16 = jnp.float32, jnp.bfloat16
    B, n_pages = page_table.shape
    H = q_lat.shape[1]
    past = n_pages * PAGE_SIZE
    rpool_t = jnp.transpose(krope_pool, (0, 2, 1))
    per_seq = lambda shape: pl.BlockSpec((1,) + shape, lambda b, pt: (b,) + (0,) * len(shape))
    any_spec = pl.BlockSpec(memory_space=pl.ANY)
    return pl.pallas_call(
        _mla_decode_kernel,
        grid_spec=pltpu.PrefetchScalarGridSpec(
            num_scalar_prefetch=1, grid=(B,),
            in_specs=[per_seq((H, KV_LORA)), per_seq((H, MLA_ROPE)), per_seq((1, KV_LORA)), per_seq((1, MLA_ROPE)),
                      any_spec, any_spec],
            out_specs=per_seq((H, KV_LORA)),
            scratch_shapes=[pltpu.VMEM((2, n_pages, PAGE_SIZE, KV_LORA), f32),
                            pltpu.VMEM((2, n_pages, MLA_ROPE, PAGE_SIZE), f32),
                            pltpu.VMEM((n_pages, PAGE_SIZE, KV_LORA), bf16),
                            pltpu.VMEM((H, past), f32),
                            pltpu.SemaphoreType.DMA((2,)), pltpu.SemaphoreType.DMA((2,))]),
        out_shape=jax.ShapeDtypeStruct((B, H, KV_LORA), f32),
        compiler_params=pltpu.CompilerParams(dimension_semantics=("arbitrary",),
                                             vmem_limit_bytes=V7X_VMEM_LIMIT_BYTES),
        name="mla_decode",
    )(page_table, q_lat.astype(bf16), q_rope.astype(bf16), ckv_new.reshape(B, 1, KV_LORA),
      kr_new.reshape(B, 1, MLA_ROPE), ckv_pool, rpool_t)


def split_cols(a, sizes):
    idx = [int(s) for s in np.cumsum(sizes)[:-1]]
    return jnp.split(a, idx, axis=-1)


def rms_norm(x, g):
    xf = x.astype(jnp.float32)
    y = xf * lax.rsqrt(jnp.mean(xf * xf, axis=-1, keepdims=True) + EPS)
    return (y * g.astype(jnp.float32)).astype(x.dtype)


def t5_bucket(dist):
    n = jnp.maximum(dist, 0)
    exact = REL_BUCKETS // 2
    nf = jnp.maximum(n, exact).astype(jnp.float32)
    large = exact + (jnp.log(nf / exact) / math.log(REL_MAX_DIST / exact) * (REL_BUCKETS - exact)).astype(jnp.int32)
    return jnp.where(n < exact, n, jnp.minimum(large, REL_BUCKETS - 1))


def apply_rope(x, pos):
    half = x.shape[-1] // 2
    inv = ROPE_THETA ** (-jnp.arange(half, dtype=jnp.float32) / half)
    ang = pos.astype(jnp.float32)[:, None] * inv[None, :]
    ang = ang.reshape(ang.shape[:1] + (1,) * (x.ndim - 3) + (half,))
    cos, sin = jnp.cos(ang).astype(x.dtype), jnp.sin(ang).astype(x.dtype)
    x1, x2 = x[..., :half], x[..., half:]
    return jnp.concatenate([x1 * cos - x2 * sin, x1 * sin + x2 * cos], axis=-1)


def causal_conv(u, buf, w, b):
    S = u.shape[1]
    full = jnp.concatenate([buf.astype(u.dtype), u], axis=1)
    out = b + sum(full[:, j:j + S] * w[j] for j in range(CONV_K))
    return out, full[:, S:]


def mlstm_chunkwise(q, k, v, i_pre, logf, C0, n0, m0):
    f32 = jnp.float32
    q, k, v, i_pre, logf = (a.astype(f32) for a in (q, k, v, i_pre, logf))
    B, H, S, D = q.shape
    L = MLSTM_CHUNK if S % MLSTM_CHUNK == 0 else S
    NC = S // L

    def chunks(a):
        return jnp.moveaxis(a.reshape((B, H, NC, L) + a.shape[3:]), 2, 0)

    causal = jnp.tril(jnp.ones((L, L), dtype=bool))

    def step(carry, inp):
        C, n, m = carry
        qc, kc, vc, ic, fc = inp
        b = jnp.cumsum(fc, axis=-1)
        g = b + m[..., None]
        dmat = jnp.where(causal, b[..., :, None] - b[..., None, :] + ic[..., None, :], -jnp.inf)
        mt = jnp.maximum(g, jnp.max(dmat, axis=-1))
        inter = jnp.exp(g - mt)
        sqk = jnp.einsum('bhtd,bhsd->bhts', qc, kc) * jnp.exp(dmat - mt[..., None])
        num = inter[..., None] * jnp.einsum('bhvd,bhtd->bhtv', C, qc) + jnp.einsum('bhts,bhsv->bhtv', sqk, vc)
        den = inter * jnp.einsum('bhd,bhtd->bht', n, qc) + jnp.sum(sqk, axis=-1)
        h = num / jnp.maximum(jnp.abs(den), jnp.exp(-mt))[..., None]
        b_end = b[..., -1]
        w_log = b_end[..., None] - b + ic
        m_new = jnp.maximum(b_end + m, jnp.max(w_log, axis=-1))
        decay = jnp.exp(b_end + m - m_new)
        w_in = jnp.exp(w_log - m_new[..., None])
        C_new = decay[..., None, None] * C + jnp.einsum('bhs,bhsv,bhsd->bhvd', w_in, vc, kc)
        n_new = decay[..., None] * n + jnp.einsum('bhs,bhsd->bhd', w_in, kc)
        return (C_new, n_new, m_new), h

    (C1, n1, m1), hs = lax.scan(step, (C0.astype(f32), n0.astype(f32), m0.astype(f32)),
                                tuple(chunks(a) for a in (q, k, v, i_pre, logf)))
    return jnp.moveaxis(hs, 0, 2).reshape(B, H, S, D), C1, n1, m1


def to_chunks(a):
    B, T = a.shape[:2]
    pad = (-T) % CMP_STRIDE
    a = jnp.pad(a, ((0, 0), (0, pad), (0, 0), (0, 0)))
    return a.reshape((B, (T + pad) // CMP_STRIDE, CMP_STRIDE) + a.shape[2:])


def cmp_summaries(chunk_list, T, pe, w1, w2):
    lo = jnp.concatenate([jnp.einsum('bcjgd,jdh->bcgh', r, w1[:CMP_STRIDE]) for r in chunk_list], axis=1)
    hi = jnp.concatenate([jnp.einsum('bcjgd,jdh->bcgh', r, w1[CMP_STRIDE:]) for r in chunk_list], axis=1)
    n_cmp = (T - CMP_LEN) // CMP_STRIDE + 1
    hid = jax.nn.silu(lo[:, :n_cmp] + hi[:, 1:n_cmp + 1] + jnp.einsum('jd,jdh->h', pe, w1))
    return hid @ w2


def even_mixer_prompt(y, g_mix, w_in, conv_w, conv_b, ml_wq, ml_wk, ml_b_i, ml_b_f, ml_out_g,
                      nsa_gq, nsa_gk_cmp, nsa_gk_slc, nsa_gk_win, pe_k, w1_k, w2_k, pe_v, w1_v, w2_v, gate_b, rel_bias):
    B, S, _ = y.shape
    assert S >= CONV_K - 1
    p = even_in_proj_prompt(y, g_mix, w_in, nsa_gq, nsa_gk_slc, nsa_gk_win)
    h_a, C1, n1, m1 = mlstm_prompt(p['u'], p['v'], p['o'], conv_w, conv_b, ml_wq, ml_wk, ml_out_g,
                                   p['i_pre'] + ml_b_i, jax.nn.log_sigmoid(p['f_pre'] + ml_b_f))
    k_cmp = rms_norm(cmp_summaries([to_chunks(p['kc'])], S, pe_k, w1_k, w2_k), nsa_gk_cmp)
    v_cmp = cmp_summaries([to_chunks(p['vc'])], S, pe_v, w1_v, w2_v)
    h_b = nsa_prompt(p['q5'], k_cmp, v_cmp, p['kst'], p['vst'], p['kwt'], p['vwt'], p['g_pre'], gate_b, rel_bias)
    nb = min(WINDOW, S)
    new = dict(C=C1, n=n1, m=m1, conv=p['u'][:, S - (CONV_K - 1):], cmp_k=p['kc'], cmp_v=p['vc'],
               slc_k=p['ks'], slc_v=p['vs'], win_k=p['kw'][:, S - nb:], win_v=p['vw'][:, S - nb:])
    return [h_a, h_b], new


def even_mixer(xn, pos0, past, w_in, w_out, conv_w, conv_b, ml_wq, ml_wk, ml_b_i, ml_b_f, ml_out_g,
               nsa_gq, nsa_gk_cmp, nsa_gk_slc, nsa_gk_win, pe_k, w1_k, w2_k, pe_v, w1_v, w2_v, gate_b, rel_bias):
    B, S, _ = xn.shape
    dt = xn.dtype
    (u, v_m, o_pre, i_pre, f_pre, q, kc, vc, ks, vs, kw, vw, g_pre) = split_cols(xn @ w_in, EV_SPLITS)
    assert S == 1
    conv_buf, C0, n0, m0 = past['conv'], past['C'], past['n'], past['m']
    c, conv_new = causal_conv(u, conv_buf, conv_w, conv_b)
    ch = jax.nn.silu(c).reshape(B, S, ML_H, ML_DH)
    qm = jnp.einsum('bshd,hde->bhse', ch, ml_wq)
    km = jnp.einsum('bshd,hde->bhse', ch, ml_wk) * (ML_DH ** -0.5)
    vm = jnp.transpose(v_m.reshape(B, S, ML_H, ML_DH), (0, 2, 1, 3))
    ig = jnp.transpose(i_pre + ml_b_i, (0, 2, 1))
    lf = jax.nn.log_sigmoid(jnp.transpose(f_pre + ml_b_f, (0, 2, 1)).astype(jnp.float32))
    hm, C1, n1, m1 = mlstm_chunkwise(qm, km, vm, ig, lf, C0, n0, m0)
    hm = rms_norm(jnp.transpose(hm, (0, 2, 1, 3)).astype(dt), ml_out_g)
    h_a = (hm * jax.nn.sigmoid(o_pre).reshape(B, S, ML_H, ML_DH)).reshape(B, S, ML_W)

    q = rms_norm(q.reshape(B, S, NSA_H, NSA_DH), nsa_gq)
    kv_shape = (B, S, NSA_KV, NSA_DH)
    kc, vc, vs, vw = (a.reshape(kv_shape) for a in (kc, vc, vs, vw))
    ks = rms_norm(ks.reshape(kv_shape), nsa_gk_slc)
    kw = rms_norm(kw.reshape(kv_shape), nsa_gk_win)
    e = past['e']
    h_b = nsa_decode(q, ks, vs, kw, vw, g_pre, gate_b, rel_bias, past['page_table'],
                     past['cmp_k'][e], past['cmp_v'][e], past['slc_k'][e], past['slc_v'][e],
                     past['win_k'], past['win_v'], pe_k, w1_k, w2_k, pe_v, w1_v, w2_v, nsa_gk_cmp)
    win_k_new = jnp.concatenate([past['win_k'][:, S:], kw], axis=1)
    win_v_new = jnp.concatenate([past['win_v'][:, S:], vw], axis=1)
    out = jnp.concatenate([h_a, h_b], axis=-1) @ w_out
    new = dict(C=C1.astype(dt), n=n1.astype(dt), m=m1.astype(dt), conv=conv_new,
               cmp_k=kc, cmp_v=vc, slc_k=ks, slc_v=vs, win_k=win_k_new, win_v=win_v_new)
    return out, new


def odd_mixer_prompt(y, g_mix, w_in, g_cq, w_uq, g_q, g_ckv, g_kr, w_uk, w_uv):
    bf16 = jnp.bfloat16
    tq = jnp.arange(y.shape[1])
    cq16, ckv, ckv16, kr, kr16 = odd_in_proj_prompt(y, g_mix, w_in, g_cq, g_ckv, g_kr, tq)
    o = mla_prompt_attention(cq16, w_uq, g_q, tq, ckv16, kr16, jnp.transpose(w_uk, (1, 0, 2)).astype(bf16),
                             jnp.transpose(w_uv, (1, 0, 2)).astype(bf16))
    return [o], dict(ckv=ckv, krope=kr)


def odd_mixer(xn, pos0, past, w_in, g_cq, w_uq, g_q, g_ckv, g_kr, w_uk, w_uv, w_out):
    B, S, _ = xn.shape
    assert S == 1
    cq, ckv, kr = split_cols(xn @ w_in, OD_SPLITS)
    tq = pos0 + jnp.arange(S)
    ckv = rms_norm(ckv, g_ckv)
    kr = apply_rope(rms_norm(kr, g_kr), tq)
    e = past['e']
    q = rms_norm((rms_norm(cq, g_cq) @ w_uq).reshape(B, S, MLA_H, MLA_NOPE + MLA_ROPE), g_q)
    q_nope = q[..., :MLA_NOPE]
    q_rope = apply_rope(q[..., MLA_NOPE:], tq)
    q_lat = jnp.einsum('bqhn,chn->bqhc', q_nope, w_uk)
    o_lat = mla_decode_attention(q_lat[:, 0], q_rope[:, 0], ckv[:, 0], kr[:, 0], past['page_table'],
                                 past['ckv'][e], past['krope'][e])[:, None]
    o = jnp.einsum('bqhc,chv->bqhv', o_lat, w_uv).reshape(B, S, MLA_H * MLA_V)
    return o @ w_out, dict(ckv=ckv, krope=kr)


def mem_kv(mem, g_mem, wk, wv, gk):
    B, M, _ = mem.shape
    m = rms_norm(mem, g_mem)
    k = rms_norm((m @ wk).reshape(B, M, XM_H, XM_DH), gk)
    v = (m @ wv).reshape(B, M, XM_H, XM_DH)
    return k, v


def mem_attend(xn, k, v, wq, gq, wo):
    B, S, _ = xn.shape
    q = rms_norm((xn @ wq).reshape(B, S, XM_H, XM_DH), gq)
    s = jnp.einsum('bshd,bmhd->bhsm', q, k.astype(q.dtype)).astype(jnp.float32) * (XM_DH ** -0.5)
    p = jax.nn.softmax(s, axis=-1)
    return jnp.einsum('bhsm,bmhd->bshd', p.astype(xn.dtype), v.astype(xn.dtype)).reshape(B, S, XM_W) @ wo


def stack_key(lst, name):
    return jnp.stack([d[name] for d in lst])


def kernel(x_prompt, x_sample, mem_prompt,
           state_ml_C, state_ml_n, state_ml_m, state_ml_conv,
           cache_cmp_k, cache_cmp_v, cache_slc_k, cache_slc_v, cache_win_k, cache_win_v,
           cache_mla_ckv, cache_mla_krope, cache_mem_k, cache_mem_v, page_table,
           rel_bias, ffn1_norm, ffn1_wg, ffn1_wu, ffn1_wd, mix_norm,
           xm_norm, xm_mem_norm, xm_wq, xm_wk, xm_wv, xm_wo, xm_gq, xm_gk,
           ffn2_norm, ffn2_wg, ffn2_wu, ffn2_wd,
           ev_w_in, ev_w_out, ml_conv_w, ml_conv_b, ml_wq, ml_wk, ml_b_i, ml_b_f, ml_out_g,
           nsa_gq, nsa_gk_cmp, nsa_gk_slc, nsa_gk_win, cmp_pe_k, cmp_w1_k, cmp_w2_k,
           cmp_pe_v, cmp_w1_v, cmp_w2_v, nsa_gate_b,
           od_w_in, mla_g_cq, mla_w_uq, mla_g_q, mla_g_ckv, mla_g_kr, mla_w_uk, mla_w_uv, od_w_out):
    past_len = page_table.shape[1] * PAGE_SIZE
    bf = jnp.bfloat16
    ffn_w = [[(n[layer], wg[layer].astype(bf), wu[layer].astype(bf), wd[layer].astype(bf))
              for n, wg, wu, wd in ((ffn1_norm, ffn1_wg, ffn1_wu, ffn1_wd), (ffn2_norm, ffn2_wg, ffn2_wu, ffn2_wd))]
             for layer in range(DEPTH)]

    def run_group(y, prompt):
        ev, od, memk, memv = [], [], [], []
        for layer in range(DEPTH):
            y = swiglu_half(y, *ffn_w[layer][0])
            xn = None if prompt else rms_norm(y, mix_norm[layer])
            if layer % 2 == 0:
                e = layer // 2
                ew = dict(w_in=ev_w_in[e], w_out=ev_w_out[e], conv_w=ml_conv_w[e], conv_b=ml_conv_b[e],
                          ml_wq=ml_wq[e], ml_wk=ml_wk[e], ml_b_i=ml_b_i[e], ml_b_f=ml_b_f[e], ml_out_g=ml_out_g[e],
                          nsa_gq=nsa_gq[e], nsa_gk_cmp=nsa_gk_cmp[e], nsa_gk_slc=nsa_gk_slc[e],
                          nsa_gk_win=nsa_gk_win[e], pe_k=cmp_pe_k[e], w1_k=cmp_w1_k[e], w2_k=cmp_w2_k[e],
                          pe_v=cmp_pe_v[e], w1_v=cmp_w1_v[e], w2_v=cmp_w2_v[e],
                          gate_b=nsa_gate_b[e], rel_bias=rel_bias)
                past = None if prompt else dict(
                    e=e, page_table=page_table, C=state_ml_C[e], n=state_ml_n[e], m=state_ml_m[e],
                    conv=state_ml_conv[e], cmp_k=cache_cmp_k, cmp_v=cache_cmp_v,
                    slc_k=cache_slc_k, slc_v=cache_slc_v, win_k=cache_win_k[e], win_v=cache_win_v[e])
                if prompt:
                    w_mix = ew.pop('w_out')
                    parts, st = even_mixer_prompt(y, mix_norm[layer], **ew)
                else:
                    h, st = even_mixer(xn, past_len, past, **ew)
                    y = y + h
                ev.append(st)
            else:
                o = layer // 2
                ow = dict(w_in=od_w_in[o], g_cq=mla_g_cq[o], w_uq=mla_w_uq[o], g_q=mla_g_q[o], g_ckv=mla_g_ckv[o],
                          g_kr=mla_g_kr[o], w_uk=mla_w_uk[o], w_uv=mla_w_uv[o], w_out=od_w_out[o])
                past = None if prompt else dict(e=o, page_table=page_table, ckv=cache_mla_ckv, krope=cache_mla_krope)
                if prompt:
                    w_mix = ow.pop('w_out')
                    parts, st = odd_mixer_prompt(y, mix_norm[layer], **ow)
                else:
                    h, st = odd_mixer(xn, past_len, past, **ow)
                    y = y + h
                od.append(st)
            if prompt:
                mk, mv = mem_kv(mem_prompt, xm_mem_norm[layer], xm_wk[layer], xm_wv[layer], xm_gk[layer])
                memk.append(mk)
                memv.append(mv)
                y = mem_attention(y, xm_norm[layer], xm_wq[layer], xm_gq[layer], mk, mv, xm_wo[layer],
                                  parts=parts, w_mix=w_mix)
            else:
                y = y + mem_attend(rms_norm(y, xm_norm[layer]), cache_mem_k[layer], cache_mem_v[layer],
                                   xm_wq[layer], xm_gq[layer], xm_wo[layer])
            y = swiglu_half(y, *ffn_w[layer][1])
        return y, ev, od, memk, memv

    ys, ev_s, od_s, _, _ = run_group(x_sample, False)
    yp, ev_p, od_p, memk_p, memv_p = run_group(x_prompt, True)
    return (yp, ys,
            stack_key(ev_p, 'C'), stack_key(ev_p, 'n'), stack_key(ev_p, 'm'), stack_key(ev_p, 'conv'),
            stack_key(ev_p, 'cmp_k'), stack_key(ev_p, 'cmp_v'), stack_key(ev_p, 'slc_k'), stack_key(ev_p, 'slc_v'),
            stack_key(ev_p, 'win_k'), stack_key(ev_p, 'win_v'),
            stack_key(od_p, 'ckv'), stack_key(od_p, 'krope'),
            jnp.stack(memk_p), jnp.stack(memv_p),
            stack_key(ev_s, 'C'), stack_key(ev_s, 'n'), stack_key(ev_s, 'm'), stack_key(ev_s, 'conv'),
            stack_key(ev_s, 'cmp_k'), stack_key(ev_s, 'cmp_v'), stack_key(ev_s, 'slc_k'), stack_key(ev_s, 'slc_v'),
            stack_key(ev_s, 'win_k'), stack_key(ev_s, 'win_v'),
            stack_key(od_s, 'ckv'), stack_key(od_s, 'krope'))
```
